```python
import functools
import jax, jax.numpy as jnp
from jax import lax
import numpy as np

D_MODEL = 2048
BATCH = 16
SEQ = 256
DEPTH = 2
DEC_BATCH = 4
DEC_SEQ = 2048
PAST_LEN = 512

GRID_W = 64
N_MOD = 9
D_FF = 5504
FFN_RES = 0.5
EPS = 1e-6
NEG_INF = -1e30

POOL_GROUPS = 4
POOL_WINDOWS = (2, 4, 8, 16)
POOL_WIDTH = 1024
POOL_GC = POOL_WIDTH // POOL_GROUPS

NA_HEADS = 8
NA_HEAD_DIM = 128
NA_WIDTH = NA_HEADS * NA_HEAD_DIM
NA_WIN_H = 8
NA_WIN_W = 16
NA_QBLK = 16
NA_KBAND = NA_QBLK + NA_WIN_W
ATTN_QBLK = 128

GLA_HEADS = 4
GLA_DK = 128
GLA_DV = 256
GLA_KW = GLA_HEADS * GLA_DK
GLA_VW = GLA_HEADS * GLA_DV
GLA_RANK = 16
GLA_TAU = 16.0
GLA_CHUNK = 64
ROPE_BASE = 10000.0

BRANCH_W = 1024
N_BRANCH = 3
IN_SPLITS = (POOL_WIDTH, NA_WIDTH, NA_WIDTH, NA_WIDTH, GLA_KW, GLA_KW, GLA_VW, 2 * GLA_RANK, GLA_VW, N_BRANCH * D_MODEL)
IN_COLS = sum(IN_SPLITS)

kernel_name = 'hybrid_pool_na_gla_diffusion_step'

f32 = jnp.float32


def rmsnorm(x, g):
    xf = x.astype(f32)
    y = xf * lax.rsqrt(jnp.mean(xf * xf, axis=-1, keepdims=True) + EPS)
    return (y * g.astype(f32)).astype(x.dtype)


def modulate(x, shift, scale):
    return x * (1 + scale) + shift


def swiglu(h, w_in, w_out):
    gt, up = jnp.split(h @ w_in, 2, axis=-1)
    return (jax.nn.silu(gt) * up) @ w_out


def heads(x, n):
    B, L, _ = x.shape
    return x.reshape(B, L, n, -1).transpose(0, 2, 1, 3)


def merge_heads(x):
    B, H, L, d = x.shape
    return x.transpose(0, 2, 1, 3).reshape(B, L, H * d)


def split_in(h, w_in):
    idx = [int(i) for i in np.cumsum(IN_SPLITS)[:-1]]
    return jnp.split(h @ w_in, idx, axis=-1)


def pool_mix(u, w_grp, scale):
    B, L, _ = u.shape
    ug = u.reshape(B, L, POOL_GROUPS, POOL_GC)
    cs = jnp.concatenate([jnp.zeros((B, 1, POOL_GROUPS, POOL_GC), f32),
                          jnp.cumsum(ug.astype(f32), axis=1)], axis=1)
    t = jnp.arange(L)
    outs = []
    for gi, win in enumerate(POOL_WINDOWS):
        lo = jnp.clip(t - win // 2, 0, L - 1)
        hi = jnp.clip(t + win - 1 - win // 2, 0, L - 1)
        csg = cs[:, :, gi]
        cnt = (hi - lo + 1).astype(f32)[None, :, None]
        outs.append((csg[:, hi + 1] - csg[:, lo]) / cnt)
    pooled = jnp.stack(outs, axis=2).astype(u.dtype) - ug
    y = jnp.einsum('blgc,gcd->blgd', pooled, w_grp).reshape(B, L, POOL_WIDTH)
    return y * scale


def context_attention(q, k, v):
    B, H, L, d = q.shape
    nb = L // ATTN_QBLK
    qb = q.reshape(B, H, nb, ATTN_QBLK, d).transpose(2, 0, 1, 3, 4)

    def blk(qi):
        s = jnp.einsum('bhqd,bhkd->bhqk', qi, k).astype(f32) * (d ** -0.5)
        p = jax.nn.softmax(s, axis=-1).astype(v.dtype)
        return jnp.einsum('bhqk,bhkd->bhqd', p, v)

    o = lax.map(blk, qb)
    return o.transpose(1, 2, 0, 3, 4).reshape(B, H, L, d)


def na_latent(q, k, v, ck, cv, rpb):
    B, H, N, hd = q.shape
    rows = N // GRID_W
    kh = min(NA_WIN_H, rows)
    nb = GRID_W // NA_QBLK
    qc = np.arange(GRID_W).reshape(nb, NA_QBLK)
    cs = np.clip(qc - NA_WIN_W // 2, 0, GRID_W - NA_WIN_W)
    bs = np.clip(np.arange(nb) * NA_QBLK - NA_WIN_W // 2, 0, GRID_W - NA_KBAND)
    kc = bs[:, None] + np.arange(NA_KBAND)
    col_ok = (kc[:, None, :] >= cs[:, :, None]) & (kc[:, None, :] < cs[:, :, None] + NA_WIN_W)
    col_idx = np.clip(kc[:, None, :] - qc[:, :, None] + NA_WIN_W - 1, 0, 2 * NA_WIN_W - 2)
    scale = hd ** -0.5
    k_grid = k.reshape(B, H, rows, GRID_W, hd)
    v_grid = v.reshape(B, H, rows, GRID_W, hd)
    q_rows = jnp.moveaxis(q.reshape(B, H, rows, GRID_W, hd), 2, 0)
    n_loc = kh * NA_KBAND

    def row_block(args):
        qr, r = args
        rs = jnp.clip(r - kh // 2, 0, rows - kh)
        kb = lax.dynamic_slice_in_dim(k_grid, rs, kh, axis=2)[:, :, :, kc]
        vb = lax.dynamic_slice_in_dim(v_grid, rs, kh, axis=2)[:, :, :, kc]
        qb = qr.reshape(B, H, nb, NA_QBLK, hd)
        s_loc = jnp.einsum('bhnqd,bhinkd->bhnqik', qb, kb).astype(f32) * scale
        bias = rpb[:, rs + jnp.arange(kh) - r + NA_WIN_H - 1][:, :, col_idx]
        s_loc = jnp.where(col_ok[:, :, None, :], s_loc + bias.transpose(0, 2, 3, 1, 4).astype(f32), NEG_INF)
        s_ctx = jnp.einsum('bhnqd,bhcd->bhnqc', qb, ck).astype(f32) * scale
        s = jnp.concatenate([s_loc.reshape(B, H, nb, NA_QBLK, n_loc), s_ctx], axis=-1)
        p = jax.nn.softmax(s, axis=-1).astype(v.dtype)
        p_loc = p[..., :n_loc].reshape(B, H, nb, NA_QBLK, kh, NA_KBAND)
        o = (jnp.einsum('bhnqik,bhinkd->bhnqd', p_loc, vb)
             + jnp.einsum('bhnqc,bhcd->bhnqd', p[..., n_loc:], cv.astype(v.dtype)))
        return o.reshape(B, H, GRID_W, hd)

    o = lax.map(row_block, (q_rows, jnp.arange(rows)))
    return jnp.moveaxis(o, 0, 2).reshape(B, H, N, hd)


def axial_rope(x):
    L = x.shape[2]
    t = jnp.arange(L)
    pos = (t // GRID_W, t % GRID_W)
    half = GLA_DK // 2
    nf = half // 2
    inv = ROPE_BASE ** (-jnp.arange(nf, dtype=f32) / nf)
    parts = []
    for ax in range(2):
        xa = x[..., ax * half:(ax + 1) * half]
        ang = pos[ax].astype(f32)[:, None] * inv
        cos, sin = jnp.cos(ang), jnp.sin(ang)
        x1, x2 = xa[..., :nf], xa[..., nf:]
        parts += [x1 * cos - x2 * sin, x2 * cos + x1 * sin]
    return jnp.concatenate(parts, axis=-1)


def gla_chunked(q, k, v, g, s0):
    B, H, L, dk = q.shape
    dv = v.shape[-1]
    n = L // GLA_CHUNK
    rs = lambda a: a.reshape(B, H, n, GLA_CHUNK, a.shape[-1])
    q, k, v, g = rs(q), rs(k), rs(v), rs(g)
    b = jnp.cumsum(g, axis=3)
    b_end = b[:, :, :, -1:, :]
    q_in = q * jnp.exp(b)
    a = jnp.einsum('bhncd,bhnsd->bhncs', q_in, k * jnp.exp(-b))
    causal = jnp.tril(jnp.ones((GLA_CHUNK, GLA_CHUNK), dtype=bool))
    o_intra = jnp.einsum('bhncs,bhnse->bhnce', jnp.where(causal, a, 0.0), v)
    k_dec = k * jnp.exp(b_end - b)
    decay = jnp.exp(b_end[:, :, :, 0, :])

    def step(S, xs):
        qi, ki, vi, di = xs
        o = jnp.einsum('bhcd,bhde->bhce', qi, S)
        S = di[..., None] * S + jnp.einsum('bhcd,bhce->bhde', ki, vi)
        return S, o

    mv = lambda a: jnp.moveaxis(a, 2, 0)
    s_fin, o_inter = lax.scan(step, s0, (mv(q_in), mv(k_dec), mv(v), mv(decay)))
    o = o_intra + jnp.moveaxis(o_inter, 0, 2)
    return o.reshape(B, H, L, dv), s_fin


def gla_bidir(q, k, v, z, w_gate, b_gate, s0):
    outs, states = [], []
    for d in range(2):
        logit = jnp.einsum('blr,rk->blk', z[:, :, d], w_gate[d].astype(f32)) + b_gate[d].astype(f32)
        g = heads(jax.nn.log_sigmoid(logit) / GLA_TAU, GLA_HEADS)
        if d == 0:
            o, s = gla_chunked(q, k, v, g, s0[:, 0].astype(f32))
        else:
            fl = lambda a: jnp.flip(a, axis=2)
            o, s = gla_chunked(fl(q), fl(k), fl(v), fl(g), s0[:, 1].astype(f32))
            o = fl(o)
        outs.append(o)
        states.append(s)
    return outs[0] + outs[1], jnp.stack(states, axis=1)


def gla_out(o, norm_g, r):
    B, H, L, dv = o.shape
    o = o * lax.rsqrt(jnp.mean(o * o, axis=-1, keepdims=True) + EPS) * norm_g.astype(f32).reshape(H, 1, dv)
    return merge_heads(o).astype(r.dtype) * jax.nn.silu(r)


def merge_branches(y_pool, y_na, y_gla, gate_logits, w_branch, w_out):
    B, L, _ = y_pool.shape
    br = jnp.stack([y_pool, y_na.astype(y_pool.dtype), y_gla], axis=2)
    y = jnp.einsum('blnc,ncd->blnd', br, w_branch)
    g = jax.nn.sigmoid(gate_logits.reshape(B, L, N_BRANCH, D_MODEL).astype(f32)).astype(y.dtype)
    return jnp.sum(g * y, axis=2) @ w_out


def gla_qkv(gq, gk, gv, rope):
    q = heads(gq, GLA_HEADS).astype(f32)
    k = heads(gk, GLA_HEADS).astype(f32)
    if rope:
        q, k = axial_rope(q), axial_rope(k)
    return q * (GLA_DK ** -0.5), k, heads(gv, GLA_HEADS).astype(f32)


def context_mixer(h, w_in, pool_w, pool_scale, gla_w_gate, gla_b_gate, gla_norm, w_branch, w_out):
    B, L, _ = h.shape
    u, nq, nk, nv, gq, gk, gv, gz, gr, gl = split_in(h, w_in)
    y_pool = pool_mix(u, pool_w, pool_scale)
    nq, nk, nv = heads(nq, NA_HEADS), heads(nk, NA_HEADS), heads(nv, NA_HEADS)
    y_na = merge_heads(context_attention(nq, nk, nv))
    q, k, v = gla_qkv(gq, gk, gv, rope=False)
    s0 = jnp.zeros((B, 2, GLA_HEADS, GLA_DK, GLA_DV), f32)
    o, s_fin = gla_bidir(q, k, v, gz.reshape(B, L, 2, GLA_RANK).astype(f32), gla_w_gate, gla_b_gate, s0)
    y_gla = gla_out(o, gla_norm, gr)
    return merge_branches(y_pool, y_na, y_gla, gl, w_branch, w_out), (nk, nv, s_fin)


def latent_mixer(h, ck, cv, s_ctx, w_in, pool_w, pool_scale, na_rpb, gla_w_gate, gla_b_gate, gla_norm, w_branch, w_out):
    B, L, _ = h.shape
    u, nq, nk, nv, gq, gk, gv, gz, gr, gl = split_in(h, w_in)
    y_pool = pool_mix(u, pool_w, pool_scale)
    nq, nk, nv = heads(nq, NA_HEADS), heads(nk, NA_HEADS), heads(nv, NA_HEADS)
    y_na = merge_heads(na_latent(nq, nk, nv, ck, cv, na_rpb))
    q, k, v = gla_qkv(gq, gk, gv, rope=True)
    o, _ = gla_bidir(q, k, v, gz.reshape(B, L, 2, GLA_RANK).astype(f32), gla_w_gate, gla_b_gate, s_ctx)
    y_gla = gla_out(o, gla_norm, gr)
    return merge_branches(y_pool, y_na, y_gla, gl, w_branch, w_out), None


def trunk_layer(x, mod, pre, post, ffn_in, ffn_out, mixer_fn):
    m = lambda i: (mod[:, None, 3 * i], mod[:, None, 3 * i + 1], mod[:, None, 3 * i + 2])
    sh, sc, gt = m(0)
    y = swiglu(modulate(rmsnorm(x, pre[0]), sh, sc), ffn_in[0], ffn_out[0])
    x = x + FFN_RES * gt * rmsnorm(y, post[0])
    sh, sc, gt = m(1)
    y, aux = mixer_fn(modulate(rmsnorm(x, pre[1]), sh, sc))
    x = x + gt * rmsnorm(y, post[1])
    sh, sc, gt = m(2)
    y = swiglu(modulate(rmsnorm(x, pre[2]), sh, sc), ffn_in[1], ffn_out[1])
    x = x + FFN_RES * gt * rmsnorm(y, post[2])
    return x, aux


def setup_inputs(seed: int = 0) -> dict:
    key = jax.random.key(seed)
    ks = jax.random.split(key, 24)
    nrm = lambda i, shape, s=1.0: jax.random.normal(ks[i], shape, f32) * s
    D = D_MODEL
    return {
        'x_prompt': nrm(0, (BATCH, SEQ, D)),
        'x_sample': nrm(1, (DEC_BATCH, DEC_SEQ, D)),
        'c': nrm(2, (DEC_BATCH, D)),
        'cache_na_k': nrm(3, (DEC_BATCH, DEPTH, NA_HEADS, PAST_LEN, NA_HEAD_DIM)),
        'cache_na_v': nrm(4, (DEC_BATCH, DEPTH, NA_HEADS, PAST_LEN, NA_HEAD_DIM)),
        'state_gla': nrm(5, (DEC_BATCH, DEPTH, 2, GLA_HEADS, GLA_DK, GLA_DV)),
        'c_ctx': nrm(6, (D,)),
        'w_mod': nrm(7, (DEPTH, D, N_MOD * D), D ** -0.5),
        'b_mod': nrm(8, (DEPTH, N_MOD * D), 0.01),
        'norm_pre': 1.0 + nrm(9, (DEPTH, 3, D), 0.05),
        'norm_post': 1.0 + nrm(10, (DEPTH, 3, D), 0.05),
        'w_ffn_in': nrm(11, (DEPTH, 2, D, 2 * D_FF), D ** -0.5),
        'w_ffn_out': nrm(12, (DEPTH, 2, D_FF, D), D_FF ** -0.5),
        'w_in': nrm(13, (DEPTH, D, IN_COLS), D ** -0.5),
        'pool_w': nrm(14, (DEPTH, POOL_GROUPS, POOL_GC, POOL_GC), POOL_GC ** -0.5),
        'pool_scale': 1.0 + nrm(15, (DEPTH, POOL_WIDTH), 0.1),
        'na_rpb': nrm(16, (DEPTH, NA_HEADS, 2 * NA_WIN_H - 1, 2 * NA_WIN_W - 1), 0.1),
        'gla_w_gate': nrm(17, (DEPTH, 2, GLA_RANK, GLA_KW), GLA_RANK ** -0.5),
        'gla_b_gate': nrm(18, (DEPTH, 2, GLA_KW), 0.1),
        'gla_norm': 1.0 + nrm(19, (DEPTH, GLA_VW), 0.05),
        'w_branch': nrm(20, (DEPTH, N_BRANCH, BRANCH_W, D), BRANCH_W ** -0.5),
        'w_out': nrm(21, (DEPTH, D, D), D ** -0.5),
    }


def reference(x_prompt, x_sample, c, cache_na_k, cache_na_v, state_gla, c_ctx, w_mod, b_mod, norm_pre,
              norm_post, w_ffn_in, w_ffn_out, w_in, pool_w, pool_scale, na_rpb, gla_w_gate, gla_b_gate,
              gla_norm, w_branch, w_out):
    xp = x_prompt
    new_k, new_v, new_s = [], [], []
    for l in range(DEPTH):
        mod = (jax.nn.silu(c_ctx)[None] @ w_mod[l] + b_mod[l]).reshape(1, N_MOD, D_MODEL)
        mixer = functools.partial(context_mixer, w_in=w_in[l], pool_w=pool_w[l], pool_scale=pool_scale[l],
                                  gla_w_gate=gla_w_gate[l], gla_b_gate=gla_b_gate[l], gla_norm=gla_norm[l],
                                  w_branch=w_branch[l], w_out=w_out[l])
        xp, (k_l, v_l, s_l) = trunk_layer(xp, mod, norm_pre[l], norm_post[l], w_ffn_in[l], w_ffn_out[l], mixer)
        new_k.append(k_l)
        new_v.append(v_l)
        new_s.append(s_l)
    new_na_k = jnp.stack(new_k, axis=1)
    new_na_v = jnp.stack(new_v, axis=1)
    new_state_gla = jnp.stack(new_s, axis=1)

    xs = x_sample
    for l in range(DEPTH):
        mod = (jax.nn.silu(c) @ w_mod[l] + b_mod[l]).reshape(-1, N_MOD, D_MODEL)
        mixer = functools.partial(latent_mixer, ck=cache_na_k[:, l], cv=cache_na_v[:, l], s_ctx=state_gla[:, l],
                                  w_in=w_in[l], pool_w=pool_w[l], pool_scale=pool_scale[l], na_rpb=na_rpb[l],
                                  gla_w_gate=gla_w_gate[l], gla_b_gate=gla_b_gate[l], gla_norm=gla_norm[l],
                                  w_branch=w_branch[l], w_out=w_out[l])
        xs, _ = trunk_layer(xs, mod, norm_pre[l], norm_post[l], w_ffn_in[l], w_ffn_out[l], mixer)

    return (xp, xs, new_na_k, new_na_v, new_state_gla)
```

```python
import functools

import numpy as np
import jax
import jax.numpy as jnp
from jax import lax
from jax.experimental import pallas as pl
from jax.experimental.pallas import tpu as pltpu

f32 = jnp.float32
bf16 = jnp.bfloat16

D_MODEL = 2048
SEQ = 256
DEC_SEQ = 2048
GRID_W = 64
N_MOD = 9
D_FF = 5504
FFN_RES = 0.5
EPS = 1e-6
NEG_INF = -1e30

POOL_GROUPS = 4
POOL_WINDOWS = (2, 4, 8, 16)
POOL_WIDTH = 1024
POOL_GC = POOL_WIDTH // POOL_GROUPS

NA_HEADS = 8
NA_HEAD_DIM = 128
NA_WIDTH = NA_HEADS * NA_HEAD_DIM
NA_WIN_H = 8
NA_WIN_W = 16

GLA_HEADS = 4
GLA_DK = 128
GLA_DV = 256
GLA_KW = GLA_HEADS * GLA_DK
GLA_VW = GLA_HEADS * GLA_DV
GLA_RANK = 16
GLA_TAU = 16.0
GLA_CHUNK = 64
ROPE_BASE = 10000.0

BRANCH_W = 1024
N_BRANCH = 3
GATE_W = N_BRANCH * D_MODEL

LANE = 128
VMEM_LIMIT = 56 * 1024 * 1024

OFF_GL = 0
OFF_POOL = OFF_GL + GATE_W
OFF_NQ = OFF_POOL + POOL_WIDTH
OFF_NK = OFF_NQ + NA_WIDTH
OFF_NV = OFF_NK + NA_WIDTH
OFF_GQ = OFF_NV + NA_WIDTH
OFF_GK = OFF_GQ + GLA_KW
OFF_GV = OFF_GK + GLA_KW
OFF_GR = OFF_GV + GLA_VW
OFF_GZ = OFF_GR + GLA_VW
TN_IN = 1536
IN_COLS_P = -(-(OFF_GZ + LANE) // TN_IN) * TN_IN

TM = 512
TF = 512
D_FF_P = -(-D_FF // TF) * TF


def _cparams(sem):
    return pltpu.CompilerParams(dimension_semantics=sem, vmem_limit_bytes=VMEM_LIMIT)


def _cond_index(i, tm, n_ctx):
    row = i * tm
    return jnp.where(row < n_ctx, 0, 1 + (row - n_ctx) // DEC_SEQ)


def _rms(x, g):
    return x * lax.rsqrt(jnp.mean(x * x, axis=-1, keepdims=True) + EPS) * g


def _dot(a, b):
    return jnp.dot(a, b, preferred_element_type=f32)


def _dot_nt(a, b):
    return lax.dot_general(a, b, (((1,), (1,)), ((), ())), preferred_element_type=f32)


def _mod_body(c_ref, w_ref, b_ref, o_ref):
    c = c_ref[...]
    s = c * jax.nn.sigmoid(c)
    o_ref[...] = _dot(s.astype(bf16), w_ref[...].astype(bf16)) + b_ref[...]


def _modulation(c_all, w_mod, b_mod):
    depth, d, n = w_mod.shape
    nc = c_all.shape[0]
    tn = 1024
    return pl.pallas_call(
        _mod_body,
        grid=(depth, n // tn),
        in_specs=[
            pl.BlockSpec((nc, d), lambda l, j: (0, 0)),
            pl.BlockSpec((None, d, tn), lambda l, j: (l, 0, j)),
            pl.BlockSpec((None, 1, tn), lambda l, j: (l, 0, j)),
        ],
        out_specs=pl.BlockSpec((None, nc, tn), lambda l, j: (l, 0, j)),
        out_shape=jax.ShapeDtypeStruct((depth, nc, n), f32),
        compiler_params=_cparams(("parallel", "parallel")),
        name="modulation",
    )(c_all, w_mod, b_mod.reshape(depth, 1, n))


def _ffn_body(x_ref, mod_ref, pre_ref, post_ref, wg_ref, wu_ref, wo_ref, o_ref, h_scr, acc_scr):
    f = pl.program_id(1)
    d = D_MODEL

    @pl.when(f == 0)
    def _():
        y = _rms(x_ref[...], pre_ref[...])
        h = y * (1.0 + mod_ref[:, d:2 * d]) + mod_ref[:, 0:d]
        h_scr[...] = h.astype(bf16)
        acc_scr[...] = jnp.zeros_like(acc_scr)

    h = h_scr[...]
    gt = _dot(h, wg_ref[...])
    up = _dot(h, wu_ref[...])
    a = gt * jax.nn.sigmoid(gt) * up
    acc_scr[...] += _dot(a.astype(bf16), wo_ref[...])

    @pl.when(f == pl.num_programs(1) - 1)
    def _():
        yn = _rms(acc_scr[...], post_ref[...])
        o_ref[...] = x_ref[...] + FFN_RES * mod_ref[:, 2 * d:3 * d] * yn


def _ffn(x, mod_l, pre, post, w_in2, w_out, sub, n_ctx):
    t, d = x.shape
    fp = w_out.shape[0]
    return pl.pallas_call(
        _ffn_body,
        grid=(t // TM, fp // TF),
        in_specs=[
            pl.BlockSpec((TM, d), lambda i, f: (i, 0)),
            pl.BlockSpec((None, 1, 3 * d), lambda i, f: (_cond_index(i, TM, n_ctx), 0, sub)),
            pl.BlockSpec((1, d), lambda i, f: (0, 0)),
            pl.BlockSpec((1, d), lambda i, f: (0, 0)),
            pl.BlockSpec((None, d, TF), lambda i, f: (0, 0, f)),
            pl.BlockSpec((None, d, TF), lambda i, f: (1, 0, f)),
            pl.BlockSpec((TF, d), lambda i, f: (f, 0)),
        ],
        out_specs=pl.BlockSpec((TM, d), lambda i, f: (i, 0)),
        out_shape=jax.ShapeDtypeStruct((t, d), f32),
        scratch_shapes=[pltpu.VMEM((TM, d), bf16), pltpu.VMEM((TM, d), f32)],
        compiler_params=_cparams(("parallel", "arbitrary")),
        name="ffn",
    )(x, mod_l, pre, post, w_in2, w_in2, w_out)


def _inproj_body(x_ref, mod_ref, pre_ref, w_ref, o_ref, h_scr):
    d = D_MODEL

    @pl.when(pl.program_id(1) == 0)
    def _():
        y = _rms(x_ref[...], pre_ref[...])
        h = y * (1.0 + mod_ref[:, d:2 * d]) + mod_ref[:, 0:d]
        h_scr[...] = h.astype(bf16)

    o_ref[...] = _dot(h_scr[...], w_ref[...])


def _inproj(x, mod_l, pre, w_in_p, n_ctx):
    t, d = x.shape
    n = w_in_p.shape[1]
    return pl.pallas_call(
        _inproj_body,
        grid=(t // TM, n // TN_IN),
        in_specs=[
            pl.BlockSpec((TM, d), lambda i, j: (i, 0)),
            pl.BlockSpec((None, 1, 3 * d), lambda i, j: (_cond_index(i, TM, n_ctx), 0, 1)),
            pl.BlockSpec((1, d), lambda i, j: (0, 0)),
            pl.BlockSpec((d, TN_IN), lambda i, j: (0, j)),
        ],
        out_specs=pl.BlockSpec((TM, TN_IN), lambda i, j: (i, j)),
        out_shape=jax.ShapeDtypeStruct((t, n), f32),
        scratch_shapes=[pltpu.VMEM((TM, d), bf16)],
        compiler_params=_cparams(("parallel", "arbitrary")),
        name="inproj",
    )(x, mod_l, pre, w_in_p)


POOL_PAD = 8


def _pool_body(u_ref, w_ref, sc_ref, o_ref, pad_scr, *, seq):
    gc = POOL_GC
    zeros = jnp.zeros((POOL_PAD, POOL_WIDTH), f32)
    pad_scr[pl.ds(0, POOL_PAD), :] = zeros
    pad_scr[pl.ds(POOL_PAD + seq, POOL_PAD), :] = zeros
    pad_scr[pl.ds(POOL_PAD, seq), :] = u_ref[...]
    t = lax.broadcasted_iota(jnp.int32, (seq, 1), 0)
    for gi, win in enumerate(POOL_WINDOWS):
        cols = pl.ds(gi * gc, gc)
        lo = jnp.maximum(t - win // 2, 0)
        hi = jnp.minimum(t + win - 1 - win // 2, seq - 1)
        cnt = (hi - lo + 1).astype(f32)
        acc = pad_scr[pl.ds(POOL_PAD - win // 2, seq), cols]
        for j in range(1, win):
            acc = acc + pad_scr[pl.ds(POOL_PAD - win // 2 + j, seq), cols]
        pooled = acc / cnt - u_ref[:, cols]
        y = _dot(pooled.astype(bf16), w_ref[gi])
        o_ref[:, cols] = (y * sc_ref[:, cols]).astype(o_ref.dtype)


def _pool(p, pool_w, pool_scale, seq, row_block0, nseq):
    cb = OFF_POOL // POOL_WIDTH
    return pl.pallas_call(
        functools.partial(_pool_body, seq=seq),
        grid=(nseq,),
        in_specs=[
            pl.BlockSpec((seq, POOL_WIDTH), lambda s: (row_block0 + s, cb)),
            pl.BlockSpec((POOL_GROUPS, POOL_GC, POOL_GC), lambda s: (0, 0, 0)),
            pl.BlockSpec((1, POOL_WIDTH), lambda s: (0, 0)),
        ],
        out_specs=pl.BlockSpec((seq, POOL_WIDTH), lambda s: (s, 0)),
        out_shape=jax.ShapeDtypeStruct((nseq * seq, POOL_WIDTH), bf16),
        scratch_shapes=[pltpu.VMEM((seq + 2 * POOL_PAD, POOL_WIDTH), f32)],
        compiler_params=_cparams(("parallel",)),
        name="pool",
    )(p, pool_w, pool_scale)


def _softmax_rows(s):
    m = jnp.max(s, axis=-1, keepdims=True)
    e = jnp.exp(s - m)
    return e / jnp.sum(e, axis=-1, keepdims=True)


def _ctx_attn_body(q_ref, k_ref, v_ref, o_ref):
    hd = NA_HEAD_DIM
    for h in range(NA_HEADS):
        cols = pl.ds(h * hd, hd)
        q = q_ref[:, cols].astype(bf16)
        k = k_ref[:, cols].astype(bf16)
        v = v_ref[:, cols].astype(bf16)
        p = _softmax_rows(_dot_nt(q, k) * (hd ** -0.5))
        o_ref[:, cols] = _dot(p.astype(bf16), v).astype(o_ref.dtype)


def _ctx_attn(p, nseq):
    spec = lambda off: pl.BlockSpec((SEQ, NA_WIDTH), lambda b: (b, off // NA_WIDTH))
    return pl.pallas_call(
        _ctx_attn_body,
        grid=(nseq,),
        in_specs=[spec(OFF_NQ), spec(OFF_NK), spec(OFF_NV)],
        out_specs=pl.BlockSpec((SEQ, NA_WIDTH), lambda b: (b, 0)),
        out_shape=jax.ShapeDtypeStruct((nseq * SEQ, NA_WIDTH), bf16),
        compiler_params=_cparams(("parallel",)),
        name="ctx_attn",
    )(p, p, p)


def _na_bias_table(rpb):
    qc = np.arange(GRID_W)[:, None]
    kc = np.arange(GRID_W)[None, :]
    cs = np.clip(qc - NA_WIN_W // 2, 0, GRID_W - NA_WIN_W)
    ok = (kc >= cs) & (kc < cs + NA_WIN_W)
    cidx = np.clip(kc - qc + NA_WIN_W - 1, 0, 2 * NA_WIN_W - 2)
    ridx = np.arange(NA_WIN_H)[:, None] + np.arange(NA_WIN_H)[None, :]
    tbl = rpb[:, ridx][:, :, :, cidx]
    tbl = jnp.where(ok[None, None, None], tbl, NEG_INF)
    tbl = tbl.transpose(0, 1, 3, 2, 4)
    return tbl.reshape(rpb.shape[0], NA_WIN_H, GRID_W, NA_WIN_H * GRID_W).astype(f32)


def _na_body(q_ref, k_ref, v_ref, ck_ref, cv_ref, bias_ref, o_ref, kb_scr, vb_scr, *, rows):
    hd = NA_HEAD_DIM
    scale = hd ** -0.5
    kh = min(NA_WIN_H, rows)
    nloc = kh * GRID_W
    kb_scr[...] = k_ref[...].astype(bf16)
    vb_scr[...] = v_ref[...].astype(bf16)
    ck = ck_ref[...].astype(bf16)
    cv = cv_ref[...].astype(bf16)

    def row(r, carry):
        rs = jnp.clip(r - kh // 2, 0, rows - kh)
        q0 = pl.multiple_of(r * GRID_W, GRID_W)
        k0 = pl.multiple_of(rs * GRID_W, GRID_W)
        q = q_ref[pl.ds(q0, GRID_W), :].astype(bf16)
        s_loc = _dot_nt(q, kb_scr[pl.ds(k0, nloc), :]) * scale + bias_ref[rs - r + NA_WIN_H - 1]
        s_ctx = _dot_nt(q, ck) * scale
        m = jnp.maximum(jnp.max(s_loc, axis=-1, keepdims=True), jnp.max(s_ctx, axis=-1, keepdims=True))
        e_loc = jnp.exp(s_loc - m)
        e_ctx = jnp.exp(s_ctx - m)
        den = jnp.sum(e_loc, axis=-1, keepdims=True) + jnp.sum(e_ctx, axis=-1, keepdims=True)
        o = (_dot((e_loc / den).astype(bf16), vb_scr[pl.ds(k0, nloc), :])
             + _dot((e_ctx / den).astype(bf16), cv))
        o_ref[pl.ds(q0, GRID_W), :] = o.astype(o_ref.dtype)
        return carry

    lax.fori_loop(0, rows, row, 0)


def _na_latent(p, cache_k, cache_v, bias_tbl, layer, row_block0, nreq):
    n = DEC_SEQ
    hd = NA_HEAD_DIM
    past = cache_k.shape[3]
    rows = n // GRID_W
    qkv = lambda off: pl.BlockSpec((n, hd), lambda b, h: (row_block0 + b, off // hd + h))
    cache = pl.BlockSpec((None, None, None, past, hd), lambda b, h: (b, layer, h, 0, 0))
    return pl.pallas_call(
        functools.partial(_na_body, rows=rows),
        grid=(nreq, NA_HEADS),
        in_specs=[qkv(OFF_NQ), qkv(OFF_NK), qkv(OFF_NV), cache, cache,
                  pl.BlockSpec((None, NA_WIN_H, GRID_W, NA_WIN_H * GRID_W), lambda b, h: (h, 0, 0, 0))],
        out_specs=pl.BlockSpec((n, hd), lambda b, h: (b, h)),
        out_shape=jax.ShapeDtypeStruct((nreq * n, NA_WIDTH), bf16),
        scratch_shapes=[pltpu.VMEM((n, hd), bf16), pltpu.VMEM((n, hd), bf16)],
        compiler_params=_cparams(("parallel", "parallel")),
        name="na_latent",
    )(p, p, p, cache_k, cache_v, bias_tbl)


GLA_PAD = 32


def _rope_tables(seq):
    t = np.arange(seq)
    half = GLA_DK // 2
    nf = half // 2
    inv = ROPE_BASE ** (-np.arange(nf, dtype=np.float64) / nf)
    cos, sin = [], []
    for pos in (t // GRID_W, t % GRID_W):
        ang = pos[:, None].astype(np.float64) * inv
        cos += [np.cos(ang), np.cos(ang)]
        sin += [-np.sin(ang), np.sin(ang)]
    return (jnp.asarray(np.concatenate(cos, axis=-1), f32), jnp.asarray(np.concatenate(sin, axis=-1), f32))


def _rope(x, cos, sin_signed):
    nf = GLA_DK // 4
    lane = lax.broadcasted_iota(jnp.int32, x.shape, 1)
    partner = jnp.where(lane % (2 * nf) < nf, pltpu.roll(x, GLA_DK - nf, 1), pltpu.roll(x, nf, 1))
    return x * cos + partner * sin_signed


def _log_sigmoid(x):
    return jnp.minimum(x, 0.0) - jnp.log1p(jnp.exp(-jnp.abs(x)))


def _gla_body(*refs, seq, rope, with_s0, with_sfin):
    refs = list(refs)
    q_ref, k_ref, v_ref, r_ref, z_ref, wg_ref, bg_ref, ng_ref = refs[:8]
    refs = refs[8:]
    if rope:
        cos_ref, sin_ref = refs[:2]
        refs = refs[2:]
    if with_s0:
        s0_ref = refs[0]
        refs = refs[1:]
    o_ref = refs[0]
    refs = refs[1:]
    if with_sfin:
        sfin_ref = refs[0]
        refs = refs[1:]
    qi_scr, kn_scr, kd_scr, dec_scr, scan_scr, of_scr, ob_scr, st_scr = refs

    ch = GLA_CHUNK
    nch = seq // ch
    dk, dv = GLA_DK, GLA_DV

    q = q_ref[...]
    k = k_ref[...]
    if rope:
        q = _rope(q, cos_ref[...], sin_ref[...])
        k = _rope(k, cos_ref[...], sin_ref[...])
    q = q * (dk ** -0.5)

    zb = z_ref[...].astype(bf16)
    pos = lax.broadcasted_iota(jnp.int32, (seq, 1), 0) % ch
    zpad = jnp.zeros((GLA_PAD, dk), f32)
    scan_scr[pl.ds(0, GLA_PAD), :] = zpad
    scan_scr[pl.ds(GLA_PAD + seq, GLA_PAD), :] = zpad
    for d in range(2):
        g = _log_sigmoid(_dot(zb, wg_ref[d]) + bg_ref[d]) / GLA_TAU
        b = g
        sh = 1
        while sh < ch:
            scan_scr[pl.ds(GLA_PAD, seq), :] = b
            if d == 0:
                b = b + jnp.where(pos >= sh, scan_scr[pl.ds(GLA_PAD - sh, seq), :], 0.0)
            else:
                b = b + jnp.where(pos < ch - sh, scan_scr[pl.ds(GLA_PAD + sh, seq), :], 0.0)
            sh *= 2
        b3 = b.reshape(nch, ch, dk)
        b_end = b3[:, ch - 1:ch, :] if d == 0 else b3[:, 0:1, :]
        qi_scr[d] = (q * jnp.exp(b)).astype(bf16)
        kn_scr[d] = (k * jnp.exp(-b)).astype(bf16)
        kd_scr[d] = (k.reshape(nch, ch, dk) * jnp.exp(b_end - b3)).reshape(seq, dk).astype(bf16)
        dec_scr[d] = jnp.exp(b_end)

    for d in range(2):
        if with_s0:
            st_scr[d] = s0_ref[d].T
        else:
            st_scr[d] = jnp.zeros((dv, dk), f32)

    ri = lax.broadcasted_iota(jnp.int32, (ch, ch), 0)
    ci = lax.broadcasted_iota(jnp.int32, (ch, ch), 1)
    keep = (ci <= ri, ci >= ri)

    def chunk(i, carry):
        for d, o_scr in ((0, of_scr), (1, ob_scr)):
            c = i if d == 0 else nch - 1 - i
            r0 = pl.multiple_of(c * ch, ch)
            rows = pl.ds(r0, ch)
            qi = qi_scr[d, rows, :]
            vb = v_ref[rows, :].astype(bf16)
            st = st_scr[d]
            a = jnp.where(keep[d], _dot_nt(qi, kn_scr[d, rows, :]), 0.0)
            o_scr[rows, :] = _dot(a.astype(bf16), vb) + _dot_nt(qi, st.astype(bf16))
            upd = lax.dot_general(vb, kd_scr[d, rows, :], (((0,), (0,)), ((), ())), preferred_element_type=f32)
            st_scr[d] = st * dec_scr[d, c] + upd
        return carry

    lax.fori_loop(0, nch, chunk, 0)

    if with_sfin:
        for d in range(2):
            sfin_ref[d] = st_scr[d].T

    o = of_scr[...] + ob_scr[...]
    r = r_ref[...]
    o = o * lax.rsqrt(jnp.mean(o * o, axis=-1, keepdims=True) + EPS) * ng_ref[...]
    o_ref[...] = (o * (r * jax.nn.sigmoid(r))).astype(o_ref.dtype)


def _gla(p, wgate_p, b_gate, gla_norm, seq, row_block0, nreq, rope_tabs=None, state=None, layer=0,
         with_sfin=False):
    dk, dv = GLA_DK, GLA_DV
    nch = seq // GLA_CHUNK
    rope = rope_tabs is not None
    with_s0 = state is not None
    blk = lambda w, off: pl.BlockSpec((seq, w), lambda b, h: (row_block0 + b, off // w + h))
    in_specs = [blk(dk, OFF_GQ), blk(dk, OFF_GK), blk(dv, OFF_GV), blk(dv, OFF_GR),
                pl.BlockSpec((seq, LANE), lambda b, h: (row_block0 + b, OFF_GZ // LANE)),
                pl.BlockSpec((2, LANE, dk), lambda b, h: (0, 0, h)),
                pl.BlockSpec((2, 1, dk), lambda b, h: (0, 0, h)),
                pl.BlockSpec((1, dv), lambda b, h: (0, h))]
    args = [p, p, p, p, p, wgate_p, b_gate.reshape(2, 1, GLA_KW), gla_norm.reshape(1, GLA_VW)]
    if rope:
        in_specs += [pl.BlockSpec((seq, dk), lambda b, h: (0, 0))] * 2
        args += list(rope_tabs)
    if with_s0:
        in_specs.append(pl.BlockSpec((None, None, 2, None, dk, dv), lambda b, h: (b, layer, 0, h, 0, 0)))
        args.append(state)
    out_specs = [pl.BlockSpec((seq, dv), lambda b, h: (b, h))]
    out_shape = [jax.ShapeDtypeStruct((nreq * seq, GLA_VW), bf16)]
    if with_sfin:
        out_specs.append(pl.BlockSpec((None, 2, None, dk, dv), lambda b, h: (b, 0, h, 0, 0)))
        out_shape.append(jax.ShapeDtypeStruct((nreq, 2, GLA_HEADS, dk, dv), f32))
    scratch = [pltpu.VMEM((2, seq, dk), bf16), pltpu.VMEM((2, seq, dk), bf16), pltpu.VMEM((2, seq, dk), bf16),
               pltpu.VMEM((2, nch, 1, dk), f32), pltpu.VMEM((seq + 2 * GLA_PAD, dk), f32),
               pltpu.VMEM((seq, dv), f32), pltpu.VMEM((seq, dv), f32), pltpu.VMEM((2, dv, dk), f32)]
    return pl.pallas_call(
        functools.partial(_gla_body, seq=seq, rope=rope, with_s0=with_s0, with_sfin=with_sfin),
        grid=(nreq, GLA_HEADS),
        in_specs=in_specs,
        out_specs=out_specs,
        out_shape=out_shape,
        scratch_shapes=scratch,
        compiler_params=_cparams(("parallel", "parallel")),
        name="gla",
    )(*args)


TM_MERGE = 256


def _merge_body(x_ref, mod_ref, post_ref, bp_ref, bn_ref, bg_ref, gl_ref, w_ref, o_ref, m_scr, mb_scr, y_scr):
    n = pl.program_id(1)
    d = D_MODEL

    @pl.when(n == 0)
    def _():
        m_scr[...] = jnp.zeros_like(m_scr)

    for bi, br_ref in enumerate((bp_ref, bn_ref, bg_ref)):
        @pl.when(n == bi)
        def _():
            m_scr[...] += jax.nn.sigmoid(gl_ref[...]) * _dot(br_ref[...], w_ref[...])

    @pl.when(n == N_BRANCH)
    def _():
        mb_scr[...] = m_scr[...].astype(bf16)
        y_scr[...] = _dot(mb_scr[:, 0:BRANCH_W], w_ref[...])

    @pl.when(n == N_BRANCH + 1)
    def _():
        y = y_scr[...] + _dot(mb_scr[:, BRANCH_W:2 * BRANCH_W], w_ref[...])
        o_ref[...] = x_ref[...] + mod_ref[:, 2 * d:3 * d] * _rms(y, post_ref[...])


def _merge(x, mod_l, post, y_pool, y_na, y_gla, p, w_stack, n_ctx):
    t, d = x.shape
    tm = TM_MERGE
    nsteps = N_BRANCH + d // BRANCH_W
    br = pl.BlockSpec((tm, BRANCH_W), lambda i, n: (i, 0))
    return pl.pallas_call(
        _merge_body,
        grid=(t // tm, nsteps),
        in_specs=[
            pl.BlockSpec((tm, d), lambda i, n: (i, 0)),
            pl.BlockSpec((None, 1, 3 * d), lambda i, n: (_cond_index(i, tm, n_ctx), 0, 1)),
            pl.BlockSpec((1, d), lambda i, n: (0, 0)),
            br, br, br,
            pl.BlockSpec((tm, d), lambda i, n: (i, jnp.minimum(n, N_BRANCH - 1))),
            pl.BlockSpec((None, BRANCH_W, d), lambda i, n: (n, 0, 0)),
        ],
        out_specs=pl.BlockSpec((tm, d), lambda i, n: (i, 0)),
        out_shape=jax.ShapeDtypeStruct((t, d), f32),
        scratch_shapes=[pltpu.VMEM((tm, d), f32), pltpu.VMEM((tm, d), bf16), pltpu.VMEM((tm, d), f32)],
        compiler_params=_cparams(("parallel", "arbitrary")),
        name="merge",
    )(x, mod_l, post, y_pool, y_na, y_gla, p, w_stack)


def _prep_w_in(w):
    o = np.cumsum((0, POOL_WIDTH, NA_WIDTH, NA_WIDTH, NA_WIDTH, GLA_KW, GLA_KW, GLA_VW, 2 * GLA_RANK, GLA_VW,
                   GATE_W))
    pool_to_gv, gz, gr, gl = w[:, o[0]:o[7]], w[:, o[7]:o[8]], w[:, o[8]:o[9]], w[:, o[9]:o[10]]
    used = OFF_GZ + 2 * GLA_RANK
    pad = jnp.zeros((w.shape[0], IN_COLS_P - used), w.dtype)
    return jnp.concatenate([gl, pool_to_gv, gr, gz, pad], axis=1).astype(bf16)


def _prep_ffn(w_in, w_out):
    d = w_in.shape[0]
    w2 = w_in.reshape(d, 2, D_FF).transpose(1, 0, 2)
    w2 = jnp.pad(w2, ((0, 0), (0, 0), (0, D_FF_P - D_FF))).astype(bf16)
    wo = jnp.pad(w_out, ((0, D_FF_P - D_FF), (0, 0))).astype(bf16)
    return w2, wo


def _prep_gate(w_gate):
    out = jnp.zeros((2, LANE, GLA_KW), f32)
    for d in range(2):
        out = out.at[d, d * GLA_RANK:(d + 1) * GLA_RANK].set(w_gate[d])
    return out.astype(bf16)


def kernel(x_prompt, x_sample, c, cache_na_k, cache_na_v, state_gla, c_ctx, w_mod, b_mod, norm_pre, norm_post,
           w_ffn_in, w_ffn_out, w_in, pool_w, pool_scale, na_rpb, gla_w_gate, gla_b_gate, gla_norm, w_branch,
           w_out):
    nb, seq, d = x_prompt.shape
    ndec, dseq, _ = x_sample.shape
    depth = w_mod.shape[0]
    n_ctx = nb * seq
    n_lat = ndec * dseq
    assert (seq, dseq, d) == (SEQ, DEC_SEQ, D_MODEL) and n_ctx % dseq == 0 and n_ctx % TM == 0

    x = jnp.concatenate([x_prompt.reshape(n_ctx, d), x_sample.reshape(n_lat, d)], axis=0)
    ncond = -(-(1 + ndec) // 8) * 8
    c_all = jnp.concatenate([c_ctx[None], c, jnp.zeros((ncond - 1 - ndec, d), f32)], axis=0)
    mod = _modulation(c_all, w_mod, b_mod)
    rope_tabs = _rope_tables(dseq)
    lat_blk = n_ctx // dseq

    new_k, new_v, new_s = [], [], []
    for l in range(depth):
        mod_l = mod[l].reshape(ncond, 1, N_MOD * d)
        pre = norm_pre[l].reshape(3, 1, d)
        post = norm_post[l].reshape(3, 1, d)
        wa_in, wa_out = _prep_ffn(w_ffn_in[l, 0], w_ffn_out[l, 0])
        wb_in, wb_out = _prep_ffn(w_ffn_in[l, 1], w_ffn_out[l, 1])
        w_in_p = _prep_w_in(w_in[l])
        w_stack = jnp.concatenate([w_branch[l], w_out[l].reshape(d // BRANCH_W, BRANCH_W, d)], axis=0).astype(bf16)
        wgate_p = _prep_gate(gla_w_gate[l])
        pw = pool_w[l].astype(bf16)
        psc = pool_scale[l].reshape(1, POOL_WIDTH)

        x = _ffn(x, mod_l, pre[0], post[0], wa_in, wa_out, 0, n_ctx)
        p = _inproj(x, mod_l, pre[1], w_in_p, n_ctx)

        y_pool = jnp.concatenate([_pool(p, pw, psc, seq, 0, nb), _pool(p, pw, psc, dseq, lat_blk, ndec)], axis=0)
        y_na = jnp.concatenate([
            _ctx_attn(p, nb),
            _na_latent(p, cache_na_k, cache_na_v, _na_bias_table(na_rpb[l]), l, lat_blk, ndec)], axis=0)
        g_ctx, s_fin = _gla(p, wgate_p, gla_b_gate[l], gla_norm[l], seq, 0, nb, with_sfin=True)
        (g_lat,) = _gla(p, wgate_p, gla_b_gate[l], gla_norm[l], dseq, lat_blk, ndec, rope_tabs=rope_tabs,
                        state=state_gla, layer=l)
        y_gla = jnp.concatenate([g_ctx, g_lat], axis=0)

        x = _merge(x, mod_l, post[1], y_pool, y_na, y_gla, p, w_stack, n_ctx)
        x = _ffn(x, mod_l, pre[2], post[2], wb_in, wb_out, 2, n_ctx)

        heads = lambda a: a.reshape(nb, seq, NA_HEADS, NA_HEAD_DIM).transpose(0, 2, 1, 3)
        new_k.append(heads(p[:n_ctx, OFF_NK:OFF_NK + NA_WIDTH]))
        new_v.append(heads(p[:n_ctx, OFF_NV:OFF_NV + NA_WIDTH]))
        new_s.append(s_fin)

    y_prompt = x[:n_ctx].reshape(nb, seq, d)
    y_sample = x[n_ctx:].reshape(ndec, dseq, d)
    return (y_prompt, y_sample, jnp.stack(new_k, axis=1), jnp.stack(new_v, axis=1), jnp.stack(new_s, axis=1))
```

```python
import functools

import numpy as np
import jax
import jax.numpy as jnp
from jax import lax
from jax.experimental import pallas as pl
from jax.experimental.pallas import tpu as pltpu

f32 = jnp.float32
bf16 = jnp.bfloat16

D_MODEL = 2048
SEQ = 256
DEC_SEQ = 2048
GRID_W = 64
N_MOD = 9
D_FF = 5504
FFN_RES = 0.5
EPS = 1e-6
NEG_INF = -1e30

POOL_GROUPS = 4
POOL_WINDOWS = (2, 4, 8, 16)
POOL_WIDTH = 1024
POOL_GC = POOL_WIDTH // POOL_GROUPS

NA_HEADS = 8
NA_HEAD_DIM = 128
NA_WIDTH = NA_HEADS * NA_HEAD_DIM
NA_WIN_H = 8
NA_WIN_W = 16

GLA_HEADS = 4
GLA_DK = 128
GLA_DV = 256
GLA_KW = GLA_HEADS * GLA_DK
GLA_VW = GLA_HEADS * GLA_DV
GLA_RANK = 16
GLA_TAU = 16.0
GLA_CHUNK = 64
ROPE_BASE = 10000.0

BRANCH_W = 1024
N_BRANCH = 3
GATE_W = N_BRANCH * D_MODEL

LANE = 128
VMEM_LIMIT = 56 * 1024 * 1024

OFF_GL = 0
OFF_POOL = OFF_GL + GATE_W
OFF_NQ = OFF_POOL + POOL_WIDTH
OFF_NK = OFF_NQ + NA_WIDTH
OFF_NV = OFF_NK + NA_WIDTH
OFF_GQ = OFF_NV + NA_WIDTH
OFF_GK = OFF_GQ + GLA_KW
OFF_GV = OFF_GK + GLA_KW
OFF_GR = OFF_GV + GLA_VW
OFF_GZ = OFF_GR + GLA_VW
TN_IN = 1536
IN_COLS_P = -(-(OFF_GZ + LANE) // TN_IN) * TN_IN

TM = 512
TF = 512
D_FF_P = -(-D_FF // TF) * TF


def _cparams(sem):
    return pltpu.CompilerParams(dimension_semantics=sem, vmem_limit_bytes=VMEM_LIMIT)


def _cond_index(i, tm, n_ctx):
    row = i * tm
    return jnp.where(row < n_ctx, 0, 1 + (row - n_ctx) // DEC_SEQ)


def _rms(x, g):
    return x * lax.rsqrt(jnp.mean(x * x, axis=-1, keepdims=True) + EPS) * g


def _dot(a, b):
    return jnp.dot(a, b, preferred_element_type=f32)


def _dot_nt(a, b):
    return lax.dot_general(a, b, (((1,), (1,)), ((), ())), preferred_element_type=f32)


def _mod_body(c_ref, w_ref, b_ref, o_ref):
    c = c_ref[...]
    s = c * jax.nn.sigmoid(c)
    o_ref[...] = _dot(s.astype(bf16), w_ref[...].astype(bf16)) + b_ref[...]


def _modulation(c_all, w_mod, b_mod):
    depth, d, n = w_mod.shape
    nc = c_all.shape[0]
    tn = 1024
    return pl.pallas_call(
        _mod_body,
        grid=(depth, n // tn),
        in_specs=[
            pl.BlockSpec((nc, d), lambda l, j: (0, 0)),
            pl.BlockSpec((None, d, tn), lambda l, j: (l, 0, j)),
            pl.BlockSpec((None, 1, tn), lambda l, j: (l, 0, j)),
        ],
        out_specs=pl.BlockSpec((None, nc, tn), lambda l, j: (l, 0, j)),
        out_shape=jax.ShapeDtypeStruct((depth, nc, n), f32),
        compiler_params=_cparams(("parallel", "parallel")),
        name="modulation",
    )(c_all, w_mod, b_mod.reshape(depth, 1, n))


def _ffn_body(x_ref, mod_ref, pre_ref, post_ref, wg_ref, wu_ref, wo_ref, o_ref, h_scr, acc_scr):
    f = pl.program_id(1)
    d = D_MODEL

    @pl.when(f == 0)
    def _():
        y = _rms(x_ref[...], pre_ref[...])
        h = y * (1.0 + mod_ref[:, d:2 * d]) + mod_ref[:, 0:d]
        h_scr[...] = h.astype(bf16)
        acc_scr[...] = jnp.zeros_like(acc_scr)

    h = h_scr[...]
    gt = _dot(h, wg_ref[...])
    up = _dot(h, wu_ref[...])
    a = gt * jax.nn.sigmoid(gt) * up
    acc_scr[...] += _dot(a.astype(bf16), wo_ref[...])

    @pl.when(f == pl.num_programs(1) - 1)
    def _():
        yn = _rms(acc_scr[...], post_ref[...])
        o_ref[...] = x_ref[...] + FFN_RES * mod_ref[:, 2 * d:3 * d] * yn


def _ffn(x, mod_l, pre, post, w_in5, w_out4, layer, slot, sub, n_ctx):
    t, d = x.shape
    fp = w_out4.shape[2]
    w_half = lambda half: pl.BlockSpec((None, None, None, d, TF), lambda i, f: (layer, slot, half, 0, f))
    return pl.pallas_call(
        _ffn_body,
        grid=(t // TM, fp // TF),
        in_specs=[
            pl.BlockSpec((TM, d), lambda i, f: (i, 0)),
            pl.BlockSpec((None, 1, 3 * d), lambda i, f: (_cond_index(i, TM, n_ctx), 0, sub)),
            pl.BlockSpec((1, d), lambda i, f: (0, 0)),
            pl.BlockSpec((1, d), lambda i, f: (0, 0)),
            w_half(0),
            w_half(1),
            pl.BlockSpec((None, None, TF, d), lambda i, f: (layer, slot, f, 0)),
        ],
        out_specs=pl.BlockSpec((TM, d), lambda i, f: (i, 0)),
        out_shape=jax.ShapeDtypeStruct((t, d), f32),
        scratch_shapes=[pltpu.VMEM((TM, d), bf16), pltpu.VMEM((TM, d), f32)],
        compiler_params=_cparams(("parallel", "arbitrary")),
        name="ffn",
    )(x, mod_l, pre, post, w_in5, w_in5, w_out4)


def _inproj_body(x_ref, mod_ref, pre_ref, w_ref, o_ref, h_scr):
    d = D_MODEL

    @pl.when(pl.program_id(1) == 0)
    def _():
        y = _rms(x_ref[...], pre_ref[...])
        h = y * (1.0 + mod_ref[:, d:2 * d]) + mod_ref[:, 0:d]
        h_scr[...] = h.astype(bf16)

    o_ref[...] = _dot(h_scr[...], w_ref[...])


def _inproj(x, mod_l, pre, w_in_p, layer, n_ctx):
    t, d = x.shape
    n = w_in_p.shape[2]
    return pl.pallas_call(
        _inproj_body,
        grid=(t // TM, n // TN_IN),
        in_specs=[
            pl.BlockSpec((TM, d), lambda i, j: (i, 0)),
            pl.BlockSpec((None, 1, 3 * d), lambda i, j: (_cond_index(i, TM, n_ctx), 0, 1)),
            pl.BlockSpec((1, d), lambda i, j: (0, 0)),
            pl.BlockSpec((None, d, TN_IN), lambda i, j: (layer, 0, j)),
        ],
        out_specs=pl.BlockSpec((TM, TN_IN), lambda i, j: (i, j)),
        out_shape=jax.ShapeDtypeStruct((t, n), f32),
        scratch_shapes=[pltpu.VMEM((TM, d), bf16)],
        compiler_params=_cparams(("parallel", "arbitrary")),
        name="inproj",
    )(x, mod_l, pre, w_in_p)


POOL_PAD = 8


def _pool_body(u_ref, w_ref, sc_ref, o_ref, pad_scr, *, seq):
    gc = POOL_GC
    zeros = jnp.zeros((POOL_PAD, POOL_WIDTH), f32)
    pad_scr[pl.ds(0, POOL_PAD), :] = zeros
    pad_scr[pl.ds(POOL_PAD + seq, POOL_PAD), :] = zeros
    pad_scr[pl.ds(POOL_PAD, seq), :] = u_ref[...]
    t = lax.broadcasted_iota(jnp.int32, (seq, 1), 0)
    for gi, win in enumerate(POOL_WINDOWS):
        cols = pl.ds(gi * gc, gc)
        lo = jnp.maximum(t - win // 2, 0)
        hi = jnp.minimum(t + win - 1 - win // 2, seq - 1)
        cnt = (hi - lo + 1).astype(f32)
        acc = pad_scr[pl.ds(POOL_PAD - win // 2, seq), cols]
        for j in range(1, win):
            acc = acc + pad_scr[pl.ds(POOL_PAD - win // 2 + j, seq), cols]
        pooled = acc / cnt - u_ref[:, cols]
        y = _dot(pooled.astype(bf16), w_ref[gi])
        o_ref[:, cols] = (y * sc_ref[:, cols]).astype(o_ref.dtype)


def _pool(p, pool_w, pool_scale, seq, row_block0, nseq):
    cb = OFF_POOL // POOL_WIDTH
    return pl.pallas_call(
        functools.partial(_pool_body, seq=seq),
        grid=(nseq,),
        in_specs=[
            pl.BlockSpec((seq, POOL_WIDTH), lambda s: (row_block0 + s, cb)),
            pl.BlockSpec((POOL_GROUPS, POOL_GC, POOL_GC), lambda s: (0, 0, 0)),
            pl.BlockSpec((1, POOL_WIDTH), lambda s: (0, 0)),
        ],
        out_specs=pl.BlockSpec((seq, POOL_WIDTH), lambda s: (s, 0)),
        out_shape=jax.ShapeDtypeStruct((nseq * seq, POOL_WIDTH), bf16),
        scratch_shapes=[pltpu.VMEM((seq + 2 * POOL_PAD, POOL_WIDTH), f32)],
        compiler_params=_cparams(("parallel",)),
        name="pool",
    )(p, pool_w, pool_scale)


def _softmax_rows(s):
    m = jnp.max(s, axis=-1, keepdims=True)
    e = jnp.exp(s - m)
    return e / jnp.sum(e, axis=-1, keepdims=True)


def _ctx_attn_body(q_ref, k_ref, v_ref, o_ref):
    hd = NA_HEAD_DIM
    for h in range(NA_HEADS):
        cols = pl.ds(h * hd, hd)
        q = q_ref[:, cols].astype(bf16)
        k = k_ref[:, cols].astype(bf16)
        v = v_ref[:, cols].astype(bf16)
        p = _softmax_rows(_dot_nt(q, k) * (hd ** -0.5))
        o_ref[:, cols] = _dot(p.astype(bf16), v).astype(o_ref.dtype)


def _ctx_attn(p, nseq):
    spec = lambda off: pl.BlockSpec((SEQ, NA_WIDTH), lambda b: (b, off // NA_WIDTH))
    return pl.pallas_call(
        _ctx_attn_body,
        grid=(nseq,),
        in_specs=[spec(OFF_NQ), spec(OFF_NK), spec(OFF_NV)],
        out_specs=pl.BlockSpec((SEQ, NA_WIDTH), lambda b: (b, 0)),
        out_shape=jax.ShapeDtypeStruct((nseq * SEQ, NA_WIDTH), bf16),
        compiler_params=_cparams(("parallel",)),
        name="ctx_attn",
    )(p, p, p)


def _na_bias_table(rpb):
    qc = np.arange(GRID_W)[:, None]
    kc = np.arange(GRID_W)[None, :]
    cs = np.clip(qc - NA_WIN_W // 2, 0, GRID_W - NA_WIN_W)
    ok = (kc >= cs) & (kc < cs + NA_WIN_W)
    cidx = np.clip(kc - qc + NA_WIN_W - 1, 0, 2 * NA_WIN_W - 2)
    ridx = np.arange(NA_WIN_H)[:, None] + np.arange(NA_WIN_H)[None, :]
    tbl = rpb[:, ridx][:, :, :, cidx]
    tbl = jnp.where(ok[None, None, None], tbl, NEG_INF)
    tbl = tbl.transpose(0, 1, 3, 2, 4)
    return tbl.reshape(rpb.shape[0], NA_WIN_H, GRID_W, NA_WIN_H * GRID_W).astype(f32)


NA_UNROLL = 8
NA_CTX_ROWS = 256


def _na_body(q_ref, k_ref, v_ref, ck_ref, cv_ref, bias_ref, o_ref,
             qb_scr, kb_scr, vb_scr, sl_scr, sc_scr, el_scr, ec_scr, den_scr, oc_scr, *, rows):
    hd = NA_HEAD_DIM
    scale = hd ** -0.5
    kh = min(NA_WIN_H, rows)
    nloc = kh * GRID_W
    n = rows * GRID_W
    qb_scr[...] = q_ref[...].astype(bf16)
    kb_scr[...] = k_ref[...].astype(bf16)
    vb_scr[...] = v_ref[...].astype(bf16)
    ck = ck_ref[...].astype(bf16)
    cv = cv_ref[...].astype(bf16)

    def row_slices(r):
        rs = jnp.clip(r - kh // 2, 0, rows - kh)
        q_rows = pl.ds(pl.multiple_of(r * GRID_W, GRID_W), GRID_W)
        k_rows = pl.ds(pl.multiple_of(rs * GRID_W, GRID_W), nloc)
        return rs, q_rows, k_rows

    def ctx_scores(i, carry):
        blk = pl.ds(pl.multiple_of(i * NA_CTX_ROWS, NA_CTX_ROWS), NA_CTX_ROWS)
        sc_scr[blk, :] = _dot_nt(qb_scr[blk, :], ck) * scale
        return carry

    lax.fori_loop(0, n // NA_CTX_ROWS, ctx_scores, 0, unroll=2)

    def loc_scores(r, carry):
        rs, q_rows, k_rows = row_slices(r)
        sl_scr[q_rows, :] = (_dot_nt(qb_scr[q_rows, :], kb_scr[k_rows, :]) * scale
                             + bias_ref[rs - r + NA_WIN_H - 1])
        return carry

    lax.fori_loop(0, rows, loc_scores, 0, unroll=NA_UNROLL)

    def numerators(r, carry):
        q_rows = pl.ds(pl.multiple_of(r * GRID_W, GRID_W), GRID_W)
        s_loc = sl_scr[q_rows, :]
        s_ctx = sc_scr[q_rows, :]
        m = jnp.maximum(jnp.max(s_loc, axis=-1, keepdims=True), jnp.max(s_ctx, axis=-1, keepdims=True))
        e_loc = jnp.exp(s_loc - m)
        e_ctx = jnp.exp(s_ctx - m)
        den = jnp.sum(e_loc, axis=-1, keepdims=True) + jnp.sum(e_ctx, axis=-1, keepdims=True)
        el_scr[q_rows, :] = e_loc.astype(bf16)
        ec_scr[q_rows, :] = e_ctx.astype(bf16)
        den_scr[q_rows, :] = jnp.broadcast_to(den, (GRID_W, hd))
        return carry

    lax.fori_loop(0, rows, numerators, 0, unroll=NA_UNROLL)

    def ctx_values(i, carry):
        blk = pl.ds(pl.multiple_of(i * NA_CTX_ROWS, NA_CTX_ROWS), NA_CTX_ROWS)
        oc_scr[blk, :] = _dot(ec_scr[blk, :], cv)
        return carry

    lax.fori_loop(0, n // NA_CTX_ROWS, ctx_values, 0, unroll=2)

    def loc_values(r, carry):
        _, q_rows, k_rows = row_slices(r)
        o = _dot(el_scr[q_rows, :], vb_scr[k_rows, :]) + oc_scr[q_rows, :]
        o_ref[q_rows, :] = (o / den_scr[q_rows, :]).astype(o_ref.dtype)
        return carry

    lax.fori_loop(0, rows, loc_values, 0, unroll=NA_UNROLL)


def _na_latent(p, cache_k, cache_v, bias_tbl, layer, row_block0, nreq):
    n = DEC_SEQ
    hd = NA_HEAD_DIM
    past = cache_k.shape[3]
    rows = n // GRID_W
    qkv = lambda off: pl.BlockSpec((n, hd), lambda b, h: (row_block0 + b, off // hd + h))
    cache = pl.BlockSpec((None, None, None, past, hd), lambda b, h: (b, layer, h, 0, 0))
    return pl.pallas_call(
        functools.partial(_na_body, rows=rows),
        grid=(nreq, NA_HEADS),
        in_specs=[qkv(OFF_NQ), qkv(OFF_NK), qkv(OFF_NV), cache, cache,
                  pl.BlockSpec((None, NA_WIN_H, GRID_W, NA_WIN_H * GRID_W), lambda b, h: (h, 0, 0, 0))],
        out_specs=pl.BlockSpec((n, hd), lambda b, h: (b, h)),
        out_shape=jax.ShapeDtypeStruct((nreq * n, NA_WIDTH), bf16),
        scratch_shapes=[pltpu.VMEM((n, hd), bf16), pltpu.VMEM((n, hd), bf16), pltpu.VMEM((n, hd), bf16),
                        pltpu.VMEM((n, NA_WIN_H * GRID_W), f32), pltpu.VMEM((n, past), f32),
                        pltpu.VMEM((n, NA_WIN_H * GRID_W), bf16), pltpu.VMEM((n, past), bf16),
                        pltpu.VMEM((n, hd), f32), pltpu.VMEM((n, hd), f32)],
        compiler_params=_cparams(("parallel", "parallel")),
        name="na_latent",
    )(p, p, p, cache_k, cache_v, bias_tbl)


GLA_PAD = 32
GLA_UNROLL = 4


def _rope_tables(seq):
    t = np.arange(seq)
    half = GLA_DK // 2
    nf = half // 2
    inv = ROPE_BASE ** (-np.arange(nf, dtype=np.float64) / nf)
    cos, sin = [], []
    for pos in (t // GRID_W, t % GRID_W):
        ang = pos[:, None].astype(np.float64) * inv
        cos += [np.cos(ang), np.cos(ang)]
        sin += [-np.sin(ang), np.sin(ang)]
    return (jnp.asarray(np.concatenate(cos, axis=-1), f32), jnp.asarray(np.concatenate(sin, axis=-1), f32))


def _rope(x, cos, sin_signed):
    nf = GLA_DK // 4
    lane = lax.broadcasted_iota(jnp.int32, x.shape, 1)
    partner = jnp.where(lane % (2 * nf) < nf, pltpu.roll(x, GLA_DK - nf, 1), pltpu.roll(x, nf, 1))
    return x * cos + partner * sin_signed


def _log_sigmoid(x):
    return jnp.minimum(x, 0.0) - jnp.log1p(jnp.exp(-jnp.abs(x)))


def _gla_body(*refs, seq, rope, with_s0, with_sfin):
    refs = list(refs)
    q_ref, k_ref, v_ref, r_ref, z_ref, wg_ref, bg_ref, ng_ref = refs[:8]
    refs = refs[8:]
    if rope:
        cos_ref, sin_ref = refs[:2]
        refs = refs[2:]
    if with_s0:
        s0_ref = refs[0]
        refs = refs[1:]
    o_ref = refs[0]
    refs = refs[1:]
    if with_sfin:
        sfin_ref = refs[0]
        refs = refs[1:]
    qi_scr, kn_scr, kd_scr, dec_scr, scan_scr, vb_scr, u_scr, sb_scr, o_scr, st_scr = refs

    ch = GLA_CHUNK
    nch = seq // ch
    dk, dv = GLA_DK, GLA_DV

    q = q_ref[...]
    k = k_ref[...]
    if rope:
        q = _rope(q, cos_ref[...], sin_ref[...])
        k = _rope(k, cos_ref[...], sin_ref[...])
    q = q * (dk ** -0.5)

    zb = z_ref[...].astype(bf16)
    pos = lax.broadcasted_iota(jnp.int32, (seq, 1), 0) % ch
    zpad = jnp.zeros((GLA_PAD, dk), f32)
    scan_scr[pl.ds(0, GLA_PAD), :] = zpad
    scan_scr[pl.ds(GLA_PAD + seq, GLA_PAD), :] = zpad
    for d in range(2):
        g = _log_sigmoid(_dot(zb, wg_ref[d]) + bg_ref[d]) / GLA_TAU
        b = g
        sh = 1
        while sh < ch:
            scan_scr[pl.ds(GLA_PAD, seq), :] = b
            if d == 0:
                b = b + jnp.where(pos >= sh, scan_scr[pl.ds(GLA_PAD - sh, seq), :], 0.0)
            else:
                b = b + jnp.where(pos < ch - sh, scan_scr[pl.ds(GLA_PAD + sh, seq), :], 0.0)
            sh *= 2
        b3 = b.reshape(nch, ch, dk)
        b_end = b3[:, ch - 1:ch, :] if d == 0 else b3[:, 0:1, :]
        qi_scr[d] = (q * jnp.exp(b)).astype(bf16)
        kn_scr[d] = (k * jnp.exp(-b)).astype(bf16)
        kd_scr[d] = (k.reshape(nch, ch, dk) * jnp.exp(b_end - b3)).reshape(seq, dk).astype(bf16)
        dec_scr[d] = jnp.exp(b_end)

    for d in range(2):
        if with_s0:
            st_scr[d] = s0_ref[d].T
        else:
            st_scr[d] = jnp.zeros((dv, dk), f32)

    ri = lax.broadcasted_iota(jnp.int32, (ch, ch), 0)
    ci = lax.broadcasted_iota(jnp.int32, (ch, ch), 1)
    keep = (ci <= ri, ci >= ri)

    vb_scr[...] = v_ref[...].astype(bf16)
    chunk_rows = lambda c: pl.ds(pl.multiple_of(c * ch, ch), ch)

    def increments(c, carry):
        rows = chunk_rows(c)
        for d in range(2):
            u_scr[d, c] = lax.dot_general(vb_scr[rows, :], kd_scr[d, rows, :], (((0,), (0,)), ((), ())),
                                          preferred_element_type=f32)
        return carry

    lax.fori_loop(0, nch, increments, 0, unroll=GLA_UNROLL)

    def states(i, carry):
        for d in range(2):
            c = i if d == 0 else nch - 1 - i
            st = st_scr[d]
            sb_scr[d, c] = st.astype(bf16)
            st_scr[d] = st * dec_scr[d, c] + u_scr[d, c]
        return carry

    lax.fori_loop(0, nch, states, 0)

    if with_sfin:
        for d in range(2):
            sfin_ref[d] = st_scr[d].T

    def outputs(c, carry):
        rows = chunk_rows(c)
        vb = vb_scr[rows, :]
        o = None
        for d in range(2):
            qi = qi_scr[d, rows, :]
            a = jnp.where(keep[d], _dot_nt(qi, kn_scr[d, rows, :]), 0.0)
            od = _dot(a.astype(bf16), vb) + _dot_nt(qi, sb_scr[d, c])
            o = od if o is None else o + od
        o_scr[rows, :] = o
        return carry

    lax.fori_loop(0, nch, outputs, 0, unroll=GLA_UNROLL)

    o = o_scr[...]
    r = r_ref[...]
    o = o * lax.rsqrt(jnp.mean(o * o, axis=-1, keepdims=True) + EPS) * ng_ref[...]
    o_ref[...] = (o * (r * jax.nn.sigmoid(r))).astype(o_ref.dtype)


def _gla(p, wgate_p, b_gate, gla_norm, seq, row_block0, nreq, rope_tabs=None, state=None, layer=0,
         with_sfin=False):
    dk, dv = GLA_DK, GLA_DV
    nch = seq // GLA_CHUNK
    rope = rope_tabs is not None
    with_s0 = state is not None
    blk = lambda w, off: pl.BlockSpec((seq, w), lambda b, h: (row_block0 + b, off // w + h))
    in_specs = [blk(dk, OFF_GQ), blk(dk, OFF_GK), blk(dv, OFF_GV), blk(dv, OFF_GR),
                pl.BlockSpec((seq, LANE), lambda b, h: (row_block0 + b, OFF_GZ // LANE)),
                pl.BlockSpec((2, LANE, dk), lambda b, h: (0, 0, h)),
                pl.BlockSpec((2, 1, dk), lambda b, h: (0, 0, h)),
                pl.BlockSpec((1, dv), lambda b, h: (0, h))]
    args = [p, p, p, p, p, wgate_p, b_gate.reshape(2, 1, GLA_KW), gla_norm.reshape(1, GLA_VW)]
    if rope:
        in_specs += [pl.BlockSpec((seq, dk), lambda b, h: (0, 0))] * 2
        args += list(rope_tabs)
    if with_s0:
        in_specs.append(pl.BlockSpec((None, None, 2, None, dk, dv), lambda b, h: (b, layer, 0, h, 0, 0)))
        args.append(state)
    out_specs = [pl.BlockSpec((seq, dv), lambda b, h: (b, h))]
    out_shape = [jax.ShapeDtypeStruct((nreq * seq, GLA_VW), bf16)]
    if with_sfin:
        out_specs.append(pl.BlockSpec((None, 2, None, dk, dv), lambda b, h: (b, 0, h, 0, 0)))
        out_shape.append(jax.ShapeDtypeStruct((nreq, 2, GLA_HEADS, dk, dv), f32))
    scratch = [pltpu.VMEM((2, seq, dk), bf16), pltpu.VMEM((2, seq, dk), bf16), pltpu.VMEM((2, seq, dk), bf16),
               pltpu.VMEM((2, nch, 1, dk), f32), pltpu.VMEM((seq + 2 * GLA_PAD, dk), f32),
               pltpu.VMEM((seq, dv), bf16), pltpu.VMEM((2, nch, dv, dk), f32), pltpu.VMEM((2, nch, dv, dk), bf16),
               pltpu.VMEM((seq, dv), f32), pltpu.VMEM((2, dv, dk), f32)]
    return pl.pallas_call(
        functools.partial(_gla_body, seq=seq, rope=rope, with_s0=with_s0, with_sfin=with_sfin),
        grid=(nreq, GLA_HEADS),
        in_specs=in_specs,
        out_specs=out_specs,
        out_shape=out_shape,
        scratch_shapes=scratch,
        compiler_params=_cparams(("parallel", "parallel")),
        name="gla",
    )(*args)


TM_MERGE = 512


def _merge_body(x_ref, mod_ref, post_ref, bp_ref, bn_ref, bg_ref, gl_ref, w_ref, o_ref, m_scr, mb_scr):
    n = pl.program_id(1)
    d = D_MODEL

    @pl.when(n == 0)
    def _():
        m_scr[...] = jnp.zeros_like(m_scr)

    for bi, br_ref in enumerate((bp_ref, bn_ref, bg_ref)):
        @pl.when(n == bi)
        def _():
            m_scr[...] += jax.nn.sigmoid(gl_ref[...]) * _dot(br_ref[...], w_ref[...])

    @pl.when(n == N_BRANCH)
    def _():
        mb_scr[...] = m_scr[...].astype(bf16)
        m_scr[...] = _dot(mb_scr[:, 0:BRANCH_W], w_ref[...])

    @pl.when(n == N_BRANCH + 1)
    def _():
        y = m_scr[...] + _dot(mb_scr[:, BRANCH_W:2 * BRANCH_W], w_ref[...])
        o_ref[...] = x_ref[...] + mod_ref[:, 2 * d:3 * d] * _rms(y, post_ref[...])


def _merge(x, mod_l, post, y_pool, y_na, y_gla, p, w_stack, n_ctx):
    t, d = x.shape
    tm = TM_MERGE
    nsteps = N_BRANCH + d // BRANCH_W
    br = pl.BlockSpec((tm, BRANCH_W), lambda i, n: (i, 0))
    return pl.pallas_call(
        _merge_body,
        grid=(t // tm, nsteps),
        in_specs=[
            pl.BlockSpec((tm, d), lambda i, n: (i, 0)),
            pl.BlockSpec((None, 1, 3 * d), lambda i, n: (_cond_index(i, tm, n_ctx), 0, 1)),
            pl.BlockSpec((1, d), lambda i, n: (0, 0)),
            br, br, br,
            pl.BlockSpec((tm, d), lambda i, n: (i, jnp.minimum(n, N_BRANCH - 1))),
            pl.BlockSpec((None, BRANCH_W, d), lambda i, n: (n, 0, 0)),
        ],
        out_specs=pl.BlockSpec((tm, d), lambda i, n: (i, 0)),
        out_shape=jax.ShapeDtypeStruct((t, d), f32),
        scratch_shapes=[pltpu.VMEM((tm, d), f32), pltpu.VMEM((tm, d), bf16)],
        compiler_params=_cparams(("parallel", "arbitrary")),
        name="merge",
    )(x, mod_l, post, y_pool, y_na, y_gla, p, w_stack)


_IN_SPLITS = (POOL_WIDTH, NA_WIDTH, NA_WIDTH, NA_WIDTH, GLA_KW, GLA_KW, GLA_VW, 2 * GLA_RANK, GLA_VW, GATE_W)
_IN_OFFS = tuple(int(v) for v in np.cumsum((0,) + _IN_SPLITS))
_IN_RUNS = ((_IN_OFFS[9], _IN_OFFS[10]), (_IN_OFFS[0], _IN_OFFS[7]), (_IN_OFFS[8], _IN_OFFS[9]),
            (_IN_OFFS[7], _IN_OFFS[8]))


def _cast_w_in_body(w_ref, o_ref):
    off = 0
    for a, b in _IN_RUNS:
        o_ref[:, off:off + b - a] = w_ref[:, a:b].astype(bf16)
        off += b - a
    o_ref[:, off:] = jnp.zeros((o_ref.shape[0], o_ref.shape[1] - off), bf16)


def _cast_w_in(w):
    depth, d, n = w.shape
    rows = 128
    return pl.pallas_call(
        _cast_w_in_body,
        grid=(depth, d // rows),
        in_specs=[pl.BlockSpec((None, rows, n), lambda l, r: (l, r, 0))],
        out_specs=pl.BlockSpec((None, rows, IN_COLS_P), lambda l, r: (l, r, 0)),
        out_shape=jax.ShapeDtypeStruct((depth, d, IN_COLS_P), bf16),
        compiler_params=_cparams(("parallel", "parallel")),
        name="cast_w_in",
    )(w)


def _cast_ffn_in_body(w_ref, o_ref):
    pad = jnp.zeros((w_ref.shape[0], D_FF_P - D_FF), bf16)
    for half in range(2):
        o_ref[half, :, 0:D_FF] = w_ref[:, half * D_FF:(half + 1) * D_FF].astype(bf16)
        o_ref[half, :, D_FF:D_FF_P] = pad


def _cast_ffn_in(w):
    depth, ns, d, _ = w.shape
    rows = 256
    return pl.pallas_call(
        _cast_ffn_in_body,
        grid=(depth, ns, d // rows),
        in_specs=[pl.BlockSpec((None, None, rows, 2 * D_FF), lambda l, s, r: (l, s, r, 0))],
        out_specs=pl.BlockSpec((None, None, 2, rows, D_FF_P), lambda l, s, r: (l, s, 0, r, 0)),
        out_shape=jax.ShapeDtypeStruct((depth, ns, 2, d, D_FF_P), bf16),
        compiler_params=_cparams(("parallel", "parallel", "parallel")),
        name="cast_ffn_in",
    )(w)


def _cast_ffn_out_body(w_ref, o_ref):
    row = pl.program_id(2) * TF + lax.broadcasted_iota(jnp.int32, (TF, 1), 0)
    o_ref[...] = jnp.where(row < D_FF, w_ref[...], 0.0).astype(bf16)


def _cast_ffn_out(w):
    depth, ns, _, d = w.shape
    return pl.pallas_call(
        _cast_ffn_out_body,
        grid=(depth, ns, D_FF_P // TF),
        in_specs=[pl.BlockSpec((None, None, TF, d), lambda l, s, j: (l, s, j, 0))],
        out_specs=pl.BlockSpec((None, None, TF, d), lambda l, s, j: (l, s, j, 0)),
        out_shape=jax.ShapeDtypeStruct((depth, ns, D_FF_P, d), bf16),
        compiler_params=_cparams(("parallel", "parallel", "parallel")),
        name="cast_ffn_out",
    )(w)


def _prep_gate(w_gate):
    out = jnp.zeros((2, LANE, GLA_KW), f32)
    for d in range(2):
        out = out.at[d, d * GLA_RANK:(d + 1) * GLA_RANK].set(w_gate[d])
    return out.astype(bf16)


def kernel(x_prompt, x_sample, c, cache_na_k, cache_na_v, state_gla, c_ctx, w_mod, b_mod, norm_pre, norm_post,
           w_ffn_in, w_ffn_out, w_in, pool_w, pool_scale, na_rpb, gla_w_gate, gla_b_gate, gla_norm, w_branch,
           w_out):
    nb, seq, d = x_prompt.shape
    ndec, dseq, _ = x_sample.shape
    depth = w_mod.shape[0]
    n_ctx = nb * seq
    n_lat = ndec * dseq
    assert (seq, dseq, d) == (SEQ, DEC_SEQ, D_MODEL) and n_ctx % dseq == 0 and n_ctx % TM == 0

    x = jnp.concatenate([x_prompt.reshape(n_ctx, d), x_sample.reshape(n_lat, d)], axis=0)
    ncond = -(-(1 + ndec) // 8) * 8
    c_all = jnp.concatenate([c_ctx[None], c, jnp.zeros((ncond - 1 - ndec, d), f32)], axis=0)
    mod = _modulation(c_all, w_mod, b_mod)
    rope_tabs = _rope_tables(dseq)
    lat_blk = n_ctx // dseq

    w_ffn_in5 = _cast_ffn_in(w_ffn_in)
    w_ffn_out4 = _cast_ffn_out(w_ffn_out)
    w_in_p = _cast_w_in(w_in)
    w_stack_all = jnp.concatenate(
        [w_branch.astype(bf16), w_out.reshape(depth, d // BRANCH_W, BRANCH_W, d).astype(bf16)], axis=1)

    new_k, new_v, new_s = [], [], []
    for l in range(depth):
        mod_l = mod[l].reshape(ncond, 1, N_MOD * d)
        pre = norm_pre[l].reshape(3, 1, d)
        post = norm_post[l].reshape(3, 1, d)
        w_stack = w_stack_all[l]
        wgate_p = _prep_gate(gla_w_gate[l])
        pw = pool_w[l].astype(bf16)
        psc = pool_scale[l].reshape(1, POOL_WIDTH)

        x = _ffn(x, mod_l, pre[0], post[0], w_ffn_in5, w_ffn_out4, l, 0, 0, n_ctx)
        p = _inproj(x, mod_l, pre[1], w_in_p, l, n_ctx)

        y_pool = jnp.concatenate([_pool(p, pw, psc, seq, 0, nb), _pool(p, pw, psc, dseq, lat_blk, ndec)], axis=0)
        y_na = jnp.concatenate([
            _ctx_attn(p, nb),
            _na_latent(p, cache_na_k, cache_na_v, _na_bias_table(na_rpb[l]), l, lat_blk, ndec)], axis=0)
        g_ctx, s_fin = _gla(p, wgate_p, gla_b_gate[l], gla_norm[l], seq, 0, nb, with_sfin=True)
        (g_lat,) = _gla(p, wgate_p, gla_b_gate[l], gla_norm[l], dseq, lat_blk, ndec, rope_tabs=rope_tabs,
                        state=state_gla, layer=l)
        y_gla = jnp.concatenate([g_ctx, g_lat], axis=0)

        x = _merge(x, mod_l, post[1], y_pool, y_na, y_gla, p, w_stack, n_ctx)
        x = _ffn(x, mod_l, pre[2], post[2], w_ffn_in5, w_ffn_out4, l, 1, 2, n_ctx)

        heads = lambda a: a.reshape(nb, seq, NA_HEADS, NA_HEAD_DIM).transpose(0, 2, 1, 3)
        new_k.append(heads(p[:n_ctx, OFF_NK:OFF_NK + NA_WIDTH]))
        new_v.append(heads(p[:n_ctx, OFF_NV:OFF_NV + NA_WIDTH]))
        new_s.append(s_fin)

    y_prompt = x[:n_ctx].reshape(nb, seq, d)
    y_sample = x[n_ctx:].reshape(ndec, dseq, d)
    return (y_prompt, y_sample, jnp.stack(new_k, axis=1), jnp.stack(new_v, axis=1), jnp.stack(new_s, axis=1))
```

```python
import functools

import numpy as np
import jax
import jax.numpy as jnp
from jax import lax
from jax.experimental import pallas as pl
from jax.experimental.pallas import tpu as pltpu

f32 = jnp.float32
bf16 = jnp.bfloat16

D_MODEL = 2048
SEQ = 256
DEC_SEQ = 2048
GRID_W = 64
N_MOD = 9
D_FF = 5504
FFN_RES = 0.5
EPS = 1e-6
NEG_INF = -1e30

POOL_GROUPS = 4
POOL_WINDOWS = (2, 4, 8, 16)
POOL_WIDTH = 1024
POOL_GC = POOL_WIDTH // POOL_GROUPS

NA_HEADS = 8
NA_HEAD_DIM = 128
NA_WIDTH = NA_HEADS * NA_HEAD_DIM
NA_WIN_H = 8
NA_WIN_W = 16

GLA_HEADS = 4
GLA_DK = 128
GLA_DV = 256
GLA_KW = GLA_HEADS * GLA_DK
GLA_VW = GLA_HEADS * GLA_DV
GLA_RANK = 16
GLA_TAU = 16.0
GLA_CHUNK = 64
ROPE_BASE = 10000.0

BRANCH_W = 1024
N_BRANCH = 3
GATE_W = N_BRANCH * D_MODEL

LANE = 128
VMEM_LIMIT = 56 * 1024 * 1024

OFF_GL = 0
OFF_POOL = OFF_GL + GATE_W
OFF_NQ = OFF_POOL + POOL_WIDTH
OFF_NK = OFF_NQ + NA_WIDTH
OFF_NV = OFF_NK + NA_WIDTH
OFF_GQ = OFF_NV + NA_WIDTH
OFF_GK = OFF_GQ + GLA_KW
OFF_GV = OFF_GK + GLA_KW
OFF_GR = OFF_GV + GLA_VW
OFF_GZ = OFF_GR + GLA_VW
TN_IN = 1536
IN_COLS_P = -(-(OFF_GZ + LANE) // TN_IN) * TN_IN

TM = 512
TF = 512
D_FF_P = -(-D_FF // TF) * TF


def _cparams(sem):
    return pltpu.CompilerParams(dimension_semantics=sem, vmem_limit_bytes=VMEM_LIMIT)


def _cond_index(i, tm, cond):
    return cond[0] + (i * tm) // cond[1]


def _rms(x, g):
    return x * lax.rsqrt(jnp.mean(x * x, axis=-1, keepdims=True) + EPS) * g


def _dot(a, b):
    return jnp.dot(a, b, preferred_element_type=f32)


def _dot_nt(a, b):
    return lax.dot_general(a, b, (((1,), (1,)), ((), ())), preferred_element_type=f32)


def _mod_body(c_ref, w_ref, b_ref, o_ref):
    c = c_ref[...]
    s = c * jax.nn.sigmoid(c)
    o_ref[...] = _dot(s.astype(bf16), w_ref[...].astype(bf16)) + b_ref[...]


def _modulation(c_all, w_mod, b_mod):
    depth, d, n = w_mod.shape
    nc = c_all.shape[0]
    tn = 1024
    return pl.pallas_call(
        _mod_body,
        grid=(depth, n // tn),
        in_specs=[
            pl.BlockSpec((nc, d), lambda l, j: (0, 0)),
            pl.BlockSpec((None, d, tn), lambda l, j: (l, 0, j)),
            pl.BlockSpec((None, 1, tn), lambda l, j: (l, 0, j)),
        ],
        out_specs=pl.BlockSpec((None, nc, tn), lambda l, j: (l, 0, j)),
        out_shape=jax.ShapeDtypeStruct((depth, nc, n), f32),
        compiler_params=_cparams(("parallel", "parallel")),
        name="modulation",
    )(c_all, w_mod, b_mod.reshape(depth, 1, n))


def _ffn_body(x_ref, mod_ref, pre_ref, post_ref, wg_ref, wu_ref, wo_ref, o_ref, h_scr, acc_scr):
    f = pl.program_id(1)
    d = D_MODEL

    @pl.when(f == 0)
    def _():
        y = _rms(x_ref[...], pre_ref[...])
        h = y * (1.0 + mod_ref[:, d:2 * d]) + mod_ref[:, 0:d]
        h_scr[...] = h.astype(bf16)
        acc_scr[...] = jnp.zeros_like(acc_scr)

    h = h_scr[...]
    gt = _dot(h, wg_ref[...])
    up = _dot(h, wu_ref[...])
    a = gt * jax.nn.sigmoid(gt) * up
    acc_scr[...] += _dot(a.astype(bf16), wo_ref[...])

    @pl.when(f == pl.num_programs(1) - 1)
    def _():
        yn = _rms(acc_scr[...], post_ref[...])
        o_ref[...] = x_ref[...] + FFN_RES * mod_ref[:, 2 * d:3 * d] * yn


def _ffn(x, mod_l, pre, post, w_in5, w_out4, layer, slot, sub, cond):
    t, d = x.shape
    fp = w_out4.shape[2]
    w_half = lambda half: pl.BlockSpec((None, None, None, d, TF), lambda i, f: (layer, slot, half, 0, f))
    return pl.pallas_call(
        _ffn_body,
        grid=(t // TM, fp // TF),
        in_specs=[
            pl.BlockSpec((TM, d), lambda i, f: (i, 0)),
            pl.BlockSpec((None, 1, 3 * d), lambda i, f: (_cond_index(i, TM, cond), 0, sub)),
            pl.BlockSpec((1, d), lambda i, f: (0, 0)),
            pl.BlockSpec((1, d), lambda i, f: (0, 0)),
            w_half(0),
            w_half(1),
            pl.BlockSpec((None, None, TF, d), lambda i, f: (layer, slot, f, 0)),
        ],
        out_specs=pl.BlockSpec((TM, d), lambda i, f: (i, 0)),
        out_shape=jax.ShapeDtypeStruct((t, d), f32),
        scratch_shapes=[pltpu.VMEM((TM, d), bf16), pltpu.VMEM((TM, d), f32)],
        compiler_params=_cparams(("parallel", "arbitrary")),
        name="ffn",
    )(x, mod_l, pre, post, w_in5, w_in5, w_out4)


def _inproj_body(x_ref, mod_ref, pre_ref, w_ref, o_ref, h_scr):
    d = D_MODEL

    @pl.when(pl.program_id(1) == 0)
    def _():
        y = _rms(x_ref[...], pre_ref[...])
        h = y * (1.0 + mod_ref[:, d:2 * d]) + mod_ref[:, 0:d]
        h_scr[...] = h.astype(bf16)

    o_ref[...] = _dot(h_scr[...], w_ref[...])


def _inproj(x, mod_l, pre, w_in_p, layer, cond):
    t, d = x.shape
    n = w_in_p.shape[2]
    return pl.pallas_call(
        _inproj_body,
        grid=(t // TM, n // TN_IN),
        in_specs=[
            pl.BlockSpec((TM, d), lambda i, j: (i, 0)),
            pl.BlockSpec((None, 1, 3 * d), lambda i, j: (_cond_index(i, TM, cond), 0, 1)),
            pl.BlockSpec((1, d), lambda i, j: (0, 0)),
            pl.BlockSpec((None, d, TN_IN), lambda i, j: (layer, 0, j)),
        ],
        out_specs=pl.BlockSpec((TM, TN_IN), lambda i, j: (i, j)),
        out_shape=jax.ShapeDtypeStruct((t, n), f32),
        scratch_shapes=[pltpu.VMEM((TM, d), bf16)],
        compiler_params=_cparams(("parallel", "arbitrary")),
        name="inproj",
    )(x, mod_l, pre, w_in_p)


POOL_PAD = 8


def _pool_body(u_ref, w_ref, sc_ref, o_ref, pad_scr, *, seq):
    gc = POOL_GC
    zeros = jnp.zeros((POOL_PAD, POOL_WIDTH), f32)
    pad_scr[pl.ds(0, POOL_PAD), :] = zeros
    pad_scr[pl.ds(POOL_PAD + seq, POOL_PAD), :] = zeros
    pad_scr[pl.ds(POOL_PAD, seq), :] = u_ref[...]
    t = lax.broadcasted_iota(jnp.int32, (seq, 1), 0)
    for gi, win in enumerate(POOL_WINDOWS):
        cols = pl.ds(gi * gc, gc)
        lo = jnp.maximum(t - win // 2, 0)
        hi = jnp.minimum(t + win - 1 - win // 2, seq - 1)
        cnt = (hi - lo + 1).astype(f32)
        acc = pad_scr[pl.ds(POOL_PAD - win // 2, seq), cols]
        for j in range(1, win):
            acc = acc + pad_scr[pl.ds(POOL_PAD - win // 2 + j, seq), cols]
        pooled = acc / cnt - u_ref[:, cols]
        y = _dot(pooled.astype(bf16), w_ref[gi])
        o_ref[:, cols] = (y * sc_ref[:, cols]).astype(o_ref.dtype)


def _pool(p, pool_w, pool_scale, seq, row_block0, nseq):
    cb = OFF_POOL // POOL_WIDTH
    return pl.pallas_call(
        functools.partial(_pool_body, seq=seq),
        grid=(nseq,),
        in_specs=[
            pl.BlockSpec((seq, POOL_WIDTH), lambda s: (row_block0 + s, cb)),
            pl.BlockSpec((POOL_GROUPS, POOL_GC, POOL_GC), lambda s: (0, 0, 0)),
            pl.BlockSpec((1, POOL_WIDTH), lambda s: (0, 0)),
        ],
        out_specs=pl.BlockSpec((seq, POOL_WIDTH), lambda s: (s, 0)),
        out_shape=jax.ShapeDtypeStruct((nseq * seq, POOL_WIDTH), bf16),
        scratch_shapes=[pltpu.VMEM((seq + 2 * POOL_PAD, POOL_WIDTH), f32)],
        compiler_params=_cparams(("parallel",)),
        name="pool",
    )(p, pool_w, pool_scale)


def _softmax_rows(s):
    m = jnp.max(s, axis=-1, keepdims=True)
    e = jnp.exp(s - m)
    return e / jnp.sum(e, axis=-1, keepdims=True)


def _ctx_attn_body(q_ref, k_ref, v_ref, *refs):
    o_ref, nk_ref, nv_ref = refs[-3:]
    hd = NA_HEAD_DIM
    for h in range(NA_HEADS):
        cols = pl.ds(h * hd, hd)
        kf = k_ref[:, cols]
        vf = v_ref[:, cols]
        nk_ref[h] = kf
        nv_ref[h] = vf
        p = _softmax_rows(_dot_nt(q_ref[:, cols].astype(bf16), kf.astype(bf16)) * (hd ** -0.5))
        o_ref[:, cols] = _dot(p.astype(bf16), vf.astype(bf16)).astype(o_ref.dtype)


def _ctx_attn(p, nseq, layer, depth, caches=None):
    spec = lambda off: pl.BlockSpec((SEQ, NA_WIDTH), lambda b: (b, off // NA_WIDTH))
    cache_spec = pl.BlockSpec((None, None, NA_HEADS, SEQ, NA_HEAD_DIM), lambda b: (b, layer, 0, 0, 0))
    cache_shape = jax.ShapeDtypeStruct((nseq, depth, NA_HEADS, SEQ, NA_HEAD_DIM), f32)
    in_specs = [spec(OFF_NQ), spec(OFF_NK), spec(OFF_NV)]
    args = [p, p, p]
    aliases = {}
    if caches is not None:
        in_specs += [pl.BlockSpec(memory_space=pl.ANY)] * 2
        args += list(caches)
        aliases = {3: 1, 4: 2}
    return pl.pallas_call(
        _ctx_attn_body,
        grid=(nseq,),
        in_specs=in_specs,
        out_specs=[pl.BlockSpec((SEQ, NA_WIDTH), lambda b: (b, 0)), cache_spec, cache_spec],
        out_shape=[jax.ShapeDtypeStruct((nseq * SEQ, NA_WIDTH), bf16), cache_shape, cache_shape],
        input_output_aliases=aliases,
        compiler_params=_cparams(("parallel",)),
        name="ctx_attn",
    )(*args)


def _na_bias_table(rpb):
    qc = np.arange(GRID_W)[:, None]
    kc = np.arange(GRID_W)[None, :]
    cs = np.clip(qc - NA_WIN_W // 2, 0, GRID_W - NA_WIN_W)
    ok = (kc >= cs) & (kc < cs + NA_WIN_W)
    cidx = np.clip(kc - qc + NA_WIN_W - 1, 0, 2 * NA_WIN_W - 2)
    onehot = jnp.asarray((cidx[None] == np.arange(2 * NA_WIN_W - 1)[:, None, None]) & ok[None], f32)
    toep = jnp.einsum('...rc,cqk->...rqk', rpb.astype(f32), onehot, precision=lax.Precision.HIGHEST)
    toep = jnp.where(ok, toep, NEG_INF)
    tbl = jnp.stack([toep[..., s:s + NA_WIN_H, :, :] for s in range(NA_WIN_H)], axis=-4)
    tbl = jnp.swapaxes(tbl, -3, -2)
    return tbl.reshape(rpb.shape[:-2] + (NA_WIN_H, GRID_W, NA_WIN_H * GRID_W))


NA_UNROLL = 8
NA_CTX_ROWS = 256


def _na_body(q_ref, k_ref, v_ref, ck_ref, cv_ref, bias_ref, o_ref,
             qb_scr, kb_scr, vb_scr, sl_scr, sc_scr, el_scr, ec_scr, den_scr, oc_scr, *, rows):
    hd = NA_HEAD_DIM
    scale = hd ** -0.5
    kh = min(NA_WIN_H, rows)
    nloc = kh * GRID_W
    n = rows * GRID_W
    qb_scr[...] = q_ref[...].astype(bf16)
    kb_scr[...] = k_ref[...].astype(bf16)
    vb_scr[...] = v_ref[...].astype(bf16)
    ck = ck_ref[...].astype(bf16)
    cv = cv_ref[...].astype(bf16)

    def row_slices(r):
        rs = jnp.clip(r - kh // 2, 0, rows - kh)
        q_rows = pl.ds(pl.multiple_of(r * GRID_W, GRID_W), GRID_W)
        k_rows = pl.ds(pl.multiple_of(rs * GRID_W, GRID_W), nloc)
        return rs, q_rows, k_rows

    def ctx_scores(i, carry):
        blk = pl.ds(pl.multiple_of(i * NA_CTX_ROWS, NA_CTX_ROWS), NA_CTX_ROWS)
        sc_scr[blk, :] = _dot_nt(qb_scr[blk, :], ck) * scale
        return carry

    lax.fori_loop(0, n // NA_CTX_ROWS, ctx_scores, 0, unroll=2)

    def loc_scores(r, carry):
        rs, q_rows, k_rows = row_slices(r)
        sl_scr[q_rows, :] = (_dot_nt(qb_scr[q_rows, :], kb_scr[k_rows, :]) * scale
                             + bias_ref[rs - r + NA_WIN_H - 1])
        return carry

    lax.fori_loop(0, rows, loc_scores, 0, unroll=NA_UNROLL)

    def numerators(r, carry):
        q_rows = pl.ds(pl.multiple_of(r * GRID_W, GRID_W), GRID_W)
        s_loc = sl_scr[q_rows, :]
        s_ctx = sc_scr[q_rows, :]
        m = jnp.maximum(jnp.max(s_loc, axis=-1, keepdims=True), jnp.max(s_ctx, axis=-1, keepdims=True))
        e_loc = jnp.exp(s_loc - m)
        e_ctx = jnp.exp(s_ctx - m)
        den = jnp.sum(e_loc, axis=-1, keepdims=True) + jnp.sum(e_ctx, axis=-1, keepdims=True)
        el_scr[q_rows, :] = e_loc.astype(bf16)
        ec_scr[q_rows, :] = e_ctx.astype(bf16)
        den_scr[q_rows, :] = jnp.broadcast_to(den, (GRID_W, hd))
        return carry

    lax.fori_loop(0, rows, numerators, 0, unroll=NA_UNROLL)

    def ctx_values(i, carry):
        blk = pl.ds(pl.multiple_of(i * NA_CTX_ROWS, NA_CTX_ROWS), NA_CTX_ROWS)
        oc_scr[blk, :] = _dot(ec_scr[blk, :], cv)
        return carry

    lax.fori_loop(0, n // NA_CTX_ROWS, ctx_values, 0, unroll=2)

    def loc_values(r, carry):
        _, q_rows, k_rows = row_slices(r)
        o = _dot(el_scr[q_rows, :], vb_scr[k_rows, :]) + oc_scr[q_rows, :]
        o_ref[q_rows, :] = (o / den_scr[q_rows, :]).astype(o_ref.dtype)
        return carry

    lax.fori_loop(0, rows, loc_values, 0, unroll=NA_UNROLL)


def _na_latent(p, cache_k, cache_v, bias_tbl, layer, row_block0, nreq):
    n = DEC_SEQ
    hd = NA_HEAD_DIM
    past = cache_k.shape[3]
    rows = n // GRID_W
    qkv = lambda off: pl.BlockSpec((n, hd), lambda b, h: (row_block0 + b, off // hd + h))
    cache = pl.BlockSpec((None, None, None, past, hd), lambda b, h: (b, layer, h, 0, 0))
    return pl.pallas_call(
        functools.partial(_na_body, rows=rows),
        grid=(nreq, NA_HEADS),
        in_specs=[qkv(OFF_NQ), qkv(OFF_NK), qkv(OFF_NV), cache, cache,
                  pl.BlockSpec((None, None, NA_WIN_H, GRID_W, NA_WIN_H * GRID_W),
                               lambda b, h: (layer, h, 0, 0, 0))],
        out_specs=pl.BlockSpec((n, hd), lambda b, h: (b, h)),
        out_shape=jax.ShapeDtypeStruct((nreq * n, NA_WIDTH), bf16),
        scratch_shapes=[pltpu.VMEM((n, hd), bf16), pltpu.VMEM((n, hd), bf16), pltpu.VMEM((n, hd), bf16),
                        pltpu.VMEM((n, NA_WIN_H * GRID_W), f32), pltpu.VMEM((n, past), f32),
                        pltpu.VMEM((n, NA_WIN_H * GRID_W), bf16), pltpu.VMEM((n, past), bf16),
                        pltpu.VMEM((n, hd), f32), pltpu.VMEM((n, hd), f32)],
        compiler_params=_cparams(("parallel", "parallel")),
        name="na_latent",
    )(p, p, p, cache_k, cache_v, bias_tbl)


GLA_PAD = 32
GLA_UNROLL = 4


def _rope_tables(seq):
    t = np.arange(seq)
    half = GLA_DK // 2
    nf = half // 2
    inv = ROPE_BASE ** (-np.arange(nf, dtype=np.float64) / nf)
    cos, sin = [], []
    for pos in (t // GRID_W, t % GRID_W):
        ang = pos[:, None].astype(np.float64) * inv
        cos += [np.cos(ang), np.cos(ang)]
        sin += [-np.sin(ang), np.sin(ang)]
    return (jnp.asarray(np.concatenate(cos, axis=-1), f32), jnp.asarray(np.concatenate(sin, axis=-1), f32))


def _rope(x, cos, sin_signed):
    nf = GLA_DK // 4
    lane = lax.broadcasted_iota(jnp.int32, x.shape, 1)
    partner = jnp.where(lane % (2 * nf) < nf, pltpu.roll(x, GLA_DK - nf, 1), pltpu.roll(x, nf, 1))
    return x * cos + partner * sin_signed


def _log_sigmoid(x):
    return jnp.minimum(x, 0.0) - jnp.log1p(jnp.exp(-jnp.abs(x)))


def _gla_body(*refs, seq, rope, with_s0, with_sfin, n_carried):
    refs = list(refs)
    q_ref, k_ref, v_ref, r_ref, z_ref, wg_ref, bg_ref, ng_ref = refs[:8]
    refs = refs[8:]
    if rope:
        cos_ref, sin_ref = refs[:2]
        refs = refs[2:]
    if with_s0:
        s0_ref = refs[0]
        refs = refs[1:]
    refs = refs[n_carried:]
    o_ref = refs[0]
    refs = refs[1:]
    if with_sfin:
        sfin_ref = refs[0]
        refs = refs[1:]
    qi_scr, kn_scr, kd_scr, dec_scr, scan_scr, vb_scr, u_scr, sb_scr, o_scr, st_scr = refs

    ch = GLA_CHUNK
    nch = seq // ch
    dk, dv = GLA_DK, GLA_DV

    q = q_ref[...]
    k = k_ref[...]
    if rope:
        q = _rope(q, cos_ref[...], sin_ref[...])
        k = _rope(k, cos_ref[...], sin_ref[...])
    q = q * (dk ** -0.5)

    zb = z_ref[...].astype(bf16)
    pos = lax.broadcasted_iota(jnp.int32, (seq, 1), 0) % ch
    zpad = jnp.zeros((GLA_PAD, dk), f32)
    scan_scr[pl.ds(0, GLA_PAD), :] = zpad
    scan_scr[pl.ds(GLA_PAD + seq, GLA_PAD), :] = zpad
    for d in range(2):
        g = _log_sigmoid(_dot(zb, wg_ref[d]) + bg_ref[d]) / GLA_TAU
        b = g
        sh = 1
        while sh < ch:
            scan_scr[pl.ds(GLA_PAD, seq), :] = b
            if d == 0:
                b = b + jnp.where(pos >= sh, scan_scr[pl.ds(GLA_PAD - sh, seq), :], 0.0)
            else:
                b = b + jnp.where(pos < ch - sh, scan_scr[pl.ds(GLA_PAD + sh, seq), :], 0.0)
            sh *= 2
        b3 = b.reshape(nch, ch, dk)
        b_end = b3[:, ch - 1:ch, :] if d == 0 else b3[:, 0:1, :]
        qi_scr[d] = (q * jnp.exp(b)).astype(bf16)
        kn_scr[d] = (k * jnp.exp(-b)).astype(bf16)
        kd_scr[d] = (k.reshape(nch, ch, dk) * jnp.exp(b_end - b3)).reshape(seq, dk).astype(bf16)
        dec_scr[d] = jnp.exp(b_end)

    for d in range(2):
        if with_s0:
            st_scr[d] = s0_ref[d].T
        else:
            st_scr[d] = jnp.zeros((dv, dk), f32)

    ri = lax.broadcasted_iota(jnp.int32, (ch, ch), 0)
    ci = lax.broadcasted_iota(jnp.int32, (ch, ch), 1)
    keep = (ci <= ri, ci >= ri)

    vb_scr[...] = v_ref[...].astype(bf16)
    chunk_rows = lambda c: pl.ds(pl.multiple_of(c * ch, ch), ch)

    def increments(c, carry):
        rows = chunk_rows(c)
        for d in range(2):
            u_scr[d, c] = lax.dot_general(vb_scr[rows, :], kd_scr[d, rows, :], (((0,), (0,)), ((), ())),
                                          preferred_element_type=f32)
        return carry

    lax.fori_loop(0, nch, increments, 0, unroll=GLA_UNROLL)

    def states(i, carry):
        for d in range(2):
            c = i if d == 0 else nch - 1 - i
            st = st_scr[d]
            sb_scr[d, c] = st.astype(bf16)
            st_scr[d] = st * dec_scr[d, c] + u_scr[d, c]
        return carry

    lax.fori_loop(0, nch, states, 0)

    if with_sfin:
        for d in range(2):
            sfin_ref[d] = st_scr[d].T

    def outputs(c, carry):
        rows = chunk_rows(c)
        vb = vb_scr[rows, :]
        o = None
        for d in range(2):
            qi = qi_scr[d, rows, :]
            a = jnp.where(keep[d], _dot_nt(qi, kn_scr[d, rows, :]), 0.0)
            od = _dot(a.astype(bf16), vb) + _dot_nt(qi, sb_scr[d, c])
            o = od if o is None else o + od
        o_scr[rows, :] = o
        return carry

    lax.fori_loop(0, nch, outputs, 0, unroll=GLA_UNROLL)

    o = o_scr[...]
    r = r_ref[...]
    o = o * lax.rsqrt(jnp.mean(o * o, axis=-1, keepdims=True) + EPS) * ng_ref[...]
    o_ref[...] = (o * (r * jax.nn.sigmoid(r))).astype(o_ref.dtype)


def _gla(p, wgate_p, b_gate, gla_norm, seq, row_block0, nreq, rope_tabs=None, state=None, layer=0,
         with_sfin=False, depth=1, carried=None):
    dk, dv = GLA_DK, GLA_DV
    nch = seq // GLA_CHUNK
    rope = rope_tabs is not None
    with_s0 = state is not None
    blk = lambda w, off: pl.BlockSpec((seq, w), lambda b, h: (row_block0 + b, off // w + h))
    in_specs = [blk(dk, OFF_GQ), blk(dk, OFF_GK), blk(dv, OFF_GV), blk(dv, OFF_GR),
                pl.BlockSpec((seq, LANE), lambda b, h: (row_block0 + b, OFF_GZ // LANE)),
                pl.BlockSpec((2, LANE, dk), lambda b, h: (0, 0, h)),
                pl.BlockSpec((2, 1, dk), lambda b, h: (0, 0, h)),
                pl.BlockSpec((1, dv), lambda b, h: (0, h))]
    args = [p, p, p, p, p, wgate_p, b_gate.reshape(2, 1, GLA_KW), gla_norm.reshape(1, GLA_VW)]
    if rope:
        in_specs += [pl.BlockSpec((seq, dk), lambda b, h: (0, 0))] * 2
        args += list(rope_tabs)
    if with_s0:
        in_specs.append(pl.BlockSpec((None, None, 2, None, dk, dv), lambda b, h: (b, layer, 0, h, 0, 0)))
        args.append(state)
    aliases = {}
    if carried is not None:
        aliases = {len(args): 1}
        in_specs.append(pl.BlockSpec(memory_space=pl.ANY))
        args.append(carried)
    out_specs = [pl.BlockSpec((seq, dv), lambda b, h: (b, h))]
    out_shape = [jax.ShapeDtypeStruct((nreq * seq, GLA_VW), bf16)]
    if with_sfin:
        out_specs.append(pl.BlockSpec((None, None, 2, None, dk, dv), lambda b, h: (b, layer, 0, h, 0, 0)))
        out_shape.append(jax.ShapeDtypeStruct((nreq, depth, 2, GLA_HEADS, dk, dv), f32))
    scratch = [pltpu.VMEM((2, seq, dk), bf16), pltpu.VMEM((2, seq, dk), bf16), pltpu.VMEM((2, seq, dk), bf16),
               pltpu.VMEM((2, nch, 1, dk), f32), pltpu.VMEM((seq + 2 * GLA_PAD, dk), f32),
               pltpu.VMEM((seq, dv), bf16), pltpu.VMEM((2, nch, dv, dk), f32), pltpu.VMEM((2, nch, dv, dk), bf16),
               pltpu.VMEM((seq, dv), f32), pltpu.VMEM((2, dv, dk), f32)]
    return pl.pallas_call(
        functools.partial(_gla_body, seq=seq, rope=rope, with_s0=with_s0, with_sfin=with_sfin,
                          n_carried=len(aliases)),
        grid=(nreq, GLA_HEADS),
        in_specs=in_specs,
        out_specs=out_specs,
        out_shape=out_shape,
        input_output_aliases=aliases,
        scratch_shapes=scratch,
        compiler_params=_cparams(("parallel", "parallel")),
        name="gla",
    )(*args)


TM_MERGE = 512


def _merge_body(x_ref, mod_ref, post_ref, bp_ref, bn_ref, bg_ref, gl_ref, w_ref, o_ref, m_scr, mb_scr):
    n = pl.program_id(1)
    d = D_MODEL

    @pl.when(n == 0)
    def _():
        m_scr[...] = jnp.zeros_like(m_scr)

    for bi, br_ref in enumerate((bp_ref, bn_ref, bg_ref)):
        @pl.when(n == bi)
        def _():
            m_scr[...] += jax.nn.sigmoid(gl_ref[...]) * _dot(br_ref[...], w_ref[...])

    @pl.when(n == N_BRANCH)
    def _():
        mb_scr[...] = m_scr[...].astype(bf16)
        m_scr[...] = _dot(mb_scr[:, 0:BRANCH_W], w_ref[...])

    @pl.when(n == N_BRANCH + 1)
    def _():
        y = m_scr[...] + _dot(mb_scr[:, BRANCH_W:2 * BRANCH_W], w_ref[...])
        o_ref[...] = x_ref[...] + mod_ref[:, 2 * d:3 * d] * _rms(y, post_ref[...])


def _merge(x, mod_l, post, y_pool, y_na, y_gla, p, w_stack, cond):
    t, d = x.shape
    tm = TM_MERGE
    nsteps = N_BRANCH + d // BRANCH_W
    br = pl.BlockSpec((tm, BRANCH_W), lambda i, n: (i, 0))
    return pl.pallas_call(
        _merge_body,
        grid=(t // tm, nsteps),
        in_specs=[
            pl.BlockSpec((tm, d), lambda i, n: (i, 0)),
            pl.BlockSpec((None, 1, 3 * d), lambda i, n: (_cond_index(i, tm, cond), 0, 1)),
            pl.BlockSpec((1, d), lambda i, n: (0, 0)),
            br, br, br,
            pl.BlockSpec((tm, d), lambda i, n: (i, jnp.minimum(n, N_BRANCH - 1))),
            pl.BlockSpec((None, BRANCH_W, d), lambda i, n: (n, 0, 0)),
        ],
        out_specs=pl.BlockSpec((tm, d), lambda i, n: (i, 0)),
        out_shape=jax.ShapeDtypeStruct((t, d), f32),
        scratch_shapes=[pltpu.VMEM((tm, d), f32), pltpu.VMEM((tm, d), bf16)],
        compiler_params=_cparams(("parallel", "arbitrary")),
        name="merge",
    )(x, mod_l, post, y_pool, y_na, y_gla, p, w_stack)


_IN_SPLITS = (POOL_WIDTH, NA_WIDTH, NA_WIDTH, NA_WIDTH, GLA_KW, GLA_KW, GLA_VW, 2 * GLA_RANK, GLA_VW, GATE_W)
_IN_OFFS = tuple(int(v) for v in np.cumsum((0,) + _IN_SPLITS))
_IN_RUNS = ((_IN_OFFS[9], _IN_OFFS[10]), (_IN_OFFS[0], _IN_OFFS[7]), (_IN_OFFS[8], _IN_OFFS[9]),
            (_IN_OFFS[7], _IN_OFFS[8]))


CT_IN = 512
_IN_TILE_STARTS = []
for _a, _b in _IN_RUNS:
    _IN_TILE_STARTS += [_a + CT_IN * _t for _t in range(-(-(_b - _a) // CT_IN))]
assert len(_IN_TILE_STARTS) * CT_IN == IN_COLS_P and all(v % 8 == 0 for v in _IN_TILE_STARTS)
_IN_LAST_VALID = (_IN_RUNS[-1][1] - _IN_RUNS[-1][0]) % CT_IN or CT_IN


assert all((b - a) % CT_IN == 0 for a, b in _IN_RUNS[:-1])


def _w_in_tile_start(j):
    out = jnp.int32(_IN_TILE_STARTS[0]) + CT_IN * j
    for t in range(1, len(_IN_TILE_STARTS)):
        if _IN_TILE_STARTS[t] != _IN_TILE_STARTS[t - 1] + CT_IN:
            out = jnp.where(j >= t, _IN_TILE_STARTS[t] + CT_IN * (j - t), out)
    return out


def _cast_w_in_body(w_ref, o_ref):
    last = pl.program_id(1) == pl.num_programs(1) - 1
    col = lax.broadcasted_iota(jnp.int32, (1, CT_IN), 1)
    valid = jnp.where(last, _IN_LAST_VALID, CT_IN)
    o_ref[...] = jnp.where(col < valid, w_ref[...].T, 0.0).astype(bf16)


def _cast_w_in(w):
    depth, d, n = w.shape
    wt = jnp.swapaxes(w, 1, 2).reshape(depth * n, d)
    return pl.pallas_call(
        _cast_w_in_body,
        grid=(depth, IN_COLS_P // CT_IN),
        in_specs=[pl.BlockSpec((pl.Element(CT_IN), pl.Element(d)), lambda l, j: (pl.multiple_of(l * n + _w_in_tile_start(j), 8), 0))],
        out_specs=pl.BlockSpec((None, d, CT_IN), lambda l, j: (l, 0, j)),
        out_shape=jax.ShapeDtypeStruct((depth, d, IN_COLS_P), bf16),
        compiler_params=_cparams(("parallel", "parallel")),
        name="cast_w_in",
    )(wt)


def _cast_ffn_in_body(w_ref, o_ref):
    pad = jnp.zeros((w_ref.shape[0], D_FF_P - D_FF), bf16)
    for half in range(2):
        o_ref[half, :, 0:D_FF] = w_ref[:, half * D_FF:(half + 1) * D_FF].astype(bf16)
        o_ref[half, :, D_FF:D_FF_P] = pad


def _cast_ffn_in(w):
    depth, ns, d, _ = w.shape
    rows = 256
    return pl.pallas_call(
        _cast_ffn_in_body,
        grid=(depth, ns, d // rows),
        in_specs=[pl.BlockSpec((None, None, rows, 2 * D_FF), lambda l, s, r: (l, s, r, 0))],
        out_specs=pl.BlockSpec((None, None, 2, rows, D_FF_P), lambda l, s, r: (l, s, 0, r, 0)),
        out_shape=jax.ShapeDtypeStruct((depth, ns, 2, d, D_FF_P), bf16),
        compiler_params=_cparams(("parallel", "parallel", "parallel")),
        name="cast_ffn_in",
    )(w)


def _cast_ffn_out_body(w_ref, o_ref):
    row = pl.program_id(2) * TF + lax.broadcasted_iota(jnp.int32, (TF, 1), 0)
    o_ref[...] = jnp.where(row < D_FF, w_ref[...], 0.0).astype(bf16)


def _cast_ffn_out(w):
    depth, ns, _, d = w.shape
    return pl.pallas_call(
        _cast_ffn_out_body,
        grid=(depth, ns, D_FF_P // TF),
        in_specs=[pl.BlockSpec((None, None, TF, d), lambda l, s, j: (l, s, j, 0))],
        out_specs=pl.BlockSpec((None, None, TF, d), lambda l, s, j: (l, s, j, 0)),
        out_shape=jax.ShapeDtypeStruct((depth, ns, D_FF_P, d), bf16),
        compiler_params=_cparams(("parallel", "parallel", "parallel")),
        name="cast_ffn_out",
    )(w)


def _prep_gate(w_gate):
    out = jnp.zeros((2, LANE, GLA_KW), f32)
    for d in range(2):
        out = out.at[d, d * GLA_RANK:(d + 1) * GLA_RANK].set(w_gate[d])
    return out.astype(bf16)


def kernel(x_prompt, x_sample, c, cache_na_k, cache_na_v, state_gla, c_ctx, w_mod, b_mod, norm_pre, norm_post,
           w_ffn_in, w_ffn_out, w_in, pool_w, pool_scale, na_rpb, gla_w_gate, gla_b_gate, gla_norm, w_branch,
           w_out):
    nb, seq, d = x_prompt.shape
    ndec, dseq, _ = x_sample.shape
    depth = w_mod.shape[0]
    n_ctx = nb * seq
    n_lat = ndec * dseq
    assert (seq, dseq, d) == (SEQ, DEC_SEQ, D_MODEL) and n_ctx % TM_MERGE == 0 and n_ctx % TM == 0

    xs = [x_prompt.reshape(n_ctx, d), x_sample.reshape(n_lat, d)]
    conds = [(0, n_ctx), (1, dseq)]
    ncond = -(-(1 + ndec) // 8) * 8
    c_all = jnp.concatenate([c_ctx[None], c, jnp.zeros((ncond - 1 - ndec, d), f32)], axis=0)
    mod = _modulation(c_all, w_mod, b_mod)
    rope_tabs = _rope_tables(dseq)
    bias_tbl = _na_bias_table(na_rpb)

    w_ffn_in5 = _cast_ffn_in(w_ffn_in)
    w_ffn_out4 = _cast_ffn_out(w_ffn_out)
    w_in_p = _cast_w_in(w_in)
    w_stack_all = jnp.concatenate(
        [w_branch.astype(bf16), w_out.reshape(depth, d // BRANCH_W, BRANCH_W, d).astype(bf16)], axis=1)

    caches = None
    new_s = None
    for l in range(depth):
        mod_l = mod[l].reshape(ncond, 1, N_MOD * d)
        pre = norm_pre[l].reshape(3, 1, d)
        post = norm_post[l].reshape(3, 1, d)
        w_stack = w_stack_all[l]
        wgate_p = _prep_gate(gla_w_gate[l])
        pw = pool_w[l].astype(bf16)
        psc = pool_scale[l].reshape(1, POOL_WIDTH)

        xs = [_ffn(x, mod_l, pre[0], post[0], w_ffn_in5, w_ffn_out4, l, 0, 0, cond) for x, cond in zip(xs, conds)]
        p_ctx, p_lat = [_inproj(x, mod_l, pre[1], w_in_p, l, cond) for x, cond in zip(xs, conds)]

        y_pool = [_pool(p_ctx, pw, psc, seq, 0, nb), _pool(p_lat, pw, psc, dseq, 0, ndec)]
        na_ctx, new_k, new_v = _ctx_attn(p_ctx, nb, l, depth, caches)
        caches = (new_k, new_v)
        y_na = [na_ctx, _na_latent(p_lat, cache_na_k, cache_na_v, bias_tbl, l, 0, ndec)]
        g_ctx, new_s = _gla(p_ctx, wgate_p, gla_b_gate[l], gla_norm[l], seq, 0, nb, layer=l, with_sfin=True,
                            depth=depth, carried=new_s)
        (g_lat,) = _gla(p_lat, wgate_p, gla_b_gate[l], gla_norm[l], dseq, 0, ndec, rope_tabs=rope_tabs,
                        state=state_gla, layer=l)
        y_gla = [g_ctx, g_lat]

        xs = [_merge(x, mod_l, post[1], yp, yn, yg, p, w_stack, cond)
              for x, yp, yn, yg, p, cond in zip(xs, y_pool, y_na, y_gla, (p_ctx, p_lat), conds)]
        xs = [_ffn(x, mod_l, pre[2], post[2], w_ffn_in5, w_ffn_out4, l, 1, 2, cond) for x, cond in zip(xs, conds)]

    return (xs[0].reshape(nb, seq, d), xs[1].reshape(ndec, dseq, d), caches[0], caches[1], new_s)
```

```python
import functools

import numpy as np
import jax
import jax.numpy as jnp
from jax import lax
from jax.experimental import pallas as pl
from jax.experimental.pallas import tpu as pltpu

f32 = jnp.float32
bf16 = jnp.bfloat16

D_MODEL = 2048
SEQ = 256
DEC_SEQ = 2048
GRID_W = 64
N_MOD = 9
D_FF = 5504
FFN_RES = 0.5
EPS = 1e-6
NEG_INF = -1e30

POOL_GROUPS = 4
POOL_WINDOWS = (2, 4, 8, 16)
POOL_WIDTH = 1024
POOL_GC = POOL_WIDTH // POOL_GROUPS

NA_HEADS = 8
NA_HEAD_DIM = 128
NA_WIDTH = NA_HEADS * NA_HEAD_DIM
NA_WIN_H = 8
NA_WIN_W = 16

GLA_HEADS = 4
GLA_DK = 128
GLA_DV = 256
GLA_KW = GLA_HEADS * GLA_DK
GLA_VW = GLA_HEADS * GLA_DV
GLA_RANK = 16
GLA_TAU = 16.0
GLA_CHUNK = 64
ROPE_BASE = 10000.0

BRANCH_W = 1024
N_BRANCH = 3
GATE_W = N_BRANCH * D_MODEL

LANE = 128
VMEM_LIMIT = 56 * 1024 * 1024

OFF_GL = 0
OFF_POOL = OFF_GL + GATE_W
OFF_NQ = OFF_POOL + POOL_WIDTH
OFF_NK = OFF_NQ + NA_WIDTH
OFF_NV = OFF_NK + NA_WIDTH
OFF_GQ = OFF_NV + NA_WIDTH
OFF_GK = OFF_GQ + GLA_KW
OFF_GV = OFF_GK + GLA_KW
OFF_GR = OFF_GV + GLA_VW
OFF_GZ = OFF_GR + GLA_VW
TN_IN = 1536
IN_COLS_P = -(-(OFF_GZ + LANE) // TN_IN) * TN_IN

TM = 512
TF = 512
D_FF_P = -(-D_FF // TF) * TF


def _cparams(sem):
    return pltpu.CompilerParams(dimension_semantics=sem, vmem_limit_bytes=VMEM_LIMIT)


def _cond_index(i, tm, cond):
    return cond[0] + (i * tm) // cond[1]


def _rms(x, g):
    return x * lax.rsqrt(jnp.mean(x * x, axis=-1, keepdims=True) + EPS) * g


def _dot(a, b):
    return jnp.dot(a, b, preferred_element_type=f32)


def _dot_nt(a, b):
    return lax.dot_general(a, b, (((1,), (1,)), ((), ())), preferred_element_type=f32)


def _mod_body(c_ref, w_ref, b_ref, o_ref):
    c = c_ref[...]
    s = c * jax.nn.sigmoid(c)
    o_ref[...] = _dot(s.astype(bf16), w_ref[...].astype(bf16)) + b_ref[...]


def _modulation(c_all, w_mod, b_mod):
    depth, d, n = w_mod.shape
    nc = c_all.shape[0]
    tn = 1024
    return pl.pallas_call(
        _mod_body,
        grid=(depth, n // tn),
        in_specs=[
            pl.BlockSpec((nc, d), lambda l, j: (0, 0)),
            pl.BlockSpec((None, d, tn), lambda l, j: (l, 0, j)),
            pl.BlockSpec((None, 1, tn), lambda l, j: (l, 0, j)),
        ],
        out_specs=pl.BlockSpec((None, nc, tn), lambda l, j: (l, 0, j)),
        out_shape=jax.ShapeDtypeStruct((depth, nc, n), f32),
        compiler_params=_cparams(("parallel", "parallel")),
        name="modulation",
    )(c_all, w_mod, b_mod.reshape(depth, 1, n))


def _ffn_body(x_ref, mod_ref, pre_ref, post_ref, wg_ref, wu_ref, wo_ref, o_ref, h_scr, acc_scr):
    f = pl.program_id(1)
    d = D_MODEL

    @pl.when(f == 0)
    def _():
        y = _rms(x_ref[...], pre_ref[...])
        h = y * (1.0 + mod_ref[:, d:2 * d]) + mod_ref[:, 0:d]
        h_scr[...] = h.astype(bf16)
        acc_scr[...] = jnp.zeros_like(acc_scr)

    h = h_scr[...]
    gt = _dot(h, wg_ref[...])
    up = _dot(h, wu_ref[...])
    a = gt * jax.nn.sigmoid(gt) * up
    acc_scr[...] += _dot(a.astype(bf16), wo_ref[...])

    @pl.when(f == pl.num_programs(1) - 1)
    def _():
        yn = _rms(acc_scr[...], post_ref[...])
        o_ref[...] = x_ref[...] + FFN_RES * mod_ref[:, 2 * d:3 * d] * yn


def _ffn(x, mod_l, pre, post, w_in5, w_out4, layer, slot, sub, cond):
    t, d = x.shape
    fp = w_out4.shape[2]
    w_half = lambda half: pl.BlockSpec((None, None, None, d, TF), lambda i, f: (layer, slot, half, 0, f))
    return pl.pallas_call(
        _ffn_body,
        grid=(t // TM, fp // TF),
        in_specs=[
            pl.BlockSpec((TM, d), lambda i, f: (i, 0)),
            pl.BlockSpec((None, 1, 3 * d), lambda i, f: (_cond_index(i, TM, cond), 0, sub)),
            pl.BlockSpec((1, d), lambda i, f: (0, 0)),
            pl.BlockSpec((1, d), lambda i, f: (0, 0)),
            w_half(0),
            w_half(1),
            pl.BlockSpec((None, None, TF, d), lambda i, f: (layer, slot, f, 0)),
        ],
        out_specs=pl.BlockSpec((TM, d), lambda i, f: (i, 0)),
        out_shape=jax.ShapeDtypeStruct((t, d), f32),
        scratch_shapes=[pltpu.VMEM((TM, d), bf16), pltpu.VMEM((TM, d), f32)],
        compiler_params=_cparams(("parallel", "arbitrary")),
        name="ffn",
    )(x, mod_l, pre, post, w_in5, w_in5, w_out4)


def _inproj_body(x_ref, mod_ref, pre_ref, w_ref, o_ref, h_scr):
    d = D_MODEL

    @pl.when(pl.program_id(1) == 0)
    def _():
        y = _rms(x_ref[...], pre_ref[...])
        h = y * (1.0 + mod_ref[:, d:2 * d]) + mod_ref[:, 0:d]
        h_scr[...] = h.astype(bf16)

    o_ref[...] = _dot(h_scr[...], w_ref[...]).astype(o_ref.dtype)


def _inproj(x, mod_l, pre, w_in_p, layer, cond, out_dtype):
    t, d = x.shape
    n = w_in_p.shape[2]
    return pl.pallas_call(
        _inproj_body,
        grid=(t // TM, n // TN_IN),
        in_specs=[
            pl.BlockSpec((TM, d), lambda i, j: (i, 0)),
            pl.BlockSpec((None, 1, 3 * d), lambda i, j: (_cond_index(i, TM, cond), 0, 1)),
            pl.BlockSpec((1, d), lambda i, j: (0, 0)),
            pl.BlockSpec((None, d, TN_IN), lambda i, j: (layer, 0, j)),
        ],
        out_specs=pl.BlockSpec((TM, TN_IN), lambda i, j: (i, j)),
        out_shape=jax.ShapeDtypeStruct((t, n), out_dtype),
        scratch_shapes=[pltpu.VMEM((TM, d), bf16)],
        compiler_params=_cparams(("parallel", "arbitrary")),
        name="inproj",
    )(x, mod_l, pre, w_in_p)


POOL_PAD = 8


def _pool_body(u_ref, w_ref, sc_ref, o_ref, pad_scr, *, seq):
    gc = POOL_GC
    zeros = jnp.zeros((POOL_PAD, POOL_WIDTH), f32)
    pad_scr[pl.ds(0, POOL_PAD), :] = zeros
    pad_scr[pl.ds(POOL_PAD + seq, POOL_PAD), :] = zeros
    pad_scr[pl.ds(POOL_PAD, seq), :] = u_ref[...].astype(f32)
    t = lax.broadcasted_iota(jnp.int32, (seq, 1), 0)
    for gi, win in enumerate(POOL_WINDOWS):
        cols = pl.ds(gi * gc, gc)
        lo = jnp.maximum(t - win // 2, 0)
        hi = jnp.minimum(t + win - 1 - win // 2, seq - 1)
        cnt = (hi - lo + 1).astype(f32)
        acc = pad_scr[pl.ds(POOL_PAD - win // 2, seq), cols]
        for j in range(1, win):
            acc = acc + pad_scr[pl.ds(POOL_PAD - win // 2 + j, seq), cols]
        pooled = acc / cnt - pad_scr[pl.ds(POOL_PAD, seq), cols]
        y = _dot(pooled.astype(bf16), w_ref[gi])
        o_ref[:, cols] = (y * sc_ref[:, cols]).astype(o_ref.dtype)


def _pool(p, pool_w, pool_scale, seq, row_block0, nseq):
    cb = OFF_POOL // POOL_WIDTH
    return pl.pallas_call(
        functools.partial(_pool_body, seq=seq),
        grid=(nseq,),
        in_specs=[
            pl.BlockSpec((seq, POOL_WIDTH), lambda s: (row_block0 + s, cb)),
            pl.BlockSpec((POOL_GROUPS, POOL_GC, POOL_GC), lambda s: (0, 0, 0)),
            pl.BlockSpec((1, POOL_WIDTH), lambda s: (0, 0)),
        ],
        out_specs=pl.BlockSpec((seq, POOL_WIDTH), lambda s: (s, 0)),
        out_shape=jax.ShapeDtypeStruct((nseq * seq, POOL_WIDTH), bf16),
        scratch_shapes=[pltpu.VMEM((seq + 2 * POOL_PAD, POOL_WIDTH), f32)],
        compiler_params=_cparams(("parallel",)),
        name="pool",
    )(p, pool_w, pool_scale)


def _softmax_rows(s):
    m = jnp.max(s, axis=-1, keepdims=True)
    e = jnp.exp(s - m)
    return e / jnp.sum(e, axis=-1, keepdims=True)


def _ctx_attn_body(q_ref, k_ref, v_ref, *refs):
    o_ref, nk_ref, nv_ref = refs[-3:]
    hd = NA_HEAD_DIM
    for h in range(NA_HEADS):
        cols = pl.ds(h * hd, hd)
        kf = k_ref[:, cols]
        vf = v_ref[:, cols]
        nk_ref[h] = kf
        nv_ref[h] = vf
        p = _softmax_rows(_dot_nt(q_ref[:, cols].astype(bf16), kf.astype(bf16)) * (hd ** -0.5))
        o_ref[:, cols] = _dot(p.astype(bf16), vf.astype(bf16)).astype(o_ref.dtype)


def _ctx_attn(p, nseq, layer, depth, caches=None):
    spec = lambda off: pl.BlockSpec((SEQ, NA_WIDTH), lambda b: (b, off // NA_WIDTH))
    cache_spec = pl.BlockSpec((None, None, NA_HEADS, SEQ, NA_HEAD_DIM), lambda b: (b, layer, 0, 0, 0))
    cache_shape = jax.ShapeDtypeStruct((nseq, depth, NA_HEADS, SEQ, NA_HEAD_DIM), f32)
    in_specs = [spec(OFF_NQ), spec(OFF_NK), spec(OFF_NV)]
    args = [p, p, p]
    aliases = {}
    if caches is not None:
        in_specs += [pl.BlockSpec(memory_space=pl.ANY)] * 2
        args += list(caches)
        aliases = {3: 1, 4: 2}
    return pl.pallas_call(
        _ctx_attn_body,
        grid=(nseq,),
        in_specs=in_specs,
        out_specs=[pl.BlockSpec((SEQ, NA_WIDTH), lambda b: (b, 0)), cache_spec, cache_spec],
        out_shape=[jax.ShapeDtypeStruct((nseq * SEQ, NA_WIDTH), bf16), cache_shape, cache_shape],
        input_output_aliases=aliases,
        compiler_params=_cparams(("parallel",)),
        name="ctx_attn",
    )(*args)


def _na_bias_table(rpb):
    qc = np.arange(GRID_W)[:, None]
    kc = np.arange(GRID_W)[None, :]
    cs = np.clip(qc - NA_WIN_W // 2, 0, GRID_W - NA_WIN_W)
    ok = (kc >= cs) & (kc < cs + NA_WIN_W)
    cidx = np.clip(kc - qc + NA_WIN_W - 1, 0, 2 * NA_WIN_W - 2)
    onehot = jnp.asarray((cidx[None] == np.arange(2 * NA_WIN_W - 1)[:, None, None]) & ok[None], f32)
    toep = jnp.einsum('...rc,cqk->...rqk', rpb.astype(f32), onehot, precision=lax.Precision.HIGHEST)
    toep = jnp.where(ok, toep, NEG_INF)
    tbl = jnp.stack([toep[..., s:s + NA_WIN_H, :, :] for s in range(NA_WIN_H)], axis=-4)
    tbl = jnp.swapaxes(tbl, -3, -2)
    return tbl.reshape(rpb.shape[:-2] + (NA_WIN_H, GRID_W, NA_WIN_H * GRID_W))


NA_UNROLL = 8
NA_CTX_ROWS = 256


def _na_body(qb_scr, kb_scr, vb_scr, ck_ref, cv_ref, bias_ref, o_ref,
             sl_scr, sc_scr, el_scr, ec_scr, den_scr, oc_scr, *, rows):
    hd = NA_HEAD_DIM
    scale = hd ** -0.5
    kh = min(NA_WIN_H, rows)
    nloc = kh * GRID_W
    n = rows * GRID_W
    assert qb_scr.dtype == bf16
    ck = ck_ref[...].astype(bf16)
    cv = cv_ref[...].astype(bf16)

    def row_slices(r):
        rs = jnp.clip(r - kh // 2, 0, rows - kh)
        q_rows = pl.ds(pl.multiple_of(r * GRID_W, GRID_W), GRID_W)
        k_rows = pl.ds(pl.multiple_of(rs * GRID_W, GRID_W), nloc)
        return rs, q_rows, k_rows

    def ctx_scores(i, carry):
        blk = pl.ds(pl.multiple_of(i * NA_CTX_ROWS, NA_CTX_ROWS), NA_CTX_ROWS)
        sc_scr[blk, :] = _dot_nt(qb_scr[blk, :], ck) * scale
        return carry

    lax.fori_loop(0, n // NA_CTX_ROWS, ctx_scores, 0, unroll=2)

    def loc_scores(r, carry):
        rs, q_rows, k_rows = row_slices(r)
        sl_scr[q_rows, :] = (_dot_nt(qb_scr[q_rows, :], kb_scr[k_rows, :]) * scale
                             + bias_ref[rs - r + NA_WIN_H - 1])
        return carry

    lax.fori_loop(0, rows, loc_scores, 0, unroll=NA_UNROLL)

    def numerators(r, carry):
        q_rows = pl.ds(pl.multiple_of(r * GRID_W, GRID_W), GRID_W)
        s_loc = sl_scr[q_rows, :]
        s_ctx = sc_scr[q_rows, :]
        m = jnp.maximum(jnp.max(s_loc, axis=-1, keepdims=True), jnp.max(s_ctx, axis=-1, keepdims=True))
        e_loc = jnp.exp(s_loc - m)
        e_ctx = jnp.exp(s_ctx - m)
        den = jnp.sum(e_loc, axis=-1, keepdims=True) + jnp.sum(e_ctx, axis=-1, keepdims=True)
        el_scr[q_rows, :] = e_loc.astype(bf16)
        ec_scr[q_rows, :] = e_ctx.astype(bf16)
        den_scr[q_rows, :] = jnp.broadcast_to(den, (GRID_W, hd))
        return carry

    lax.fori_loop(0, rows, numerators, 0, unroll=NA_UNROLL)

    def ctx_values(i, carry):
        blk = pl.ds(pl.multiple_of(i * NA_CTX_ROWS, NA_CTX_ROWS), NA_CTX_ROWS)
        oc_scr[blk, :] = _dot(ec_scr[blk, :], cv)
        return carry

    lax.fori_loop(0, n // NA_CTX_ROWS, ctx_values, 0, unroll=2)

    def loc_values(r, carry):
        _, q_rows, k_rows = row_slices(r)
        o = _dot(el_scr[q_rows, :], vb_scr[k_rows, :]) + oc_scr[q_rows, :]
        o_ref[q_rows, :] = (o / den_scr[q_rows, :]).astype(o_ref.dtype)
        return carry

    lax.fori_loop(0, rows, loc_values, 0, unroll=NA_UNROLL)


def _na_latent(p, cache_k, cache_v, bias_tbl, layer, row_block0, nreq):
    n = DEC_SEQ
    hd = NA_HEAD_DIM
    past = cache_k.shape[3]
    rows = n // GRID_W
    qkv = lambda off: pl.BlockSpec((n, hd), lambda b, h: (row_block0 + b, off // hd + h))
    cache = pl.BlockSpec((None, None, None, past, hd), lambda b, h: (b, layer, h, 0, 0))
    return pl.pallas_call(
        functools.partial(_na_body, rows=rows),
        grid=(nreq, NA_HEADS),
        in_specs=[qkv(OFF_NQ), qkv(OFF_NK), qkv(OFF_NV), cache, cache,
                  pl.BlockSpec((None, None, NA_WIN_H, GRID_W, NA_WIN_H * GRID_W),
                               lambda b, h: (layer, h, 0, 0, 0))],
        out_specs=pl.BlockSpec((n, hd), lambda b, h: (b, h)),
        out_shape=jax.ShapeDtypeStruct((nreq * n, NA_WIDTH), bf16),
        scratch_shapes=[pltpu.VMEM((n, NA_WIN_H * GRID_W), f32), pltpu.VMEM((n, past), f32),
                        pltpu.VMEM((n, NA_WIN_H * GRID_W), bf16), pltpu.VMEM((n, past), bf16),
                        pltpu.VMEM((n, hd), f32), pltpu.VMEM((n, hd), f32)],
        compiler_params=_cparams(("parallel", "parallel")),
        name="na_latent",
    )(p, p, p, cache_k, cache_v, bias_tbl)


GLA_PAD = 32
GLA_UNROLL = 4


def _rope_tables(seq):
    t = np.arange(seq)
    half = GLA_DK // 2
    nf = half // 2
    inv = ROPE_BASE ** (-np.arange(nf, dtype=np.float64) / nf)
    cos, sin = [], []
    for pos in (t // GRID_W, t % GRID_W):
        ang = pos[:, None].astype(np.float64) * inv
        cos += [np.cos(ang), np.cos(ang)]
        sin += [-np.sin(ang), np.sin(ang)]
    return (jnp.asarray(np.concatenate(cos, axis=-1), f32), jnp.asarray(np.concatenate(sin, axis=-1), f32))


def _rope(x, cos, sin_signed):
    nf = GLA_DK // 4
    lane = lax.broadcasted_iota(jnp.int32, x.shape, 1)
    partner = jnp.where(lane % (2 * nf) < nf, pltpu.roll(x, GLA_DK - nf, 1), pltpu.roll(x, nf, 1))
    return x * cos + partner * sin_signed


def _log_sigmoid(x):
    return jnp.minimum(x, 0.0) - jnp.log1p(jnp.exp(-jnp.abs(x)))


def _gla_body(*refs, seq, rope, with_s0, with_sfin, n_carried):
    refs = list(refs)
    q_ref, k_ref, v_ref, r_ref, z_ref, wg_ref, bg_ref, ng_ref = refs[:8]
    refs = refs[8:]
    if rope:
        cos_ref, sin_ref = refs[:2]
        refs = refs[2:]
    if with_s0:
        s0_ref = refs[0]
        refs = refs[1:]
    refs = refs[n_carried:]
    o_ref = refs[0]
    refs = refs[1:]
    if with_sfin:
        sfin_ref = refs[0]
        refs = refs[1:]
    qi_scr, kn_scr, kd_scr, dec_scr, scan_scr, vb_scr, u_scr, sb_scr, o_scr, st_scr = refs

    ch = GLA_CHUNK
    nch = seq // ch
    dk, dv = GLA_DK, GLA_DV

    q = q_ref[...].astype(f32)
    k = k_ref[...].astype(f32)
    if rope:
        q = _rope(q, cos_ref[...], sin_ref[...])
        k = _rope(k, cos_ref[...], sin_ref[...])
    q = q * (dk ** -0.5)

    zb = z_ref[...].astype(bf16)
    pos = lax.broadcasted_iota(jnp.int32, (seq, 1), 0) % ch
    zpad = jnp.zeros((GLA_PAD, dk), f32)
    scan_scr[pl.ds(0, GLA_PAD), :] = zpad
    scan_scr[pl.ds(GLA_PAD + seq, GLA_PAD), :] = zpad
    for d in range(2):
        g = _log_sigmoid(_dot(zb, wg_ref[d]) + bg_ref[d]) / GLA_TAU
        b = g
        sh = 1
        while sh < ch:
            scan_scr[pl.ds(GLA_PAD, seq), :] = b
            if d == 0:
                b = b + jnp.where(pos >= sh, scan_scr[pl.ds(GLA_PAD - sh, seq), :], 0.0)
            else:
                b = b + jnp.where(pos < ch - sh, scan_scr[pl.ds(GLA_PAD + sh, seq), :], 0.0)
            sh *= 2
        b3 = b.reshape(nch, ch, dk)
        b_end = b3[:, ch - 1:ch, :] if d == 0 else b3[:, 0:1, :]
        lanes = pl.ds(d * dk, dk)
        qi_scr[:, lanes] = (q * jnp.exp(b)).astype(bf16)
        kn_scr[d] = (k * jnp.exp(-b)).astype(bf16)
        kd_scr[:, lanes] = (k.reshape(nch, ch, dk) * jnp.exp(b_end - b3)).reshape(seq, dk).astype(bf16)
        dec_scr[d] = jnp.exp(b_end)

    for d in range(2):
        if with_s0:
            st_scr[d] = s0_ref[d].T
        else:
            st_scr[d] = jnp.zeros((dv, dk), f32)

    ri = lax.broadcasted_iota(jnp.int32, (ch, ch), 0)
    ci = lax.broadcasted_iota(jnp.int32, (ch, ch), 1)

    vb_scr[...] = v_ref[...].astype(bf16)
    chunk_rows = lambda c: pl.ds(pl.multiple_of(c * ch, ch), ch)
    fwd, bwd = pl.ds(0, dk), pl.ds(dk, dk)

    def increments(c, carry):
        rows = chunk_rows(c)
        u_scr[c] = lax.dot_general(vb_scr[rows, :], kd_scr[rows, :], (((0,), (0,)), ((), ())),
                                   preferred_element_type=f32)
        return carry

    lax.fori_loop(0, nch, increments, 0, unroll=GLA_UNROLL)

    def states(i, carry):
        for d, c, lanes in ((0, i, fwd), (1, nch - 1 - i, bwd)):
            st = st_scr[d]
            sb_scr[c, :, lanes] = st.astype(bf16)
            st_scr[d] = st * dec_scr[d, c] + u_scr[c, :, lanes]
        return carry

    lax.fori_loop(0, nch, states, 0)

    if with_sfin:
        for d in range(2):
            sfin_ref[d] = st_scr[d].T

    def outputs(c, carry):
        rows = chunk_rows(c)
        qi = qi_scr[rows, :]
        pf = _dot_nt(qi[:, 0:dk], kn_scr[0, rows, :])
        pb = _dot_nt(qi[:, dk:2 * dk], kn_scr[1, rows, :])
        a = jnp.where(ci < ri, pf, jnp.where(ci > ri, pb, pf + pb))
        o_scr[rows, :] = _dot(a.astype(bf16), vb_scr[rows, :]) + _dot_nt(qi, sb_scr[c])
        return carry

    lax.fori_loop(0, nch, outputs, 0, unroll=GLA_UNROLL)

    o = o_scr[...]
    r = r_ref[...].astype(f32)
    o = o * lax.rsqrt(jnp.mean(o * o, axis=-1, keepdims=True) + EPS) * ng_ref[...]
    o_ref[...] = (o * (r * jax.nn.sigmoid(r))).astype(o_ref.dtype)


def _gla(p, wgate_p, b_gate, gla_norm, seq, row_block0, nreq, rope_tabs=None, state=None, layer=0,
         with_sfin=False, depth=1, carried=None):
    dk, dv = GLA_DK, GLA_DV
    nch = seq // GLA_CHUNK
    rope = rope_tabs is not None
    with_s0 = state is not None
    blk = lambda w, off: pl.BlockSpec((seq, w), lambda b, h: (row_block0 + b, off // w + h))
    in_specs = [blk(dk, OFF_GQ), blk(dk, OFF_GK), blk(dv, OFF_GV), blk(dv, OFF_GR),
                pl.BlockSpec((seq, LANE), lambda b, h: (row_block0 + b, OFF_GZ // LANE)),
                pl.BlockSpec((2, LANE, dk), lambda b, h: (0, 0, h)),
                pl.BlockSpec((2, 1, dk), lambda b, h: (0, 0, h)),
                pl.BlockSpec((1, dv), lambda b, h: (0, h))]
    args = [p, p, p, p, p, wgate_p, b_gate.reshape(2, 1, GLA_KW), gla_norm.reshape(1, GLA_VW)]
    if rope:
        in_specs += [pl.BlockSpec((seq, dk), lambda b, h: (0, 0))] * 2
        args += list(rope_tabs)
    if with_s0:
        in_specs.append(pl.BlockSpec((None, None, 2, None, dk, dv), lambda b, h: (b, layer, 0, h, 0, 0)))
        args.append(state)
    aliases = {}
    if carried is not None:
        aliases = {len(args): 1}
        in_specs.append(pl.BlockSpec(memory_space=pl.ANY))
        args.append(carried)
    out_specs = [pl.BlockSpec((seq, dv), lambda b, h: (b, h))]
    out_shape = [jax.ShapeDtypeStruct((nreq * seq, GLA_VW), bf16)]
    if with_sfin:
        out_specs.append(pl.BlockSpec((None, None, 2, None, dk, dv), lambda b, h: (b, layer, 0, h, 0, 0)))
        out_shape.append(jax.ShapeDtypeStruct((nreq, depth, 2, GLA_HEADS, dk, dv), f32))
    scratch = [pltpu.VMEM((seq, 2 * dk), bf16), pltpu.VMEM((2, seq, dk), bf16), pltpu.VMEM((seq, 2 * dk), bf16),
               pltpu.VMEM((2, nch, 1, dk), f32), pltpu.VMEM((seq + 2 * GLA_PAD, dk), f32),
               pltpu.VMEM((seq, dv), bf16), pltpu.VMEM((nch, dv, 2 * dk), f32), pltpu.VMEM((nch, dv, 2 * dk), bf16),
               pltpu.VMEM((seq, dv), f32), pltpu.VMEM((2, dv, dk), f32)]
    return pl.pallas_call(
        functools.partial(_gla_body, seq=seq, rope=rope, with_s0=with_s0, with_sfin=with_sfin,
                          n_carried=len(aliases)),
        grid=(nreq, GLA_HEADS),
        in_specs=in_specs,
        out_specs=out_specs,
        out_shape=out_shape,
        input_output_aliases=aliases,
        scratch_shapes=scratch,
        compiler_params=_cparams(("parallel", "parallel")),
        name="gla",
    )(*args)


TM_MERGE = 512


def _merge_body(x_ref, mod_ref, post_ref, bp_ref, bn_ref, bg_ref, gl_ref, w_ref, o_ref, m_scr, mb_scr):
    n = pl.program_id(1)
    d = D_MODEL

    @pl.when(n == 0)
    def _():
        m_scr[...] = jnp.zeros_like(m_scr)

    for bi, br_ref in enumerate((bp_ref, bn_ref, bg_ref)):
        @pl.when(n == bi)
        def _():
            m_scr[...] += jax.nn.sigmoid(gl_ref[...].astype(f32)) * _dot(br_ref[...], w_ref[...])

    @pl.when(n == N_BRANCH)
    def _():
        mb_scr[...] = m_scr[...].astype(bf16)
        m_scr[...] = _dot(mb_scr[:, 0:BRANCH_W], w_ref[...])

    @pl.when(n == N_BRANCH + 1)
    def _():
        y = m_scr[...] + _dot(mb_scr[:, BRANCH_W:2 * BRANCH_W], w_ref[...])
        o_ref[...] = x_ref[...] + mod_ref[:, 2 * d:3 * d] * _rms(y, post_ref[...])


def _merge(x, mod_l, post, y_pool, y_na, y_gla, p, w_stack, cond):
    t, d = x.shape
    tm = TM_MERGE
    nsteps = N_BRANCH + d // BRANCH_W
    br = pl.BlockSpec((tm, BRANCH_W), lambda i, n: (i, 0))
    return pl.pallas_call(
        _merge_body,
        grid=(t // tm, nsteps),
        in_specs=[
            pl.BlockSpec((tm, d), lambda i, n: (i, 0)),
            pl.BlockSpec((None, 1, 3 * d), lambda i, n: (_cond_index(i, tm, cond), 0, 1)),
            pl.BlockSpec((1, d), lambda i, n: (0, 0)),
            br, br, br,
            pl.BlockSpec((tm, d), lambda i, n: (i, jnp.minimum(n, N_BRANCH - 1))),
            pl.BlockSpec((None, BRANCH_W, d), lambda i, n: (n, 0, 0)),
        ],
        out_specs=pl.BlockSpec((tm, d), lambda i, n: (i, 0)),
        out_shape=jax.ShapeDtypeStruct((t, d), f32),
        scratch_shapes=[pltpu.VMEM((tm, d), f32), pltpu.VMEM((tm, d), bf16)],
        compiler_params=_cparams(("parallel", "arbitrary")),
        name="merge",
    )(x, mod_l, post, y_pool, y_na, y_gla, p, w_stack)


_IN_SPLITS = (POOL_WIDTH, NA_WIDTH, NA_WIDTH, NA_WIDTH, GLA_KW, GLA_KW, GLA_VW, 2 * GLA_RANK, GLA_VW, GATE_W)
_IN_OFFS = tuple(int(v) for v in np.cumsum((0,) + _IN_SPLITS))
_IN_RUNS = ((_IN_OFFS[9], _IN_OFFS[10]), (_IN_OFFS[0], _IN_OFFS[7]), (_IN_OFFS[8], _IN_OFFS[9]),
            (_IN_OFFS[7], _IN_OFFS[8]))


CT_IN = 512
_IN_TILE_STARTS = []
for _a, _b in _IN_RUNS:
    _IN_TILE_STARTS += [_a + CT_IN * _t for _t in range(-(-(_b - _a) // CT_IN))]
assert len(_IN_TILE_STARTS) * CT_IN == IN_COLS_P and all(v % 8 == 0 for v in _IN_TILE_STARTS)
_IN_LAST_VALID = (_IN_RUNS[-1][1] - _IN_RUNS[-1][0]) % CT_IN or CT_IN


assert all((b - a) % CT_IN == 0 for a, b in _IN_RUNS[:-1])


def _w_in_tile_start(j):
    out = jnp.int32(_IN_TILE_STARTS[0]) + CT_IN * j
    for t in range(1, len(_IN_TILE_STARTS)):
        if _IN_TILE_STARTS[t] != _IN_TILE_STARTS[t - 1] + CT_IN:
            out = jnp.where(j >= t, _IN_TILE_STARTS[t] + CT_IN * (j - t), out)
    return out


def _cast_w_in_body(w_ref, o_ref):
    last = pl.program_id(1) == pl.num_programs(1) - 1
    col = lax.broadcasted_iota(jnp.int32, (1, CT_IN), 1)
    valid = jnp.where(last, _IN_LAST_VALID, CT_IN)
    o_ref[...] = jnp.where(col < valid, w_ref[...].T, 0.0).astype(bf16)


def _cast_w_in(w):
    depth, d, n = w.shape
    wt = jnp.swapaxes(w, 1, 2).reshape(depth * n, d)
    return pl.pallas_call(
        _cast_w_in_body,
        grid=(depth, IN_COLS_P // CT_IN),
        in_specs=[pl.BlockSpec((pl.Element(CT_IN), pl.Element(d)), lambda l, j: (pl.multiple_of(l * n + _w_in_tile_start(j), 8), 0))],
        out_specs=pl.BlockSpec((None, d, CT_IN), lambda l, j: (l, 0, j)),
        out_shape=jax.ShapeDtypeStruct((depth, d, IN_COLS_P), bf16),
        compiler_params=_cparams(("parallel", "parallel")),
        name="cast_w_in",
    )(wt)


def _cast_ffn_in_body(w_ref, o_ref):
    pad = jnp.zeros((w_ref.shape[0], D_FF_P - D_FF), bf16)
    for half in range(2):
        o_ref[half, :, 0:D_FF] = w_ref[:, half * D_FF:(half + 1) * D_FF].astype(bf16)
        o_ref[half, :, D_FF:D_FF_P] = pad


def _cast_ffn_in(w):
    depth, ns, d, _ = w.shape
    rows = 256
    return pl.pallas_call(
        _cast_ffn_in_body,
        grid=(depth, ns, d // rows),
        in_specs=[pl.BlockSpec((None, None, rows, 2 * D_FF), lambda l, s, r: (l, s, r, 0))],
        out_specs=pl.BlockSpec((None, None, 2, rows, D_FF_P), lambda l, s, r: (l, s, 0, r, 0)),
        out_shape=jax.ShapeDtypeStruct((depth, ns, 2, d, D_FF_P), bf16),
        compiler_params=_cparams(("parallel", "parallel", "parallel")),
        name="cast_ffn_in",
    )(w)


def _cast_ffn_out_body(w_ref, o_ref):
    row = pl.program_id(2) * TF + lax.broadcasted_iota(jnp.int32, (TF, 1), 0)
    o_ref[...] = jnp.where(row < D_FF, w_ref[...], 0.0).astype(bf16)


def _cast_ffn_out(w):
    depth, ns, _, d = w.shape
    return pl.pallas_call(
        _cast_ffn_out_body,
        grid=(depth, ns, D_FF_P // TF),
        in_specs=[pl.BlockSpec((None, None, TF, d), lambda l, s, j: (l, s, j, 0))],
        out_specs=pl.BlockSpec((None, None, TF, d), lambda l, s, j: (l, s, j, 0)),
        out_shape=jax.ShapeDtypeStruct((depth, ns, D_FF_P, d), bf16),
        compiler_params=_cparams(("parallel", "parallel", "parallel")),
        name="cast_ffn_out",
    )(w)


def _prep_gate(w_gate):
    out = jnp.zeros((2, LANE, GLA_KW), f32)
    for d in range(2):
        out = out.at[d, d * GLA_RANK:(d + 1) * GLA_RANK].set(w_gate[d])
    return out.astype(bf16)


def kernel(x_prompt, x_sample, c, cache_na_k, cache_na_v, state_gla, c_ctx, w_mod, b_mod, norm_pre, norm_post,
           w_ffn_in, w_ffn_out, w_in, pool_w, pool_scale, na_rpb, gla_w_gate, gla_b_gate, gla_norm, w_branch,
           w_out):
    nb, seq, d = x_prompt.shape
    ndec, dseq, _ = x_sample.shape
    depth = w_mod.shape[0]
    n_ctx = nb * seq
    n_lat = ndec * dseq
    assert (seq, dseq, d) == (SEQ, DEC_SEQ, D_MODEL) and n_ctx % TM_MERGE == 0 and n_ctx % TM == 0

    xs = [x_prompt.reshape(n_ctx, d), x_sample.reshape(n_lat, d)]
    conds = [(0, n_ctx), (1, dseq)]
    ncond = -(-(1 + ndec) // 8) * 8
    c_all = jnp.concatenate([c_ctx[None], c, jnp.zeros((ncond - 1 - ndec, d), f32)], axis=0)
    mod = _modulation(c_all, w_mod, b_mod)
    rope_tabs = _rope_tables(dseq)
    bias_tbl = _na_bias_table(na_rpb)

    w_ffn_in5 = _cast_ffn_in(w_ffn_in)
    w_ffn_out4 = _cast_ffn_out(w_ffn_out)
    w_in_p = _cast_w_in(w_in)
    w_stack_all = jnp.concatenate(
        [w_branch.astype(bf16), w_out.reshape(depth, d // BRANCH_W, BRANCH_W, d).astype(bf16)], axis=1)

    caches = None
    new_s = None
    for l in range(depth):
        mod_l = mod[l].reshape(ncond, 1, N_MOD * d)
        pre = norm_pre[l].reshape(3, 1, d)
        post = norm_post[l].reshape(3, 1, d)
        w_stack = w_stack_all[l]
        wgate_p = _prep_gate(gla_w_gate[l])
        pw = pool_w[l].astype(bf16)
        psc = pool_scale[l].reshape(1, POOL_WIDTH)

        xs = [_ffn(x, mod_l, pre[0], post[0], w_ffn_in5, w_ffn_out4, l, 0, 0, cond) for x, cond in zip(xs, conds)]
        p_ctx, p_lat = [_inproj(x, mod_l, pre[1], w_in_p, l, cond, dt) for x, cond, dt in zip(xs, conds, (f32, bf16))]

        y_pool = [_pool(p_ctx, pw, psc, seq, 0, nb), _pool(p_lat, pw, psc, dseq, 0, ndec)]
        na_ctx, new_k, new_v = _ctx_attn(p_ctx, nb, l, depth, caches)
        caches = (new_k, new_v)
        y_na = [na_ctx, _na_latent(p_lat, cache_na_k, cache_na_v, bias_tbl, l, 0, ndec)]
        g_ctx, new_s = _gla(p_ctx, wgate_p, gla_b_gate[l], gla_norm[l], seq, 0, nb, layer=l, with_sfin=True,
                            depth=depth, carried=new_s)
        (g_lat,) = _gla(p_lat, wgate_p, gla_b_gate[l], gla_norm[l], dseq, 0, ndec, rope_tabs=rope_tabs,
                        state=state_gla, layer=l)
        y_gla = [g_ctx, g_lat]

        xs = [_merge(x, mod_l, post[1], yp, yn, yg, p, w_stack, cond)
              for x, yp, yn, yg, p, cond in zip(xs, y_pool, y_na, y_gla, (p_ctx, p_lat), conds)]
        xs = [_ffn(x, mod_l, pre[2], post[2], w_ffn_in5, w_ffn_out4, l, 1, 2, cond) for x, cond in zip(xs, conds)]

    return (xs[0].reshape(nb, seq, d), xs[1].reshape(ndec, dseq, d), caches[0], caches[1], new_s)
```

```python
import functools

import numpy as np
import jax
import jax.numpy as jnp
from jax import lax
from jax.experimental import pallas as pl
from jax.experimental.pallas import tpu as pltpu

f32 = jnp.float32
bf16 = jnp.bfloat16

D_MODEL = 2048
SEQ = 256
DEC_SEQ = 2048
GRID_W = 64
N_MOD = 9
D_FF = 5504
FFN_RES = 0.5
EPS = 1e-6
NEG_INF = -1e30

POOL_GROUPS = 4
POOL_WINDOWS = (2, 4, 8, 16)
POOL_WIDTH = 1024
POOL_GC = POOL_WIDTH // POOL_GROUPS

NA_HEADS = 8
NA_HEAD_DIM = 128
NA_WIDTH = NA_HEADS * NA_HEAD_DIM
NA_WIN_H = 8
NA_WIN_W = 16

GLA_HEADS = 4
GLA_DK = 128
GLA_DV = 256
GLA_KW = GLA_HEADS * GLA_DK
GLA_VW = GLA_HEADS * GLA_DV
GLA_RANK = 16
GLA_TAU = 16.0
GLA_CHUNK = 64
ROPE_BASE = 10000.0

BRANCH_W = 1024
N_BRANCH = 3
GATE_W = N_BRANCH * D_MODEL

LANE = 128
VMEM_LIMIT = 56 * 1024 * 1024

OFF_GL = 0
OFF_POOL = OFF_GL + GATE_W
OFF_NQ = OFF_POOL + POOL_WIDTH
OFF_NK = OFF_NQ + NA_WIDTH
OFF_NV = OFF_NK + NA_WIDTH
OFF_GQ = OFF_NV + NA_WIDTH
OFF_GK = OFF_GQ + GLA_KW
OFF_GV = OFF_GK + GLA_KW
OFF_GR = OFF_GV + GLA_VW
OFF_GZ = OFF_GR + GLA_VW
TN_IN = 1536
IN_COLS_P = -(-(OFF_GZ + LANE) // TN_IN) * TN_IN

TM = 512
TF = 512
D_FF_P = -(-D_FF // TF) * TF


def _cparams(sem):
    return pltpu.CompilerParams(dimension_semantics=sem, vmem_limit_bytes=VMEM_LIMIT)


def _cond_index(i, tm, cond):
    return cond[0] + (i * tm) // cond[1]


def _rms(x, g):
    return x * lax.rsqrt(jnp.mean(x * x, axis=-1, keepdims=True) + EPS) * g


ROW_CHUNK = 16


def _row_sweep(nrows, fn, unroll=4):
    def trip(i, carry):
        fn(pl.ds(pl.multiple_of(i * ROW_CHUNK, ROW_CHUNK), ROW_CHUNK))
        return carry

    lax.fori_loop(0, nrows // ROW_CHUNK, trip, 0, unroll=unroll)


def _row_rsqrt(x_ref, r_scr):
    n = x_ref.shape[1]

    def fn(rows):
        x = x_ref[rows, :]
        ss = jnp.sum(_lane_fold(x * x, jnp.add), axis=-1, keepdims=True)
        r_scr[rows, :] = jnp.broadcast_to(lax.rsqrt(ss * (1.0 / n) + EPS), (ROW_CHUNK, LANE))

    _row_sweep(x_ref.shape[0], fn, unroll=16)


def _lanes(r, n):
    return jnp.concatenate([r] * (n // LANE), axis=1)


def _norm_modulate(x_ref, mod_ref, g_ref, h_ref, r_scr):
    d = D_MODEL
    shift = mod_ref[:, 0:d]
    w = g_ref[...] * (1.0 + mod_ref[:, d:2 * d])
    _row_rsqrt(x_ref, r_scr)

    def fn(rows):
        h_ref[rows, :] = (x_ref[rows, :] * _lanes(r_scr[rows, :], d) * w + shift).astype(h_ref.dtype)

    _row_sweep(x_ref.shape[0], fn)


def _norm_gate_residual(y_ref, x_ref, mod_ref, g_ref, o_ref, r_scr, res_weight):
    d = D_MODEL
    w = (res_weight * mod_ref[:, 2 * d:3 * d]) * g_ref[...]
    _row_rsqrt(y_ref, r_scr)

    def fn(rows):
        o_ref[rows, :] = x_ref[rows, :] + y_ref[rows, :] * _lanes(r_scr[rows, :], d) * w

    _row_sweep(x_ref.shape[0], fn)


def _dot(a, b):
    return jnp.dot(a, b, preferred_element_type=f32)


def _dot_nt(a, b):
    return lax.dot_general(a, b, (((1,), (1,)), ((), ())), preferred_element_type=f32)


def _mod_body(c_ref, w_ref, b_ref, o_ref):
    c = c_ref[...]
    s = c * jax.nn.sigmoid(c)
    o_ref[...] = _dot(s.astype(bf16), w_ref[...].astype(bf16)) + b_ref[...]


def _modulation(c_all, w_mod, b_mod):
    depth, d, n = w_mod.shape
    nc = c_all.shape[0]
    tn = 1024
    return pl.pallas_call(
        _mod_body,
        grid=(depth, n // tn),
        in_specs=[
            pl.BlockSpec((nc, d), lambda l, j: (0, 0)),
            pl.BlockSpec((None, d, tn), lambda l, j: (l, 0, j)),
            pl.BlockSpec((None, 1, tn), lambda l, j: (l, 0, j)),
        ],
        out_specs=pl.BlockSpec((None, nc, tn), lambda l, j: (l, 0, j)),
        out_shape=jax.ShapeDtypeStruct((depth, nc, n), f32),
        compiler_params=_cparams(("parallel", "parallel")),
        name="modulation",
    )(c_all, w_mod, b_mod.reshape(depth, 1, n))


def _ffn_body(x_ref, mod_ref, pre_ref, post_ref, wg_ref, wu_ref, wo_ref, o_ref, h_scr, acc_scr, r_scr):
    f = pl.program_id(1)
    d = D_MODEL

    @pl.when(f == 0)
    def _():
        _norm_modulate(x_ref, mod_ref, pre_ref, h_scr, r_scr)
        acc_scr[...] = jnp.zeros_like(acc_scr)

    h = h_scr[...]
    gt = _dot(h, wg_ref[...])
    up = _dot(h, wu_ref[...])
    a = gt * jax.nn.sigmoid(gt) * up
    acc_scr[...] += _dot(a.astype(bf16), wo_ref[...])

    @pl.when(f == pl.num_programs(1) - 1)
    def _():
        _norm_gate_residual(acc_scr, x_ref, mod_ref, post_ref, o_ref, r_scr, FFN_RES)


def _ffn(x, mod_l, pre, post, w_in5, w_out4, layer, slot, sub, cond):
    t, d = x.shape
    fp = w_out4.shape[2]
    w_half = lambda half: pl.BlockSpec((None, None, None, d, TF), lambda i, f: (layer, slot, half, 0, f))
    return pl.pallas_call(
        _ffn_body,
        grid=(t // TM, fp // TF),
        in_specs=[
            pl.BlockSpec((TM, d), lambda i, f: (i, 0)),
            pl.BlockSpec((None, 1, 3 * d), lambda i, f: (_cond_index(i, TM, cond), 0, sub)),
            pl.BlockSpec((1, d), lambda i, f: (0, 0)),
            pl.BlockSpec((1, d), lambda i, f: (0, 0)),
            w_half(0),
            w_half(1),
            pl.BlockSpec((None, None, TF, d), lambda i, f: (layer, slot, f, 0)),
        ],
        out_specs=pl.BlockSpec((TM, d), lambda i, f: (i, 0)),
        out_shape=jax.ShapeDtypeStruct((t, d), f32),
        scratch_shapes=[pltpu.VMEM((TM, d), bf16), pltpu.VMEM((TM, d), f32), pltpu.VMEM((TM, LANE), f32)],
        compiler_params=_cparams(("parallel", "arbitrary")),
        name="ffn",
    )(x, mod_l, pre, post, w_in5, w_in5, w_out4)


def _inproj_body(x_ref, mod_ref, pre_ref, w_ref, o_ref, h_scr, r_scr):
    d = D_MODEL

    @pl.when(pl.program_id(1) == 0)
    def _():
        _norm_modulate(x_ref, mod_ref, pre_ref, h_scr, r_scr)

    o_ref[...] = _dot(h_scr[...], w_ref[...]).astype(o_ref.dtype)


def _inproj(x, mod_l, pre, w_in_p, layer, cond, out_dtype):
    t, d = x.shape
    n = w_in_p.shape[2]
    return pl.pallas_call(
        _inproj_body,
        grid=(t // TM, n // TN_IN),
        in_specs=[
            pl.BlockSpec((TM, d), lambda i, j: (i, 0)),
            pl.BlockSpec((None, 1, 3 * d), lambda i, j: (_cond_index(i, TM, cond), 0, 1)),
            pl.BlockSpec((1, d), lambda i, j: (0, 0)),
            pl.BlockSpec((None, d, TN_IN), lambda i, j: (layer, 0, j)),
        ],
        out_specs=pl.BlockSpec((TM, TN_IN), lambda i, j: (i, j)),
        out_shape=jax.ShapeDtypeStruct((t, n), out_dtype),
        scratch_shapes=[pltpu.VMEM((TM, d), bf16), pltpu.VMEM((TM, LANE), f32)],
        compiler_params=_cparams(("parallel", "arbitrary")),
        name="inproj",
    )(x, mod_l, pre, w_in_p)


POOL_PAD = 8


def _pool_body(u_ref, w_ref, sc_ref, o_ref, pad_scr, *, seq):
    gc = POOL_GC
    zeros = jnp.zeros((POOL_PAD, POOL_WIDTH), f32)
    pad_scr[pl.ds(0, POOL_PAD), :] = zeros
    pad_scr[pl.ds(POOL_PAD + seq, POOL_PAD), :] = zeros
    pad_scr[pl.ds(POOL_PAD, seq), :] = u_ref[...].astype(f32)
    t = lax.broadcasted_iota(jnp.int32, (seq, 1), 0)
    for gi, win in enumerate(POOL_WINDOWS):
        cols = pl.ds(gi * gc, gc)
        lo = jnp.maximum(t - win // 2, 0)
        hi = jnp.minimum(t + win - 1 - win // 2, seq - 1)
        cnt = (hi - lo + 1).astype(f32)
        acc = pad_scr[pl.ds(POOL_PAD - win // 2, seq), cols]
        for j in range(1, win):
            acc = acc + pad_scr[pl.ds(POOL_PAD - win // 2 + j, seq), cols]
        pooled = acc / cnt - pad_scr[pl.ds(POOL_PAD, seq), cols]
        y = _dot(pooled.astype(bf16), w_ref[gi])
        o_ref[:, cols] = (y * sc_ref[:, cols]).astype(o_ref.dtype)


def _pool(p, pool_w, pool_scale, seq, row_block0, nseq):
    cb = OFF_POOL // POOL_WIDTH
    return pl.pallas_call(
        functools.partial(_pool_body, seq=seq),
        grid=(nseq,),
        in_specs=[
            pl.BlockSpec((seq, POOL_WIDTH), lambda s: (row_block0 + s, cb)),
            pl.BlockSpec((POOL_GROUPS, POOL_GC, POOL_GC), lambda s: (0, 0, 0)),
            pl.BlockSpec((1, POOL_WIDTH), lambda s: (0, 0)),
        ],
        out_specs=pl.BlockSpec((seq, POOL_WIDTH), lambda s: (s, 0)),
        out_shape=jax.ShapeDtypeStruct((nseq * seq, POOL_WIDTH), bf16),
        scratch_shapes=[pltpu.VMEM((seq + 2 * POOL_PAD, POOL_WIDTH), f32)],
        compiler_params=_cparams(("parallel",)),
        name="pool",
    )(p, pool_w, pool_scale)


def _lane_fold(x, op):
    parts = [x[:, i * LANE:(i + 1) * LANE] for i in range(x.shape[1] // LANE)]
    while len(parts) > 1:
        parts = [op(parts[i], parts[i + 1]) for i in range(0, len(parts) - 1, 2)] + parts[len(parts) & ~1:]
    return parts[0]


def _softmax_rows(s):
    m = jnp.max(_lane_fold(s, jnp.maximum), axis=-1, keepdims=True)
    e = jnp.exp(s - m)
    return e / jnp.sum(_lane_fold(e, jnp.add), axis=-1, keepdims=True)


def _ctx_attn_body(q_ref, k_ref, v_ref, *refs):
    o_ref, nk_ref, nv_ref = refs[-3:]
    hd = NA_HEAD_DIM
    for h in range(NA_HEADS):
        cols = pl.ds(h * hd, hd)
        kf = k_ref[:, cols]
        vf = v_ref[:, cols]
        nk_ref[h] = kf
        nv_ref[h] = vf
        p = _softmax_rows(_dot_nt(q_ref[:, cols].astype(bf16), kf.astype(bf16)) * (hd ** -0.5))
        o_ref[:, cols] = _dot(p.astype(bf16), vf.astype(bf16)).astype(o_ref.dtype)


def _ctx_attn(p, nseq, layer, depth, caches=None):
    spec = lambda off: pl.BlockSpec((SEQ, NA_WIDTH), lambda b: (b, off // NA_WIDTH))
    cache_spec = pl.BlockSpec((None, None, NA_HEADS, SEQ, NA_HEAD_DIM), lambda b: (b, layer, 0, 0, 0))
    cache_shape = jax.ShapeDtypeStruct((nseq, depth, NA_HEADS, SEQ, NA_HEAD_DIM), f32)
    in_specs = [spec(OFF_NQ), spec(OFF_NK), spec(OFF_NV)]
    args = [p, p, p]
    aliases = {}
    if caches is not None:
        in_specs += [pl.BlockSpec(memory_space=pl.ANY)] * 2
        args += list(caches)
        aliases = {3: 1, 4: 2}
    return pl.pallas_call(
        _ctx_attn_body,
        grid=(nseq,),
        in_specs=in_specs,
        out_specs=[pl.BlockSpec((SEQ, NA_WIDTH), lambda b: (b, 0)), cache_spec, cache_spec],
        out_shape=[jax.ShapeDtypeStruct((nseq * SEQ, NA_WIDTH), bf16), cache_shape, cache_shape],
        input_output_aliases=aliases,
        compiler_params=_cparams(("parallel",)),
        name="ctx_attn",
    )(*args)


def _na_bias_table(rpb):
    qc = np.arange(GRID_W)[:, None]
    kc = np.arange(GRID_W)[None, :]
    cs = np.clip(qc - NA_WIN_W // 2, 0, GRID_W - NA_WIN_W)
    ok = (kc >= cs) & (kc < cs + NA_WIN_W)
    cidx = np.clip(kc - qc + NA_WIN_W - 1, 0, 2 * NA_WIN_W - 2)
    onehot = jnp.asarray((cidx[None] == np.arange(2 * NA_WIN_W - 1)[:, None, None]) & ok[None], f32)
    toep = jnp.einsum('...rc,cqk->...rqk', rpb.astype(f32), onehot, precision=lax.Precision.HIGHEST)
    toep = jnp.where(ok, toep, NEG_INF)
    tbl = jnp.stack([toep[..., s:s + NA_WIN_H, :, :] for s in range(NA_WIN_H)], axis=-4)
    tbl = jnp.swapaxes(tbl, -3, -2)
    return tbl.reshape(rpb.shape[:-2] + (NA_WIN_H, GRID_W, NA_WIN_H * GRID_W))


NA_UNROLL = 8
NA_CTX_ROWS = 256


def _na_body(qb_scr, kb_scr, vb_scr, ck_ref, cv_ref, bias_ref, o_ref,
             sl_scr, sc_scr, el_scr, ec_scr, den_scr, oc_scr, *, rows):
    hd = NA_HEAD_DIM
    scale = hd ** -0.5
    kh = min(NA_WIN_H, rows)
    nloc = kh * GRID_W
    n = rows * GRID_W
    assert qb_scr.dtype == bf16
    ck = ck_ref[...].astype(bf16)
    cv = cv_ref[...].astype(bf16)

    def row_slices(r):
        rs = jnp.clip(r - kh // 2, 0, rows - kh)
        q_rows = pl.ds(pl.multiple_of(r * GRID_W, GRID_W), GRID_W)
        k_rows = pl.ds(pl.multiple_of(rs * GRID_W, GRID_W), nloc)
        return rs, q_rows, k_rows

    def ctx_scores(i, carry):
        blk = pl.ds(pl.multiple_of(i * NA_CTX_ROWS, NA_CTX_ROWS), NA_CTX_ROWS)
        sc_scr[blk, :] = _dot_nt(qb_scr[blk, :], ck) * scale
        return carry

    lax.fori_loop(0, n // NA_CTX_ROWS, ctx_scores, 0, unroll=2)

    def loc_scores(r, carry):
        rs, q_rows, k_rows = row_slices(r)
        sl_scr[q_rows, :] = (_dot_nt(qb_scr[q_rows, :], kb_scr[k_rows, :]) * scale
                             + bias_ref[rs - r + NA_WIN_H - 1])
        return carry

    lax.fori_loop(0, rows, loc_scores, 0, unroll=NA_UNROLL)

    def numerators(r, carry):
        q_rows = pl.ds(pl.multiple_of(r * GRID_W, GRID_W), GRID_W)
        s_loc = sl_scr[q_rows, :]
        s_ctx = sc_scr[q_rows, :]
        m = jnp.max(jnp.maximum(_lane_fold(s_loc, jnp.maximum), _lane_fold(s_ctx, jnp.maximum)),
                    axis=-1, keepdims=True)
        e_loc = jnp.exp(s_loc - m)
        e_ctx = jnp.exp(s_ctx - m)
        den = jnp.sum(_lane_fold(e_loc, jnp.add) + _lane_fold(e_ctx, jnp.add), axis=-1, keepdims=True)
        el_scr[q_rows, :] = e_loc.astype(bf16)
        ec_scr[q_rows, :] = e_ctx.astype(bf16)
        den_scr[q_rows, :] = jnp.broadcast_to(den, (GRID_W, hd))
        return carry

    lax.fori_loop(0, rows, numerators, 0, unroll=NA_UNROLL)

    def ctx_values(i, carry):
        blk = pl.ds(pl.multiple_of(i * NA_CTX_ROWS, NA_CTX_ROWS), NA_CTX_ROWS)
        oc_scr[blk, :] = _dot(ec_scr[blk, :], cv)
        return carry

    lax.fori_loop(0, n // NA_CTX_ROWS, ctx_values, 0, unroll=2)

    def loc_values(r, carry):
        _, q_rows, k_rows = row_slices(r)
        o = _dot(el_scr[q_rows, :], vb_scr[k_rows, :]) + oc_scr[q_rows, :]
        o_ref[q_rows, :] = (o / den_scr[q_rows, :]).astype(o_ref.dtype)
        return carry

    lax.fori_loop(0, rows, loc_values, 0, unroll=NA_UNROLL)


def _na_latent(p, cache_k, cache_v, bias_tbl, layer, row_block0, nreq):
    n = DEC_SEQ
    hd = NA_HEAD_DIM
    past = cache_k.shape[3]
    rows = n // GRID_W
    qkv = lambda off: pl.BlockSpec((n, hd), lambda b, h: (row_block0 + b, off // hd + h))
    cache = pl.BlockSpec((None, None, None, past, hd), lambda b, h: (b, layer, h, 0, 0))
    return pl.pallas_call(
        functools.partial(_na_body, rows=rows),
        grid=(nreq, NA_HEADS),
        in_specs=[qkv(OFF_NQ), qkv(OFF_NK), qkv(OFF_NV), cache, cache,
                  pl.BlockSpec((None, None, NA_WIN_H, GRID_W, NA_WIN_H * GRID_W),
                               lambda b, h: (layer, h, 0, 0, 0))],
        out_specs=pl.BlockSpec((n, hd), lambda b, h: (b, h)),
        out_shape=jax.ShapeDtypeStruct((nreq * n, NA_WIDTH), bf16),
        scratch_shapes=[pltpu.VMEM((n, NA_WIN_H * GRID_W), f32), pltpu.VMEM((n, past), f32),
                        pltpu.VMEM((n, NA_WIN_H * GRID_W), bf16), pltpu.VMEM((n, past), bf16),
                        pltpu.VMEM((n, hd), f32), pltpu.VMEM((n, hd), f32)],
        compiler_params=_cparams(("parallel", "parallel")),
        name="na_latent",
    )(p, p, p, cache_k, cache_v, bias_tbl)


GLA_PAD = 32
GLA_UNROLL = 4


def _rope_tables(seq):
    t = np.arange(seq)
    half = GLA_DK // 2
    nf = half // 2
    inv = ROPE_BASE ** (-np.arange(nf, dtype=np.float64) / nf)
    cos, sin = [], []
    for pos in (t // GRID_W, t % GRID_W):
        ang = pos[:, None].astype(np.float64) * inv
        cos += [np.cos(ang), np.cos(ang)]
        sin += [-np.sin(ang), np.sin(ang)]
    return (jnp.asarray(np.concatenate(cos, axis=-1), f32), jnp.asarray(np.concatenate(sin, axis=-1), f32))


def _rope(x, cos, sin_signed):
    nf = GLA_DK // 4
    lane = lax.broadcasted_iota(jnp.int32, x.shape, 1)
    partner = jnp.where(lane % (2 * nf) < nf, pltpu.roll(x, GLA_DK - nf, 1), pltpu.roll(x, nf, 1))
    return x * cos + partner * sin_signed


def _log_sigmoid(x):
    return jnp.minimum(x, 0.0) - jnp.log1p(jnp.exp(-jnp.abs(x)))


def _gla_body(*refs, seq, rope, with_s0, with_sfin, n_carried):
    refs = list(refs)
    q_ref, k_ref, v_ref, r_ref, z_ref, wg_ref, bg_ref, ng_ref = refs[:8]
    refs = refs[8:]
    if rope:
        cos_ref, sin_ref = refs[:2]
        refs = refs[2:]
    if with_s0:
        s0_ref = refs[0]
        refs = refs[1:]
    refs = refs[n_carried:]
    o_ref = refs[0]
    refs = refs[1:]
    if with_sfin:
        sfin_ref = refs[0]
        refs = refs[1:]
    qi_scr, kn_scr, kd_scr, dec_scr, scan_scr, vb_scr, u_scr, sb_scr, o_scr, st_scr = refs

    ch = GLA_CHUNK
    nch = seq // ch
    dk, dv = GLA_DK, GLA_DV

    q = q_ref[...].astype(f32)
    k = k_ref[...].astype(f32)
    if rope:
        q = _rope(q, cos_ref[...], sin_ref[...])
        k = _rope(k, cos_ref[...], sin_ref[...])
    q = q * (dk ** -0.5)

    zb = z_ref[...].astype(bf16)
    pos = lax.broadcasted_iota(jnp.int32, (seq, 1), 0) % ch
    zpad = jnp.zeros((GLA_PAD, dk), f32)
    scan_scr[pl.ds(0, GLA_PAD), :] = zpad
    scan_scr[pl.ds(GLA_PAD + seq, GLA_PAD), :] = zpad
    for d in range(2):
        g = _log_sigmoid(_dot(zb, wg_ref[d]) + bg_ref[d]) / GLA_TAU
        b = g
        sh = 1
        while sh < ch:
            scan_scr[pl.ds(GLA_PAD, seq), :] = b
            if d == 0:
                b = b + jnp.where(pos >= sh, scan_scr[pl.ds(GLA_PAD - sh, seq), :], 0.0)
            else:
                b = b + jnp.where(pos < ch - sh, scan_scr[pl.ds(GLA_PAD + sh, seq), :], 0.0)
            sh *= 2
        b3 = b.reshape(nch, ch, dk)
        b_end = b3[:, ch - 1:ch, :] if d == 0 else b3[:, 0:1, :]
        lanes = pl.ds(d * dk, dk)
        qi_scr[:, lanes] = (q * jnp.exp(b)).astype(bf16)
        kn_scr[d] = (k * jnp.exp(-b)).astype(bf16)
        kd_scr[:, lanes] = (k.reshape(nch, ch, dk) * jnp.exp(b_end - b3)).reshape(seq, dk).astype(bf16)
        dec_scr[d] = jnp.exp(b_end)

    for d in range(2):
        if with_s0:
            st_scr[d] = s0_ref[d].T
        else:
            st_scr[d] = jnp.zeros((dv, dk), f32)

    ri = lax.broadcasted_iota(jnp.int32, (ch, ch), 0)
    ci = lax.broadcasted_iota(jnp.int32, (ch, ch), 1)

    vb_scr[...] = v_ref[...].astype(bf16)
    chunk_rows = lambda c: pl.ds(pl.multiple_of(c * ch, ch), ch)
    fwd, bwd = pl.ds(0, dk), pl.ds(dk, dk)

    def increments(c, carry):
        rows = chunk_rows(c)
        u_scr[c] = lax.dot_general(vb_scr[rows, :], kd_scr[rows, :], (((0,), (0,)), ((), ())),
                                   preferred_element_type=f32)
        return carry

    lax.fori_loop(0, nch, increments, 0, unroll=GLA_UNROLL)

    def states(i, carry):
        for d, c, lanes in ((0, i, fwd), (1, nch - 1 - i, bwd)):
            st = st_scr[d]
            sb_scr[c, :, lanes] = st.astype(bf16)
            st_scr[d] = st * dec_scr[d, c] + u_scr[c, :, lanes]
        return carry

    lax.fori_loop(0, nch, states, 0)

    if with_sfin:
        for d in range(2):
            sfin_ref[d] = st_scr[d].T

    def outputs(c, carry):
        rows = chunk_rows(c)
        qi = qi_scr[rows, :]
        pf = _dot_nt(qi[:, 0:dk], kn_scr[0, rows, :])
        pb = _dot_nt(qi[:, dk:2 * dk], kn_scr[1, rows, :])
        a = jnp.where(ci < ri, pf, jnp.where(ci > ri, pb, pf + pb))
        o_scr[rows, :] = _dot(a.astype(bf16), vb_scr[rows, :]) + _dot_nt(qi, sb_scr[c])
        return carry

    lax.fori_loop(0, nch, outputs, 0, unroll=GLA_UNROLL)

    o = o_scr[...]
    r = r_ref[...].astype(f32)
    o = o * lax.rsqrt(jnp.mean(o * o, axis=-1, keepdims=True) + EPS) * ng_ref[...]
    o_ref[...] = (o * (r * jax.nn.sigmoid(r))).astype(o_ref.dtype)


def _gla(p, wgate_p, b_gate, gla_norm, seq, row_block0, nreq, rope_tabs=None, state=None, layer=0,
         with_sfin=False, depth=1, carried=None):
    dk, dv = GLA_DK, GLA_DV
    nch = seq // GLA_CHUNK
    rope = rope_tabs is not None
    with_s0 = state is not None
    blk = lambda w, off: pl.BlockSpec((seq, w), lambda b, h: (row_block0 + b, off // w + h))
    in_specs = [blk(dk, OFF_GQ), blk(dk, OFF_GK), blk(dv, OFF_GV), blk(dv, OFF_GR),
                pl.BlockSpec((seq, LANE), lambda b, h: (row_block0 + b, OFF_GZ // LANE)),
                pl.BlockSpec((2, LANE, dk), lambda b, h: (0, 0, h)),
                pl.BlockSpec((2, 1, dk), lambda b, h: (0, 0, h)),
                pl.BlockSpec((1, dv), lambda b, h: (0, h))]
    args = [p, p, p, p, p, wgate_p, b_gate.reshape(2, 1, GLA_KW), gla_norm.reshape(1, GLA_VW)]
    if rope:
        in_specs += [pl.BlockSpec((seq, dk), lambda b, h: (0, 0))] * 2
        args += list(rope_tabs)
    if with_s0:
        in_specs.append(pl.BlockSpec((None, None, 2, None, dk, dv), lambda b, h: (b, layer, 0, h, 0, 0)))
        args.append(state)
    aliases = {}
    if carried is not None:
        aliases = {len(args): 1}
        in_specs.append(pl.BlockSpec(memory_space=pl.ANY))
        args.append(carried)
    out_specs = [pl.BlockSpec((seq, dv), lambda b, h: (b, h))]
    out_shape = [jax.ShapeDtypeStruct((nreq * seq, GLA_VW), bf16)]
    if with_sfin:
        out_specs.append(pl.BlockSpec((None, None, 2, None, dk, dv), lambda b, h: (b, layer, 0, h, 0, 0)))
        out_shape.append(jax.ShapeDtypeStruct((nreq, depth, 2, GLA_HEADS, dk, dv), f32))
    scratch = [pltpu.VMEM((seq, 2 * dk), bf16), pltpu.VMEM((2, seq, dk), bf16), pltpu.VMEM((seq, 2 * dk), bf16),
               pltpu.VMEM((2, nch, 1, dk), f32), pltpu.VMEM((seq + 2 * GLA_PAD, dk), f32),
               pltpu.VMEM((seq, dv), bf16), pltpu.VMEM((nch, dv, 2 * dk), f32), pltpu.VMEM((nch, dv, 2 * dk), bf16),
               pltpu.VMEM((seq, dv), f32), pltpu.VMEM((2, dv, dk), f32)]
    return pl.pallas_call(
        functools.partial(_gla_body, seq=seq, rope=rope, with_s0=with_s0, with_sfin=with_sfin,
                          n_carried=len(aliases)),
        grid=(nreq, GLA_HEADS),
        in_specs=in_specs,
        out_specs=out_specs,
        out_shape=out_shape,
        input_output_aliases=aliases,
        scratch_shapes=scratch,
        compiler_params=_cparams(("parallel", "parallel")),
        name="gla",
    )(*args)


TM_MERGE = 512


def _merge_body(x_ref, mod_ref, post_ref, bp_ref, bn_ref, bg_ref, gl_ref, w_ref, o_ref, m_scr, mb_scr, r_scr):
    n = pl.program_id(1)
    d = D_MODEL

    @pl.when(n == 0)
    def _():
        m_scr[...] = jnp.zeros_like(m_scr)

    for bi, br_ref in enumerate((bp_ref, bn_ref, bg_ref)):
        @pl.when(n == bi)
        def _():
            m_scr[...] += jax.nn.sigmoid(gl_ref[...].astype(f32)) * _dot(br_ref[...], w_ref[...])

    @pl.when(n == N_BRANCH)
    def _():
        mb_scr[...] = m_scr[...].astype(bf16)
        m_scr[...] = _dot(mb_scr[:, 0:BRANCH_W], w_ref[...])

    @pl.when(n == N_BRANCH + 1)
    def _():
        m_scr[...] += _dot(mb_scr[:, BRANCH_W:2 * BRANCH_W], w_ref[...])
        _norm_gate_residual(m_scr, x_ref, mod_ref, post_ref, o_ref, r_scr, 1.0)


def _merge(x, mod_l, post, y_pool, y_na, y_gla, p, w_stack, cond):
    t, d = x.shape
    tm = TM_MERGE
    nsteps = N_BRANCH + d // BRANCH_W

    def from_step(first):
        return lambda i, n: (jnp.where(n >= first, i, jnp.maximum(i - 1, 0)), 0)

    br = lambda first: pl.BlockSpec((tm, BRANCH_W), from_step(first))
    return pl.pallas_call(
        _merge_body,
        grid=(t // tm, nsteps),
        in_specs=[
            pl.BlockSpec((tm, d), from_step(nsteps - 2)),
            pl.BlockSpec((None, 1, 3 * d), lambda i, n: (_cond_index(i, tm, cond), 0, 1)),
            pl.BlockSpec((1, d), lambda i, n: (0, 0)),
            br(0), br(1), br(2),
            pl.BlockSpec((tm, d), lambda i, n: (i, jnp.minimum(n, N_BRANCH - 1))),
            pl.BlockSpec((None, BRANCH_W, d), lambda i, n: (n, 0, 0)),
        ],
        out_specs=pl.BlockSpec((tm, d), lambda i, n: (i, 0)),
        out_shape=jax.ShapeDtypeStruct((t, d), f32),
        scratch_shapes=[pltpu.VMEM((tm, d), f32), pltpu.VMEM((tm, d), bf16), pltpu.VMEM((tm, LANE), f32)],
        compiler_params=_cparams(("parallel", "arbitrary")),
        name="merge",
    )(x, mod_l, post, y_pool, y_na, y_gla, p, w_stack)


_IN_SPLITS = (POOL_WIDTH, NA_WIDTH, NA_WIDTH, NA_WIDTH, GLA_KW, GLA_KW, GLA_VW, 2 * GLA_RANK, GLA_VW, GATE_W)
_IN_OFFS = tuple(int(v) for v in np.cumsum((0,) + _IN_SPLITS))
_IN_RUNS = ((_IN_OFFS[9], _IN_OFFS[10]), (_IN_OFFS[0], _IN_OFFS[7]), (_IN_OFFS[8], _IN_OFFS[9]),
            (_IN_OFFS[7], _IN_OFFS[8]))


CT_IN = 512
_IN_TILE_STARTS = []
for _a, _b in _IN_RUNS:
    _IN_TILE_STARTS += [_a + CT_IN * _t for _t in range(-(-(_b - _a) // CT_IN))]
assert len(_IN_TILE_STARTS) * CT_IN == IN_COLS_P and all(v % 8 == 0 for v in _IN_TILE_STARTS)
_IN_LAST_VALID = (_IN_RUNS[-1][1] - _IN_RUNS[-1][0]) % CT_IN or CT_IN


assert all((b - a) % CT_IN == 0 for a, b in _IN_RUNS[:-1])


def _w_in_tile_start(j):
    out = jnp.int32(_IN_TILE_STARTS[0]) + CT_IN * j
    for t in range(1, len(_IN_TILE_STARTS)):
        if _IN_TILE_STARTS[t] != _IN_TILE_STARTS[t - 1] + CT_IN:
            out = jnp.where(j >= t, _IN_TILE_STARTS[t] + CT_IN * (j - t), out)
    return out


def _cast_w_in_body(w_ref, o_ref):
    last = pl.program_id(1) == pl.num_programs(1) - 1
    col = lax.broadcasted_iota(jnp.int32, (1, CT_IN), 1)
    valid = jnp.where(last, _IN_LAST_VALID, CT_IN)
    o_ref[...] = jnp.where(col < valid, w_ref[...].T, 0.0).astype(bf16)


def _cast_w_in(w):
    depth, d, n = w.shape
    wt = jnp.swapaxes(w, 1, 2).reshape(depth * n, d)
    return pl.pallas_call(
        _cast_w_in_body,
        grid=(depth, IN_COLS_P // CT_IN),
        in_specs=[pl.BlockSpec((pl.Element(CT_IN), pl.Element(d)), lambda l, j: (pl.multiple_of(l * n + _w_in_tile_start(j), 8), 0))],
        out_specs=pl.BlockSpec((None, d, CT_IN), lambda l, j: (l, 0, j)),
        out_shape=jax.ShapeDtypeStruct((depth, d, IN_COLS_P), bf16),
        compiler_params=_cparams(("parallel", "parallel")),
        name="cast_w_in",
    )(wt)


def _cast_ffn_in_body(w_ref, o_ref):
    pad = jnp.zeros((w_ref.shape[0], D_FF_P - D_FF), bf16)
    for half in range(2):
        o_ref[half, :, 0:D_FF] = w_ref[:, half * D_FF:(half + 1) * D_FF].astype(bf16)
        o_ref[half, :, D_FF:D_FF_P] = pad


def _cast_ffn_in(w):
    depth, ns, d, _ = w.shape
    rows = 256
    return pl.pallas_call(
        _cast_ffn_in_body,
        grid=(depth, ns, d // rows),
        in_specs=[pl.BlockSpec((None, None, rows, 2 * D_FF), lambda l, s, r: (l, s, r, 0))],
        out_specs=pl.BlockSpec((None, None, 2, rows, D_FF_P), lambda l, s, r: (l, s, 0, r, 0)),
        out_shape=jax.ShapeDtypeStruct((depth, ns, 2, d, D_FF_P), bf16),
        compiler_params=_cparams(("parallel", "parallel", "parallel")),
        name="cast_ffn_in",
    )(w)


def _cast_ffn_out_body(w_ref, o_ref):
    row = pl.program_id(2) * TF + lax.broadcasted_iota(jnp.int32, (TF, 1), 0)
    o_ref[...] = jnp.where(row < D_FF, w_ref[...], 0.0).astype(bf16)


def _cast_ffn_out(w):
    depth, ns, _, d = w.shape
    return pl.pallas_call(
        _cast_ffn_out_body,
        grid=(depth, ns, D_FF_P // TF),
        in_specs=[pl.BlockSpec((None, None, TF, d), lambda l, s, j: (l, s, j, 0))],
        out_specs=pl.BlockSpec((None, None, TF, d), lambda l, s, j: (l, s, j, 0)),
        out_shape=jax.ShapeDtypeStruct((depth, ns, D_FF_P, d), bf16),
        compiler_params=_cparams(("parallel", "parallel", "parallel")),
        name="cast_ffn_out",
    )(w)


def _prep_gate(w_gate):
    out = jnp.zeros((2, LANE, GLA_KW), f32)
    for d in range(2):
        out = out.at[d, d * GLA_RANK:(d + 1) * GLA_RANK].set(w_gate[d])
    return out.astype(bf16)


def kernel(x_prompt, x_sample, c, cache_na_k, cache_na_v, state_gla, c_ctx, w_mod, b_mod, norm_pre, norm_post,
           w_ffn_in, w_ffn_out, w_in, pool_w, pool_scale, na_rpb, gla_w_gate, gla_b_gate, gla_norm, w_branch,
           w_out):
    nb, seq, d = x_prompt.shape
    ndec, dseq, _ = x_sample.shape
    depth = w_mod.shape[0]
    n_ctx = nb * seq
    n_lat = ndec * dseq
    assert (seq, dseq, d) == (SEQ, DEC_SEQ, D_MODEL) and n_ctx % TM_MERGE == 0 and n_ctx % TM == 0

    xs = [x_prompt.reshape(n_ctx, d), x_sample.reshape(n_lat, d)]
    conds = [(0, n_ctx), (1, dseq)]
    ncond = -(-(1 + ndec) // 8) * 8
    c_all = jnp.concatenate([c_ctx[None], c, jnp.zeros((ncond - 1 - ndec, d), f32)], axis=0)
    mod = _modulation(c_all, w_mod, b_mod)
    rope_tabs = _rope_tables(dseq)
    bias_tbl = _na_bias_table(na_rpb)

    w_ffn_in5 = _cast_ffn_in(w_ffn_in)
    w_ffn_out4 = _cast_ffn_out(w_ffn_out)
    w_in_p = _cast_w_in(w_in)
    w_stack_all = jnp.concatenate(
        [w_branch.astype(bf16), w_out.reshape(depth, d // BRANCH_W, BRANCH_W, d).astype(bf16)], axis=1)

    caches = None
    new_s = None
    for l in range(depth):
        mod_l = mod[l].reshape(ncond, 1, N_MOD * d)
        pre = norm_pre[l].reshape(3, 1, d)
        post = norm_post[l].reshape(3, 1, d)
        w_stack = w_stack_all[l]
        wgate_p = _prep_gate(gla_w_gate[l])
        pw = pool_w[l].astype(bf16)
        psc = pool_scale[l].reshape(1, POOL_WIDTH)

        xs = [_ffn(x, mod_l, pre[0], post[0], w_ffn_in5, w_ffn_out4, l, 0, 0, cond) for x, cond in zip(xs, conds)]
        p_ctx, p_lat = [_inproj(x, mod_l, pre[1], w_in_p, l, cond, dt) for x, cond, dt in zip(xs, conds, (f32, bf16))]

        y_pool = [_pool(p_ctx, pw, psc, seq, 0, nb), _pool(p_lat, pw, psc, dseq, 0, ndec)]
        na_ctx, new_k, new_v = _ctx_attn(p_ctx, nb, l, depth, caches)
        caches = (new_k, new_v)
        y_na = [na_ctx, _na_latent(p_lat, cache_na_k, cache_na_v, bias_tbl, l, 0, ndec)]
        g_ctx, new_s = _gla(p_ctx, wgate_p, gla_b_gate[l], gla_norm[l], seq, 0, nb, layer=l, with_sfin=True,
                            depth=depth, carried=new_s)
        (g_lat,) = _gla(p_lat, wgate_p, gla_b_gate[l], gla_norm[l], dseq, 0, ndec, rope_tabs=rope_tabs,
                        state=state_gla, layer=l)
        y_gla = [g_ctx, g_lat]

        xs = [_merge(x, mod_l, post[1], yp, yn, yg, p, w_stack, cond)
              for x, yp, yn, yg, p, cond in zip(xs, y_pool, y_na, y_gla, (p_ctx, p_lat), conds)]
        xs = [_ffn(x, mod_l, pre[2], post[2], w_ffn_in5, w_ffn_out4, l, 1, 2, cond) for x, cond in zip(xs, conds)]

    return (xs[0].reshape(nb, seq, d), xs[1].reshape(ndec, dseq, d), caches[0], caches[1], new_s)
```

```python
import functools

import numpy as np
import jax
import jax.numpy as jnp
from jax import lax
from jax.experimental import pallas as pl
from jax.experimental.pallas import tpu as pltpu

f32 = jnp.float32
bf16 = jnp.bfloat16

D_MODEL = 2048
SEQ = 256
DEC_SEQ = 2048
GRID_W = 64
N_MOD = 9
D_FF = 5504
FFN_RES = 0.5
EPS = 1e-6
NEG_INF = -1e30

POOL_GROUPS = 4
POOL_WINDOWS = (2, 4, 8, 16)
POOL_WIDTH = 1024
POOL_GC = POOL_WIDTH // POOL_GROUPS

NA_HEADS = 8
NA_HEAD_DIM = 128
NA_WIDTH = NA_HEADS * NA_HEAD_DIM
NA_WIN_H = 8
NA_WIN_W = 16

GLA_HEADS = 4
GLA_DK = 128
GLA_DV = 256
GLA_KW = GLA_HEADS * GLA_DK
GLA_VW = GLA_HEADS * GLA_DV
GLA_RANK = 16
GLA_TAU = 16.0
GLA_CHUNK = 64
ROPE_BASE = 10000.0

BRANCH_W = 1024
N_BRANCH = 3
GATE_W = N_BRANCH * D_MODEL

LANE = 128
VMEM_LIMIT = 56 * 1024 * 1024

OFF_GL = 0
OFF_POOL = OFF_GL + GATE_W
OFF_NQ = OFF_POOL + POOL_WIDTH
OFF_NK = OFF_NQ + NA_WIDTH
OFF_NV = OFF_NK + NA_WIDTH
OFF_GQ = OFF_NV + NA_WIDTH
OFF_GK = OFF_GQ + GLA_KW
OFF_GV = OFF_GK + GLA_KW
OFF_GR = OFF_GV + GLA_VW
OFF_GZ = OFF_GR + GLA_VW
TN_IN = 2304
IN_COLS_P = -(-(OFF_GZ + LANE) // TN_IN) * TN_IN

TM = 512
TF = 512
D_FF_P = -(-D_FF // TF) * TF


def _cparams(sem):
    return pltpu.CompilerParams(dimension_semantics=sem, vmem_limit_bytes=VMEM_LIMIT)


def _cond_index(i, tm, cond):
    return cond[0] + (i * tm) // cond[1]


def _rms(x, g):
    return x * lax.rsqrt(jnp.mean(x * x, axis=-1, keepdims=True) + EPS) * g


ROW_CHUNK = 16


def _row_sweep(nrows, fn, unroll=4):
    def trip(i, carry):
        fn(pl.ds(pl.multiple_of(i * ROW_CHUNK, ROW_CHUNK), ROW_CHUNK))
        return carry

    lax.fori_loop(0, nrows // ROW_CHUNK, trip, 0, unroll=unroll)


def _row_rsqrt(x_ref, r_scr):
    n = x_ref.shape[1]

    def fn(rows):
        x = x_ref[rows, :]
        ss = jnp.sum(_lane_fold(x * x, jnp.add), axis=-1, keepdims=True)
        r_scr[rows, :] = jnp.broadcast_to(lax.rsqrt(ss * (1.0 / n) + EPS), (ROW_CHUNK, LANE))

    _row_sweep(x_ref.shape[0], fn, unroll=16)


def _lanes(r, n):
    return jnp.concatenate([r] * (n // LANE), axis=1)


def _norm_modulate(x_ref, mod_ref, g_ref, h_ref, r_scr):
    d = D_MODEL
    shift = mod_ref[:, 0:d]
    w = g_ref[...] * (1.0 + mod_ref[:, d:2 * d])
    _row_rsqrt(x_ref, r_scr)

    def fn(rows):
        h_ref[rows, :] = (x_ref[rows, :] * _lanes(r_scr[rows, :], d) * w + shift).astype(h_ref.dtype)

    _row_sweep(x_ref.shape[0], fn)


def _norm_gate_residual(y_ref, x_ref, mod_ref, g_ref, o_ref, r_scr, res_weight):
    d = D_MODEL
    w = (res_weight * mod_ref[:, 2 * d:3 * d]) * g_ref[...]
    _row_rsqrt(y_ref, r_scr)

    def fn(rows):
        o_ref[rows, :] = x_ref[rows, :] + y_ref[rows, :] * _lanes(r_scr[rows, :], d) * w

    _row_sweep(x_ref.shape[0], fn)


def _dot(a, b):
    return jnp.dot(a, b, preferred_element_type=f32)


def _dot_nt(a, b):
    return lax.dot_general(a, b, (((1,), (1,)), ((), ())), preferred_element_type=f32)


def _mod_body(c_ref, w_ref, b_ref, o_ref):
    c = c_ref[...]
    s = c * jax.nn.sigmoid(c)
    o_ref[...] = _dot(s.astype(bf16), w_ref[...].astype(bf16)) + b_ref[...]


def _modulation(c_all, w_mod, b_mod):
    depth, d, n = w_mod.shape
    nc = c_all.shape[0]
    tn = 1024
    return pl.pallas_call(
        _mod_body,
        grid=(depth, n // tn),
        in_specs=[
            pl.BlockSpec((nc, d), lambda l, j: (0, 0)),
            pl.BlockSpec((None, d, tn), lambda l, j: (l, 0, j)),
            pl.BlockSpec((None, 1, tn), lambda l, j: (l, 0, j)),
        ],
        out_specs=pl.BlockSpec((None, nc, tn), lambda l, j: (l, 0, j)),
        out_shape=jax.ShapeDtypeStruct((depth, nc, n), f32),
        compiler_params=_cparams(("parallel", "parallel")),
        name="modulation",
    )(c_all, w_mod, b_mod.reshape(depth, 1, n))


def _ffn_body(x_ref, mod_ref, pre_ref, post_ref, wg_ref, wu_ref, wo_ref, o_ref, h_scr, acc_scr, r_scr):
    f = pl.program_id(1)
    d = D_MODEL

    @pl.when(f == 0)
    def _():
        _norm_modulate(x_ref, mod_ref, pre_ref, h_scr, r_scr)
        acc_scr[...] = jnp.zeros_like(acc_scr)

    h = h_scr[...]
    gt = _dot(h, wg_ref[...])
    up = _dot(h, wu_ref[...])
    a = gt * jax.nn.sigmoid(gt) * up
    acc_scr[...] += _dot(a.astype(bf16), wo_ref[...])

    @pl.when(f == pl.num_programs(1) - 1)
    def _():
        _norm_gate_residual(acc_scr, x_ref, mod_ref, post_ref, o_ref, r_scr, FFN_RES)


def _ffn(x, mod_l, pre, post, w_in5, w_out4, layer, slot, sub, cond):
    t, d = x.shape
    fp = w_out4.shape[2]
    w_half = lambda half: pl.BlockSpec((None, None, None, d, TF), lambda i, f: (layer, slot, half, 0, f))
    return pl.pallas_call(
        _ffn_body,
        grid=(t // TM, fp // TF),
        in_specs=[
            pl.BlockSpec((TM, d), lambda i, f: (i, 0)),
            pl.BlockSpec((None, 1, 3 * d), lambda i, f: (_cond_index(i, TM, cond), 0, sub)),
            pl.BlockSpec((1, d), lambda i, f: (0, 0)),
            pl.BlockSpec((1, d), lambda i, f: (0, 0)),
            w_half(0),
            w_half(1),
            pl.BlockSpec((None, None, TF, d), lambda i, f: (layer, slot, f, 0)),
        ],
        out_specs=pl.BlockSpec((TM, d), lambda i, f: (i, 0)),
        out_shape=jax.ShapeDtypeStruct((t, d), f32),
        scratch_shapes=[pltpu.VMEM((TM, d), bf16), pltpu.VMEM((TM, d), f32), pltpu.VMEM((TM, LANE), f32)],
        compiler_params=_cparams(("parallel", "arbitrary")),
        name="ffn",
    )(x, mod_l, pre, post, w_in5, w_in5, w_out4)


def _inproj_body(x_ref, mod_ref, pre_ref, w_ref, o_ref, h_scr, r_scr):
    d = D_MODEL

    @pl.when(pl.program_id(1) == 0)
    def _():
        _norm_modulate(x_ref, mod_ref, pre_ref, h_scr, r_scr)

    o_ref[...] = _dot(h_scr[...], w_ref[...]).astype(o_ref.dtype)


def _inproj(x, mod_l, pre, w_in_p, layer, cond, out_dtype):
    t, d = x.shape
    n = w_in_p.shape[2]
    return pl.pallas_call(
        _inproj_body,
        grid=(t // TM, n // TN_IN),
        in_specs=[
            pl.BlockSpec((TM, d), lambda i, j: (i, 0)),
            pl.BlockSpec((None, 1, 3 * d), lambda i, j: (_cond_index(i, TM, cond), 0, 1)),
            pl.BlockSpec((1, d), lambda i, j: (0, 0)),
            pl.BlockSpec((None, d, TN_IN), lambda i, j: (layer, 0, j)),
        ],
        out_specs=pl.BlockSpec((TM, TN_IN), lambda i, j: (i, j)),
        out_shape=jax.ShapeDtypeStruct((t, n), out_dtype),
        scratch_shapes=[pltpu.VMEM((TM, d), bf16), pltpu.VMEM((TM, LANE), f32)],
        compiler_params=_cparams(("parallel", "arbitrary")),
        name="inproj",
    )(x, mod_l, pre, w_in_p)


POOL_PAD = 8


def _pool_body(u_ref, w_ref, sc_ref, o_ref, pad_scr, *, seq):
    gc = POOL_GC
    zeros = jnp.zeros((POOL_PAD, POOL_WIDTH), f32)
    pad_scr[pl.ds(0, POOL_PAD), :] = zeros
    pad_scr[pl.ds(POOL_PAD + seq, POOL_PAD), :] = zeros
    pad_scr[pl.ds(POOL_PAD, seq), :] = u_ref[...].astype(f32)
    t = lax.broadcasted_iota(jnp.int32, (seq, 1), 0)
    for gi, win in enumerate(POOL_WINDOWS):
        cols = pl.ds(gi * gc, gc)
        lo = jnp.maximum(t - win // 2, 0)
        hi = jnp.minimum(t + win - 1 - win // 2, seq - 1)
        cnt = (hi - lo + 1).astype(f32)
        acc = pad_scr[pl.ds(POOL_PAD - win // 2, seq), cols]
        for j in range(1, win):
            acc = acc + pad_scr[pl.ds(POOL_PAD - win // 2 + j, seq), cols]
        pooled = acc / cnt - pad_scr[pl.ds(POOL_PAD, seq), cols]
        y = _dot(pooled.astype(bf16), w_ref[gi])
        o_ref[:, cols] = (y * sc_ref[:, cols]).astype(o_ref.dtype)


def _pool(p, pool_w, pool_scale, seq, row_block0, nseq):
    cb = OFF_POOL // POOL_WIDTH
    return pl.pallas_call(
        functools.partial(_pool_body, seq=seq),
        grid=(nseq,),
        in_specs=[
            pl.BlockSpec((seq, POOL_WIDTH), lambda s: (row_block0 + s, cb)),
            pl.BlockSpec((POOL_GROUPS, POOL_GC, POOL_GC), lambda s: (0, 0, 0)),
            pl.BlockSpec((1, POOL_WIDTH), lambda s: (0, 0)),
        ],
        out_specs=pl.BlockSpec((seq, POOL_WIDTH), lambda s: (s, 0)),
        out_shape=jax.ShapeDtypeStruct((nseq * seq, POOL_WIDTH), bf16),
        scratch_shapes=[pltpu.VMEM((seq + 2 * POOL_PAD, POOL_WIDTH), f32)],
        compiler_params=_cparams(("parallel",)),
        name="pool",
    )(p, pool_w, pool_scale)


def _lane_fold(x, op):
    parts = [x[:, i * LANE:(i + 1) * LANE] for i in range(x.shape[1] // LANE)]
    while len(parts) > 1:
        parts = [op(parts[i], parts[i + 1]) for i in range(0, len(parts) - 1, 2)] + parts[len(parts) & ~1:]
    return parts[0]


def _softmax_rows(s):
    m = jnp.max(_lane_fold(s, jnp.maximum), axis=-1, keepdims=True)
    e = jnp.exp(s - m)
    return e / jnp.sum(_lane_fold(e, jnp.add), axis=-1, keepdims=True)


def _ctx_attn_body(q_ref, k_ref, v_ref, *refs):
    o_ref, nk_ref, nv_ref = refs[-3:]
    hd = NA_HEAD_DIM
    for h in range(NA_HEADS):
        cols = pl.ds(h * hd, hd)
        kf = k_ref[:, cols]
        vf = v_ref[:, cols]
        nk_ref[h] = kf
        nv_ref[h] = vf
        p = _softmax_rows(_dot_nt(q_ref[:, cols].astype(bf16), kf.astype(bf16)) * (hd ** -0.5))
        o_ref[:, cols] = _dot(p.astype(bf16), vf.astype(bf16)).astype(o_ref.dtype)


def _ctx_attn(p, nseq, layer, depth, caches=None):
    spec = lambda off: pl.BlockSpec((SEQ, NA_WIDTH), lambda b: (b, off // NA_WIDTH))
    cache_spec = pl.BlockSpec((None, None, NA_HEADS, SEQ, NA_HEAD_DIM), lambda b: (b, layer, 0, 0, 0))
    cache_shape = jax.ShapeDtypeStruct((nseq, depth, NA_HEADS, SEQ, NA_HEAD_DIM), f32)
    in_specs = [spec(OFF_NQ), spec(OFF_NK), spec(OFF_NV)]
    args = [p, p, p]
    aliases = {}
    if caches is not None:
        in_specs += [pl.BlockSpec(memory_space=pl.ANY)] * 2
        args += list(caches)
        aliases = {3: 1, 4: 2}
    return pl.pallas_call(
        _ctx_attn_body,
        grid=(nseq,),
        in_specs=in_specs,
        out_specs=[pl.BlockSpec((SEQ, NA_WIDTH), lambda b: (b, 0)), cache_spec, cache_spec],
        out_shape=[jax.ShapeDtypeStruct((nseq * SEQ, NA_WIDTH), bf16), cache_shape, cache_shape],
        input_output_aliases=aliases,
        compiler_params=_cparams(("parallel",)),
        name="ctx_attn",
    )(*args)


def _na_bias_table(rpb):
    qc = np.arange(GRID_W)[:, None]
    kc = np.arange(GRID_W)[None, :]
    cs = np.clip(qc - NA_WIN_W // 2, 0, GRID_W - NA_WIN_W)
    ok = (kc >= cs) & (kc < cs + NA_WIN_W)
    cidx = np.clip(kc - qc + NA_WIN_W - 1, 0, 2 * NA_WIN_W - 2)
    onehot = jnp.asarray((cidx[None] == np.arange(2 * NA_WIN_W - 1)[:, None, None]) & ok[None], f32)
    toep = jnp.einsum('...rc,cqk->...rqk', rpb.astype(f32), onehot, precision=lax.Precision.HIGHEST)
    toep = jnp.where(ok, toep, NEG_INF)
    tbl = jnp.stack([toep[..., s:s + NA_WIN_H, :, :] for s in range(NA_WIN_H)], axis=-4)
    tbl = jnp.swapaxes(tbl, -3, -2)
    return tbl.reshape(rpb.shape[:-2] + (NA_WIN_H, GRID_W, NA_WIN_H * GRID_W))


NA_UNROLL = 8
NA_CTX_ROWS = 256


def _na_body(qb_scr, kb_scr, vb_scr, ck_ref, cv_ref, bias_ref, o_ref,
             sl_scr, sc_scr, el_scr, ec_scr, den_scr, oc_scr, *, rows):
    hd = NA_HEAD_DIM
    scale = hd ** -0.5
    kh = min(NA_WIN_H, rows)
    nloc = kh * GRID_W
    n = rows * GRID_W
    assert qb_scr.dtype == bf16
    ck = ck_ref[...].astype(bf16)
    cv = cv_ref[...].astype(bf16)

    def row_slices(r):
        rs = jnp.clip(r - kh // 2, 0, rows - kh)
        q_rows = pl.ds(pl.multiple_of(r * GRID_W, GRID_W), GRID_W)
        k_rows = pl.ds(pl.multiple_of(rs * GRID_W, GRID_W), nloc)
        return rs, q_rows, k_rows

    def ctx_scores(i, carry):
        blk = pl.ds(pl.multiple_of(i * NA_CTX_ROWS, NA_CTX_ROWS), NA_CTX_ROWS)
        sc_scr[blk, :] = _dot_nt(qb_scr[blk, :], ck) * scale
        return carry

    lax.fori_loop(0, n // NA_CTX_ROWS, ctx_scores, 0, unroll=2)

    def loc_scores(r, carry):
        rs, q_rows, k_rows = row_slices(r)
        sl_scr[q_rows, :] = (_dot_nt(qb_scr[q_rows, :], kb_scr[k_rows, :]) * scale
                             + bias_ref[rs - r + NA_WIN_H - 1])
        return carry

    lax.fori_loop(0, rows, loc_scores, 0, unroll=NA_UNROLL)

    def numerators(r, carry):
        q_rows = pl.ds(pl.multiple_of(r * GRID_W, GRID_W), GRID_W)
        s_loc = sl_scr[q_rows, :]
        s_ctx = sc_scr[q_rows, :]
        m = jnp.max(jnp.maximum(_lane_fold(s_loc, jnp.maximum), _lane_fold(s_ctx, jnp.maximum)),
                    axis=-1, keepdims=True)
        e_loc = jnp.exp(s_loc - m)
        e_ctx = jnp.exp(s_ctx - m)
        den = jnp.sum(_lane_fold(e_loc, jnp.add) + _lane_fold(e_ctx, jnp.add), axis=-1, keepdims=True)
        el_scr[q_rows, :] = e_loc.astype(bf16)
        ec_scr[q_rows, :] = e_ctx.astype(bf16)
        den_scr[q_rows, :] = jnp.broadcast_to(den, (GRID_W, hd))
        return carry

    lax.fori_loop(0, rows, numerators, 0, unroll=NA_UNROLL)

    def ctx_values(i, carry):
        blk = pl.ds(pl.multiple_of(i * NA_CTX_ROWS, NA_CTX_ROWS), NA_CTX_ROWS)
        oc_scr[blk, :] = _dot(ec_scr[blk, :], cv)
        return carry

    lax.fori_loop(0, n // NA_CTX_ROWS, ctx_values, 0, unroll=2)

    def loc_values(r, carry):
        _, q_rows, k_rows = row_slices(r)
        o = _dot(el_scr[q_rows, :], vb_scr[k_rows, :]) + oc_scr[q_rows, :]
        o_ref[q_rows, :] = (o / den_scr[q_rows, :]).astype(o_ref.dtype)
        return carry

    lax.fori_loop(0, rows, loc_values, 0, unroll=NA_UNROLL)


def _na_latent(p, cache_k, cache_v, bias_tbl, layer, row_block0, nreq):
    n = DEC_SEQ
    hd = NA_HEAD_DIM
    past = cache_k.shape[3]
    rows = n // GRID_W
    qkv = lambda off: pl.BlockSpec((n, hd), lambda b, h: (row_block0 + b, off // hd + h))
    cache = pl.BlockSpec((None, None, None, past, hd), lambda b, h: (b, layer, h, 0, 0))
    return pl.pallas_call(
        functools.partial(_na_body, rows=rows),
        grid=(nreq, NA_HEADS),
        in_specs=[qkv(OFF_NQ), qkv(OFF_NK), qkv(OFF_NV), cache, cache,
                  pl.BlockSpec((None, None, NA_WIN_H, GRID_W, NA_WIN_H * GRID_W),
                               lambda b, h: (layer, h, 0, 0, 0))],
        out_specs=pl.BlockSpec((n, hd), lambda b, h: (b, h)),
        out_shape=jax.ShapeDtypeStruct((nreq * n, NA_WIDTH), bf16),
        scratch_shapes=[pltpu.VMEM((n, NA_WIN_H * GRID_W), f32), pltpu.VMEM((n, past), f32),
                        pltpu.VMEM((n, NA_WIN_H * GRID_W), bf16), pltpu.VMEM((n, past), bf16),
                        pltpu.VMEM((n, hd), f32), pltpu.VMEM((n, hd), f32)],
        compiler_params=_cparams(("parallel", "parallel")),
        name="na_latent",
    )(p, p, p, cache_k, cache_v, bias_tbl)


GLA_PAD = 32
GLA_UNROLL = 4


def _rope_tables(seq):
    t = np.arange(seq)
    half = GLA_DK // 2
    nf = half // 2
    inv = ROPE_BASE ** (-np.arange(nf, dtype=np.float64) / nf)
    cos, sin = [], []
    for pos in (t // GRID_W, t % GRID_W):
        ang = pos[:, None].astype(np.float64) * inv
        cos += [np.cos(ang), np.cos(ang)]
        sin += [-np.sin(ang), np.sin(ang)]
    return (jnp.asarray(np.concatenate(cos, axis=-1), f32), jnp.asarray(np.concatenate(sin, axis=-1), f32))


def _rope(x, cos, sin_signed):
    nf = GLA_DK // 4
    lane = lax.broadcasted_iota(jnp.int32, x.shape, 1)
    partner = jnp.where(lane % (2 * nf) < nf, pltpu.roll(x, GLA_DK - nf, 1), pltpu.roll(x, nf, 1))
    return x * cos + partner * sin_signed


def _log_sigmoid(x):
    return jnp.minimum(x, 0.0) - jnp.log1p(jnp.exp(-jnp.abs(x)))


def _gla_body(*refs, seq, rope, with_s0, with_sfin, n_carried):
    refs = list(refs)
    q_ref, k_ref, v_ref, r_ref, z_ref, wg_ref, bg_ref, ng_ref = refs[:8]
    refs = refs[8:]
    if rope:
        cos_ref, sin_ref = refs[:2]
        refs = refs[2:]
    if with_s0:
        s0_ref = refs[0]
        refs = refs[1:]
    refs = refs[n_carried:]
    o_ref = refs[0]
    refs = refs[1:]
    if with_sfin:
        sfin_ref = refs[0]
        refs = refs[1:]
    qi_scr, kn_scr, kd_scr, dec_scr, scan_scr, vb_scr, u_scr, sb_scr, o_scr, st_scr = refs

    ch = GLA_CHUNK
    nch = seq // ch
    dk, dv = GLA_DK, GLA_DV

    q = q_ref[...].astype(f32)
    k = k_ref[...].astype(f32)
    if rope:
        q = _rope(q, cos_ref[...], sin_ref[...])
        k = _rope(k, cos_ref[...], sin_ref[...])
    q = q * (dk ** -0.5)

    zb = z_ref[...].astype(bf16)
    pos = lax.broadcasted_iota(jnp.int32, (seq, 1), 0) % ch
    zpad = jnp.zeros((GLA_PAD, dk), f32)
    scan_scr[pl.ds(0, GLA_PAD), :] = zpad
    scan_scr[pl.ds(GLA_PAD + seq, GLA_PAD), :] = zpad
    for d in range(2):
        g = _log_sigmoid(_dot(zb, wg_ref[d]) + bg_ref[d]) / GLA_TAU
        b = g
        sh = 1
        while sh < ch:
            scan_scr[pl.ds(GLA_PAD, seq), :] = b
            if d == 0:
                b = b + jnp.where(pos >= sh, scan_scr[pl.ds(GLA_PAD - sh, seq), :], 0.0)
            else:
                b = b + jnp.where(pos < ch - sh, scan_scr[pl.ds(GLA_PAD + sh, seq), :], 0.0)
            sh *= 2
        b3 = b.reshape(nch, ch, dk)
        b_end = b3[:, ch - 1:ch, :] if d == 0 else b3[:, 0:1, :]
        lanes = pl.ds(d * dk, dk)
        qi_scr[:, lanes] = (q * jnp.exp(b)).astype(bf16)
        kn_scr[d] = (k * jnp.exp(-b)).astype(bf16)
        kd_scr[:, lanes] = (k.reshape(nch, ch, dk) * jnp.exp(b_end - b3)).reshape(seq, dk).astype(bf16)
        dec_scr[d] = jnp.exp(b_end)

    for d in range(2):
        if with_s0:
            st_scr[d] = s0_ref[d].T
        else:
            st_scr[d] = jnp.zeros((dv, dk), f32)

    ri = lax.broadcasted_iota(jnp.int32, (ch, ch), 0)
    ci = lax.broadcasted_iota(jnp.int32, (ch, ch), 1)

    vb_scr[...] = v_ref[...].astype(bf16)
    chunk_rows = lambda c: pl.ds(pl.multiple_of(c * ch, ch), ch)
    fwd, bwd = pl.ds(0, dk), pl.ds(dk, dk)

    def increments(c, carry):
        rows = chunk_rows(c)
        u_scr[c] = lax.dot_general(vb_scr[rows, :], kd_scr[rows, :], (((0,), (0,)), ((), ())),
                                   preferred_element_type=f32)
        return carry

    lax.fori_loop(0, nch, increments, 0, unroll=GLA_UNROLL)

    def states(i, carry):
        for d, c, lanes in ((0, i, fwd), (1, nch - 1 - i, bwd)):
            st = st_scr[d]
            sb_scr[c, :, lanes] = st.astype(bf16)
            st_scr[d] = st * dec_scr[d, c] + u_scr[c, :, lanes]
        return carry

    lax.fori_loop(0, nch, states, 0)

    if with_sfin:
        for d in range(2):
            sfin_ref[d] = st_scr[d].T

    def outputs(c, carry):
        rows = chunk_rows(c)
        qi = qi_scr[rows, :]
        pf = _dot_nt(qi[:, 0:dk], kn_scr[0, rows, :])
        pb = _dot_nt(qi[:, dk:2 * dk], kn_scr[1, rows, :])
        a = jnp.where(ci < ri, pf, jnp.where(ci > ri, pb, pf + pb))
        o_scr[rows, :] = _dot(a.astype(bf16), vb_scr[rows, :]) + _dot_nt(qi, sb_scr[c])
        return carry

    lax.fori_loop(0, nch, outputs, 0, unroll=GLA_UNROLL)

    o = o_scr[...]
    r = r_ref[...].astype(f32)
    o = o * lax.rsqrt(jnp.mean(o * o, axis=-1, keepdims=True) + EPS) * ng_ref[...]
    o_ref[...] = (o * (r * jax.nn.sigmoid(r))).astype(o_ref.dtype)


def _gla(p, wgate_p, b_gate, gla_norm, seq, row_block0, nreq, rope_tabs=None, state=None, layer=0,
         with_sfin=False, depth=1, carried=None):
    dk, dv = GLA_DK, GLA_DV
    nch = seq // GLA_CHUNK
    rope = rope_tabs is not None
    with_s0 = state is not None
    blk = lambda w, off: pl.BlockSpec((seq, w), lambda b, h: (row_block0 + b, off // w + h))
    in_specs = [blk(dk, OFF_GQ), blk(dk, OFF_GK), blk(dv, OFF_GV), blk(dv, OFF_GR),
                pl.BlockSpec((seq, LANE), lambda b, h: (row_block0 + b, OFF_GZ // LANE)),
                pl.BlockSpec((2, LANE, dk), lambda b, h: (0, 0, h)),
                pl.BlockSpec((2, 1, dk), lambda b, h: (0, 0, h)),
                pl.BlockSpec((1, dv), lambda b, h: (0, h))]
    args = [p, p, p, p, p, wgate_p, b_gate.reshape(2, 1, GLA_KW), gla_norm.reshape(1, GLA_VW)]
    if rope:
        in_specs += [pl.BlockSpec((seq, dk), lambda b, h: (0, 0))] * 2
        args += list(rope_tabs)
    if with_s0:
        in_specs.append(pl.BlockSpec((None, None, 2, None, dk, dv), lambda b, h: (b, layer, 0, h, 0, 0)))
        args.append(state)
    aliases = {}
    if carried is not None:
        aliases = {len(args): 1}
        in_specs.append(pl.BlockSpec(memory_space=pl.ANY))
        args.append(carried)
    out_specs = [pl.BlockSpec((seq, dv), lambda b, h: (b, h))]
    out_shape = [jax.ShapeDtypeStruct((nreq * seq, GLA_VW), bf16)]
    if with_sfin:
        out_specs.append(pl.BlockSpec((None, None, 2, None, dk, dv), lambda b, h: (b, layer, 0, h, 0, 0)))
        out_shape.append(jax.ShapeDtypeStruct((nreq, depth, 2, GLA_HEADS, dk, dv), f32))
    scratch = [pltpu.VMEM((seq, 2 * dk), bf16), pltpu.VMEM((2, seq, dk), bf16), pltpu.VMEM((seq, 2 * dk), bf16),
               pltpu.VMEM((2, nch, 1, dk), f32), pltpu.VMEM((seq + 2 * GLA_PAD, dk), f32),
               pltpu.VMEM((seq, dv), bf16), pltpu.VMEM((nch, dv, 2 * dk), f32), pltpu.VMEM((nch, dv, 2 * dk), bf16),
               pltpu.VMEM((seq, dv), f32), pltpu.VMEM((2, dv, dk), f32)]
    return pl.pallas_call(
        functools.partial(_gla_body, seq=seq, rope=rope, with_s0=with_s0, with_sfin=with_sfin,
                          n_carried=len(aliases)),
        grid=(nreq, GLA_HEADS),
        in_specs=in_specs,
        out_specs=out_specs,
        out_shape=out_shape,
        input_output_aliases=aliases,
        scratch_shapes=scratch,
        compiler_params=_cparams(("parallel", "parallel")),
        name="gla",
    )(*args)


TM_MERGE = 512


MERGE_SUB = 2


def _merge_body(x_ref, mod_ref, post_ref, bp_ref, bn_ref, bg_ref, gl_ref, w_ref, o_ref, m_scr, mb_scr, r_scr):
    n = pl.program_id(1)
    m_ref = m_scr.at[pl.program_id(2)]
    mb_ref = mb_scr.at[pl.program_id(2)]

    for bi, br_ref in enumerate((bp_ref, bn_ref, bg_ref)):
        @pl.when(n == bi)
        def _():
            y = jax.nn.sigmoid(gl_ref[...].astype(f32)) * _dot(br_ref[...], w_ref[...])
            m_ref[...] = y if bi == 0 else m_ref[...] + y

    @pl.when(n == N_BRANCH)
    def _():
        mb_ref[...] = m_ref[...].astype(bf16)
        m_ref[...] = _dot(mb_ref[:, 0:BRANCH_W], w_ref[...])

    @pl.when(n == N_BRANCH + 1)
    def _():
        m_ref[...] += _dot(mb_ref[:, BRANCH_W:2 * BRANCH_W], w_ref[...])
        _norm_gate_residual(m_ref, x_ref, mod_ref, post_ref, o_ref, r_scr, 1.0)


def _merge(x, mod_l, post, y_pool, y_na, y_gla, p, w_stack, cond):
    t, d = x.shape
    tm = TM_MERGE
    sub = MERGE_SUB
    nsteps = N_BRANCH + d // BRANCH_W
    tile = lambda i, s: i * sub + s

    def rows_at(first, last):
        def index(i, n, s):
            return jnp.where(n < first, jnp.maximum(tile(i, 0) - 1, 0),
                             jnp.where(n > last, tile(i, sub - 1), tile(i, s)))
        return index

    last = nsteps - 1
    br = lambda step: pl.BlockSpec((tm, BRANCH_W), lambda i, n, s: (rows_at(step, step)(i, n, s), 0))
    xo = pl.BlockSpec((tm, d), lambda i, n, s: (rows_at(last, last)(i, n, s), 0))
    return pl.pallas_call(
        _merge_body,
        grid=(t // (tm * sub), nsteps, sub),
        in_specs=[
            xo,
            pl.BlockSpec((None, 1, 3 * d), lambda i, n, s: (_cond_index(tile(i, s), tm, cond), 0, 1)),
            pl.BlockSpec((1, d), lambda i, n, s: (0, 0)),
            br(0), br(1), br(2),
            pl.BlockSpec((tm, d),
                         lambda i, n, s: (rows_at(0, N_BRANCH - 1)(i, n, s), jnp.minimum(n, N_BRANCH - 1))),
            pl.BlockSpec((None, BRANCH_W, d), lambda i, n, s: (n, 0, 0)),
        ],
        out_specs=xo,
        out_shape=jax.ShapeDtypeStruct((t, d), f32),
        scratch_shapes=[pltpu.VMEM((sub, tm, d), f32), pltpu.VMEM((sub, tm, d), bf16), pltpu.VMEM((tm, LANE), f32)],
        compiler_params=_cparams(("arbitrary", "arbitrary", "arbitrary")),
        name="merge",
    )(x, mod_l, post, y_pool, y_na, y_gla, p, w_stack)


_IN_SPLITS = (POOL_WIDTH, NA_WIDTH, NA_WIDTH, NA_WIDTH, GLA_KW, GLA_KW, GLA_VW, 2 * GLA_RANK, GLA_VW, GATE_W)
_IN_OFFS = tuple(int(v) for v in np.cumsum((0,) + _IN_SPLITS))
_IN_RUNS = ((_IN_OFFS[9], _IN_OFFS[10]), (_IN_OFFS[0], _IN_OFFS[7]), (_IN_OFFS[8], _IN_OFFS[9]),
            (_IN_OFFS[7], _IN_OFFS[8]))


CT_IN = 512
_IN_TILE_STARTS = []
for _a, _b in _IN_RUNS:
    _IN_TILE_STARTS += [_a + CT_IN * _t for _t in range(-(-(_b - _a) // CT_IN))]
assert len(_IN_TILE_STARTS) * CT_IN == IN_COLS_P and all(v % 8 == 0 for v in _IN_TILE_STARTS)
_IN_LAST_VALID = (_IN_RUNS[-1][1] - _IN_RUNS[-1][0]) % CT_IN or CT_IN


assert all((b - a) % CT_IN == 0 for a, b in _IN_RUNS[:-1])


def _w_in_tile_start(j):
    out = jnp.int32(_IN_TILE_STARTS[0]) + CT_IN * j
    for t in range(1, len(_IN_TILE_STARTS)):
        if _IN_TILE_STARTS[t] != _IN_TILE_STARTS[t - 1] + CT_IN:
            out = jnp.where(j >= t, _IN_TILE_STARTS[t] + CT_IN * (j - t), out)
    return out


def _cast_w_in_body(w_ref, o_ref):
    last = pl.program_id(1) == pl.num_programs(1) - 1
    col = lax.broadcasted_iota(jnp.int32, (1, CT_IN), 1)
    valid = jnp.where(last, _IN_LAST_VALID, CT_IN)
    o_ref[...] = jnp.where(col < valid, w_ref[...].T, 0.0).astype(bf16)


def _cast_w_in(w):
    depth, d, n = w.shape
    wt = jnp.swapaxes(w, 1, 2).reshape(depth * n, d)
    return pl.pallas_call(
        _cast_w_in_body,
        grid=(depth, IN_COLS_P // CT_IN),
        in_specs=[pl.BlockSpec((pl.Element(CT_IN), pl.Element(d)), lambda l, j: (pl.multiple_of(l * n + _w_in_tile_start(j), 8), 0))],
        out_specs=pl.BlockSpec((None, d, CT_IN), lambda l, j: (l, 0, j)),
        out_shape=jax.ShapeDtypeStruct((depth, d, IN_COLS_P), bf16),
        compiler_params=_cparams(("parallel", "parallel")),
        name="cast_w_in",
    )(wt)


def _cast_ffn_in_body(w_ref, o_ref):
    pad = jnp.zeros((w_ref.shape[0], D_FF_P - D_FF), bf16)
    for half in range(2):
        o_ref[half, :, 0:D_FF] = w_ref[:, half * D_FF:(half + 1) * D_FF].astype(bf16)
        o_ref[half, :, D_FF:D_FF_P] = pad


def _cast_ffn_in(w):
    depth, ns, d, _ = w.shape
    rows = 256
    return pl.pallas_call(
        _cast_ffn_in_body,
        grid=(depth, ns, d // rows),
        in_specs=[pl.BlockSpec((None, None, rows, 2 * D_FF), lambda l, s, r: (l, s, r, 0))],
        out_specs=pl.BlockSpec((None, None, 2, rows, D_FF_P), lambda l, s, r: (l, s, 0, r, 0)),
        out_shape=jax.ShapeDtypeStruct((depth, ns, 2, d, D_FF_P), bf16),
        compiler_params=_cparams(("parallel", "parallel", "parallel")),
        name="cast_ffn_in",
    )(w)


def _cast_ffn_out_body(w_ref, o_ref):
    row = pl.program_id(2) * TF + lax.broadcasted_iota(jnp.int32, (TF, 1), 0)
    o_ref[...] = jnp.where(row < D_FF, w_ref[...], 0.0).astype(bf16)


def _cast_ffn_out(w):
    depth, ns, _, d = w.shape
    return pl.pallas_call(
        _cast_ffn_out_body,
        grid=(depth, ns, D_FF_P // TF),
        in_specs=[pl.BlockSpec((None, None, TF, d), lambda l, s, j: (l, s, j, 0))],
        out_specs=pl.BlockSpec((None, None, TF, d), lambda l, s, j: (l, s, j, 0)),
        out_shape=jax.ShapeDtypeStruct((depth, ns, D_FF_P, d), bf16),
        compiler_params=_cparams(("parallel", "parallel", "parallel")),
        name="cast_ffn_out",
    )(w)


def _prep_gate(w_gate):
    out = jnp.zeros((2, LANE, GLA_KW), f32)
    for d in range(2):
        out = out.at[d, d * GLA_RANK:(d + 1) * GLA_RANK].set(w_gate[d])
    return out.astype(bf16)


def kernel(x_prompt, x_sample, c, cache_na_k, cache_na_v, state_gla, c_ctx, w_mod, b_mod, norm_pre, norm_post,
           w_ffn_in, w_ffn_out, w_in, pool_w, pool_scale, na_rpb, gla_w_gate, gla_b_gate, gla_norm, w_branch,
           w_out):
    nb, seq, d = x_prompt.shape
    ndec, dseq, _ = x_sample.shape
    depth = w_mod.shape[0]
    n_ctx = nb * seq
    n_lat = ndec * dseq
    assert (seq, dseq, d) == (SEQ, DEC_SEQ, D_MODEL) and n_ctx % (TM_MERGE * MERGE_SUB) == 0 and n_ctx % TM == 0

    xs = [x_prompt.reshape(n_ctx, d), x_sample.reshape(n_lat, d)]
    conds = [(0, n_ctx), (1, dseq)]
    ncond = -(-(1 + ndec) // 8) * 8
    c_all = jnp.concatenate([c_ctx[None], c, jnp.zeros((ncond - 1 - ndec, d), f32)], axis=0)
    mod = _modulation(c_all, w_mod, b_mod)
    rope_tabs = _rope_tables(dseq)
    bias_tbl = _na_bias_table(na_rpb)

    w_ffn_in5 = _cast_ffn_in(w_ffn_in)
    w_ffn_out4 = _cast_ffn_out(w_ffn_out)
    w_in_p = _cast_w_in(w_in)
    w_stack_all = jnp.concatenate(
        [w_branch.astype(bf16), w_out.reshape(depth, d // BRANCH_W, BRANCH_W, d).astype(bf16)], axis=1)

    caches = None
    new_s = None
    for l in range(depth):
        mod_l = mod[l].reshape(ncond, 1, N_MOD * d)
        pre = norm_pre[l].reshape(3, 1, d)
        post = norm_post[l].reshape(3, 1, d)
        w_stack = w_stack_all[l]
        wgate_p = _prep_gate(gla_w_gate[l])
        pw = pool_w[l].astype(bf16)
        psc = pool_scale[l].reshape(1, POOL_WIDTH)

        xs = [_ffn(x, mod_l, pre[0], post[0], w_ffn_in5, w_ffn_out4, l, 0, 0, cond) for x, cond in zip(xs, conds)]
        p_ctx, p_lat = [_inproj(x, mod_l, pre[1], w_in_p, l, cond, dt) for x, cond, dt in zip(xs, conds, (f32, bf16))]

        y_pool = [_pool(p_ctx, pw, psc, seq, 0, nb), _pool(p_lat, pw, psc, dseq, 0, ndec)]
        na_ctx, new_k, new_v = _ctx_attn(p_ctx, nb, l, depth, caches)
        caches = (new_k, new_v)
        y_na = [na_ctx, _na_latent(p_lat, cache_na_k, cache_na_v, bias_tbl, l, 0, ndec)]
        g_ctx, new_s = _gla(p_ctx, wgate_p, gla_b_gate[l], gla_norm[l], seq, 0, nb, layer=l, with_sfin=True,
                            depth=depth, carried=new_s)
        (g_lat,) = _gla(p_lat, wgate_p, gla_b_gate[l], gla_norm[l], dseq, 0, ndec, rope_tabs=rope_tabs,
                        state=state_gla, layer=l)
        y_gla = [g_ctx, g_lat]

        xs = [_merge(x, mod_l, post[1], yp, yn, yg, p, w_stack, cond)
              for x, yp, yn, yg, p, cond in zip(xs, y_pool, y_na, y_gla, (p_ctx, p_lat), conds)]
        xs = [_ffn(x, mod_l, pre[2], post[2], w_ffn_in5, w_ffn_out4, l, 1, 2, cond) for x, cond in zip(xs, conds)]

    return (xs[0].reshape(nb, seq, d), xs[1].reshape(ndec, dseq, d), caches[0], caches[1], new_s)
```

```python
import functools

import numpy as np
import jax
import jax.numpy as jnp
from jax import lax
from jax.experimental import pallas as pl
from jax.experimental.pallas import tpu as pltpu

f32 = jnp.float32
bf16 = jnp.bfloat16

D_MODEL = 2048
SEQ = 256
DEC_SEQ = 2048
GRID_W = 64
N_MOD = 9
D_FF = 5504
FFN_RES = 0.5
EPS = 1e-6
NEG_INF = -1e30

POOL_GROUPS = 4
POOL_WINDOWS = (2, 4, 8, 16)
POOL_WIDTH = 1024
POOL_GC = POOL_WIDTH // POOL_GROUPS

NA_HEADS = 8
NA_HEAD_DIM = 128
NA_WIDTH = NA_HEADS * NA_HEAD_DIM
NA_WIN_H = 8
NA_WIN_W = 16

GLA_HEADS = 4
GLA_DK = 128
GLA_DV = 256
GLA_KW = GLA_HEADS * GLA_DK
GLA_VW = GLA_HEADS * GLA_DV
GLA_RANK = 16
GLA_TAU = 16.0
GLA_CHUNK = 64
ROPE_BASE = 10000.0

BRANCH_W = 1024
N_BRANCH = 3
GATE_W = N_BRANCH * D_MODEL

LANE = 128
VMEM_LIMIT = 56 * 1024 * 1024

OFF_GL = 0
OFF_POOL = OFF_GL + GATE_W
OFF_NQ = OFF_POOL + POOL_WIDTH
OFF_NK = OFF_NQ + NA_WIDTH
OFF_NV = OFF_NK + NA_WIDTH
OFF_GQ = OFF_NV + NA_WIDTH
OFF_GK = OFF_GQ + GLA_KW
OFF_GV = OFF_GK + GLA_KW
OFF_GR = OFF_GV + GLA_VW
OFF_GZ = OFF_GR + GLA_VW
TN_IN = 2304
IN_COLS_P = -(-(OFF_GZ + LANE) // TN_IN) * TN_IN

TM = 512
TF = 512
D_FF_P = -(-D_FF // TF) * TF


def _cparams(sem):
    return pltpu.CompilerParams(dimension_semantics=sem, vmem_limit_bytes=VMEM_LIMIT)


def _cond_index(i, tm, cond):
    return cond[0] + (i * tm) // cond[1]


def _rms(x, g):
    return x * lax.rsqrt(jnp.mean(x * x, axis=-1, keepdims=True) + EPS) * g


ROW_CHUNK = 16


def _row_sweep(nrows, fn, unroll=4):
    def trip(i, carry):
        fn(pl.ds(pl.multiple_of(i * ROW_CHUNK, ROW_CHUNK), ROW_CHUNK))
        return carry

    lax.fori_loop(0, nrows // ROW_CHUNK, trip, 0, unroll=unroll)


def _row_rsqrt(x_ref, r_scr):
    n = x_ref.shape[1]

    def fn(rows):
        x = x_ref[rows, :]
        ss = jnp.sum(_lane_fold(x * x, jnp.add), axis=-1, keepdims=True)
        r_scr[rows, :] = jnp.broadcast_to(lax.rsqrt(ss * (1.0 / n) + EPS), (ROW_CHUNK, LANE))

    _row_sweep(x_ref.shape[0], fn, unroll=16)


def _lanes(r, n):
    return jnp.concatenate([r] * (n // LANE), axis=1)


def _norm_modulate(x_ref, mod_ref, g_ref, h_ref, r_scr):
    d = D_MODEL
    shift = mod_ref[:, 0:d]
    w = g_ref[...] * (1.0 + mod_ref[:, d:2 * d])
    _row_rsqrt(x_ref, r_scr)

    def fn(rows):
        h_ref[rows, :] = (x_ref[rows, :] * _lanes(r_scr[rows, :], d) * w + shift).astype(h_ref.dtype)

    _row_sweep(x_ref.shape[0], fn)


def _norm_gate_residual(y_ref, x_ref, mod_ref, g_ref, o_ref, r_scr, res_weight):
    d = D_MODEL
    w = (res_weight * mod_ref[:, 2 * d:3 * d]) * g_ref[...]
    _row_rsqrt(y_ref, r_scr)

    def fn(rows):
        o_ref[rows, :] = x_ref[rows, :] + y_ref[rows, :] * _lanes(r_scr[rows, :], d) * w

    _row_sweep(x_ref.shape[0], fn)


def _dot(a, b):
    return jnp.dot(a, b, preferred_element_type=f32)


def _dot_nt(a, b):
    return lax.dot_general(a, b, (((1,), (1,)), ((), ())), preferred_element_type=f32)


def _mod_body(c_ref, w_ref, b_ref, o_ref):
    c = c_ref[...]
    s = c * jax.nn.sigmoid(c)
    o_ref[...] = _dot(s.astype(bf16), w_ref[...].astype(bf16)) + b_ref[...]


def _modulation(c_all, w_mod, b_mod):
    depth, d, n = w_mod.shape
    nc = c_all.shape[0]
    tn = 1024
    return pl.pallas_call(
        _mod_body,
        grid=(depth, n // tn),
        in_specs=[
            pl.BlockSpec((nc, d), lambda l, j: (0, 0)),
            pl.BlockSpec((None, d, tn), lambda l, j: (l, 0, j)),
            pl.BlockSpec((None, 1, tn), lambda l, j: (l, 0, j)),
        ],
        out_specs=pl.BlockSpec((None, nc, tn), lambda l, j: (l, 0, j)),
        out_shape=jax.ShapeDtypeStruct((depth, nc, n), f32),
        compiler_params=_cparams(("parallel", "parallel")),
        name="modulation",
    )(c_all, w_mod, b_mod.reshape(depth, 1, n))


TM_FFN = 1024


def _ffn_body(x_ref, mod_ref, pre_ref, post_ref, wg_ref, wu_ref, wo_ref, o_ref, h_scr, r_scr):
    f = pl.program_id(1)

    @pl.when(f == 0)
    def _():
        _norm_modulate(x_ref, mod_ref, pre_ref, h_scr, r_scr)
        o_ref[...] = jnp.zeros_like(o_ref)

    h = h_scr[...]
    gt = _dot(h, wg_ref[...])
    up = _dot(h, wu_ref[...])
    a = gt * jax.nn.sigmoid(gt) * up
    o_ref[...] += _dot(a.astype(bf16), wo_ref[...])

    @pl.when(f == pl.num_programs(1) - 1)
    def _():
        _norm_gate_residual(o_ref, x_ref, mod_ref, post_ref, o_ref, r_scr, FFN_RES)


def _ffn(x, mod_l, pre, post, w_in5, w_out4, layer, slot, sub, cond):
    t, d = x.shape
    fp = w_out4.shape[2]
    w_half = lambda half: pl.BlockSpec((None, None, None, d, TF), lambda i, f: (layer, slot, half, 0, f))
    return pl.pallas_call(
        _ffn_body,
        grid=(t // TM_FFN, fp // TF),
        in_specs=[
            pl.BlockSpec((TM_FFN, d), lambda i, f: (i, 0)),
            pl.BlockSpec((None, 1, 3 * d), lambda i, f: (_cond_index(i, TM_FFN, cond), 0, sub)),
            pl.BlockSpec((1, d), lambda i, f: (0, 0)),
            pl.BlockSpec((1, d), lambda i, f: (0, 0)),
            w_half(0),
            w_half(1),
            pl.BlockSpec((None, None, TF, d), lambda i, f: (layer, slot, f, 0)),
        ],
        out_specs=pl.BlockSpec((TM_FFN, d), lambda i, f: (i, 0)),
        out_shape=jax.ShapeDtypeStruct((t, d), f32),
        scratch_shapes=[pltpu.VMEM((TM_FFN, d), bf16), pltpu.VMEM((TM_FFN, LANE), f32)],
        compiler_params=_cparams(("parallel", "arbitrary")),
        name="ffn",
    )(x, mod_l, pre, post, w_in5, w_in5, w_out4)


def _inproj_body(x_ref, mod_ref, pre_ref, w_ref, o_ref, h_scr, r_scr):
    d = D_MODEL

    @pl.when(pl.program_id(1) == 0)
    def _():
        _norm_modulate(x_ref, mod_ref, pre_ref, h_scr, r_scr)

    o_ref[...] = _dot(h_scr[...], w_ref[...]).astype(o_ref.dtype)


def _inproj(x, mod_l, pre, w_in_p, layer, cond, out_dtype):
    t, d = x.shape
    n = w_in_p.shape[2]
    return pl.pallas_call(
        _inproj_body,
        grid=(t // TM, n // TN_IN),
        in_specs=[
            pl.BlockSpec((TM, d), lambda i, j: (i, 0)),
            pl.BlockSpec((None, 1, 3 * d), lambda i, j: (_cond_index(i, TM, cond), 0, 1)),
            pl.BlockSpec((1, d), lambda i, j: (0, 0)),
            pl.BlockSpec((None, d, TN_IN), lambda i, j: (layer, 0, j)),
        ],
        out_specs=pl.BlockSpec((TM, TN_IN), lambda i, j: (i, j)),
        out_shape=jax.ShapeDtypeStruct((t, n), out_dtype),
        scratch_shapes=[pltpu.VMEM((TM, d), bf16), pltpu.VMEM((TM, LANE), f32)],
        compiler_params=_cparams(("parallel", "arbitrary")),
        name="inproj",
    )(x, mod_l, pre, w_in_p)


POOL_PAD = 8


def _pool_body(u_ref, w_ref, sc_ref, o_ref, pad_scr, *, seq):
    gc = POOL_GC
    zeros = jnp.zeros((POOL_PAD, POOL_WIDTH), f32)
    pad_scr[pl.ds(0, POOL_PAD), :] = zeros
    pad_scr[pl.ds(POOL_PAD + seq, POOL_PAD), :] = zeros
    pad_scr[pl.ds(POOL_PAD, seq), :] = u_ref[...].astype(f32)
    t = lax.broadcasted_iota(jnp.int32, (seq, 1), 0)
    for gi, win in enumerate(POOL_WINDOWS):
        cols = pl.ds(gi * gc, gc)
        lo = jnp.maximum(t - win // 2, 0)
        hi = jnp.minimum(t + win - 1 - win // 2, seq - 1)
        cnt = (hi - lo + 1).astype(f32)
        acc = pad_scr[pl.ds(POOL_PAD - win // 2, seq), cols]
        for j in range(1, win):
            acc = acc + pad_scr[pl.ds(POOL_PAD - win // 2 + j, seq), cols]
        pooled = acc / cnt - pad_scr[pl.ds(POOL_PAD, seq), cols]
        y = _dot(pooled.astype(bf16), w_ref[gi])
        o_ref[:, cols] = (y * sc_ref[:, cols]).astype(o_ref.dtype)


def _pool(p, pool_w, pool_scale, seq, row_block0, nseq):
    cb = OFF_POOL // POOL_WIDTH
    return pl.pallas_call(
        functools.partial(_pool_body, seq=seq),
        grid=(nseq,),
        in_specs=[
            pl.BlockSpec((seq, POOL_WIDTH), lambda s: (row_block0 + s, cb)),
            pl.BlockSpec((POOL_GROUPS, POOL_GC, POOL_GC), lambda s: (0, 0, 0)),
            pl.BlockSpec((1, POOL_WIDTH), lambda s: (0, 0)),
        ],
        out_specs=pl.BlockSpec((seq, POOL_WIDTH), lambda s: (s, 0)),
        out_shape=jax.ShapeDtypeStruct((nseq * seq, POOL_WIDTH), bf16),
        scratch_shapes=[pltpu.VMEM((seq + 2 * POOL_PAD, POOL_WIDTH), f32)],
        compiler_params=_cparams(("parallel",)),
        name="pool",
    )(p, pool_w, pool_scale)


def _lane_fold(x, op):
    parts = [x[:, i * LANE:(i + 1) * LANE] for i in range(x.shape[1] // LANE)]
    while len(parts) > 1:
        parts = [op(parts[i], parts[i + 1]) for i in range(0, len(parts) - 1, 2)] + parts[len(parts) & ~1:]
    return parts[0]


def _softmax_rows(s):
    m = jnp.max(_lane_fold(s, jnp.maximum), axis=-1, keepdims=True)
    e = jnp.exp(s - m)
    return e / jnp.sum(_lane_fold(e, jnp.add), axis=-1, keepdims=True)


def _ctx_attn_body(q_ref, k_ref, v_ref, *refs):
    o_ref, nk_ref, nv_ref = refs[-3:]
    hd = NA_HEAD_DIM
    for h in range(NA_HEADS):
        cols = pl.ds(h * hd, hd)
        kf = k_ref[:, cols]
        vf = v_ref[:, cols]
        nk_ref[h] = kf
        nv_ref[h] = vf
        p = _softmax_rows(_dot_nt(q_ref[:, cols].astype(bf16), kf.astype(bf16)) * (hd ** -0.5))
        o_ref[:, cols] = _dot(p.astype(bf16), vf.astype(bf16)).astype(o_ref.dtype)


def _ctx_attn(p, nseq, layer, depth, caches=None):
    spec = lambda off: pl.BlockSpec((SEQ, NA_WIDTH), lambda b: (b, off // NA_WIDTH))
    cache_spec = pl.BlockSpec((None, None, NA_HEADS, SEQ, NA_HEAD_DIM), lambda b: (b, layer, 0, 0, 0))
    cache_shape = jax.ShapeDtypeStruct((nseq, depth, NA_HEADS, SEQ, NA_HEAD_DIM), f32)
    in_specs = [spec(OFF_NQ), spec(OFF_NK), spec(OFF_NV)]
    args = [p, p, p]
    aliases = {}
    if caches is not None:
        in_specs += [pl.BlockSpec(memory_space=pl.ANY)] * 2
        args += list(caches)
        aliases = {3: 1, 4: 2}
    return pl.pallas_call(
        _ctx_attn_body,
        grid=(nseq,),
        in_specs=in_specs,
        out_specs=[pl.BlockSpec((SEQ, NA_WIDTH), lambda b: (b, 0)), cache_spec, cache_spec],
        out_shape=[jax.ShapeDtypeStruct((nseq * SEQ, NA_WIDTH), bf16), cache_shape, cache_shape],
        input_output_aliases=aliases,
        compiler_params=_cparams(("parallel",)),
        name="ctx_attn",
    )(*args)


def _na_bias_table(rpb):
    qc = np.arange(GRID_W)[:, None]
    kc = np.arange(GRID_W)[None, :]
    cs = np.clip(qc - NA_WIN_W // 2, 0, GRID_W - NA_WIN_W)
    ok = (kc >= cs) & (kc < cs + NA_WIN_W)
    cidx = np.clip(kc - qc + NA_WIN_W - 1, 0, 2 * NA_WIN_W - 2)
    onehot = jnp.asarray((cidx[None] == np.arange(2 * NA_WIN_W - 1)[:, None, None]) & ok[None], f32)
    toep = jnp.einsum('...rc,cqk->...rqk', rpb.astype(f32), onehot, precision=lax.Precision.HIGHEST)
    toep = jnp.where(ok, toep, NEG_INF)
    tbl = jnp.stack([toep[..., s:s + NA_WIN_H, :, :] for s in range(NA_WIN_H)], axis=-4)
    tbl = jnp.swapaxes(tbl, -3, -2)
    return tbl.reshape(rpb.shape[:-2] + (NA_WIN_H, GRID_W, NA_WIN_H * GRID_W))


NA_UNROLL = 8
NA_CTX_ROWS = 256


def _na_body(qb_scr, kb_scr, vb_scr, ck_ref, cv_ref, bias_ref, o_ref,
             sl_scr, sc_scr, el_scr, ec_scr, den_scr, oc_scr, *, rows):
    hd = NA_HEAD_DIM
    scale = hd ** -0.5
    kh = min(NA_WIN_H, rows)
    nloc = kh * GRID_W
    n = rows * GRID_W
    assert qb_scr.dtype == bf16
    ck = ck_ref[...].astype(bf16)
    cv = cv_ref[...].astype(bf16)

    def row_slices(r):
        rs = jnp.clip(r - kh // 2, 0, rows - kh)
        q_rows = pl.ds(pl.multiple_of(r * GRID_W, GRID_W), GRID_W)
        k_rows = pl.ds(pl.multiple_of(rs * GRID_W, GRID_W), nloc)
        return rs, q_rows, k_rows

    def ctx_scores(i, carry):
        blk = pl.ds(pl.multiple_of(i * NA_CTX_ROWS, NA_CTX_ROWS), NA_CTX_ROWS)
        sc_scr[blk, :] = _dot_nt(qb_scr[blk, :], ck) * scale
        return carry

    lax.fori_loop(0, n // NA_CTX_ROWS, ctx_scores, 0, unroll=2)

    def loc_scores(r, carry):
        rs, q_rows, k_rows = row_slices(r)
        sl_scr[q_rows, :] = (_dot_nt(qb_scr[q_rows, :], kb_scr[k_rows, :]) * scale
                             + bias_ref[rs - r + NA_WIN_H - 1])
        return carry

    lax.fori_loop(0, rows, loc_scores, 0, unroll=NA_UNROLL)

    def numerators(r, carry):
        q_rows = pl.ds(pl.multiple_of(r * GRID_W, GRID_W), GRID_W)
        s_loc = sl_scr[q_rows, :]
        s_ctx = sc_scr[q_rows, :]
        m = jnp.max(jnp.maximum(_lane_fold(s_loc, jnp.maximum), _lane_fold(s_ctx, jnp.maximum)),
                    axis=-1, keepdims=True)
        e_loc = jnp.exp(s_loc - m)
        e_ctx = jnp.exp(s_ctx - m)
        den = jnp.sum(_lane_fold(e_loc, jnp.add) + _lane_fold(e_ctx, jnp.add), axis=-1, keepdims=True)
        el_scr[q_rows, :] = e_loc.astype(bf16)
        ec_scr[q_rows, :] = e_ctx.astype(bf16)
        den_scr[q_rows, :] = jnp.broadcast_to(den, (GRID_W, hd))
        return carry

    lax.fori_loop(0, rows, numerators, 0, unroll=NA_UNROLL)

    def ctx_values(i, carry):
        blk = pl.ds(pl.multiple_of(i * NA_CTX_ROWS, NA_CTX_ROWS), NA_CTX_ROWS)
        oc_scr[blk, :] = _dot(ec_scr[blk, :], cv)
        return carry

    lax.fori_loop(0, n // NA_CTX_ROWS, ctx_values, 0, unroll=2)

    def loc_values(r, carry):
        _, q_rows, k_rows = row_slices(r)
        o = _dot(el_scr[q_rows, :], vb_scr[k_rows, :]) + oc_scr[q_rows, :]
        o_ref[q_rows, :] = (o / den_scr[q_rows, :]).astype(o_ref.dtype)
        return carry

    lax.fori_loop(0, rows, loc_values, 0, unroll=NA_UNROLL)


def _na_latent(p, cache_k, cache_v, bias_tbl, layer, row_block0, nreq):
    n = DEC_SEQ
    hd = NA_HEAD_DIM
    past = cache_k.shape[3]
    rows = n // GRID_W
    qkv = lambda off: pl.BlockSpec((n, hd), lambda b, h: (row_block0 + b, off // hd + h))
    cache = pl.BlockSpec((None, None, None, past, hd), lambda b, h: (b, layer, h, 0, 0))
    return pl.pallas_call(
        functools.partial(_na_body, rows=rows),
        grid=(nreq, NA_HEADS),
        in_specs=[qkv(OFF_NQ), qkv(OFF_NK), qkv(OFF_NV), cache, cache,
                  pl.BlockSpec((None, None, NA_WIN_H, GRID_W, NA_WIN_H * GRID_W),
                               lambda b, h: (layer, h, 0, 0, 0))],
        out_specs=pl.BlockSpec((n, hd), lambda b, h: (b, h)),
        out_shape=jax.ShapeDtypeStruct((nreq * n, NA_WIDTH), bf16),
        scratch_shapes=[pltpu.VMEM((n, NA_WIN_H * GRID_W), f32), pltpu.VMEM((n, past), f32),
                        pltpu.VMEM((n, NA_WIN_H * GRID_W), bf16), pltpu.VMEM((n, past), bf16),
                        pltpu.VMEM((n, hd), f32), pltpu.VMEM((n, hd), f32)],
        compiler_params=_cparams(("parallel", "parallel")),
        name="na_latent",
    )(p, p, p, cache_k, cache_v, bias_tbl)


GLA_PAD = 32
GLA_UNROLL = 4


def _rope_tables(seq):
    t = np.arange(seq)
    half = GLA_DK // 2
    nf = half // 2
    inv = ROPE_BASE ** (-np.arange(nf, dtype=np.float64) / nf)
    cos, sin = [], []
    for pos in (t // GRID_W, t % GRID_W):
        ang = pos[:, None].astype(np.float64) * inv
        cos += [np.cos(ang), np.cos(ang)]
        sin += [-np.sin(ang), np.sin(ang)]
    return (jnp.asarray(np.concatenate(cos, axis=-1), f32), jnp.asarray(np.concatenate(sin, axis=-1), f32))


def _rope(x, cos, sin_signed):
    nf = GLA_DK // 4
    lane = lax.broadcasted_iota(jnp.int32, x.shape, 1)
    partner = jnp.where(lane % (2 * nf) < nf, pltpu.roll(x, GLA_DK - nf, 1), pltpu.roll(x, nf, 1))
    return x * cos + partner * sin_signed


def _log_sigmoid(x):
    return jnp.minimum(x, 0.0) - jnp.log1p(jnp.exp(-jnp.abs(x)))


def _gla_body(*refs, seq, rope, with_s0, with_sfin, n_carried):
    refs = list(refs)
    q_ref, k_ref, v_ref, r_ref, z_ref, wg_ref, bg_ref, ng_ref = refs[:8]
    refs = refs[8:]
    if rope:
        cos_ref, sin_ref = refs[:2]
        refs = refs[2:]
    if with_s0:
        s0_ref = refs[0]
        refs = refs[1:]
    refs = refs[n_carried:]
    o_ref = refs[0]
    refs = refs[1:]
    if with_sfin:
        sfin_ref = refs[0]
        refs = refs[1:]
    qi_scr, kn_scr, kd_scr, dec_scr, scan_scr, vb_scr, u_scr, sb_scr, o_scr, st_scr = refs

    ch = GLA_CHUNK
    nch = seq // ch
    dk, dv = GLA_DK, GLA_DV

    q = q_ref[...].astype(f32)
    k = k_ref[...].astype(f32)
    if rope:
        q = _rope(q, cos_ref[...], sin_ref[...])
        k = _rope(k, cos_ref[...], sin_ref[...])
    q = q * (dk ** -0.5)

    zb = z_ref[...].astype(bf16)
    pos = lax.broadcasted_iota(jnp.int32, (seq, 1), 0) % ch
    zpad = jnp.zeros((GLA_PAD, dk), f32)
    scan_scr[pl.ds(0, GLA_PAD), :] = zpad
    scan_scr[pl.ds(GLA_PAD + seq, GLA_PAD), :] = zpad
    for d in range(2):
        g = _log_sigmoid(_dot(zb, wg_ref[d]) + bg_ref[d]) / GLA_TAU
        b = g
        sh = 1
        while sh < ch:
            scan_scr[pl.ds(GLA_PAD, seq), :] = b
            if d == 0:
                b = b + jnp.where(pos >= sh, scan_scr[pl.ds(GLA_PAD - sh, seq), :], 0.0)
            else:
                b = b + jnp.where(pos < ch - sh, scan_scr[pl.ds(GLA_PAD + sh, seq), :], 0.0)
            sh *= 2
        b3 = b.reshape(nch, ch, dk)
        b_end = b3[:, ch - 1:ch, :] if d == 0 else b3[:, 0:1, :]
        lanes = pl.ds(d * dk, dk)
        qi_scr[:, lanes] = (q * jnp.exp(b)).astype(bf16)
        kn_scr[d] = (k * jnp.exp(-b)).astype(bf16)
        kd_scr[:, lanes] = (k.reshape(nch, ch, dk) * jnp.exp(b_end - b3)).reshape(seq, dk).astype(bf16)
        dec_scr[d] = jnp.exp(b_end)

    for d in range(2):
        if with_s0:
            st_scr[d] = s0_ref[d].T
        else:
            st_scr[d] = jnp.zeros((dv, dk), f32)

    ri = lax.broadcasted_iota(jnp.int32, (ch, ch), 0)
    ci = lax.broadcasted_iota(jnp.int32, (ch, ch), 1)

    vb_scr[...] = v_ref[...].astype(bf16)
    chunk_rows = lambda c: pl.ds(pl.multiple_of(c * ch, ch), ch)
    fwd, bwd = pl.ds(0, dk), pl.ds(dk, dk)

    def increments(c, carry):
        rows = chunk_rows(c)
        u_scr[c] = lax.dot_general(vb_scr[rows, :], kd_scr[rows, :], (((0,), (0,)), ((), ())),
                                   preferred_element_type=f32)
        return carry

    lax.fori_loop(0, nch, increments, 0, unroll=GLA_UNROLL)

    def states(i, carry):
        for d, c, lanes in ((0, i, fwd), (1, nch - 1 - i, bwd)):
            st = st_scr[d]
            sb_scr[c, :, lanes] = st.astype(bf16)
            st_scr[d] = st * dec_scr[d, c] + u_scr[c, :, lanes]
        return carry

    lax.fori_loop(0, nch, states, 0)

    if with_sfin:
        for d in range(2):
            sfin_ref[d] = st_scr[d].T

    def outputs(c, carry):
        rows = chunk_rows(c)
        qi = qi_scr[rows, :]
        pf = _dot_nt(qi[:, 0:dk], kn_scr[0, rows, :])
        pb = _dot_nt(qi[:, dk:2 * dk], kn_scr[1, rows, :])
        a = jnp.where(ci < ri, pf, jnp.where(ci > ri, pb, pf + pb))
        o_scr[rows, :] = _dot(a.astype(bf16), vb_scr[rows, :]) + _dot_nt(qi, sb_scr[c])
        return carry

    lax.fori_loop(0, nch, outputs, 0, unroll=GLA_UNROLL)

    o = o_scr[...]
    r = r_ref[...].astype(f32)
    o = o * lax.rsqrt(jnp.mean(o * o, axis=-1, keepdims=True) + EPS) * ng_ref[...]
    o_ref[...] = (o * (r * jax.nn.sigmoid(r))).astype(o_ref.dtype)


def _gla(p, wgate_p, b_gate, gla_norm, seq, row_block0, nreq, rope_tabs=None, state=None, layer=0,
         with_sfin=False, depth=1, carried=None):
    dk, dv = GLA_DK, GLA_DV
    nch = seq // GLA_CHUNK
    rope = rope_tabs is not None
    with_s0 = state is not None
    blk = lambda w, off: pl.BlockSpec((seq, w), lambda b, h: (row_block0 + b, off // w + h))
    in_specs = [blk(dk, OFF_GQ), blk(dk, OFF_GK), blk(dv, OFF_GV), blk(dv, OFF_GR),
                pl.BlockSpec((seq, LANE), lambda b, h: (row_block0 + b, OFF_GZ // LANE)),
                pl.BlockSpec((2, LANE, dk), lambda b, h: (0, 0, h)),
                pl.BlockSpec((2, 1, dk), lambda b, h: (0, 0, h)),
                pl.BlockSpec((1, dv), lambda b, h: (0, h))]
    args = [p, p, p, p, p, wgate_p, b_gate.reshape(2, 1, GLA_KW), gla_norm.reshape(1, GLA_VW)]
    if rope:
        in_specs += [pl.BlockSpec((seq, dk), lambda b, h: (0, 0))] * 2
        args += list(rope_tabs)
    if with_s0:
        in_specs.append(pl.BlockSpec((None, None, 2, None, dk, dv), lambda b, h: (b, layer, 0, h, 0, 0)))
        args.append(state)
    aliases = {}
    if carried is not None:
        aliases = {len(args): 1}
        in_specs.append(pl.BlockSpec(memory_space=pl.ANY))
        args.append(carried)
    out_specs = [pl.BlockSpec((seq, dv), lambda b, h: (b, h))]
    out_shape = [jax.ShapeDtypeStruct((nreq * seq, GLA_VW), bf16)]
    if with_sfin:
        out_specs.append(pl.BlockSpec((None, None, 2, None, dk, dv), lambda b, h: (b, layer, 0, h, 0, 0)))
        out_shape.append(jax.ShapeDtypeStruct((nreq, depth, 2, GLA_HEADS, dk, dv), f32))
    scratch = [pltpu.VMEM((seq, 2 * dk), bf16), pltpu.VMEM((2, seq, dk), bf16), pltpu.VMEM((seq, 2 * dk), bf16),
               pltpu.VMEM((2, nch, 1, dk), f32), pltpu.VMEM((seq + 2 * GLA_PAD, dk), f32),
               pltpu.VMEM((seq, dv), bf16), pltpu.VMEM((nch, dv, 2 * dk), f32), pltpu.VMEM((nch, dv, 2 * dk), bf16),
               pltpu.VMEM((seq, dv), f32), pltpu.VMEM((2, dv, dk), f32)]
    return pl.pallas_call(
        functools.partial(_gla_body, seq=seq, rope=rope, with_s0=with_s0, with_sfin=with_sfin,
                          n_carried=len(aliases)),
        grid=(nreq, GLA_HEADS),
        in_specs=in_specs,
        out_specs=out_specs,
        out_shape=out_shape,
        input_output_aliases=aliases,
        scratch_shapes=scratch,
        compiler_params=_cparams(("parallel", "parallel")),
        name="gla",
    )(*args)


TM_MERGE = 512


MERGE_SUB = 2


def _merge_body(x_ref, mod_ref, post_ref, bp_ref, bn_ref, bg_ref, gl_ref, w_ref, o_ref, m_scr, mb_scr, r_scr):
    n = pl.program_id(1)
    m_ref = m_scr.at[pl.program_id(2)]
    mb_ref = mb_scr.at[pl.program_id(2)]

    for bi, br_ref in enumerate((bp_ref, bn_ref, bg_ref)):
        @pl.when(n == bi)
        def _():
            y = jax.nn.sigmoid(gl_ref[...].astype(f32)) * _dot(br_ref[...], w_ref[...])
            m_ref[...] = y if bi == 0 else m_ref[...] + y

    @pl.when(n == N_BRANCH)
    def _():
        mb_ref[...] = m_ref[...].astype(bf16)
        m_ref[...] = _dot(mb_ref[:, 0:BRANCH_W], w_ref[...])

    @pl.when(n == N_BRANCH + 1)
    def _():
        m_ref[...] += _dot(mb_ref[:, BRANCH_W:2 * BRANCH_W], w_ref[...])
        _norm_gate_residual(m_ref, x_ref, mod_ref, post_ref, o_ref, r_scr, 1.0)


def _merge(x, mod_l, post, y_pool, y_na, y_gla, p, w_stack, cond):
    t, d = x.shape
    tm = TM_MERGE
    sub = MERGE_SUB
    nsteps = N_BRANCH + d // BRANCH_W
    tile = lambda i, s: i * sub + s

    def rows_at(first, last):
        def index(i, n, s):
            return jnp.where(n < first, jnp.maximum(tile(i, 0) - 1, 0),
                             jnp.where(n > last, tile(i, sub - 1), tile(i, s)))
        return index

    last = nsteps - 1
    br = lambda step: pl.BlockSpec((tm, BRANCH_W), lambda i, n, s: (rows_at(step, step)(i, n, s), 0))
    xo = pl.BlockSpec((tm, d), lambda i, n, s: (rows_at(last, last)(i, n, s), 0))
    return pl.pallas_call(
        _merge_body,
        grid=(t // (tm * sub), nsteps, sub),
        in_specs=[
            xo,
            pl.BlockSpec((None, 1, 3 * d), lambda i, n, s: (_cond_index(tile(i, s), tm, cond), 0, 1)),
            pl.BlockSpec((1, d), lambda i, n, s: (0, 0)),
            br(0), br(1), br(2),
            pl.BlockSpec((tm, d),
                         lambda i, n, s: (rows_at(0, N_BRANCH - 1)(i, n, s), jnp.minimum(n, N_BRANCH - 1))),
            pl.BlockSpec((None, BRANCH_W, d), lambda i, n, s: (n, 0, 0)),
        ],
        out_specs=xo,
        out_shape=jax.ShapeDtypeStruct((t, d), f32),
        scratch_shapes=[pltpu.VMEM((sub, tm, d), f32), pltpu.VMEM((sub, tm, d), bf16), pltpu.VMEM((tm, LANE), f32)],
        compiler_params=_cparams(("arbitrary", "arbitrary", "arbitrary")),
        name="merge",
    )(x, mod_l, post, y_pool, y_na, y_gla, p, w_stack)


_IN_SPLITS = (POOL_WIDTH, NA_WIDTH, NA_WIDTH, NA_WIDTH, GLA_KW, GLA_KW, GLA_VW, 2 * GLA_RANK, GLA_VW, GATE_W)
_IN_OFFS = tuple(int(v) for v in np.cumsum((0,) + _IN_SPLITS))
_IN_RUNS = ((_IN_OFFS[9], _IN_OFFS[10]), (_IN_OFFS[0], _IN_OFFS[7]), (_IN_OFFS[8], _IN_OFFS[9]),
            (_IN_OFFS[7], _IN_OFFS[8]))


CT_IN = 512
_IN_TILE_STARTS = []
for _a, _b in _IN_RUNS:
    _IN_TILE_STARTS += [_a + CT_IN * _t for _t in range(-(-(_b - _a) // CT_IN))]
assert len(_IN_TILE_STARTS) * CT_IN == IN_COLS_P and all(v % 8 == 0 for v in _IN_TILE_STARTS)
_IN_LAST_VALID = (_IN_RUNS[-1][1] - _IN_RUNS[-1][0]) % CT_IN or CT_IN


assert all((b - a) % CT_IN == 0 for a, b in _IN_RUNS[:-1])


def _w_in_tile_start(j):
    out = jnp.int32(_IN_TILE_STARTS[0]) + CT_IN * j
    for t in range(1, len(_IN_TILE_STARTS)):
        if _IN_TILE_STARTS[t] != _IN_TILE_STARTS[t - 1] + CT_IN:
            out = jnp.where(j >= t, _IN_TILE_STARTS[t] + CT_IN * (j - t), out)
    return out


def _cast_w_in_body(w_ref, o_ref):
    last = pl.program_id(1) == pl.num_programs(1) - 1
    col = lax.broadcasted_iota(jnp.int32, (1, CT_IN), 1)
    valid = jnp.where(last, _IN_LAST_VALID, CT_IN)
    o_ref[...] = jnp.where(col < valid, w_ref[...].T, 0.0).astype(bf16)


def _cast_w_in(w):
    depth, d, n = w.shape
    wt = jnp.swapaxes(w, 1, 2).reshape(depth * n, d)
    return pl.pallas_call(
        _cast_w_in_body,
        grid=(depth, IN_COLS_P // CT_IN),
        in_specs=[pl.BlockSpec((pl.Element(CT_IN), pl.Element(d)), lambda l, j: (pl.multiple_of(l * n + _w_in_tile_start(j), 8), 0))],
        out_specs=pl.BlockSpec((None, d, CT_IN), lambda l, j: (l, 0, j)),
        out_shape=jax.ShapeDtypeStruct((depth, d, IN_COLS_P), bf16),
        compiler_params=_cparams(("parallel", "parallel")),
        name="cast_w_in",
    )(wt)


def _cast_ffn_in_body(w_ref, o_ref):
    pad = jnp.zeros((w_ref.shape[0], D_FF_P - D_FF), bf16)
    for half in range(2):
        o_ref[half, :, 0:D_FF] = w_ref[:, half * D_FF:(half + 1) * D_FF].astype(bf16)
        o_ref[half, :, D_FF:D_FF_P] = pad


def _cast_ffn_in(w):
    depth, ns, d, _ = w.shape
    rows = 256
    return pl.pallas_call(
        _cast_ffn_in_body,
        grid=(depth, ns, d // rows),
        in_specs=[pl.BlockSpec((None, None, rows, 2 * D_FF), lambda l, s, r: (l, s, r, 0))],
        out_specs=pl.BlockSpec((None, None, 2, rows, D_FF_P), lambda l, s, r: (l, s, 0, r, 0)),
        out_shape=jax.ShapeDtypeStruct((depth, ns, 2, d, D_FF_P), bf16),
        compiler_params=_cparams(("parallel", "parallel", "parallel")),
        name="cast_ffn_in",
    )(w)


def _cast_ffn_out_body(w_ref, o_ref):
    row = pl.program_id(2) * TF + lax.broadcasted_iota(jnp.int32, (TF, 1), 0)
    o_ref[...] = jnp.where(row < D_FF, w_ref[...], 0.0).astype(bf16)


def _cast_ffn_out(w):
    depth, ns, _, d = w.shape
    return pl.pallas_call(
        _cast_ffn_out_body,
        grid=(depth, ns, D_FF_P // TF),
        in_specs=[pl.BlockSpec((None, None, TF, d), lambda l, s, j: (l, s, j, 0))],
        out_specs=pl.BlockSpec((None, None, TF, d), lambda l, s, j: (l, s, j, 0)),
        out_shape=jax.ShapeDtypeStruct((depth, ns, D_FF_P, d), bf16),
        compiler_params=_cparams(("parallel", "parallel", "parallel")),
        name="cast_ffn_out",
    )(w)


def _prep_gate(w_gate):
    out = jnp.zeros((2, LANE, GLA_KW), f32)
    for d in range(2):
        out = out.at[d, d * GLA_RANK:(d + 1) * GLA_RANK].set(w_gate[d])
    return out.astype(bf16)


def kernel(x_prompt, x_sample, c, cache_na_k, cache_na_v, state_gla, c_ctx, w_mod, b_mod, norm_pre, norm_post,
           w_ffn_in, w_ffn_out, w_in, pool_w, pool_scale, na_rpb, gla_w_gate, gla_b_gate, gla_norm, w_branch,
           w_out):
    nb, seq, d = x_prompt.shape
    ndec, dseq, _ = x_sample.shape
    depth = w_mod.shape[0]
    n_ctx = nb * seq
    n_lat = ndec * dseq
    assert (seq, dseq, d) == (SEQ, DEC_SEQ, D_MODEL) and n_ctx % (TM_MERGE * MERGE_SUB) == 0 and n_ctx % TM_FFN == 0 and dseq % TM_FFN == 0

    xs = [x_prompt.reshape(n_ctx, d), x_sample.reshape(n_lat, d)]
    conds = [(0, n_ctx), (1, dseq)]
    ncond = -(-(1 + ndec) // 8) * 8
    c_all = jnp.concatenate([c_ctx[None], c, jnp.zeros((ncond - 1 - ndec, d), f32)], axis=0)
    mod = _modulation(c_all, w_mod, b_mod)
    rope_tabs = _rope_tables(dseq)
    bias_tbl = _na_bias_table(na_rpb)

    w_ffn_in5 = _cast_ffn_in(w_ffn_in)
    w_ffn_out4 = _cast_ffn_out(w_ffn_out)
    w_in_p = _cast_w_in(w_in)
    w_stack_all = jnp.concatenate(
        [w_branch.astype(bf16), w_out.reshape(depth, d // BRANCH_W, BRANCH_W, d).astype(bf16)], axis=1)

    caches = None
    new_s = None
    for l in range(depth):
        mod_l = mod[l].reshape(ncond, 1, N_MOD * d)
        pre = norm_pre[l].reshape(3, 1, d)
        post = norm_post[l].reshape(3, 1, d)
        w_stack = w_stack_all[l]
        wgate_p = _prep_gate(gla_w_gate[l])
        pw = pool_w[l].astype(bf16)
        psc = pool_scale[l].reshape(1, POOL_WIDTH)

        xs = [_ffn(x, mod_l, pre[0], post[0], w_ffn_in5, w_ffn_out4, l, 0, 0, cond) for x, cond in zip(xs, conds)]
        p_ctx, p_lat = [_inproj(x, mod_l, pre[1], w_in_p, l, cond, dt) for x, cond, dt in zip(xs, conds, (f32, bf16))]

        y_pool = [_pool(p_ctx, pw, psc, seq, 0, nb), _pool(p_lat, pw, psc, dseq, 0, ndec)]
        na_ctx, new_k, new_v = _ctx_attn(p_ctx, nb, l, depth, caches)
        caches = (new_k, new_v)
        y_na = [na_ctx, _na_latent(p_lat, cache_na_k, cache_na_v, bias_tbl, l, 0, ndec)]
        g_ctx, new_s = _gla(p_ctx, wgate_p, gla_b_gate[l], gla_norm[l], seq, 0, nb, layer=l, with_sfin=True,
                            depth=depth, carried=new_s)
        (g_lat,) = _gla(p_lat, wgate_p, gla_b_gate[l], gla_norm[l], dseq, 0, ndec, rope_tabs=rope_tabs,
                        state=state_gla, layer=l)
        y_gla = [g_ctx, g_lat]

        xs = [_merge(x, mod_l, post[1], yp, yn, yg, p, w_stack, cond)
              for x, yp, yn, yg, p, cond in zip(xs, y_pool, y_na, y_gla, (p_ctx, p_lat), conds)]
        xs = [_ffn(x, mod_l, pre[2], post[2], w_ffn_in5, w_ffn_out4, l, 1, 2, cond) for x, cond in zip(xs, conds)]

    return (xs[0].reshape(nb, seq, d), xs[1].reshape(ndec, dseq, d), caches[0], caches[1], new_s)
```

```python
import functools

import numpy as np
import jax
import jax.numpy as jnp
from jax import lax
from jax.experimental import pallas as pl
from jax.experimental.pallas import tpu as pltpu

f32 = jnp.float32
bf16 = jnp.bfloat16

D_MODEL = 2048
SEQ = 256
DEC_SEQ = 2048
GRID_W = 64
N_MOD = 9
D_FF = 5504
FFN_RES = 0.5
EPS = 1e-6
NEG_INF = -1e30

POOL_GROUPS = 4
POOL_WINDOWS = (2, 4, 8, 16)
POOL_WIDTH = 1024
POOL_GC = POOL_WIDTH // POOL_GROUPS

NA_HEADS = 8
NA_HEAD_DIM = 128
NA_WIDTH = NA_HEADS * NA_HEAD_DIM
NA_WIN_H = 8
NA_WIN_W = 16

GLA_HEADS = 4
GLA_DK = 128
GLA_DV = 256
GLA_KW = GLA_HEADS * GLA_DK
GLA_VW = GLA_HEADS * GLA_DV
GLA_RANK = 16
GLA_TAU = 16.0
GLA_CHUNK = 64
ROPE_BASE = 10000.0

BRANCH_W = 1024
N_BRANCH = 3
GATE_W = N_BRANCH * D_MODEL

LANE = 128
VMEM_LIMIT = 56 * 1024 * 1024

OFF_GL = 0
OFF_POOL = OFF_GL + GATE_W
OFF_NQ = OFF_POOL + POOL_WIDTH
OFF_NK = OFF_NQ + NA_WIDTH
OFF_NV = OFF_NK + NA_WIDTH
OFF_GQ = OFF_NV + NA_WIDTH
OFF_GK = OFF_GQ + GLA_KW
OFF_GV = OFF_GK + GLA_KW
OFF_GR = OFF_GV + GLA_VW
OFF_GZ = OFF_GR + GLA_VW
TN_IN = 2304
IN_COLS_P = -(-(OFF_GZ + LANE) // TN_IN) * TN_IN

TM = 512
TF = 512
D_FF_P = -(-D_FF // TF) * TF


def _cparams(sem):
    return pltpu.CompilerParams(dimension_semantics=sem, vmem_limit_bytes=VMEM_LIMIT)


def _cond_index(i, tm, cond):
    return cond[0] + (i * tm) // cond[1]


def _rms(x, g):
    return x * lax.rsqrt(jnp.mean(x * x, axis=-1, keepdims=True) + EPS) * g


ROW_CHUNK = 16


def _row_sweep(nrows, fn, unroll=4):
    def trip(i, carry):
        fn(pl.ds(pl.multiple_of(i * ROW_CHUNK, ROW_CHUNK), ROW_CHUNK))
        return carry

    lax.fori_loop(0, nrows // ROW_CHUNK, trip, 0, unroll=unroll)


def _row_rsqrt(x_ref, r_scr):
    n = x_ref.shape[1]

    def fn(rows):
        x = x_ref[rows, :]
        ss = jnp.sum(_lane_fold(x * x, jnp.add), axis=-1, keepdims=True)
        r_scr[rows, :] = jnp.broadcast_to(lax.rsqrt(ss * (1.0 / n) + EPS), (ROW_CHUNK, LANE))

    _row_sweep(x_ref.shape[0], fn, unroll=16)


def _lanes(r, n):
    return jnp.concatenate([r] * (n // LANE), axis=1)


def _norm_modulate(x_ref, mod_ref, g_ref, h_ref, r_scr):
    d = D_MODEL
    shift = mod_ref[:, 0:d]
    w = g_ref[...] * (1.0 + mod_ref[:, d:2 * d])
    _row_rsqrt(x_ref, r_scr)

    def fn(rows):
        h_ref[rows, :] = (x_ref[rows, :] * _lanes(r_scr[rows, :], d) * w + shift).astype(h_ref.dtype)

    _row_sweep(x_ref.shape[0], fn)


def _norm_gate_residual(y_ref, x_ref, mod_ref, g_ref, o_ref, r_scr, res_weight):
    d = D_MODEL
    w = (res_weight * mod_ref[:, 2 * d:3 * d]) * g_ref[...]
    _row_rsqrt(y_ref, r_scr)

    def fn(rows):
        o_ref[rows, :] = x_ref[rows, :] + y_ref[rows, :] * _lanes(r_scr[rows, :], d) * w

    _row_sweep(x_ref.shape[0], fn)


def _dot(a, b):
    return jnp.dot(a, b, preferred_element_type=f32)


def _dot_nt(a, b):
    return lax.dot_general(a, b, (((1,), (1,)), ((), ())), preferred_element_type=f32)


def _mod_body(c_ref, w_ref, b_ref, o_ref):
    c = c_ref[...]
    s = c * jax.nn.sigmoid(c)
    o_ref[...] = _dot(s.astype(bf16), w_ref[...].astype(bf16)) + b_ref[...]


def _modulation(c_all, w_mod, b_mod):
    depth, d, n = w_mod.shape
    nc = c_all.shape[0]
    tn = 1024
    return pl.pallas_call(
        _mod_body,
        grid=(depth, n // tn),
        in_specs=[
            pl.BlockSpec((nc, d), lambda l, j: (0, 0)),
            pl.BlockSpec((None, d, tn), lambda l, j: (l, 0, j)),
            pl.BlockSpec((None, 1, tn), lambda l, j: (l, 0, j)),
        ],
        out_specs=pl.BlockSpec((None, nc, tn), lambda l, j: (l, 0, j)),
        out_shape=jax.ShapeDtypeStruct((depth, nc, n), f32),
        compiler_params=_cparams(("parallel", "parallel")),
        name="modulation",
    )(c_all, w_mod, b_mod.reshape(depth, 1, n))


TM_FFN = 1024


def _ffn_body(x_ref, mod_ref, pre_ref, post_ref, wg_ref, wu_ref, wo_ref, o_ref, h_scr, r_scr):
    f = pl.program_id(1)

    @pl.when(f == 0)
    def _():
        _norm_modulate(x_ref, mod_ref, pre_ref, h_scr, r_scr)
        o_ref[...] = jnp.zeros_like(o_ref)

    h = h_scr[...]
    gt = _dot(h, wg_ref[...])
    up = _dot(h, wu_ref[...])
    a = gt * jax.nn.sigmoid(gt) * up
    o_ref[...] += _dot(a.astype(bf16), wo_ref[...])

    @pl.when(f == pl.num_programs(1) - 1)
    def _():
        _norm_gate_residual(o_ref, x_ref, mod_ref, post_ref, o_ref, r_scr, FFN_RES)


def _ffn(x, mod_l, pre, post, w_in5, w_out4, layer, slot, sub, cond):
    t, d = x.shape
    fp = w_out4.shape[2]
    w_half = lambda half: pl.BlockSpec((None, None, None, d, TF), lambda i, f: (layer, slot, half, 0, f))
    return pl.pallas_call(
        _ffn_body,
        grid=(t // TM_FFN, fp // TF),
        in_specs=[
            pl.BlockSpec((TM_FFN, d), lambda i, f: (i, 0)),
            pl.BlockSpec((None, 1, 3 * d), lambda i, f: (_cond_index(i, TM_FFN, cond), 0, sub)),
            pl.BlockSpec((1, d), lambda i, f: (0, 0)),
            pl.BlockSpec((1, d), lambda i, f: (0, 0)),
            w_half(0),
            w_half(1),
            pl.BlockSpec((None, None, TF, d), lambda i, f: (layer, slot, f, 0)),
        ],
        out_specs=pl.BlockSpec((TM_FFN, d), lambda i, f: (i, 0)),
        out_shape=jax.ShapeDtypeStruct((t, d), f32),
        scratch_shapes=[pltpu.VMEM((TM_FFN, d), bf16), pltpu.VMEM((TM_FFN, LANE), f32)],
        compiler_params=_cparams(("parallel", "arbitrary")),
        name="ffn",
    )(x, mod_l, pre, post, w_in5, w_in5, w_out4)


def _inproj_body(x_ref, mod_ref, pre_ref, w_ref, o_ref, h_scr, r_scr):
    d = D_MODEL

    @pl.when(pl.program_id(1) == 0)
    def _():
        _norm_modulate(x_ref, mod_ref, pre_ref, h_scr, r_scr)

    o_ref[...] = _dot(h_scr[...], w_ref[...]).astype(o_ref.dtype)


def _inproj(x, mod_l, pre, w_in_p, layer, cond, out_dtype):
    t, d = x.shape
    tm = TM * (4 // jnp.dtype(out_dtype).itemsize)
    n = w_in_p.shape[2]
    return pl.pallas_call(
        _inproj_body,
        grid=(t // tm, n // TN_IN),
        in_specs=[
            pl.BlockSpec((tm, d), lambda i, j: (i, 0)),
            pl.BlockSpec((None, 1, 3 * d), lambda i, j: (_cond_index(i, tm, cond), 0, 1)),
            pl.BlockSpec((1, d), lambda i, j: (0, 0)),
            pl.BlockSpec((None, d, TN_IN), lambda i, j: (layer, 0, j)),
        ],
        out_specs=pl.BlockSpec((tm, TN_IN), lambda i, j: (i, j)),
        out_shape=jax.ShapeDtypeStruct((t, n), out_dtype),
        scratch_shapes=[pltpu.VMEM((tm, d), bf16), pltpu.VMEM((tm, LANE), f32)],
        compiler_params=_cparams(("parallel", "arbitrary")),
        name="inproj",
    )(x, mod_l, pre, w_in_p)


POOL_PAD = 8


def _pool_body(u_ref, w_ref, sc_ref, o_ref, pad_scr, *, seq):
    gc = POOL_GC
    zeros = jnp.zeros((POOL_PAD, POOL_WIDTH), f32)
    pad_scr[pl.ds(0, POOL_PAD), :] = zeros
    pad_scr[pl.ds(POOL_PAD + seq, POOL_PAD), :] = zeros
    pad_scr[pl.ds(POOL_PAD, seq), :] = u_ref[...].astype(f32)
    t = lax.broadcasted_iota(jnp.int32, (seq, 1), 0)
    for gi, win in enumerate(POOL_WINDOWS):
        cols = pl.ds(gi * gc, gc)
        lo = jnp.maximum(t - win // 2, 0)
        hi = jnp.minimum(t + win - 1 - win // 2, seq - 1)
        cnt = (hi - lo + 1).astype(f32)
        acc = pad_scr[pl.ds(POOL_PAD - win // 2, seq), cols]
        for j in range(1, win):
            acc = acc + pad_scr[pl.ds(POOL_PAD - win // 2 + j, seq), cols]
        pooled = acc / cnt - pad_scr[pl.ds(POOL_PAD, seq), cols]
        y = _dot(pooled.astype(bf16), w_ref[gi])
        o_ref[:, cols] = (y * sc_ref[:, cols]).astype(o_ref.dtype)


def _pool(p, pool_w, pool_scale, seq, row_block0, nseq):
    cb = OFF_POOL // POOL_WIDTH
    return pl.pallas_call(
        functools.partial(_pool_body, seq=seq),
        grid=(nseq,),
        in_specs=[
            pl.BlockSpec((seq, POOL_WIDTH), lambda s: (row_block0 + s, cb)),
            pl.BlockSpec((POOL_GROUPS, POOL_GC, POOL_GC), lambda s: (0, 0, 0)),
            pl.BlockSpec((1, POOL_WIDTH), lambda s: (0, 0)),
        ],
        out_specs=pl.BlockSpec((seq, POOL_WIDTH), lambda s: (s, 0)),
        out_shape=jax.ShapeDtypeStruct((nseq * seq, POOL_WIDTH), bf16),
        scratch_shapes=[pltpu.VMEM((seq + 2 * POOL_PAD, POOL_WIDTH), f32)],
        compiler_params=_cparams(("parallel",)),
        name="pool",
    )(p, pool_w, pool_scale)


def _lane_fold(x, op):
    parts = [x[:, i * LANE:(i + 1) * LANE] for i in range(x.shape[1] // LANE)]
    while len(parts) > 1:
        parts = [op(parts[i], parts[i + 1]) for i in range(0, len(parts) - 1, 2)] + parts[len(parts) & ~1:]
    return parts[0]


def _softmax_rows(s):
    m = jnp.max(_lane_fold(s, jnp.maximum), axis=-1, keepdims=True)
    e = jnp.exp(s - m)
    return e / jnp.sum(_lane_fold(e, jnp.add), axis=-1, keepdims=True)


def _ctx_attn_body(q_ref, k_ref, v_ref, *refs):
    o_ref, nk_ref, nv_ref = refs[-3:]
    hd = NA_HEAD_DIM
    for h in range(NA_HEADS):
        cols = pl.ds(h * hd, hd)
        kf = k_ref[:, cols]
        vf = v_ref[:, cols]
        nk_ref[h] = kf
        nv_ref[h] = vf
        p = _softmax_rows(_dot_nt(q_ref[:, cols].astype(bf16), kf.astype(bf16)) * (hd ** -0.5))
        o_ref[:, cols] = _dot(p.astype(bf16), vf.astype(bf16)).astype(o_ref.dtype)


def _ctx_attn(p, nseq, layer, depth, caches=None):
    spec = lambda off: pl.BlockSpec((SEQ, NA_WIDTH), lambda b: (b, off // NA_WIDTH))
    cache_spec = pl.BlockSpec((None, None, NA_HEADS, SEQ, NA_HEAD_DIM), lambda b: (b, layer, 0, 0, 0))
    cache_shape = jax.ShapeDtypeStruct((nseq, depth, NA_HEADS, SEQ, NA_HEAD_DIM), f32)
    in_specs = [spec(OFF_NQ), spec(OFF_NK), spec(OFF_NV)]
    args = [p, p, p]
    aliases = {}
    if caches is not None:
        in_specs += [pl.BlockSpec(memory_space=pl.ANY)] * 2
        args += list(caches)
        aliases = {3: 1, 4: 2}
    return pl.pallas_call(
        _ctx_attn_body,
        grid=(nseq,),
        in_specs=in_specs,
        out_specs=[pl.BlockSpec((SEQ, NA_WIDTH), lambda b: (b, 0)), cache_spec, cache_spec],
        out_shape=[jax.ShapeDtypeStruct((nseq * SEQ, NA_WIDTH), bf16), cache_shape, cache_shape],
        input_output_aliases=aliases,
        compiler_params=_cparams(("parallel",)),
        name="ctx_attn",
    )(*args)


def _na_bias_table(rpb):
    qc = np.arange(GRID_W)[:, None]
    kc = np.arange(GRID_W)[None, :]
    cs = np.clip(qc - NA_WIN_W // 2, 0, GRID_W - NA_WIN_W)
    ok = (kc >= cs) & (kc < cs + NA_WIN_W)
    cidx = np.clip(kc - qc + NA_WIN_W - 1, 0, 2 * NA_WIN_W - 2)
    onehot = jnp.asarray((cidx[None] == np.arange(2 * NA_WIN_W - 1)[:, None, None]) & ok[None], f32)
    toep = jnp.einsum('...rc,cqk->...rqk', rpb.astype(f32), onehot, precision=lax.Precision.HIGHEST)
    toep = jnp.where(ok, toep, NEG_INF)
    return jnp.concatenate([toep[..., :-1, :, :], toep[..., 1:, :, :]], axis=-1)


NA_UNROLL = 8
NA_CTX_ROWS = 256


def _na_body(qb_scr, kb_scr, vb_scr, ck_ref, cv_ref, bias_ref, o_ref,
             sl_scr, sc_scr, el_scr, ec_scr, den_scr, oc_scr, *, rows):
    hd = NA_HEAD_DIM
    scale = hd ** -0.5
    kh = min(NA_WIN_H, rows)
    nloc = kh * GRID_W
    n = rows * GRID_W
    assert qb_scr.dtype == bf16
    ck = ck_ref[...].astype(bf16)
    cv = cv_ref[...].astype(bf16)

    def row_slices(r):
        rs = jnp.clip(r - kh // 2, 0, rows - kh)
        q_rows = pl.ds(pl.multiple_of(r * GRID_W, GRID_W), GRID_W)
        k_rows = pl.ds(pl.multiple_of(rs * GRID_W, GRID_W), nloc)
        return rs, q_rows, k_rows

    def ctx_scores(i, carry):
        blk = pl.ds(pl.multiple_of(i * NA_CTX_ROWS, NA_CTX_ROWS), NA_CTX_ROWS)
        sc_scr[blk, :] = _dot_nt(qb_scr[blk, :], ck) * scale
        return carry

    lax.fori_loop(0, n // NA_CTX_ROWS, ctx_scores, 0, unroll=2)

    def loc_scores(r, carry):
        rs, q_rows, k_rows = row_slices(r)
        first = rs - r + NA_WIN_H - 1
        bias = jnp.concatenate([bias_ref[first + 2 * e] for e in range(kh // 2)], axis=1)
        sl_scr[q_rows, :] = _dot_nt(qb_scr[q_rows, :], kb_scr[k_rows, :]) * scale + bias
        return carry

    lax.fori_loop(0, rows, loc_scores, 0, unroll=NA_UNROLL)

    def numerators(r, carry):
        q_rows = pl.ds(pl.multiple_of(r * GRID_W, GRID_W), GRID_W)
        s_loc = sl_scr[q_rows, :]
        s_ctx = sc_scr[q_rows, :]
        m = jnp.max(jnp.maximum(_lane_fold(s_loc, jnp.maximum), _lane_fold(s_ctx, jnp.maximum)),
                    axis=-1, keepdims=True)
        e_loc = jnp.exp(s_loc - m)
        e_ctx = jnp.exp(s_ctx - m)
        den = jnp.sum(_lane_fold(e_loc, jnp.add) + _lane_fold(e_ctx, jnp.add), axis=-1, keepdims=True)
        el_scr[q_rows, :] = e_loc.astype(bf16)
        ec_scr[q_rows, :] = e_ctx.astype(bf16)
        den_scr[q_rows, :] = jnp.broadcast_to(den, (GRID_W, hd))
        return carry

    lax.fori_loop(0, rows, numerators, 0, unroll=NA_UNROLL)

    def ctx_values(i, carry):
        blk = pl.ds(pl.multiple_of(i * NA_CTX_ROWS, NA_CTX_ROWS), NA_CTX_ROWS)
        oc_scr[blk, :] = _dot(ec_scr[blk, :], cv)
        return carry

    lax.fori_loop(0, n // NA_CTX_ROWS, ctx_values, 0, unroll=2)

    def loc_values(r, carry):
        _, q_rows, k_rows = row_slices(r)
        o = _dot(el_scr[q_rows, :], vb_scr[k_rows, :]) + oc_scr[q_rows, :]
        o_ref[q_rows, :] = (o / den_scr[q_rows, :]).astype(o_ref.dtype)
        return carry

    lax.fori_loop(0, rows, loc_values, 0, unroll=NA_UNROLL)


def _na_latent(p, cache_k, cache_v, bias_tbl, layer, row_block0, nreq):
    n = DEC_SEQ
    hd = NA_HEAD_DIM
    past = cache_k.shape[3]
    rows = n // GRID_W
    qkv = lambda off: pl.BlockSpec((n, hd), lambda b, h: (row_block0 + b, off // hd + h))
    cache = pl.BlockSpec((None, None, None, past, hd), lambda b, h: (b, layer, h, 0, 0))
    return pl.pallas_call(
        functools.partial(_na_body, rows=rows),
        grid=(nreq, NA_HEADS),
        in_specs=[qkv(OFF_NQ), qkv(OFF_NK), qkv(OFF_NV), cache, cache,
                  pl.BlockSpec((None, None, 2 * NA_WIN_H - 2, GRID_W, 2 * GRID_W),
                               lambda b, h: (layer, h, 0, 0, 0))],
        out_specs=pl.BlockSpec((n, hd), lambda b, h: (b, h)),
        out_shape=jax.ShapeDtypeStruct((nreq * n, NA_WIDTH), bf16),
        scratch_shapes=[pltpu.VMEM((n, NA_WIN_H * GRID_W), f32), pltpu.VMEM((n, past), f32),
                        pltpu.VMEM((n, NA_WIN_H * GRID_W), bf16), pltpu.VMEM((n, past), bf16),
                        pltpu.VMEM((n, hd), f32), pltpu.VMEM((n, hd), f32)],
        compiler_params=_cparams(("parallel", "parallel")),
        name="na_latent",
    )(p, p, p, cache_k, cache_v, bias_tbl)


GLA_PAD = 32
GLA_UNROLL = 4


def _rope_tables(seq):
    t = np.arange(seq)
    half = GLA_DK // 2
    nf = half // 2
    inv = ROPE_BASE ** (-np.arange(nf, dtype=np.float64) / nf)
    cos, sin = [], []
    for pos in (t // GRID_W, t % GRID_W):
        ang = pos[:, None].astype(np.float64) * inv
        cos += [np.cos(ang), np.cos(ang)]
        sin += [-np.sin(ang), np.sin(ang)]
    return (jnp.asarray(np.concatenate(cos, axis=-1), f32), jnp.asarray(np.concatenate(sin, axis=-1), f32))


def _rope(x, cos, sin_signed):
    nf = GLA_DK // 4
    lane = lax.broadcasted_iota(jnp.int32, x.shape, 1)
    partner = jnp.where(lane % (2 * nf) < nf, pltpu.roll(x, GLA_DK - nf, 1), pltpu.roll(x, nf, 1))
    return x * cos + partner * sin_signed


def _log_sigmoid(x):
    return jnp.minimum(x, 0.0) - jnp.log1p(jnp.exp(-jnp.abs(x)))


def _gla_body(*refs, seq, rope, with_s0, with_sfin, n_carried):
    refs = list(refs)
    q_ref, k_ref, v_ref, r_ref, z_ref, wg_ref, bg_ref, ng_ref = refs[:8]
    refs = refs[8:]
    if rope:
        cos_ref, sin_ref = refs[:2]
        refs = refs[2:]
    if with_s0:
        s0_ref = refs[0]
        refs = refs[1:]
    refs = refs[n_carried:]
    o_ref = refs[0]
    refs = refs[1:]
    if with_sfin:
        sfin_ref = refs[0]
        refs = refs[1:]
    qi_scr, kn_scr, kd_scr, dec_scr, scan_scr, vb_scr, u_scr, sb_scr, o_scr, st_scr = refs

    ch = GLA_CHUNK
    nch = seq // ch
    dk, dv = GLA_DK, GLA_DV

    q = q_ref[...].astype(f32)
    k = k_ref[...].astype(f32)
    if rope:
        q = _rope(q, cos_ref[...], sin_ref[...])
        k = _rope(k, cos_ref[...], sin_ref[...])
    q = q * (dk ** -0.5)

    zb = z_ref[...].astype(bf16)
    pos = lax.broadcasted_iota(jnp.int32, (seq, 1), 0) % ch
    zpad = jnp.zeros((GLA_PAD, dk), f32)
    scan_scr[pl.ds(0, GLA_PAD), :] = zpad
    scan_scr[pl.ds(GLA_PAD + seq, GLA_PAD), :] = zpad
    for d in range(2):
        g = _log_sigmoid(_dot(zb, wg_ref[d]) + bg_ref[d]) / GLA_TAU
        b = g
        sh = 1
        while sh < ch:
            scan_scr[pl.ds(GLA_PAD, seq), :] = b
            if d == 0:
                b = b + jnp.where(pos >= sh, scan_scr[pl.ds(GLA_PAD - sh, seq), :], 0.0)
            else:
                b = b + jnp.where(pos < ch - sh, scan_scr[pl.ds(GLA_PAD + sh, seq), :], 0.0)
            sh *= 2
        b3 = b.reshape(nch, ch, dk)
        b_end = b3[:, ch - 1:ch, :] if d == 0 else b3[:, 0:1, :]
        lanes = pl.ds(d * dk, dk)
        qi_scr[:, lanes] = (q * jnp.exp(b)).astype(bf16)
        kn_scr[d] = (k * jnp.exp(-b)).astype(bf16)
        kd_scr[:, lanes] = (k.reshape(nch, ch, dk) * jnp.exp(b_end - b3)).reshape(seq, dk).astype(bf16)
        dec_scr[d] = jnp.exp(b_end)

    for d in range(2):
        if with_s0:
            st_scr[d] = s0_ref[d].T
        else:
            st_scr[d] = jnp.zeros((dv, dk), f32)

    ri = lax.broadcasted_iota(jnp.int32, (ch, ch), 0)
    ci = lax.broadcasted_iota(jnp.int32, (ch, ch), 1)

    vb_scr[...] = v_ref[...].astype(bf16)
    chunk_rows = lambda c: pl.ds(pl.multiple_of(c * ch, ch), ch)
    fwd, bwd = pl.ds(0, dk), pl.ds(dk, dk)

    def increments(c, carry):
        rows = chunk_rows(c)
        u_scr[c] = lax.dot_general(vb_scr[rows, :], kd_scr[rows, :], (((0,), (0,)), ((), ())),
                                   preferred_element_type=f32)
        return carry

    lax.fori_loop(0, nch, increments, 0, unroll=GLA_UNROLL)

    def states(i, carry):
        for d, c, lanes in ((0, i, fwd), (1, nch - 1 - i, bwd)):
            st = st_scr[d]
            sb_scr[c, :, lanes] = st.astype(bf16)
            st_scr[d] = st * dec_scr[d, c] + u_scr[c, :, lanes]
        return carry

    lax.fori_loop(0, nch, states, 0)

    if with_sfin:
        for d in range(2):
            sfin_ref[d] = st_scr[d].T

    def outputs(c, carry):
        rows = chunk_rows(c)
        qi = qi_scr[rows, :]
        pf = _dot_nt(qi[:, 0:dk], kn_scr[0, rows, :])
        pb = _dot_nt(qi[:, dk:2 * dk], kn_scr[1, rows, :])
        a = jnp.where(ci < ri, pf, jnp.where(ci > ri, pb, pf + pb))
        o_scr[rows, :] = _dot(a.astype(bf16), vb_scr[rows, :]) + _dot_nt(qi, sb_scr[c])
        return carry

    lax.fori_loop(0, nch, outputs, 0, unroll=GLA_UNROLL)

    o = o_scr[...]
    r = r_ref[...].astype(f32)
    o = o * lax.rsqrt(jnp.mean(o * o, axis=-1, keepdims=True) + EPS) * ng_ref[...]
    o_ref[...] = (o * (r * jax.nn.sigmoid(r))).astype(o_ref.dtype)


def _gla(p, wgate_p, b_gate, gla_norm, seq, row_block0, nreq, rope_tabs=None, state=None, layer=0,
         with_sfin=False, depth=1, carried=None):
    dk, dv = GLA_DK, GLA_DV
    nch = seq // GLA_CHUNK
    rope = rope_tabs is not None
    with_s0 = state is not None
    blk = lambda w, off: pl.BlockSpec((seq, w), lambda b, h: (row_block0 + b, off // w + h))
    in_specs = [blk(dk, OFF_GQ), blk(dk, OFF_GK), blk(dv, OFF_GV), blk(dv, OFF_GR),
                pl.BlockSpec((seq, LANE), lambda b, h: (row_block0 + b, OFF_GZ // LANE)),
                pl.BlockSpec((2, LANE, dk), lambda b, h: (0, 0, h)),
                pl.BlockSpec((2, 1, dk), lambda b, h: (0, 0, h)),
                pl.BlockSpec((1, dv), lambda b, h: (0, h))]
    args = [p, p, p, p, p, wgate_p, b_gate.reshape(2, 1, GLA_KW), gla_norm.reshape(1, GLA_VW)]
    if rope:
        in_specs += [pl.BlockSpec((seq, dk), lambda b, h: (0, 0))] * 2
        args += list(rope_tabs)
    if with_s0:
        in_specs.append(pl.BlockSpec((None, None, 2, None, dk, dv), lambda b, h: (b, layer, 0, h, 0, 0)))
        args.append(state)
    aliases = {}
    if carried is not None:
        aliases = {len(args): 1}
        in_specs.append(pl.BlockSpec(memory_space=pl.ANY))
        args.append(carried)
    out_specs = [pl.BlockSpec((seq, dv), lambda b, h: (b, h))]
    out_shape = [jax.ShapeDtypeStruct((nreq * seq, GLA_VW), bf16)]
    if with_sfin:
        out_specs.append(pl.BlockSpec((None, None, 2, None, dk, dv), lambda b, h: (b, layer, 0, h, 0, 0)))
        out_shape.append(jax.ShapeDtypeStruct((nreq, depth, 2, GLA_HEADS, dk, dv), f32))
    scratch = [pltpu.VMEM((seq, 2 * dk), bf16), pltpu.VMEM((2, seq, dk), bf16), pltpu.VMEM((seq, 2 * dk), bf16),
               pltpu.VMEM((2, nch, 1, dk), f32), pltpu.VMEM((seq + 2 * GLA_PAD, dk), f32),
               pltpu.VMEM((seq, dv), bf16), pltpu.VMEM((nch, dv, 2 * dk), f32), pltpu.VMEM((nch, dv, 2 * dk), bf16),
               pltpu.VMEM((seq, dv), f32), pltpu.VMEM((2, dv, dk), f32)]
    return pl.pallas_call(
        functools.partial(_gla_body, seq=seq, rope=rope, with_s0=with_s0, with_sfin=with_sfin,
                          n_carried=len(aliases)),
        grid=(nreq, GLA_HEADS),
        in_specs=in_specs,
        out_specs=out_specs,
        out_shape=out_shape,
        input_output_aliases=aliases,
        scratch_shapes=scratch,
        compiler_params=_cparams(("parallel", "parallel")),
        name="gla",
    )(*args)


TM_MERGE = 512


MERGE_SUB = 2


def _merge_body(x_ref, mod_ref, post_ref, bp_ref, bn_ref, bg_ref, gl_ref, w_ref, o_ref, m_scr, mb_scr, r_scr):
    n = pl.program_id(1)
    m_ref = m_scr.at[pl.program_id(2)]
    mb_ref = mb_scr.at[pl.program_id(2)]

    for bi, br_ref in enumerate((bp_ref, bn_ref, bg_ref)):
        @pl.when(n == bi)
        def _():
            y = jax.nn.sigmoid(gl_ref[...].astype(f32)) * _dot(br_ref[...], w_ref[...])
            m_ref[...] = y if bi == 0 else m_ref[...] + y

    @pl.when(n == N_BRANCH)
    def _():
        mb_ref[...] = m_ref[...].astype(bf16)
        m_ref[...] = _dot(mb_ref[:, 0:BRANCH_W], w_ref[...])

    @pl.when(n == N_BRANCH + 1)
    def _():
        m_ref[...] += _dot(mb_ref[:, BRANCH_W:2 * BRANCH_W], w_ref[...])
        _norm_gate_residual(m_ref, x_ref, mod_ref, post_ref, o_ref, r_scr, 1.0)


def _merge(x, mod_l, post, y_pool, y_na, y_gla, p, w_stack, cond):
    t, d = x.shape
    tm = TM_MERGE
    sub = MERGE_SUB
    nsteps = N_BRANCH + d // BRANCH_W
    tile = lambda i, s: i * sub + s

    def rows_at(first, last):
        def index(i, n, s):
            return jnp.where(n < first, jnp.maximum(tile(i, 0) - 1, 0),
                             jnp.where(n > last, tile(i, sub - 1), tile(i, s)))
        return index

    last = nsteps - 1
    br = lambda step: pl.BlockSpec((tm, BRANCH_W), lambda i, n, s: (rows_at(step, step)(i, n, s), 0))
    xo = pl.BlockSpec((tm, d), lambda i, n, s: (rows_at(last, last)(i, n, s), 0))
    return pl.pallas_call(
        _merge_body,
        grid=(t // (tm * sub), nsteps, sub),
        in_specs=[
            xo,
            pl.BlockSpec((None, 1, 3 * d), lambda i, n, s: (_cond_index(tile(i, s), tm, cond), 0, 1)),
            pl.BlockSpec((1, d), lambda i, n, s: (0, 0)),
            br(0), br(1), br(2),
            pl.BlockSpec((tm, d),
                         lambda i, n, s: (rows_at(0, N_BRANCH - 1)(i, n, s), jnp.minimum(n, N_BRANCH - 1))),
            pl.BlockSpec((None, BRANCH_W, d), lambda i, n, s: (n, 0, 0)),
        ],
        out_specs=xo,
        out_shape=jax.ShapeDtypeStruct((t, d), f32),
        scratch_shapes=[pltpu.VMEM((sub, tm, d), f32), pltpu.VMEM((sub, tm, d), bf16), pltpu.VMEM((tm, LANE), f32)],
        compiler_params=_cparams(("arbitrary", "arbitrary", "arbitrary")),
        name="merge",
    )(x, mod_l, post, y_pool, y_na, y_gla, p, w_stack)


_IN_SPLITS = (POOL_WIDTH, NA_WIDTH, NA_WIDTH, NA_WIDTH, GLA_KW, GLA_KW, GLA_VW, 2 * GLA_RANK, GLA_VW, GATE_W)
_IN_OFFS = tuple(int(v) for v in np.cumsum((0,) + _IN_SPLITS))
_IN_RUNS = ((_IN_OFFS[9], _IN_OFFS[10]), (_IN_OFFS[0], _IN_OFFS[7]), (_IN_OFFS[8], _IN_OFFS[9]),
            (_IN_OFFS[7], _IN_OFFS[8]))


CT_IN = 512
_IN_TILE_STARTS = []
for _a, _b in _IN_RUNS:
    _IN_TILE_STARTS += [_a + CT_IN * _t for _t in range(-(-(_b - _a) // CT_IN))]
assert len(_IN_TILE_STARTS) * CT_IN == IN_COLS_P and all(v % 8 == 0 for v in _IN_TILE_STARTS)
_IN_LAST_VALID = (_IN_RUNS[-1][1] - _IN_RUNS[-1][0]) % CT_IN or CT_IN


assert all((b - a) % CT_IN == 0 for a, b in _IN_RUNS[:-1])


def _w_in_tile_start(j):
    out = jnp.int32(_IN_TILE_STARTS[0]) + CT_IN * j
    for t in range(1, len(_IN_TILE_STARTS)):
        if _IN_TILE_STARTS[t] != _IN_TILE_STARTS[t - 1] + CT_IN:
            out = jnp.where(j >= t, _IN_TILE_STARTS[t] + CT_IN * (j - t), out)
    return out


def _cast_w_in_body(w_ref, o_ref):
    last = pl.program_id(1) == pl.num_programs(1) - 1
    col = lax.broadcasted_iota(jnp.int32, (1, CT_IN), 1)
    valid = jnp.where(last, _IN_LAST_VALID, CT_IN)
    o_ref[...] = jnp.where(col < valid, w_ref[...].T, 0.0).astype(bf16)


def _cast_w_in(w):
    depth, d, n = w.shape
    wt = jnp.swapaxes(w, 1, 2).reshape(depth * n, d)
    return pl.pallas_call(
        _cast_w_in_body,
        grid=(depth, IN_COLS_P // CT_IN),
        in_specs=[pl.BlockSpec((pl.Element(CT_IN), pl.Element(d)), lambda l, j: (pl.multiple_of(l * n + _w_in_tile_start(j), 8), 0))],
        out_specs=pl.BlockSpec((None, d, CT_IN), lambda l, j: (l, 0, j)),
        out_shape=jax.ShapeDtypeStruct((depth, d, IN_COLS_P), bf16),
        compiler_params=_cparams(("parallel", "parallel")),
        name="cast_w_in",
    )(wt)


def _cast_ffn_in_body(w_ref, o_ref):
    pad = jnp.zeros((w_ref.shape[0], D_FF_P - D_FF), bf16)
    for half in range(2):
        o_ref[half, :, 0:D_FF] = w_ref[:, half * D_FF:(half + 1) * D_FF].astype(bf16)
        o_ref[half, :, D_FF:D_FF_P] = pad


def _cast_ffn_in(w):
    depth, ns, d, _ = w.shape
    rows = 256
    return pl.pallas_call(
        _cast_ffn_in_body,
        grid=(depth, ns, d // rows),
        in_specs=[pl.BlockSpec((None, None, rows, 2 * D_FF), lambda l, s, r: (l, s, r, 0))],
        out_specs=pl.BlockSpec((None, None, 2, rows, D_FF_P), lambda l, s, r: (l, s, 0, r, 0)),
        out_shape=jax.ShapeDtypeStruct((depth, ns, 2, d, D_FF_P), bf16),
        compiler_params=_cparams(("parallel", "parallel", "parallel")),
        name="cast_ffn_in",
    )(w)


def _cast_ffn_out_body(w_ref, o_ref):
    row = pl.program_id(2) * TF + lax.broadcasted_iota(jnp.int32, (TF, 1), 0)
    o_ref[...] = jnp.where(row < D_FF, w_ref[...], 0.0).astype(bf16)


def _cast_ffn_out(w):
    depth, ns, _, d = w.shape
    return pl.pallas_call(
        _cast_ffn_out_body,
        grid=(depth, ns, D_FF_P // TF),
        in_specs=[pl.BlockSpec((None, None, TF, d), lambda l, s, j: (l, s, j, 0))],
        out_specs=pl.BlockSpec((None, None, TF, d), lambda l, s, j: (l, s, j, 0)),
        out_shape=jax.ShapeDtypeStruct((depth, ns, D_FF_P, d), bf16),
        compiler_params=_cparams(("parallel", "parallel", "parallel")),
        name="cast_ffn_out",
    )(w)


def _prep_gate(w_gate):
    out = jnp.zeros((2, LANE, GLA_KW), f32)
    for d in range(2):
        out = out.at[d, d * GLA_RANK:(d + 1) * GLA_RANK].set(w_gate[d])
    return out.astype(bf16)


def kernel(x_prompt, x_sample, c, cache_na_k, cache_na_v, state_gla, c_ctx, w_mod, b_mod, norm_pre, norm_post,
           w_ffn_in, w_ffn_out, w_in, pool_w, pool_scale, na_rpb, gla_w_gate, gla_b_gate, gla_norm, w_branch,
           w_out):
    nb, seq, d = x_prompt.shape
    ndec, dseq, _ = x_sample.shape
    depth = w_mod.shape[0]
    n_ctx = nb * seq
    n_lat = ndec * dseq
    assert (seq, dseq, d) == (SEQ, DEC_SEQ, D_MODEL) and n_ctx % (TM_MERGE * MERGE_SUB) == 0 and n_ctx % TM_FFN == 0 and dseq % TM_FFN == 0

    xs = [x_prompt.reshape(n_ctx, d), x_sample.reshape(n_lat, d)]
    conds = [(0, n_ctx), (1, dseq)]
    ncond = -(-(1 + ndec) // 8) * 8
    c_all = jnp.concatenate([c_ctx[None], c, jnp.zeros((ncond - 1 - ndec, d), f32)], axis=0)
    mod = _modulation(c_all, w_mod, b_mod)
    rope_tabs = _rope_tables(dseq)
    bias_tbl = _na_bias_table(na_rpb)

    w_ffn_in5 = _cast_ffn_in(w_ffn_in)
    w_ffn_out4 = _cast_ffn_out(w_ffn_out)
    w_in_p = _cast_w_in(w_in)
    w_stack_all = jnp.concatenate(
        [w_branch.astype(bf16), w_out.reshape(depth, d // BRANCH_W, BRANCH_W, d).astype(bf16)], axis=1)

    caches = None
    new_s = None
    for l in range(depth):
        mod_l = mod[l].reshape(ncond, 1, N_MOD * d)
        pre = norm_pre[l].reshape(3, 1, d)
        post = norm_post[l].reshape(3, 1, d)
        w_stack = w_stack_all[l]
        wgate_p = _prep_gate(gla_w_gate[l])
        pw = pool_w[l].astype(bf16)
        psc = pool_scale[l].reshape(1, POOL_WIDTH)

        xs = [_ffn(x, mod_l, pre[0], post[0], w_ffn_in5, w_ffn_out4, l, 0, 0, cond) for x, cond in zip(xs, conds)]
        p_ctx, p_lat = [_inproj(x, mod_l, pre[1], w_in_p, l, cond, dt) for x, cond, dt in zip(xs, conds, (f32, bf16))]

        y_pool = [_pool(p_ctx, pw, psc, seq, 0, nb), _pool(p_lat, pw, psc, dseq, 0, ndec)]
        na_ctx, new_k, new_v = _ctx_attn(p_ctx, nb, l, depth, caches)
        caches = (new_k, new_v)
        y_na = [na_ctx, _na_latent(p_lat, cache_na_k, cache_na_v, bias_tbl, l, 0, ndec)]
        g_ctx, new_s = _gla(p_ctx, wgate_p, gla_b_gate[l], gla_norm[l], seq, 0, nb, layer=l, with_sfin=True,
                            depth=depth, carried=new_s)
        (g_lat,) = _gla(p_lat, wgate_p, gla_b_gate[l], gla_norm[l], dseq, 0, ndec, rope_tabs=rope_tabs,
                        state=state_gla, layer=l)
        y_gla = [g_ctx, g_lat]

        xs = [_merge(x, mod_l, post[1], yp, yn, yg, p, w_stack, cond)
              for x, yp, yn, yg, p, cond in zip(xs, y_pool, y_na, y_gla, (p_ctx, p_lat), conds)]
        xs = [_ffn(x, mod_l, pre[2], post[2], w_ffn_in5, w_ffn_out4, l, 1, 2, cond) for x, cond in zip(xs, conds)]

    return (xs[0].reshape(nb, seq, d), xs[1].reshape(ndec, dseq, d), caches[0], caches[1], new_s)
```

```python
import functools

import numpy as np
import jax
import jax.numpy as jnp
from jax import lax
from jax.experimental import pallas as pl
from jax.experimental.pallas import tpu as pltpu

f32 = jnp.float32
bf16 = jnp.bfloat16

D_MODEL = 2048
SEQ = 256
DEC_SEQ = 2048
GRID_W = 64
N_MOD = 9
D_FF = 5504
FFN_RES = 0.5
EPS = 1e-6
NEG_INF = -1e30

POOL_GROUPS = 4
POOL_WINDOWS = (2, 4, 8, 16)
POOL_WIDTH = 1024
POOL_GC = POOL_WIDTH // POOL_GROUPS

NA_HEADS = 8
NA_HEAD_DIM = 128
NA_WIDTH = NA_HEADS * NA_HEAD_DIM
NA_WIN_H = 8
NA_WIN_W = 16

GLA_HEADS = 4
GLA_DK = 128
GLA_DV = 256
GLA_KW = GLA_HEADS * GLA_DK
GLA_VW = GLA_HEADS * GLA_DV
GLA_RANK = 16
GLA_TAU = 16.0
GLA_CHUNK = 64
ROPE_BASE = 10000.0

BRANCH_W = 1024
N_BRANCH = 3
GATE_W = N_BRANCH * D_MODEL

LANE = 128
VMEM_LIMIT = 56 * 1024 * 1024

OFF_GL = 0
OFF_POOL = OFF_GL + GATE_W
OFF_NQ = OFF_POOL + POOL_WIDTH
OFF_NK = OFF_NQ + NA_WIDTH
OFF_NV = OFF_NK + NA_WIDTH
OFF_GQ = OFF_NV + NA_WIDTH
OFF_GK = OFF_GQ + GLA_KW
OFF_GV = OFF_GK + GLA_KW
OFF_GR = OFF_GV + GLA_VW
OFF_GZ = OFF_GR + GLA_VW
TN_IN = 2304
IN_COLS_P = -(-(OFF_GZ + LANE) // TN_IN) * TN_IN

TM = 512
TF = 512
D_FF_P = -(-D_FF // TF) * TF


def _cparams(sem):
    return pltpu.CompilerParams(dimension_semantics=sem, vmem_limit_bytes=VMEM_LIMIT)


def _cond_index(i, tm, cond):
    return cond[0] + (i * tm) // cond[1]


def _rms(x, g):
    return x * lax.rsqrt(jnp.mean(x * x, axis=-1, keepdims=True) + EPS) * g


ROW_CHUNK = 16


def _row_sweep(nrows, fn, unroll=4):
    def trip(i, carry):
        fn(pl.ds(pl.multiple_of(i * ROW_CHUNK, ROW_CHUNK), ROW_CHUNK))
        return carry

    lax.fori_loop(0, nrows // ROW_CHUNK, trip, 0, unroll=unroll)


def _row_rsqrt(x_ref, r_scr):
    n = x_ref.shape[1]

    def fn(rows):
        x = x_ref[rows, :]
        ss = jnp.sum(_lane_fold(x * x, jnp.add), axis=-1, keepdims=True)
        r_scr[rows, :] = jnp.broadcast_to(lax.rsqrt(ss * (1.0 / n) + EPS), (ROW_CHUNK, LANE))

    _row_sweep(x_ref.shape[0], fn, unroll=16)


def _lanes(r, n):
    return jnp.concatenate([r] * (n // LANE), axis=1)


def _norm_modulate(x_ref, mod_ref, g_ref, h_ref, r_scr):
    d = D_MODEL
    shift = mod_ref[:, 0:d]
    w = g_ref[...] * (1.0 + mod_ref[:, d:2 * d])
    _row_rsqrt(x_ref, r_scr)

    def fn(rows):
        h_ref[rows, :] = (x_ref[rows, :] * _lanes(r_scr[rows, :], d) * w + shift).astype(h_ref.dtype)

    _row_sweep(x_ref.shape[0], fn)


def _norm_gate_residual(y_ref, x_ref, mod_ref, g_ref, o_ref, r_scr, res_weight):
    d = D_MODEL
    w = (res_weight * mod_ref[:, 2 * d:3 * d]) * g_ref[...]
    _row_rsqrt(y_ref, r_scr)

    def fn(rows):
        o_ref[rows, :] = x_ref[rows, :] + y_ref[rows, :] * _lanes(r_scr[rows, :], d) * w

    _row_sweep(x_ref.shape[0], fn)


def _dot(a, b):
    return jnp.dot(a, b, preferred_element_type=f32)


def _dot_nt(a, b):
    return lax.dot_general(a, b, (((1,), (1,)), ((), ())), preferred_element_type=f32)


def _mod_body(c_ref, w_ref, b_ref, o_ref):
    c = c_ref[...]
    s = c * jax.nn.sigmoid(c)
    o_ref[...] = _dot(s.astype(bf16), w_ref[...].astype(bf16)) + b_ref[...]


def _modulation(c_all, w_mod, b_mod):
    depth, d, n = w_mod.shape
    nc = c_all.shape[0]
    tn = 1024
    return pl.pallas_call(
        _mod_body,
        grid=(depth, n // tn),
        in_specs=[
            pl.BlockSpec((nc, d), lambda l, j: (0, 0)),
            pl.BlockSpec((None, d, tn), lambda l, j: (l, 0, j)),
            pl.BlockSpec((None, 1, tn), lambda l, j: (l, 0, j)),
        ],
        out_specs=pl.BlockSpec((None, nc, tn), lambda l, j: (l, 0, j)),
        out_shape=jax.ShapeDtypeStruct((depth, nc, n), f32),
        compiler_params=_cparams(("parallel", "parallel")),
        name="modulation",
    )(c_all, w_mod, b_mod.reshape(depth, 1, n))


TM_FFN = 1024


def _ffn_body(x_ref, mod_ref, pre_ref, post_ref, wg_ref, wu_ref, wo_ref, o_ref, h_scr, r_scr):
    f = pl.program_id(1)

    @pl.when(f == 0)
    def _():
        _norm_modulate(x_ref, mod_ref, pre_ref, h_scr, r_scr)
        o_ref[...] = jnp.zeros_like(o_ref)

    h = h_scr[...]
    gt = _dot(h, wg_ref[...])
    up = _dot(h, wu_ref[...])
    a = gt * jax.nn.sigmoid(gt) * up
    o_ref[...] += _dot(a.astype(bf16), wo_ref[...])

    @pl.when(f == pl.num_programs(1) - 1)
    def _():
        _norm_gate_residual(o_ref, x_ref, mod_ref, post_ref, o_ref, r_scr, FFN_RES)


def _ffn(x, mod_l, pre, post, w_in5, w_out4, layer, slot, sub, cond):
    t, d = x.shape
    fp = w_out4.shape[2]
    w_half = lambda half: pl.BlockSpec((None, None, None, d, TF), lambda i, f: (layer, slot, half, 0, f))
    return pl.pallas_call(
        _ffn_body,
        grid=(t // TM_FFN, fp // TF),
        in_specs=[
            pl.BlockSpec((TM_FFN, d), lambda i, f: (i, 0)),
            pl.BlockSpec((None, 1, 3 * d), lambda i, f: (_cond_index(i, TM_FFN, cond), 0, sub)),
            pl.BlockSpec((1, d), lambda i, f: (0, 0)),
            pl.BlockSpec((1, d), lambda i, f: (0, 0)),
            w_half(0),
            w_half(1),
            pl.BlockSpec((None, None, TF, d), lambda i, f: (layer, slot, f, 0)),
        ],
        out_specs=pl.BlockSpec((TM_FFN, d), lambda i, f: (i, 0)),
        out_shape=jax.ShapeDtypeStruct((t, d), f32),
        scratch_shapes=[pltpu.VMEM((TM_FFN, d), bf16), pltpu.VMEM((TM_FFN, LANE), f32)],
        compiler_params=_cparams(("parallel", "arbitrary")),
        name="ffn",
    )(x, mod_l, pre, post, w_in5, w_in5, w_out4)


def _inproj_body(x_ref, mod_ref, pre_ref, w_ref, o_ref, h_scr, r_scr):
    d = D_MODEL

    @pl.when(pl.program_id(1) == 0)
    def _():
        _norm_modulate(x_ref, mod_ref, pre_ref, h_scr, r_scr)

    o_ref[...] = _dot(h_scr[...], w_ref[...]).astype(o_ref.dtype)


def _inproj(x, mod_l, pre, w_in_p, layer, cond, out_dtype):
    t, d = x.shape
    tm = TM * (4 // jnp.dtype(out_dtype).itemsize)
    n = w_in_p.shape[2]
    return pl.pallas_call(
        _inproj_body,
        grid=(t // tm, n // TN_IN),
        in_specs=[
            pl.BlockSpec((tm, d), lambda i, j: (i, 0)),
            pl.BlockSpec((None, 1, 3 * d), lambda i, j: (_cond_index(i, tm, cond), 0, 1)),
            pl.BlockSpec((1, d), lambda i, j: (0, 0)),
            pl.BlockSpec((None, d, TN_IN), lambda i, j: (layer, 0, j)),
        ],
        out_specs=pl.BlockSpec((tm, TN_IN), lambda i, j: (i, j)),
        out_shape=jax.ShapeDtypeStruct((t, n), out_dtype),
        scratch_shapes=[pltpu.VMEM((tm, d), bf16), pltpu.VMEM((tm, LANE), f32)],
        compiler_params=_cparams(("parallel", "arbitrary")),
        name="inproj",
    )(x, mod_l, pre, w_in_p)


POOL_PAD = 8


def _pool_body(u_ref, w_ref, sc_ref, o_ref, pad_scr, lvl_scr, *, seq):
    gc = POOL_GC
    pad = POOL_PAD
    n_lvl = seq + pad
    zeros = jnp.zeros((pad, POOL_WIDTH), f32)
    pad_scr[pl.ds(0, pad), :] = zeros
    pad_scr[pl.ds(pad + seq, pad), :] = zeros
    pad_scr[pl.ds(pad, seq), :] = u_ref[...].astype(f32)
    lvl_scr[:, pl.ds(n_lvl, pad), :] = jnp.zeros((2, pad, gc), f32)
    t = lax.broadcasted_iota(jnp.int32, (seq, 1), 0)
    for gi, win in enumerate(POOL_WINDOWS):
        cols = pl.ds(gi * gc, gc)
        read = lambda off, n: pad_scr[pl.ds(off, n), cols]
        k, slot = 1, 0
        while 2 * k < win:
            lvl_scr[slot, pl.ds(0, n_lvl), :] = read(0, n_lvl) + read(k, n_lvl)
            read = functools.partial(lambda s_, off, n: lvl_scr[s_, pl.ds(off, n), :], slot)
            k, slot = 2 * k, 1 - slot
        lo = jnp.maximum(t - win // 2, 0)
        hi = jnp.minimum(t + win - 1 - win // 2, seq - 1)
        inv_cnt = 1.0 / (hi - lo + 1).astype(f32)
        acc = read(pad - win // 2, seq) + read(pad, seq)
        pooled = acc * inv_cnt - pad_scr[pl.ds(pad, seq), cols]
        y = _dot(pooled.astype(bf16), w_ref[gi])
        o_ref[:, cols] = (y * sc_ref[:, cols]).astype(o_ref.dtype)


def _pool(p, pool_w, pool_scale, seq, row_block0, nseq):
    cb = OFF_POOL // POOL_WIDTH
    return pl.pallas_call(
        functools.partial(_pool_body, seq=seq),
        grid=(nseq,),
        in_specs=[
            pl.BlockSpec((seq, POOL_WIDTH), lambda s: (row_block0 + s, cb)),
            pl.BlockSpec((POOL_GROUPS, POOL_GC, POOL_GC), lambda s: (0, 0, 0)),
            pl.BlockSpec((1, POOL_WIDTH), lambda s: (0, 0)),
        ],
        out_specs=pl.BlockSpec((seq, POOL_WIDTH), lambda s: (s, 0)),
        out_shape=jax.ShapeDtypeStruct((nseq * seq, POOL_WIDTH), bf16),
        scratch_shapes=[pltpu.VMEM((seq + 2 * POOL_PAD, POOL_WIDTH), f32),
                        pltpu.VMEM((2, seq + 2 * POOL_PAD, POOL_GC), f32)],
        compiler_params=_cparams(("parallel",)),
        name="pool",
    )(p, pool_w, pool_scale)


def _lane_fold(x, op):
    parts = [x[:, i * LANE:(i + 1) * LANE] for i in range(x.shape[1] // LANE)]
    while len(parts) > 1:
        parts = [op(parts[i], parts[i + 1]) for i in range(0, len(parts) - 1, 2)] + parts[len(parts) & ~1:]
    return parts[0]


def _softmax_rows(s):
    m = jnp.max(_lane_fold(s, jnp.maximum), axis=-1, keepdims=True)
    e = jnp.exp(s - m)
    return e / jnp.sum(_lane_fold(e, jnp.add), axis=-1, keepdims=True)


def _ctx_attn_body(q_ref, k_ref, v_ref, *refs):
    o_ref, nk_ref, nv_ref = refs[-3:]
    hd = NA_HEAD_DIM
    for h in range(NA_HEADS):
        cols = pl.ds(h * hd, hd)
        kf = k_ref[:, cols]
        vf = v_ref[:, cols]
        nk_ref[h] = kf
        nv_ref[h] = vf
        p = _softmax_rows(_dot_nt(q_ref[:, cols].astype(bf16), kf.astype(bf16)) * (hd ** -0.5))
        o_ref[:, cols] = _dot(p.astype(bf16), vf.astype(bf16)).astype(o_ref.dtype)


def _ctx_attn(p, nseq, layer, depth, caches=None):
    spec = lambda off: pl.BlockSpec((SEQ, NA_WIDTH), lambda b: (b, off // NA_WIDTH))
    cache_spec = pl.BlockSpec((None, None, NA_HEADS, SEQ, NA_HEAD_DIM), lambda b: (b, layer, 0, 0, 0))
    cache_shape = jax.ShapeDtypeStruct((nseq, depth, NA_HEADS, SEQ, NA_HEAD_DIM), f32)
    in_specs = [spec(OFF_NQ), spec(OFF_NK), spec(OFF_NV)]
    args = [p, p, p]
    aliases = {}
    if caches is not None:
        in_specs += [pl.BlockSpec(memory_space=pl.ANY)] * 2
        args += list(caches)
        aliases = {3: 1, 4: 2}
    return pl.pallas_call(
        _ctx_attn_body,
        grid=(nseq,),
        in_specs=in_specs,
        out_specs=[pl.BlockSpec((SEQ, NA_WIDTH), lambda b: (b, 0)), cache_spec, cache_spec],
        out_shape=[jax.ShapeDtypeStruct((nseq * SEQ, NA_WIDTH), bf16), cache_shape, cache_shape],
        input_output_aliases=aliases,
        compiler_params=_cparams(("parallel",)),
        name="ctx_attn",
    )(*args)


def _na_bias_table(rpb):
    qc = np.arange(GRID_W)[:, None]
    kc = np.arange(GRID_W)[None, :]
    cs = np.clip(qc - NA_WIN_W // 2, 0, GRID_W - NA_WIN_W)
    ok = (kc >= cs) & (kc < cs + NA_WIN_W)
    cidx = np.clip(kc - qc + NA_WIN_W - 1, 0, 2 * NA_WIN_W - 2)
    onehot = jnp.asarray((cidx[None] == np.arange(2 * NA_WIN_W - 1)[:, None, None]) & ok[None], f32)
    toep = jnp.einsum('...rc,cqk->...rqk', rpb.astype(f32), onehot, precision=lax.Precision.HIGHEST)
    toep = jnp.where(ok, toep * LOG2E, NEG_INF)
    return jnp.concatenate([toep[..., :-1, :, :], toep[..., 1:, :, :]], axis=-1)


LOG2E = float(np.log2(np.e))
NA_UNROLL = 8
NA_CTX_ROWS = 256


def _na_body(qb_scr, kb_scr, vb_scr, ck_ref, cv_ref, bias_ref, o_ref,
             sl_scr, sc_scr, el_scr, ec_scr, den_scr, oc_scr, *, rows):
    hd = NA_HEAD_DIM
    scale = hd ** -0.5 * LOG2E
    kh = min(NA_WIN_H, rows)
    nloc = kh * GRID_W
    n = rows * GRID_W
    assert qb_scr.dtype == bf16
    ck = ck_ref[...].astype(bf16)
    cv = cv_ref[...].astype(bf16)

    def row_slices(r):
        rs = jnp.clip(r - kh // 2, 0, rows - kh)
        q_rows = pl.ds(pl.multiple_of(r * GRID_W, GRID_W), GRID_W)
        k_rows = pl.ds(pl.multiple_of(rs * GRID_W, GRID_W), nloc)
        return rs, q_rows, k_rows

    def ctx_scores(i, carry):
        blk = pl.ds(pl.multiple_of(i * NA_CTX_ROWS, NA_CTX_ROWS), NA_CTX_ROWS)
        sc_scr[blk, :] = _dot_nt(qb_scr[blk, :], ck) * scale
        return carry

    lax.fori_loop(0, n // NA_CTX_ROWS, ctx_scores, 0, unroll=2)

    def loc_scores(r, carry):
        rs, q_rows, k_rows = row_slices(r)
        first = rs - r + NA_WIN_H - 1
        bias = jnp.concatenate([bias_ref[first + 2 * e] for e in range(kh // 2)], axis=1)
        sl_scr[q_rows, :] = _dot_nt(qb_scr[q_rows, :], kb_scr[k_rows, :]) * scale + bias
        return carry

    lax.fori_loop(0, rows, loc_scores, 0, unroll=NA_UNROLL)

    def numerators(r, carry):
        q_rows = pl.ds(pl.multiple_of(r * GRID_W, GRID_W), GRID_W)
        s_loc = sl_scr[q_rows, :]
        s_ctx = sc_scr[q_rows, :]
        m = jnp.max(jnp.maximum(_lane_fold(s_loc, jnp.maximum), _lane_fold(s_ctx, jnp.maximum)),
                    axis=-1, keepdims=True)
        e_loc = jnp.exp2(s_loc - m)
        e_ctx = jnp.exp2(s_ctx - m)
        den = jnp.sum(_lane_fold(e_loc, jnp.add) + _lane_fold(e_ctx, jnp.add), axis=-1, keepdims=True)
        el_scr[q_rows, :] = e_loc.astype(bf16)
        ec_scr[q_rows, :] = e_ctx.astype(bf16)
        den_scr[q_rows, :] = jnp.broadcast_to(den, (GRID_W, hd))
        return carry

    lax.fori_loop(0, rows, numerators, 0, unroll=NA_UNROLL)

    def ctx_values(i, carry):
        blk = pl.ds(pl.multiple_of(i * NA_CTX_ROWS, NA_CTX_ROWS), NA_CTX_ROWS)
        oc_scr[blk, :] = _dot(ec_scr[blk, :], cv)
        return carry

    lax.fori_loop(0, n // NA_CTX_ROWS, ctx_values, 0, unroll=2)

    def loc_values(r, carry):
        _, q_rows, k_rows = row_slices(r)
        o = _dot(el_scr[q_rows, :], vb_scr[k_rows, :]) + oc_scr[q_rows, :]
        o_ref[q_rows, :] = (o / den_scr[q_rows, :]).astype(o_ref.dtype)
        return carry

    lax.fori_loop(0, rows, loc_values, 0, unroll=NA_UNROLL)


def _na_latent(p, cache_k, cache_v, bias_tbl, layer, row_block0, nreq):
    n = DEC_SEQ
    hd = NA_HEAD_DIM
    past = cache_k.shape[3]
    rows = n // GRID_W
    qkv = lambda off: pl.BlockSpec((n, hd), lambda b, h: (row_block0 + b, off // hd + h))
    cache = pl.BlockSpec((None, None, None, past, hd), lambda b, h: (b, layer, h, 0, 0))
    return pl.pallas_call(
        functools.partial(_na_body, rows=rows),
        grid=(nreq, NA_HEADS),
        in_specs=[qkv(OFF_NQ), qkv(OFF_NK), qkv(OFF_NV), cache, cache,
                  pl.BlockSpec((None, None, 2 * NA_WIN_H - 2, GRID_W, 2 * GRID_W),
                               lambda b, h: (layer, h, 0, 0, 0))],
        out_specs=pl.BlockSpec((n, hd), lambda b, h: (b, h)),
        out_shape=jax.ShapeDtypeStruct((nreq * n, NA_WIDTH), bf16),
        scratch_shapes=[pltpu.VMEM((n, NA_WIN_H * GRID_W), f32), pltpu.VMEM((n, past), f32),
                        pltpu.VMEM((n, NA_WIN_H * GRID_W), bf16), pltpu.VMEM((n, past), bf16),
                        pltpu.VMEM((n, hd), f32), pltpu.VMEM((n, hd), f32)],
        compiler_params=_cparams(("parallel", "parallel")),
        name="na_latent",
    )(p, p, p, cache_k, cache_v, bias_tbl)


GLA_PAD = 32
GLA_UNROLL = 4


def _rope_tables(seq):
    t = np.arange(seq)
    half = GLA_DK // 2
    nf = half // 2
    inv = ROPE_BASE ** (-np.arange(nf, dtype=np.float64) / nf)
    cos, sin = [], []
    for pos in (t // GRID_W, t % GRID_W):
        ang = pos[:, None].astype(np.float64) * inv
        cos += [np.cos(ang), np.cos(ang)]
        sin += [-np.sin(ang), np.sin(ang)]
    return (jnp.asarray(np.concatenate(cos, axis=-1), f32), jnp.asarray(np.concatenate(sin, axis=-1), f32))


def _rope(x, cos, sin_signed):
    nf = GLA_DK // 4
    lane = lax.broadcasted_iota(jnp.int32, x.shape, 1)
    partner = jnp.where(lane % (2 * nf) < nf, pltpu.roll(x, GLA_DK - nf, 1), pltpu.roll(x, nf, 1))
    return x * cos + partner * sin_signed


def _log_sigmoid(x):
    return jnp.minimum(x, 0.0) - jnp.log1p(jnp.exp(-jnp.abs(x)))


def _gla_body(*refs, seq, rope, with_s0, with_sfin, n_carried):
    refs = list(refs)
    q_ref, k_ref, v_ref, r_ref, z_ref, wg_ref, bg_ref, ng_ref = refs[:8]
    refs = refs[8:]
    if rope:
        cos_ref, sin_ref = refs[:2]
        refs = refs[2:]
    if with_s0:
        s0_ref = refs[0]
        refs = refs[1:]
    refs = refs[n_carried:]
    o_ref = refs[0]
    refs = refs[1:]
    if with_sfin:
        sfin_ref = refs[0]
        refs = refs[1:]
    qi_scr, kn_scr, kd_scr, dec_scr, scan_scr, vb_scr, u_scr, sb_scr, o_scr, st_scr = refs

    ch = GLA_CHUNK
    nch = seq // ch
    dk, dv = GLA_DK, GLA_DV

    q = q_ref[...].astype(f32)
    k = k_ref[...].astype(f32)
    if rope:
        q = _rope(q, cos_ref[...], sin_ref[...])
        k = _rope(k, cos_ref[...], sin_ref[...])
    q = q * (dk ** -0.5)

    zb = z_ref[...].astype(bf16)
    pos = lax.broadcasted_iota(jnp.int32, (seq, 1), 0) % ch
    zpad = jnp.zeros((GLA_PAD, dk), f32)
    scan_scr[pl.ds(0, GLA_PAD), :] = zpad
    scan_scr[pl.ds(GLA_PAD + seq, GLA_PAD), :] = zpad
    for d in range(2):
        g = _log_sigmoid(_dot(zb, wg_ref[d]) + bg_ref[d]) / GLA_TAU
        b = g
        sh = 1
        while sh < ch:
            scan_scr[pl.ds(GLA_PAD, seq), :] = b
            if d == 0:
                b = b + jnp.where(pos >= sh, scan_scr[pl.ds(GLA_PAD - sh, seq), :], 0.0)
            else:
                b = b + jnp.where(pos < ch - sh, scan_scr[pl.ds(GLA_PAD + sh, seq), :], 0.0)
            sh *= 2
        b3 = b.reshape(nch, ch, dk)
        b_end = b3[:, ch - 1:ch, :] if d == 0 else b3[:, 0:1, :]
        lanes = pl.ds(d * dk, dk)
        qi_scr[:, lanes] = (q * jnp.exp(b)).astype(bf16)
        kn_scr[d] = (k * jnp.exp(-b)).astype(bf16)
        kd_scr[:, lanes] = (k.reshape(nch, ch, dk) * jnp.exp(b_end - b3)).reshape(seq, dk).astype(bf16)
        dec_scr[d] = jnp.exp(b_end)

    for d in range(2):
        if with_s0:
            st_scr[d] = s0_ref[d].T
        else:
            st_scr[d] = jnp.zeros((dv, dk), f32)

    ri = lax.broadcasted_iota(jnp.int32, (ch, ch), 0)
    ci = lax.broadcasted_iota(jnp.int32, (ch, ch), 1)

    vb_scr[...] = v_ref[...].astype(bf16)
    chunk_rows = lambda c: pl.ds(pl.multiple_of(c * ch, ch), ch)
    fwd, bwd = pl.ds(0, dk), pl.ds(dk, dk)

    def increments(c, carry):
        rows = chunk_rows(c)
        u_scr[c] = lax.dot_general(vb_scr[rows, :], kd_scr[rows, :], (((0,), (0,)), ((), ())),
                                   preferred_element_type=f32)
        return carry

    lax.fori_loop(0, nch, increments, 0, unroll=GLA_UNROLL)

    def states(i, carry):
        for d, c, lanes in ((0, i, fwd), (1, nch - 1 - i, bwd)):
            st = st_scr[d]
            sb_scr[c, :, lanes] = st.astype(bf16)
            st_scr[d] = st * dec_scr[d, c] + u_scr[c, :, lanes]
        return carry

    lax.fori_loop(0, nch, states, 0)

    if with_sfin:
        for d in range(2):
            sfin_ref[d] = st_scr[d].T

    def outputs(c, carry):
        rows = chunk_rows(c)
        qi = qi_scr[rows, :]
        pf = _dot_nt(qi[:, 0:dk], kn_scr[0, rows, :])
        pb = _dot_nt(qi[:, dk:2 * dk], kn_scr[1, rows, :])
        a = jnp.where(ci < ri, pf, jnp.where(ci > ri, pb, pf + pb))
        o_scr[rows, :] = _dot(a.astype(bf16), vb_scr[rows, :]) + _dot_nt(qi, sb_scr[c])
        return carry

    lax.fori_loop(0, nch, outputs, 0, unroll=GLA_UNROLL)

    o = o_scr[...]
    r = r_ref[...].astype(f32)
    o = o * lax.rsqrt(jnp.mean(o * o, axis=-1, keepdims=True) + EPS) * ng_ref[...]
    o_ref[...] = (o * (r * jax.nn.sigmoid(r))).astype(o_ref.dtype)


def _gla(p, wgate_p, b_gate, gla_norm, seq, row_block0, nreq, rope_tabs=None, state=None, layer=0,
         with_sfin=False, depth=1, carried=None):
    dk, dv = GLA_DK, GLA_DV
    nch = seq // GLA_CHUNK
    rope = rope_tabs is not None
    with_s0 = state is not None
    blk = lambda w, off: pl.BlockSpec((seq, w), lambda b, h: (row_block0 + b, off // w + h))
    in_specs = [blk(dk, OFF_GQ), blk(dk, OFF_GK), blk(dv, OFF_GV), blk(dv, OFF_GR),
                pl.BlockSpec((seq, LANE), lambda b, h: (row_block0 + b, OFF_GZ // LANE)),
                pl.BlockSpec((2, LANE, dk), lambda b, h: (0, 0, h)),
                pl.BlockSpec((2, 1, dk), lambda b, h: (0, 0, h)),
                pl.BlockSpec((1, dv), lambda b, h: (0, h))]
    args = [p, p, p, p, p, wgate_p, b_gate.reshape(2, 1, GLA_KW), gla_norm.reshape(1, GLA_VW)]
    if rope:
        in_specs += [pl.BlockSpec((seq, dk), lambda b, h: (0, 0))] * 2
        args += list(rope_tabs)
    if with_s0:
        in_specs.append(pl.BlockSpec((None, None, 2, None, dk, dv), lambda b, h: (b, layer, 0, h, 0, 0)))
        args.append(state)
    aliases = {}
    if carried is not None:
        aliases = {len(args): 1}
        in_specs.append(pl.BlockSpec(memory_space=pl.ANY))
        args.append(carried)
    out_specs = [pl.BlockSpec((seq, dv), lambda b, h: (b, h))]
    out_shape = [jax.ShapeDtypeStruct((nreq * seq, GLA_VW), bf16)]
    if with_sfin:
        out_specs.append(pl.BlockSpec((None, None, 2, None, dk, dv), lambda b, h: (b, layer, 0, h, 0, 0)))
        out_shape.append(jax.ShapeDtypeStruct((nreq, depth, 2, GLA_HEADS, dk, dv), f32))
    scratch = [pltpu.VMEM((seq, 2 * dk), bf16), pltpu.VMEM((2, seq, dk), bf16), pltpu.VMEM((seq, 2 * dk), bf16),
               pltpu.VMEM((2, nch, 1, dk), f32), pltpu.VMEM((seq + 2 * GLA_PAD, dk), f32),
               pltpu.VMEM((seq, dv), bf16), pltpu.VMEM((nch, dv, 2 * dk), f32), pltpu.VMEM((nch, dv, 2 * dk), bf16),
               pltpu.VMEM((seq, dv), f32), pltpu.VMEM((2, dv, dk), f32)]
    return pl.pallas_call(
        functools.partial(_gla_body, seq=seq, rope=rope, with_s0=with_s0, with_sfin=with_sfin,
                          n_carried=len(aliases)),
        grid=(nreq, GLA_HEADS),
        in_specs=in_specs,
        out_specs=out_specs,
        out_shape=out_shape,
        input_output_aliases=aliases,
        scratch_shapes=scratch,
        compiler_params=_cparams(("parallel", "parallel")),
        name="gla",
    )(*args)


TM_MERGE = 512


MERGE_SUB = 2


def _merge_body(x_ref, mod_ref, post_ref, bp_ref, bn_ref, bg_ref, gl_ref, w_ref, o_ref, m_scr, mb_scr, r_scr):
    n = pl.program_id(1)
    m_ref = m_scr.at[pl.program_id(2)]
    mb_ref = mb_scr.at[pl.program_id(2)]

    for bi, br_ref in enumerate((bp_ref, bn_ref, bg_ref)):
        @pl.when(n == bi)
        def _():
            y = jax.nn.sigmoid(gl_ref[...].astype(f32)) * _dot(br_ref[...], w_ref[...])
            m_ref[...] = y if bi == 0 else m_ref[...] + y

    @pl.when(n == N_BRANCH)
    def _():
        mb_ref[...] = m_ref[...].astype(bf16)
        m_ref[...] = _dot(mb_ref[:, 0:BRANCH_W], w_ref[...])

    @pl.when(n == N_BRANCH + 1)
    def _():
        m_ref[...] += _dot(mb_ref[:, BRANCH_W:2 * BRANCH_W], w_ref[...])
        _norm_gate_residual(m_ref, x_ref, mod_ref, post_ref, o_ref, r_scr, 1.0)


def _merge(x, mod_l, post, y_pool, y_na, y_gla, p, w_stack, cond):
    t, d = x.shape
    tm = TM_MERGE
    sub = MERGE_SUB
    nsteps = N_BRANCH + d // BRANCH_W
    tile = lambda i, s: i * sub + s

    def rows_at(first, last):
        def index(i, n, s):
            return jnp.where(n < first, jnp.maximum(tile(i, 0) - 1, 0),
                             jnp.where(n > last, tile(i, sub - 1), tile(i, s)))
        return index

    last = nsteps - 1
    br = lambda step: pl.BlockSpec((tm, BRANCH_W), lambda i, n, s: (rows_at(step, step)(i, n, s), 0))
    xo = pl.BlockSpec((tm, d), lambda i, n, s: (rows_at(last, last)(i, n, s), 0))
    return pl.pallas_call(
        _merge_body,
        grid=(t // (tm * sub), nsteps, sub),
        in_specs=[
            xo,
            pl.BlockSpec((None, 1, 3 * d), lambda i, n, s: (_cond_index(tile(i, s), tm, cond), 0, 1)),
            pl.BlockSpec((1, d), lambda i, n, s: (0, 0)),
            br(0), br(1), br(2),
            pl.BlockSpec((tm, d),
                         lambda i, n, s: (rows_at(0, N_BRANCH - 1)(i, n, s), jnp.minimum(n, N_BRANCH - 1))),
            pl.BlockSpec((None, BRANCH_W, d), lambda i, n, s: (n, 0, 0)),
        ],
        out_specs=xo,
        out_shape=jax.ShapeDtypeStruct((t, d), f32),
        scratch_shapes=[pltpu.VMEM((sub, tm, d), f32), pltpu.VMEM((sub, tm, d), bf16), pltpu.VMEM((tm, LANE), f32)],
        compiler_params=_cparams(("arbitrary", "arbitrary", "arbitrary")),
        name="merge",
    )(x, mod_l, post, y_pool, y_na, y_gla, p, w_stack)


_IN_SPLITS = (POOL_WIDTH, NA_WIDTH, NA_WIDTH, NA_WIDTH, GLA_KW, GLA_KW, GLA_VW, 2 * GLA_RANK, GLA_VW, GATE_W)
_IN_OFFS = tuple(int(v) for v in np.cumsum((0,) + _IN_SPLITS))
_IN_RUNS = ((_IN_OFFS[9], _IN_OFFS[10]), (_IN_OFFS[0], _IN_OFFS[7]), (_IN_OFFS[8], _IN_OFFS[9]),
            (_IN_OFFS[7], _IN_OFFS[8]))


CT_IN = 512
_IN_TILE_STARTS = []
for _a, _b in _IN_RUNS:
    _IN_TILE_STARTS += [_a + CT_IN * _t for _t in range(-(-(_b - _a) // CT_IN))]
assert len(_IN_TILE_STARTS) * CT_IN == IN_COLS_P and all(v % 8 == 0 for v in _IN_TILE_STARTS)
_IN_LAST_VALID = (_IN_RUNS[-1][1] - _IN_RUNS[-1][0]) % CT_IN or CT_IN


assert all((b - a) % CT_IN == 0 for a, b in _IN_RUNS[:-1])


def _w_in_tile_start(j):
    out = jnp.int32(_IN_TILE_STARTS[0]) + CT_IN * j
    for t in range(1, len(_IN_TILE_STARTS)):
        if _IN_TILE_STARTS[t] != _IN_TILE_STARTS[t - 1] + CT_IN:
            out = jnp.where(j >= t, _IN_TILE_STARTS[t] + CT_IN * (j - t), out)
    return out


def _cast_w_in_body(w_ref, o_ref):
    last = pl.program_id(1) == pl.num_programs(1) - 1
    col = lax.broadcasted_iota(jnp.int32, (1, CT_IN), 1)
    valid = jnp.where(last, _IN_LAST_VALID, CT_IN)
    o_ref[...] = jnp.where(col < valid, w_ref[...].T, 0.0).astype(bf16)


def _cast_w_in(w):
    depth, d, n = w.shape
    wt = jnp.swapaxes(w, 1, 2).reshape(depth * n, d)
    return pl.pallas_call(
        _cast_w_in_body,
        grid=(depth, IN_COLS_P // CT_IN),
        in_specs=[pl.BlockSpec((pl.Element(CT_IN), pl.Element(d)), lambda l, j: (pl.multiple_of(l * n + _w_in_tile_start(j), 8), 0))],
        out_specs=pl.BlockSpec((None, d, CT_IN), lambda l, j: (l, 0, j)),
        out_shape=jax.ShapeDtypeStruct((depth, d, IN_COLS_P), bf16),
        compiler_params=_cparams(("parallel", "parallel")),
        name="cast_w_in",
    )(wt)


def _cast_ffn_in_body(w_ref, o_ref):
    pad = jnp.zeros((w_ref.shape[0], D_FF_P - D_FF), bf16)
    for half in range(2):
        o_ref[half, :, 0:D_FF] = w_ref[:, half * D_FF:(half + 1) * D_FF].astype(bf16)
        o_ref[half, :, D_FF:D_FF_P] = pad


def _cast_ffn_in(w):
    depth, ns, d, _ = w.shape
    rows = 256
    return pl.pallas_call(
        _cast_ffn_in_body,
        grid=(depth, ns, d // rows),
        in_specs=[pl.BlockSpec((None, None, rows, 2 * D_FF), lambda l, s, r: (l, s, r, 0))],
        out_specs=pl.BlockSpec((None, None, 2, rows, D_FF_P), lambda l, s, r: (l, s, 0, r, 0)),
        out_shape=jax.ShapeDtypeStruct((depth, ns, 2, d, D_FF_P), bf16),
        compiler_params=_cparams(("parallel", "parallel", "parallel")),
        name="cast_ffn_in",
    )(w)


def _cast_ffn_out_body(w_ref, o_ref):
    row = pl.program_id(2) * TF + lax.broadcasted_iota(jnp.int32, (TF, 1), 0)
    o_ref[...] = jnp.where(row < D_FF, w_ref[...], 0.0).astype(bf16)


def _cast_ffn_out(w):
    depth, ns, _, d = w.shape
    return pl.pallas_call(
        _cast_ffn_out_body,
        grid=(depth, ns, D_FF_P // TF),
        in_specs=[pl.BlockSpec((None, None, TF, d), lambda l, s, j: (l, s, j, 0))],
        out_specs=pl.BlockSpec((None, None, TF, d), lambda l, s, j: (l, s, j, 0)),
        out_shape=jax.ShapeDtypeStruct((depth, ns, D_FF_P, d), bf16),
        compiler_params=_cparams(("parallel", "parallel", "parallel")),
        name="cast_ffn_out",
    )(w)


def _cast_w_stack_body(wb_ref, wo_ref, o_ref):
    is_branch = pl.program_id(1) < N_BRANCH

    @pl.when(is_branch)
    def _():
        o_ref[...] = wb_ref[...].astype(bf16)

    @pl.when(jnp.logical_not(is_branch))
    def _():
        o_ref[...] = wo_ref[...].astype(bf16)


def _cast_w_stack(w_branch, w_out):
    depth, nb, bw, d = w_branch.shape
    kparts = d // bw
    wo = w_out.reshape(depth, kparts, bw, d)
    blk = lambda index: pl.BlockSpec((None, None, bw, d), index)
    return pl.pallas_call(
        _cast_w_stack_body,
        grid=(depth, nb + kparts),
        in_specs=[blk(lambda l, j: (l, jnp.minimum(j, nb - 1), 0, 0)),
                  blk(lambda l, j: (l, jnp.maximum(j - nb, 0), 0, 0))],
        out_specs=blk(lambda l, j: (l, j, 0, 0)),
        out_shape=jax.ShapeDtypeStruct((depth, nb + kparts, bw, d), bf16),
        compiler_params=_cparams(("parallel", "arbitrary")),
        name="cast_w_stack",
    )(w_branch, wo)


def _prep_gate(w_gate):
    out = jnp.zeros((2, LANE, GLA_KW), f32)
    for d in range(2):
        out = out.at[d, d * GLA_RANK:(d + 1) * GLA_RANK].set(w_gate[d])
    return out.astype(bf16)


def kernel(x_prompt, x_sample, c, cache_na_k, cache_na_v, state_gla, c_ctx, w_mod, b_mod, norm_pre, norm_post,
           w_ffn_in, w_ffn_out, w_in, pool_w, pool_scale, na_rpb, gla_w_gate, gla_b_gate, gla_norm, w_branch,
           w_out):
    nb, seq, d = x_prompt.shape
    ndec, dseq, _ = x_sample.shape
    depth = w_mod.shape[0]
    n_ctx = nb * seq
    n_lat = ndec * dseq
    assert (seq, dseq, d) == (SEQ, DEC_SEQ, D_MODEL) and n_ctx % (TM_MERGE * MERGE_SUB) == 0 and n_ctx % TM_FFN == 0 and dseq % TM_FFN == 0

    xs = [x_prompt.reshape(n_ctx, d), x_sample.reshape(n_lat, d)]
    conds = [(0, n_ctx), (1, dseq)]
    ncond = -(-(1 + ndec) // 8) * 8
    c_all = jnp.concatenate([c_ctx[None], c, jnp.zeros((ncond - 1 - ndec, d), f32)], axis=0)
    mod = _modulation(c_all, w_mod, b_mod)
    rope_tabs = _rope_tables(dseq)
    bias_tbl = _na_bias_table(na_rpb)

    w_ffn_in5 = _cast_ffn_in(w_ffn_in)
    w_ffn_out4 = _cast_ffn_out(w_ffn_out)
    w_in_p = _cast_w_in(w_in)
    w_stack_all = _cast_w_stack(w_branch, w_out)

    caches = None
    new_s = None
    for l in range(depth):
        mod_l = mod[l].reshape(ncond, 1, N_MOD * d)
        pre = norm_pre[l].reshape(3, 1, d)
        post = norm_post[l].reshape(3, 1, d)
        w_stack = w_stack_all[l]
        wgate_p = _prep_gate(gla_w_gate[l])
        pw = pool_w[l].astype(bf16)
        psc = pool_scale[l].reshape(1, POOL_WIDTH)

        xs = [_ffn(x, mod_l, pre[0], post[0], w_ffn_in5, w_ffn_out4, l, 0, 0, cond) for x, cond in zip(xs, conds)]
        p_ctx, p_lat = [_inproj(x, mod_l, pre[1], w_in_p, l, cond, dt) for x, cond, dt in zip(xs, conds, (f32, bf16))]

        y_pool = [_pool(p_ctx, pw, psc, seq, 0, nb), _pool(p_lat, pw, psc, dseq, 0, ndec)]
        na_ctx, new_k, new_v = _ctx_attn(p_ctx, nb, l, depth, caches)
        caches = (new_k, new_v)
        y_na = [na_ctx, _na_latent(p_lat, cache_na_k, cache_na_v, bias_tbl, l, 0, ndec)]
        g_ctx, new_s = _gla(p_ctx, wgate_p, gla_b_gate[l], gla_norm[l], seq, 0, nb, layer=l, with_sfin=True,
                            depth=depth, carried=new_s)
        (g_lat,) = _gla(p_lat, wgate_p, gla_b_gate[l], gla_norm[l], dseq, 0, ndec, rope_tabs=rope_tabs,
                        state=state_gla, layer=l)
        y_gla = [g_ctx, g_lat]

        xs = [_merge(x, mod_l, post[1], yp, yn, yg, p, w_stack, cond)
              for x, yp, yn, yg, p, cond in zip(xs, y_pool, y_na, y_gla, (p_ctx, p_lat), conds)]
        xs = [_ffn(x, mod_l, pre[2], post[2], w_ffn_in5, w_ffn_out4, l, 1, 2, cond) for x, cond in zip(xs, conds)]

    return (xs[0].reshape(nb, seq, d), xs[1].reshape(ndec, dseq, d), caches[0], caches[1], new_s)
```

```python
import functools
from typing import Any, Callable, NamedTuple

import numpy as np
import jax
import jax.numpy as jnp
from jax import lax
from jax.experimental import pallas as pl
from jax.experimental.pallas import tpu as pltpu

f32 = jnp.float32
bf16 = jnp.bfloat16

D_MODEL = 2048
SEQ = 256
DEC_SEQ = 2048
GRID_W = 64
N_MOD = 9
D_FF = 5504
FFN_RES = 0.5
EPS = 1e-6
NEG_INF = -1e30

POOL_GROUPS = 4
POOL_WINDOWS = (2, 4, 8, 16)
POOL_WIDTH = 1024
POOL_GC = POOL_WIDTH // POOL_GROUPS

NA_HEADS = 8
NA_HEAD_DIM = 128
NA_WIDTH = NA_HEADS * NA_HEAD_DIM
NA_WIN_H = 8
NA_WIN_W = 16

GLA_HEADS = 4
GLA_DK = 128
GLA_DV = 256
GLA_KW = GLA_HEADS * GLA_DK
GLA_VW = GLA_HEADS * GLA_DV
GLA_RANK = 16
GLA_TAU = 16.0
GLA_CHUNK = 64
ROPE_BASE = 10000.0

BRANCH_W = 1024
N_BRANCH = 3
GATE_W = N_BRANCH * D_MODEL

LANE = 128
VMEM_LIMIT = 56 * 1024 * 1024

OFF_GL = 0
OFF_POOL = OFF_GL + GATE_W
OFF_NQ = OFF_POOL + POOL_WIDTH
OFF_NK = OFF_NQ + NA_WIDTH
OFF_NV = OFF_NK + NA_WIDTH
OFF_GQ = OFF_NV + NA_WIDTH
OFF_GK = OFF_GQ + GLA_KW
OFF_GV = OFF_GK + GLA_KW
OFF_GR = OFF_GV + GLA_VW
OFF_GZ = OFF_GR + GLA_VW
TN_IN = 2304
IN_COLS_P = -(-(OFF_GZ + LANE) // TN_IN) * TN_IN

TM = 512
TF = 512
D_FF_P = -(-D_FF // TF) * TF


def _cparams(sem):
    return pltpu.CompilerParams(dimension_semantics=sem, vmem_limit_bytes=VMEM_LIMIT)


def _cond_index(i, tm, cond):
    return cond[0] + (i * tm) // cond[1]


ROW_CHUNK = 16


def _row_sweep(nrows, fn, unroll=4):
    def trip(i, carry):
        fn(pl.ds(pl.multiple_of(i * ROW_CHUNK, ROW_CHUNK), ROW_CHUNK))
        return carry

    lax.fori_loop(0, nrows // ROW_CHUNK, trip, 0, unroll=unroll)


def _row_rsqrt(x_ref, r_scr):
    n = x_ref.shape[1]

    def fn(rows):
        x = x_ref[rows, :]
        ss = jnp.sum(_lane_fold(x * x, jnp.add), axis=-1, keepdims=True)
        r_scr[rows, :] = jnp.broadcast_to(lax.rsqrt(ss * (1.0 / n) + EPS), (ROW_CHUNK, LANE))

    _row_sweep(x_ref.shape[0], fn, unroll=16)


def _lanes(r, n):
    return jnp.concatenate([r] * (n // LANE), axis=1)


def _norm_modulate(x_ref, mod_ref, g_ref, h_ref, r_scr):
    d = D_MODEL
    shift = mod_ref[:, 0:d]
    w = g_ref[...] * (1.0 + mod_ref[:, d:2 * d])
    _row_rsqrt(x_ref, r_scr)

    def fn(rows):
        h_ref[rows, :] = (x_ref[rows, :] * _lanes(r_scr[rows, :], d) * w + shift).astype(h_ref.dtype)

    _row_sweep(x_ref.shape[0], fn)


def _norm_gate_residual(y_ref, x_ref, mod_ref, g_ref, o_ref, r_scr, res_weight):
    d = D_MODEL
    w = (res_weight * mod_ref[:, 2 * d:3 * d]) * g_ref[...]
    _row_rsqrt(y_ref, r_scr)

    def fn(rows):
        o_ref[rows, :] = x_ref[rows, :] + y_ref[rows, :] * _lanes(r_scr[rows, :], d) * w

    _row_sweep(x_ref.shape[0], fn)


def _dot(a, b):
    return jnp.dot(a, b, preferred_element_type=f32)


def _dot_nt(a, b):
    return lax.dot_general(a, b, (((1,), (1,)), ((), ())), preferred_element_type=f32)


def _mod_body(c_ref, w_ref, b_ref, o_ref):
    c = c_ref[...]
    s = c * jax.nn.sigmoid(c)
    o_ref[...] = _dot(s.astype(bf16), w_ref[...].astype(bf16)) + b_ref[...]


def _modulation(c_all, w_mod, b_mod):
    depth, d, n = w_mod.shape
    nc = c_all.shape[0]
    tn = 1024
    return pl.pallas_call(
        _mod_body,
        grid=(depth, n // tn),
        in_specs=[
            pl.BlockSpec((nc, d), lambda l, j: (0, 0)),
            pl.BlockSpec((None, d, tn), lambda l, j: (l, 0, j)),
            pl.BlockSpec((None, 1, tn), lambda l, j: (l, 0, j)),
        ],
        out_specs=pl.BlockSpec((None, nc, tn), lambda l, j: (l, 0, j)),
        out_shape=jax.ShapeDtypeStruct((depth, nc, n), f32),
        compiler_params=_cparams(("parallel", "parallel")),
        name="modulation",
    )(c_all, w_mod, b_mod.reshape(depth, 1, n))


TM_FFN = 1024


def _ffn_body(x_ref, mod_ref, pre_ref, post_ref, wg_ref, wu_ref, wo_ref, o_ref, h_scr, r_scr):
    f = pl.program_id(1)

    @pl.when(f == 0)
    def _():
        _norm_modulate(x_ref, mod_ref, pre_ref, h_scr, r_scr)
        o_ref[...] = jnp.zeros_like(o_ref)

    h = h_scr[...]
    gt = _dot(h, wg_ref[...])
    up = _dot(h, wu_ref[...])
    a = gt * jax.nn.sigmoid(gt) * up
    o_ref[...] += _dot(a.astype(bf16), wo_ref[...])

    @pl.when(f == pl.num_programs(1) - 1)
    def _():
        _norm_gate_residual(o_ref, x_ref, mod_ref, post_ref, o_ref, r_scr, FFN_RES)


def _ffn(x, mod_l, pre, post, w_in3, w_out2, sub, cond):
    t, d = x.shape
    fp = w_out2.shape[0]
    w_half = lambda half: pl.BlockSpec((None, d, TF), lambda i, f: (half, 0, f))
    return pl.pallas_call(
        _ffn_body,
        grid=(t // TM_FFN, fp // TF),
        in_specs=[
            pl.BlockSpec((TM_FFN, d), lambda i, f: (i, 0)),
            pl.BlockSpec((None, 1, 3 * d), lambda i, f: (_cond_index(i, TM_FFN, cond), 0, sub)),
            pl.BlockSpec((1, d), lambda i, f: (0, 0)),
            pl.BlockSpec((1, d), lambda i, f: (0, 0)),
            w_half(0),
            w_half(1),
            pl.BlockSpec((TF, d), lambda i, f: (f, 0)),
        ],
        out_specs=pl.BlockSpec((TM_FFN, d), lambda i, f: (i, 0)),
        out_shape=jax.ShapeDtypeStruct((t, d), f32),
        scratch_shapes=[pltpu.VMEM((TM_FFN, d), bf16), pltpu.VMEM((TM_FFN, LANE), f32)],
        compiler_params=_cparams(("parallel", "arbitrary")),
        name="ffn",
    )(x, mod_l, pre, post, w_in3, w_in3, w_out2)


def _inproj_body(x_ref, mod_ref, pre_ref, w_ref, o_ref, h_scr, r_scr):
    @pl.when(pl.program_id(1) == 0)
    def _():
        _norm_modulate(x_ref, mod_ref, pre_ref, h_scr, r_scr)

    o_ref[...] = _dot(h_scr[...], w_ref[...]).astype(o_ref.dtype)


def _inproj(x, mod_l, pre, w_in_p, layer, cond, out_dtype):
    t, d = x.shape
    tm = TM * (4 // jnp.dtype(out_dtype).itemsize)
    n = w_in_p.shape[2]
    return pl.pallas_call(
        _inproj_body,
        grid=(t // tm, n // TN_IN),
        in_specs=[
            pl.BlockSpec((tm, d), lambda i, j: (i, 0)),
            pl.BlockSpec((None, 1, 3 * d), lambda i, j: (_cond_index(i, tm, cond), 0, 1)),
            pl.BlockSpec((1, d), lambda i, j: (0, 0)),
            pl.BlockSpec((None, d, TN_IN), lambda i, j: (layer, 0, j)),
        ],
        out_specs=pl.BlockSpec((tm, TN_IN), lambda i, j: (i, j)),
        out_shape=jax.ShapeDtypeStruct((t, n), out_dtype),
        scratch_shapes=[pltpu.VMEM((tm, d), bf16), pltpu.VMEM((tm, LANE), f32)],
        compiler_params=_cparams(("parallel", "arbitrary")),
        name="inproj",
    )(x, mod_l, pre, w_in_p)


POOL_PAD = 8


def _pool_body(u_ref, w_ref, sc_ref, o_ref, pad_scr, lvl_scr, *, seq):
    gc = POOL_GC
    pad = POOL_PAD
    n_lvl = seq + pad
    zeros = jnp.zeros((pad, POOL_WIDTH), f32)
    pad_scr[pl.ds(0, pad), :] = zeros
    pad_scr[pl.ds(pad + seq, pad), :] = zeros
    pad_scr[pl.ds(pad, seq), :] = u_ref[...].astype(f32)
    lvl_scr[:, pl.ds(n_lvl, pad), :] = jnp.zeros((2, pad, gc), f32)
    t = lax.broadcasted_iota(jnp.int32, (seq, 1), 0)
    for gi, win in enumerate(POOL_WINDOWS):
        cols = pl.ds(gi * gc, gc)
        read = lambda off, n: pad_scr[pl.ds(off, n), cols]
        k, slot = 1, 0
        while 2 * k < win:
            lvl_scr[slot, pl.ds(0, n_lvl), :] = read(0, n_lvl) + read(k, n_lvl)
            read = functools.partial(lambda s_, off, n: lvl_scr[s_, pl.ds(off, n), :], slot)
            k, slot = 2 * k, 1 - slot
        lo = jnp.maximum(t - win // 2, 0)
        hi = jnp.minimum(t + win - 1 - win // 2, seq - 1)
        inv_cnt = 1.0 / (hi - lo + 1).astype(f32)
        acc = read(pad - win // 2, seq) + read(pad, seq)
        pooled = acc * inv_cnt - pad_scr[pl.ds(pad, seq), cols]
        y = _dot(pooled.astype(bf16), w_ref[gi])
        o_ref[:, cols] = (y * sc_ref[:, cols]).astype(o_ref.dtype)


def _pool(p, pool_w, pool_scale, seq, row_block0, nseq):
    cb = OFF_POOL // POOL_WIDTH
    return pl.pallas_call(
        functools.partial(_pool_body, seq=seq),
        grid=(nseq,),
        in_specs=[
            pl.BlockSpec((seq, POOL_WIDTH), lambda s: (row_block0 + s, cb)),
            pl.BlockSpec((POOL_GROUPS, POOL_GC, POOL_GC), lambda s: (0, 0, 0)),
            pl.BlockSpec((1, POOL_WIDTH), lambda s: (0, 0)),
        ],
        out_specs=pl.BlockSpec((seq, POOL_WIDTH), lambda s: (s, 0)),
        out_shape=jax.ShapeDtypeStruct((nseq * seq, POOL_WIDTH), bf16),
        scratch_shapes=[pltpu.VMEM((seq + 2 * POOL_PAD, POOL_WIDTH), f32),
                        pltpu.VMEM((2, seq + 2 * POOL_PAD, POOL_GC), f32)],
        compiler_params=_cparams(("parallel",)),
        name="pool",
    )(p, pool_w, pool_scale)


def _lane_fold(x, op):
    parts = [x[:, i * LANE:(i + 1) * LANE] for i in range(x.shape[1] // LANE)]
    while len(parts) > 1:
        parts = [op(parts[i], parts[i + 1]) for i in range(0, len(parts) - 1, 2)] + parts[len(parts) & ~1:]
    return parts[0]


def _softmax_rows(s):
    m = jnp.max(_lane_fold(s, jnp.maximum), axis=-1, keepdims=True)
    e = jnp.exp(s - m)
    return e / jnp.sum(_lane_fold(e, jnp.add), axis=-1, keepdims=True)


def _ctx_attn_body(q_ref, k_ref, v_ref, *refs):
    o_ref, nk_ref, nv_ref = refs[-3:]
    hd = NA_HEAD_DIM
    for h in range(NA_HEADS):
        cols = pl.ds(h * hd, hd)
        kf = k_ref[:, cols]
        vf = v_ref[:, cols]
        nk_ref[h] = kf
        nv_ref[h] = vf
        p = _softmax_rows(_dot_nt(q_ref[:, cols].astype(bf16), kf.astype(bf16)) * (hd ** -0.5))
        o_ref[:, cols] = _dot(p.astype(bf16), vf.astype(bf16)).astype(o_ref.dtype)


def _ctx_attn(p, nseq, layer, depth, caches=None):
    spec = lambda off: pl.BlockSpec((SEQ, NA_WIDTH), lambda b: (b, off // NA_WIDTH))
    cache_spec = pl.BlockSpec((None, None, NA_HEADS, SEQ, NA_HEAD_DIM), lambda b: (b, layer, 0, 0, 0))
    cache_shape = jax.ShapeDtypeStruct((nseq, depth, NA_HEADS, SEQ, NA_HEAD_DIM), f32)
    in_specs = [spec(OFF_NQ), spec(OFF_NK), spec(OFF_NV)]
    args = [p, p, p]
    aliases = {}
    if caches is not None:
        in_specs += [pl.BlockSpec(memory_space=pl.ANY)] * 2
        args += list(caches)
        aliases = {3: 1, 4: 2}
    return pl.pallas_call(
        _ctx_attn_body,
        grid=(nseq,),
        in_specs=in_specs,
        out_specs=[pl.BlockSpec((SEQ, NA_WIDTH), lambda b: (b, 0)), cache_spec, cache_spec],
        out_shape=[jax.ShapeDtypeStruct((nseq * SEQ, NA_WIDTH), bf16), cache_shape, cache_shape],
        input_output_aliases=aliases,
        compiler_params=_cparams(("parallel",)),
        name="ctx_attn",
    )(*args)


LOG2E = float(np.log2(np.e))


def _na_bias_table(rpb):
    qc = np.arange(GRID_W)[:, None]
    kc = np.arange(GRID_W)[None, :]
    cs = np.clip(qc - NA_WIN_W // 2, 0, GRID_W - NA_WIN_W)
    ok = (kc >= cs) & (kc < cs + NA_WIN_W)
    cidx = np.clip(kc - qc + NA_WIN_W - 1, 0, 2 * NA_WIN_W - 2)
    onehot = jnp.asarray((cidx[None] == np.arange(2 * NA_WIN_W - 1)[:, None, None]) & ok[None], f32)
    toep = jnp.einsum('...rc,cqk->...rqk', rpb.astype(f32), onehot, precision=lax.Precision.HIGHEST)
    toep = jnp.where(ok, toep * LOG2E, NEG_INF)
    return jnp.concatenate([toep[..., :-1, :, :], toep[..., 1:, :, :]], axis=-1)


NA_UNROLL = 8
NA_CTX_ROWS = 256


def _na_body(*refs, rows, side):
    ns = len(side)
    qb_scr, kb_scr, vb_scr, ck_ref, cv_ref, bias_ref = refs[:6]
    o_ref = refs[6 + ns]
    sl_scr, sc_scr, el_scr, ec_scr, den_scr, oc_scr = refs[7 + 2 * ns:]
    step = pl.program_id(0) * pl.num_programs(1) + pl.program_id(1)
    for cast, src_ref, dst_ref in zip(side, refs[6:6 + ns], refs[7 + ns:7 + 2 * ns]):
        cast(step, src_ref, dst_ref)

    hd = NA_HEAD_DIM
    scale = hd ** -0.5 * LOG2E
    kh = min(NA_WIN_H, rows)
    nloc = kh * GRID_W
    n = rows * GRID_W
    assert qb_scr.dtype == bf16
    ck = ck_ref[...].astype(bf16)
    cv = cv_ref[...].astype(bf16)

    def row_slices(r):
        rs = jnp.clip(r - kh // 2, 0, rows - kh)
        q_rows = pl.ds(pl.multiple_of(r * GRID_W, GRID_W), GRID_W)
        k_rows = pl.ds(pl.multiple_of(rs * GRID_W, GRID_W), nloc)
        return rs, q_rows, k_rows

    def ctx_scores(i, carry):
        blk = pl.ds(pl.multiple_of(i * NA_CTX_ROWS, NA_CTX_ROWS), NA_CTX_ROWS)
        sc_scr[blk, :] = _dot_nt(qb_scr[blk, :], ck) * scale
        return carry

    lax.fori_loop(0, n // NA_CTX_ROWS, ctx_scores, 0, unroll=2)

    def loc_scores(r, carry):
        rs, q_rows, k_rows = row_slices(r)
        first = rs - r + NA_WIN_H - 1
        bias = jnp.concatenate([bias_ref[first + 2 * e] for e in range(kh // 2)], axis=1)
        sl_scr[q_rows, :] = _dot_nt(qb_scr[q_rows, :], kb_scr[k_rows, :]) * scale + bias
        return carry

    lax.fori_loop(0, rows, loc_scores, 0, unroll=NA_UNROLL)

    def numerators(r, carry):
        q_rows = pl.ds(pl.multiple_of(r * GRID_W, GRID_W), GRID_W)
        s_loc = sl_scr[q_rows, :]
        s_ctx = sc_scr[q_rows, :]
        m = jnp.max(jnp.maximum(_lane_fold(s_loc, jnp.maximum), _lane_fold(s_ctx, jnp.maximum)),
                    axis=-1, keepdims=True)
        e_loc = jnp.exp2(s_loc - m)
        e_ctx = jnp.exp2(s_ctx - m)
        den = jnp.sum(_lane_fold(e_loc, jnp.add) + _lane_fold(e_ctx, jnp.add), axis=-1, keepdims=True)
        el_scr[q_rows, :] = e_loc.astype(bf16)
        ec_scr[q_rows, :] = e_ctx.astype(bf16)
        den_scr[q_rows, :] = jnp.broadcast_to(den, (GRID_W, hd))
        return carry

    lax.fori_loop(0, rows, numerators, 0, unroll=NA_UNROLL)

    def ctx_values(i, carry):
        blk = pl.ds(pl.multiple_of(i * NA_CTX_ROWS, NA_CTX_ROWS), NA_CTX_ROWS)
        oc_scr[blk, :] = _dot(ec_scr[blk, :], cv)
        return carry

    lax.fori_loop(0, n // NA_CTX_ROWS, ctx_values, 0, unroll=2)

    def loc_values(r, carry):
        _, q_rows, k_rows = row_slices(r)
        o = _dot(el_scr[q_rows, :], vb_scr[k_rows, :]) + oc_scr[q_rows, :]
        o_ref[q_rows, :] = (o / den_scr[q_rows, :]).astype(o_ref.dtype)
        return carry

    lax.fori_loop(0, rows, loc_values, 0, unroll=NA_UNROLL)


def _na_latent(p, cache_k, cache_v, bias_tbl, layer, row_block0, nreq, make_side=()):
    n = DEC_SEQ
    hd = NA_HEAD_DIM
    past = cache_k.shape[3]
    rows = n // GRID_W
    side = [make(nreq * NA_HEADS, lambda b, h: b * NA_HEADS + h) for make in make_side]
    qkv = lambda off: pl.BlockSpec((n, hd), lambda b, h: (row_block0 + b, off // hd + h))
    cache = pl.BlockSpec((None, None, None, past, hd), lambda b, h: (b, layer, h, 0, 0))
    return pl.pallas_call(
        functools.partial(_na_body, rows=rows, side=tuple(j.body for j in side)),
        grid=(nreq, NA_HEADS),
        in_specs=[qkv(OFF_NQ), qkv(OFF_NK), qkv(OFF_NV), cache, cache,
                  pl.BlockSpec((None, None, 2 * NA_WIN_H - 2, GRID_W, 2 * GRID_W),
                               lambda b, h: (layer, h, 0, 0, 0))] + [j.in_spec for j in side],
        out_specs=[pl.BlockSpec((n, hd), lambda b, h: (b, h))] + [j.out_spec for j in side],
        out_shape=[jax.ShapeDtypeStruct((nreq * n, NA_WIDTH), bf16)] + [j.out_shape for j in side],
        scratch_shapes=[pltpu.VMEM((n, NA_WIN_H * GRID_W), f32), pltpu.VMEM((n, past), f32),
                        pltpu.VMEM((n, NA_WIN_H * GRID_W), bf16), pltpu.VMEM((n, past), bf16),
                        pltpu.VMEM((n, hd), f32), pltpu.VMEM((n, hd), f32)],
        compiler_params=_cparams(("arbitrary", "arbitrary")),
        name="na_latent",
    )(p, p, p, cache_k, cache_v, bias_tbl, *[j.array for j in side])


GLA_PAD = 32
GLA_UNROLL = 4


def _rope_tables(seq):
    t = np.arange(seq)
    half = GLA_DK // 2
    nf = half // 2
    inv = ROPE_BASE ** (-np.arange(nf, dtype=np.float64) / nf)
    cos, sin = [], []
    for pos in (t // GRID_W, t % GRID_W):
        ang = pos[:, None].astype(np.float64) * inv
        cos += [np.cos(ang), np.cos(ang)]
        sin += [-np.sin(ang), np.sin(ang)]
    return (jnp.asarray(np.concatenate(cos, axis=-1), f32), jnp.asarray(np.concatenate(sin, axis=-1), f32))


def _rope(x, cos, sin_signed):
    nf = GLA_DK // 4
    lane = lax.broadcasted_iota(jnp.int32, x.shape, 1)
    partner = jnp.where(lane % (2 * nf) < nf, pltpu.roll(x, GLA_DK - nf, 1), pltpu.roll(x, nf, 1))
    return x * cos + partner * sin_signed


def _log_sigmoid(x):
    return jnp.minimum(x, 0.0) - jnp.log1p(jnp.exp(-jnp.abs(x)))


def _gla_body(*refs, seq, rope, with_s0, with_sfin, n_carried, side):
    refs = list(refs)
    q_ref, k_ref, v_ref, r_ref, z_ref, wg_ref, bg_ref, ng_ref = refs[:8]
    refs = refs[8:]
    if rope:
        cos_ref, sin_ref = refs[:2]
        refs = refs[2:]
    if with_s0:
        s0_ref = refs[0]
        refs = refs[1:]
    refs = refs[n_carried:]
    side_src, refs = refs[:len(side)], refs[len(side):]
    o_ref = refs[0]
    refs = refs[1:]
    if with_sfin:
        sfin_ref = refs[0]
        refs = refs[1:]
    side_dst, refs = refs[:len(side)], refs[len(side):]
    step = pl.program_id(0) * pl.num_programs(1) + pl.program_id(1)
    for cast, src_ref, dst_ref in zip(side, side_src, side_dst):
        cast(step, src_ref, dst_ref)
    qi_scr, kn_scr, kd_scr, dec_scr, scan_scr, vb_scr, u_scr, sb_scr, o_scr, st_scr = refs

    ch = GLA_CHUNK
    nch = seq // ch
    dk, dv = GLA_DK, GLA_DV

    q = q_ref[...].astype(f32)
    k = k_ref[...].astype(f32)
    if rope:
        q = _rope(q, cos_ref[...], sin_ref[...])
        k = _rope(k, cos_ref[...], sin_ref[...])
    q = q * (dk ** -0.5)

    zb = z_ref[...].astype(bf16)
    pos = lax.broadcasted_iota(jnp.int32, (seq, 1), 0) % ch
    zpad = jnp.zeros((GLA_PAD, dk), f32)
    scan_scr[pl.ds(0, GLA_PAD), :] = zpad
    scan_scr[pl.ds(GLA_PAD + seq, GLA_PAD), :] = zpad
    for d in range(2):
        g = _log_sigmoid(_dot(zb, wg_ref[d]) + bg_ref[d]) / GLA_TAU
        b = g
        sh = 1
        while sh < ch:
            scan_scr[pl.ds(GLA_PAD, seq), :] = b
            if d == 0:
                b = b + jnp.where(pos >= sh, scan_scr[pl.ds(GLA_PAD - sh, seq), :], 0.0)
            else:
                b = b + jnp.where(pos < ch - sh, scan_scr[pl.ds(GLA_PAD + sh, seq), :], 0.0)
            sh *= 2
        b3 = b.reshape(nch, ch, dk)
        b_end = b3[:, ch - 1:ch, :] if d == 0 else b3[:, 0:1, :]
        lanes = pl.ds(d * dk, dk)
        qi_scr[:, lanes] = (q * jnp.exp(b)).astype(bf16)
        kn_scr[d] = (k * jnp.exp(-b)).astype(bf16)
        kd_scr[:, lanes] = (k.reshape(nch, ch, dk) * jnp.exp(b_end - b3)).reshape(seq, dk).astype(bf16)
        dec_scr[d] = jnp.exp(b_end)

    for d in range(2):
        if with_s0:
            st_scr[d] = s0_ref[d].T
        else:
            st_scr[d] = jnp.zeros((dv, dk), f32)

    ri = lax.broadcasted_iota(jnp.int32, (ch, ch), 0)
    ci = lax.broadcasted_iota(jnp.int32, (ch, ch), 1)

    vb_scr[...] = v_ref[...].astype(bf16)
    chunk_rows = lambda c: pl.ds(pl.multiple_of(c * ch, ch), ch)
    fwd, bwd = pl.ds(0, dk), pl.ds(dk, dk)

    def increments(c, carry):
        rows = chunk_rows(c)
        u_scr[c] = lax.dot_general(vb_scr[rows, :], kd_scr[rows, :], (((0,), (0,)), ((), ())),
                                   preferred_element_type=f32)
        return carry

    lax.fori_loop(0, nch, increments, 0, unroll=GLA_UNROLL)

    def states(i, carry):
        for d, c, lanes in ((0, i, fwd), (1, nch - 1 - i, bwd)):
            st = st_scr[d]
            sb_scr[c, :, lanes] = st.astype(bf16)
            st_scr[d] = st * dec_scr[d, c] + u_scr[c, :, lanes]
        return carry

    lax.fori_loop(0, nch, states, 0)

    if with_sfin:
        for d in range(2):
            sfin_ref[d] = st_scr[d].T

    def outputs(c, carry):
        rows = chunk_rows(c)
        qi = qi_scr[rows, :]
        pf = _dot_nt(qi[:, 0:dk], kn_scr[0, rows, :])
        pb = _dot_nt(qi[:, dk:2 * dk], kn_scr[1, rows, :])
        a = jnp.where(ci < ri, pf, jnp.where(ci > ri, pb, pf + pb))
        o_scr[rows, :] = _dot(a.astype(bf16), vb_scr[rows, :]) + _dot_nt(qi, sb_scr[c])
        return carry

    lax.fori_loop(0, nch, outputs, 0, unroll=GLA_UNROLL)

    o = o_scr[...]
    r = r_ref[...].astype(f32)
    o = o * lax.rsqrt(jnp.mean(o * o, axis=-1, keepdims=True) + EPS) * ng_ref[...]
    o_ref[...] = (o * (r * jax.nn.sigmoid(r))).astype(o_ref.dtype)


def _gla(p, wgate_p, b_gate, gla_norm, seq, row_block0, nreq, rope_tabs=None, state=None, layer=0,
         with_sfin=False, depth=1, carried=None, make_side=()):
    dk, dv = GLA_DK, GLA_DV
    nch = seq // GLA_CHUNK
    rope = rope_tabs is not None
    with_s0 = state is not None
    blk = lambda w, off: pl.BlockSpec((seq, w), lambda b, h: (row_block0 + b, off // w + h))
    in_specs = [blk(dk, OFF_GQ), blk(dk, OFF_GK), blk(dv, OFF_GV), blk(dv, OFF_GR),
                pl.BlockSpec((seq, LANE), lambda b, h: (row_block0 + b, OFF_GZ // LANE)),
                pl.BlockSpec((2, LANE, dk), lambda b, h: (0, 0, h)),
                pl.BlockSpec((2, 1, dk), lambda b, h: (0, 0, h)),
                pl.BlockSpec((1, dv), lambda b, h: (0, h))]
    args = [p, p, p, p, p, wgate_p, b_gate.reshape(2, 1, GLA_KW), gla_norm.reshape(1, GLA_VW)]
    if rope:
        in_specs += [pl.BlockSpec((seq, dk), lambda b, h: (0, 0))] * 2
        args += list(rope_tabs)
    if with_s0:
        in_specs.append(pl.BlockSpec((None, None, 2, None, dk, dv), lambda b, h: (b, layer, 0, h, 0, 0)))
        args.append(state)
    aliases = {}
    if carried is not None:
        aliases = {len(args): 1}
        in_specs.append(pl.BlockSpec(memory_space=pl.ANY))
        args.append(carried)
    side = [make(nreq * GLA_HEADS, lambda b, h: b * GLA_HEADS + h) for make in make_side]
    in_specs += [j.in_spec for j in side]
    args += [j.array for j in side]
    out_specs = [pl.BlockSpec((seq, dv), lambda b, h: (b, h))]
    out_shape = [jax.ShapeDtypeStruct((nreq * seq, GLA_VW), bf16)]
    if with_sfin:
        out_specs.append(pl.BlockSpec((None, None, 2, None, dk, dv), lambda b, h: (b, layer, 0, h, 0, 0)))
        out_shape.append(jax.ShapeDtypeStruct((nreq, depth, 2, GLA_HEADS, dk, dv), f32))
    out_specs += [j.out_spec for j in side]
    out_shape += [j.out_shape for j in side]
    scratch = [pltpu.VMEM((seq, 2 * dk), bf16), pltpu.VMEM((2, seq, dk), bf16), pltpu.VMEM((seq, 2 * dk), bf16),
               pltpu.VMEM((2, nch, 1, dk), f32), pltpu.VMEM((seq + 2 * GLA_PAD, dk), f32),
               pltpu.VMEM((seq, dv), bf16), pltpu.VMEM((nch, dv, 2 * dk), f32), pltpu.VMEM((nch, dv, 2 * dk), bf16),
               pltpu.VMEM((seq, dv), f32), pltpu.VMEM((2, dv, dk), f32)]
    return pl.pallas_call(
        functools.partial(_gla_body, seq=seq, rope=rope, with_s0=with_s0, with_sfin=with_sfin,
                          n_carried=len(aliases), side=tuple(j.body for j in side)),
        grid=(nreq, GLA_HEADS),
        in_specs=in_specs,
        out_specs=out_specs,
        out_shape=out_shape,
        input_output_aliases=aliases,
        scratch_shapes=scratch,
        compiler_params=_cparams(("arbitrary", "arbitrary")),
        name="gla",
    )(*args)


TM_MERGE = 512
MERGE_SUB = 2


def _merge_body(x_ref, mod_ref, post_ref, bp_ref, bn_ref, bg_ref, gl_ref, w_ref, o_ref, m_scr, mb_scr, r_scr):
    n = pl.program_id(1)
    m_ref = m_scr.at[pl.program_id(2)]
    mb_ref = mb_scr.at[pl.program_id(2)]

    for bi, br_ref in enumerate((bp_ref, bn_ref, bg_ref)):
        @pl.when(n == bi)
        def _():
            y = jax.nn.sigmoid(gl_ref[...].astype(f32)) * _dot(br_ref[...], w_ref[...])
            m_ref[...] = y if bi == 0 else m_ref[...] + y

    @pl.when(n == N_BRANCH)
    def _():
        mb_ref[...] = m_ref[...].astype(bf16)
        m_ref[...] = _dot(mb_ref[:, 0:BRANCH_W], w_ref[...])

    @pl.when(n == N_BRANCH + 1)
    def _():
        m_ref[...] += _dot(mb_ref[:, BRANCH_W:2 * BRANCH_W], w_ref[...])
        _norm_gate_residual(m_ref, x_ref, mod_ref, post_ref, o_ref, r_scr, 1.0)


def _merge(x, mod_l, post, y_pool, y_na, y_gla, p, w_stack, layer, cond):
    t, d = x.shape
    tm = TM_MERGE
    sub = MERGE_SUB
    nsteps = N_BRANCH + d // BRANCH_W
    tile = lambda i, s: i * sub + s

    def rows_at(first, last):
        def index(i, n, s):
            return jnp.where(n < first, jnp.maximum(tile(i, 0) - 1, 0),
                             jnp.where(n > last, tile(i, sub - 1), tile(i, s)))
        return index

    last = nsteps - 1
    br = lambda step: pl.BlockSpec((tm, BRANCH_W), lambda i, n, s: (rows_at(step, step)(i, n, s), 0))
    xo = pl.BlockSpec((tm, d), lambda i, n, s: (rows_at(last, last)(i, n, s), 0))
    return pl.pallas_call(
        _merge_body,
        grid=(t // (tm * sub), nsteps, sub),
        in_specs=[
            xo,
            pl.BlockSpec((None, 1, 3 * d), lambda i, n, s: (_cond_index(tile(i, s), tm, cond), 0, 1)),
            pl.BlockSpec((1, d), lambda i, n, s: (0, 0)),
            br(0), br(1), br(2),
            pl.BlockSpec((tm, d),
                         lambda i, n, s: (rows_at(0, N_BRANCH - 1)(i, n, s), jnp.minimum(n, N_BRANCH - 1))),
            pl.BlockSpec((None, None, BRANCH_W, d), lambda i, n, s: (layer, n, 0, 0)),
        ],
        out_specs=xo,
        out_shape=jax.ShapeDtypeStruct((t, d), f32),
        scratch_shapes=[pltpu.VMEM((sub, tm, d), f32), pltpu.VMEM((sub, tm, d), bf16), pltpu.VMEM((tm, LANE), f32)],
        compiler_params=_cparams(("arbitrary", "arbitrary", "arbitrary")),
        name="merge",
    )(x, mod_l, post, y_pool, y_na, y_gla, p, w_stack)


_IN_SPLITS = (POOL_WIDTH, NA_WIDTH, NA_WIDTH, NA_WIDTH, GLA_KW, GLA_KW, GLA_VW, 2 * GLA_RANK, GLA_VW, GATE_W)
_IN_OFFS = tuple(int(v) for v in np.cumsum((0,) + _IN_SPLITS))
_IN_RUNS = ((_IN_OFFS[9], _IN_OFFS[10]), (_IN_OFFS[0], _IN_OFFS[7]), (_IN_OFFS[8], _IN_OFFS[9]),
            (_IN_OFFS[7], _IN_OFFS[8]))


CT_IN = 512
_IN_TILE_STARTS = []
for _a, _b in _IN_RUNS:
    _IN_TILE_STARTS += [_a + CT_IN * _t for _t in range(-(-(_b - _a) // CT_IN))]
assert len(_IN_TILE_STARTS) * CT_IN == IN_COLS_P and all(v % 8 == 0 for v in _IN_TILE_STARTS)
_IN_LAST_VALID = (_IN_RUNS[-1][1] - _IN_RUNS[-1][0]) % CT_IN or CT_IN
assert all((b - a) % CT_IN == 0 for a, b in _IN_RUNS[:-1])


def _w_in_tile_start(j):
    out = jnp.int32(_IN_TILE_STARTS[0]) + CT_IN * j
    for t in range(1, len(_IN_TILE_STARTS)):
        if _IN_TILE_STARTS[t] != _IN_TILE_STARTS[t - 1] + CT_IN:
            out = jnp.where(j >= t, _IN_TILE_STARTS[t] + CT_IN * (j - t), out)
    return out


def _cast_w_in_body(w_ref, o_ref):
    last = pl.program_id(1) == pl.num_programs(1) - 1
    col = lax.broadcasted_iota(jnp.int32, (1, CT_IN), 1)
    valid = jnp.where(last, _IN_LAST_VALID, CT_IN)
    o_ref[...] = jnp.where(col < valid, w_ref[...].T, 0.0).astype(bf16)


def _cast_w_in(w):
    depth, d, n = w.shape
    wt = jnp.swapaxes(w, 1, 2).reshape(depth * n, d)
    return pl.pallas_call(
        _cast_w_in_body,
        grid=(depth, IN_COLS_P // CT_IN),
        in_specs=[pl.BlockSpec((pl.Element(CT_IN), pl.Element(d)),
                               lambda l, j: (pl.multiple_of(l * n + _w_in_tile_start(j), 8), 0))],
        out_specs=pl.BlockSpec((None, d, CT_IN), lambda l, j: (l, 0, j)),
        out_shape=jax.ShapeDtypeStruct((depth, d, IN_COLS_P), bf16),
        compiler_params=_cparams(("parallel", "parallel")),
        name="cast_w_in",
    )(wt)


class _SideCast(NamedTuple):
    array: Any
    in_spec: pl.BlockSpec
    out_spec: pl.BlockSpec
    out_shape: jax.ShapeDtypeStruct
    body: Callable


def _cast_ffn_in_body(step, w_ref, o_ref):
    del step
    pad = jnp.zeros((w_ref.shape[0], D_FF_P - D_FF), bf16)
    for half in range(2):
        o_ref[half, :, 0:D_FF] = w_ref[:, half * D_FF:(half + 1) * D_FF].astype(bf16)
        o_ref[half, :, D_FF:D_FF_P] = pad


def _ffn_in_cast(w, layer, slot, nsteps, step_of):
    d = w.shape[2]
    rows = d // nsteps
    assert rows * nsteps == d and rows % 16 == 0
    return _SideCast(
        w,
        pl.BlockSpec((None, None, rows, 2 * D_FF), lambda *g: (layer, slot, step_of(*g), 0)),
        pl.BlockSpec((2, rows, D_FF_P), lambda *g: (0, step_of(*g), 0)),
        jax.ShapeDtypeStruct((2, d, D_FF_P), bf16),
        _cast_ffn_in_body)


def _cast_ffn_out_body(step, w_ref, o_ref):
    rows = w_ref.shape[0]
    row = step * rows + lax.broadcasted_iota(jnp.int32, (rows, 1), 0)
    o_ref[...] = jnp.where(row < D_FF, w_ref[...], 0.0).astype(bf16)


def _ffn_out_cast(w, layer, slot, nsteps, step_of):
    d = w.shape[3]
    rows = D_FF_P // nsteps
    assert rows * nsteps == D_FF_P and rows % 16 == 0 and rows * (nsteps - 1) < D_FF
    return _SideCast(
        w,
        pl.BlockSpec((None, None, rows, d), lambda *g: (layer, slot, step_of(*g), 0)),
        pl.BlockSpec((rows, d), lambda *g: (step_of(*g), 0)),
        jax.ShapeDtypeStruct((D_FF_P, d), bf16),
        _cast_ffn_out_body)


def _run_cast_body(w_ref, o_ref, *, cast):
    cast(pl.program_id(0), w_ref, o_ref)


def _run_cast(make_job, nsteps, name):
    job = make_job(nsteps, lambda r: r)
    return pl.pallas_call(
        functools.partial(_run_cast_body, cast=job.body),
        grid=(nsteps,),
        in_specs=[job.in_spec],
        out_specs=job.out_spec,
        out_shape=job.out_shape,
        compiler_params=_cparams(("parallel",)),
        name=name,
    )(job.array)


def _cast_w_stack_body(wb_ref, wo_ref, o_ref):
    is_branch = pl.program_id(1) < N_BRANCH

    @pl.when(is_branch)
    def _():
        o_ref[...] = wb_ref[...].astype(bf16)

    @pl.when(jnp.logical_not(is_branch))
    def _():
        o_ref[...] = wo_ref[...].astype(bf16)


def _cast_w_stack(w_branch, w_out):
    depth, nb, bw, d = w_branch.shape
    kparts = d // bw
    wo = w_out.reshape(depth, kparts, bw, d)
    blk = lambda index: pl.BlockSpec((None, None, bw, d), index)
    return pl.pallas_call(
        _cast_w_stack_body,
        grid=(depth, nb + kparts),
        in_specs=[blk(lambda l, j: (l, jnp.minimum(j, nb - 1), 0, 0)),
                  blk(lambda l, j: (l, jnp.maximum(j - nb, 0), 0, 0))],
        out_specs=blk(lambda l, j: (l, j, 0, 0)),
        out_shape=jax.ShapeDtypeStruct((depth, nb + kparts, bw, d), bf16),
        compiler_params=_cparams(("parallel", "arbitrary")),
        name="cast_w_stack",
    )(w_branch, wo)


def _prep_gate(w_gate):
    out = jnp.zeros((2, LANE, GLA_KW), f32)
    for d in range(2):
        out = out.at[d, d * GLA_RANK:(d + 1) * GLA_RANK].set(w_gate[d])
    return out.astype(bf16)


def kernel(x_prompt, x_sample, c, cache_na_k, cache_na_v, state_gla, c_ctx, w_mod, b_mod, norm_pre, norm_post,
           w_ffn_in, w_ffn_out, w_in, pool_w, pool_scale, na_rpb, gla_w_gate, gla_b_gate, gla_norm, w_branch,
           w_out):
    nb, seq, d = x_prompt.shape
    ndec, dseq, _ = x_sample.shape
    depth = w_mod.shape[0]
    n_ctx = nb * seq
    n_lat = ndec * dseq
    assert (seq, dseq, d) == (SEQ, DEC_SEQ, D_MODEL)
    assert n_ctx % (TM_MERGE * MERGE_SUB) == 0 and n_ctx % TM_FFN == 0 and dseq % TM_FFN == 0

    xs = [x_prompt.reshape(n_ctx, d), x_sample.reshape(n_lat, d)]
    conds = [(0, n_ctx), (1, dseq)]
    ncond = -(-(1 + ndec) // 8) * 8
    c_all = jnp.concatenate([c_ctx[None], c, jnp.zeros((ncond - 1 - ndec, d), f32)], axis=0)
    mod = _modulation(c_all, w_mod, b_mod)
    rope_tabs = _rope_tables(dseq)
    bias_tbl = _na_bias_table(na_rpb)

    in_cast = lambda l, s: functools.partial(_ffn_in_cast, w_ffn_in, l, s)
    out_cast = lambda l, s: functools.partial(_ffn_out_cast, w_ffn_out, l, s)
    ffn_w_in = {(0, 0): _run_cast(in_cast(0, 0), d // 256, "cast_ffn_in")}
    ffn_w_out = {(0, 0): _run_cast(out_cast(0, 0), D_FF_P // TF, "cast_ffn_out")}
    w_in_p = _cast_w_in(w_in)
    w_stack_all = _cast_w_stack(w_branch, w_out)

    caches = None
    new_s = None
    for l in range(depth):
        mod_l = mod[l].reshape(ncond, 1, N_MOD * d)
        pre = norm_pre[l].reshape(3, 1, d)
        post = norm_post[l].reshape(3, 1, d)
        wgate_p = _prep_gate(gla_w_gate[l])
        pw = pool_w[l].astype(bf16)
        psc = pool_scale[l].reshape(1, POOL_WIDTH)
        later = [(l, 1)] + ([(l + 1, 0)] if l + 1 < depth else [])

        xs = [_ffn(x, mod_l, pre[0], post[0], ffn_w_in[l, 0], ffn_w_out[l, 0], 0, cond) for x, cond in zip(xs, conds)]
        p_ctx, p_lat = [_inproj(x, mod_l, pre[1], w_in_p, l, cond, dt) for x, cond, dt in zip(xs, conds, (f32, bf16))]

        y_pool = [_pool(p_ctx, pw, psc, seq, 0, nb), _pool(p_lat, pw, psc, dseq, 0, ndec)]
        na_ctx, new_k, new_v = _ctx_attn(p_ctx, nb, l, depth, caches)
        caches = (new_k, new_v)
        na_lat, *cast = _na_latent(p_lat, cache_na_k, cache_na_v, bias_tbl, l, 0, ndec,
                                   make_side=[in_cast(*ls) for ls in later])
        ffn_w_in.update(zip(later, cast))
        y_na = [na_ctx, na_lat]
        g_ctx, new_s = _gla(p_ctx, wgate_p, gla_b_gate[l], gla_norm[l], seq, 0, nb, layer=l, with_sfin=True,
                            depth=depth, carried=new_s)
        g_lat, *cast = _gla(p_lat, wgate_p, gla_b_gate[l], gla_norm[l], dseq, 0, ndec, rope_tabs=rope_tabs,
                            state=state_gla, layer=l, make_side=[out_cast(*ls) for ls in later])
        ffn_w_out.update(zip(later, cast))
        y_gla = [g_ctx, g_lat]

        xs = [_merge(x, mod_l, post[1], yp, yn, yg, p, w_stack_all, l, cond)
              for x, yp, yn, yg, p, cond in zip(xs, y_pool, y_na, y_gla, (p_ctx, p_lat), conds)]
        xs = [_ffn(x, mod_l, pre[2], post[2], ffn_w_in[l, 1], ffn_w_out[l, 1], 2, cond) for x, cond in zip(xs, conds)]

    return (xs[0].reshape(nb, seq, d), xs[1].reshape(ndec, dseq, d), caches[0], caches[1], new_s)
```

```python
import functools
from typing import Any, Callable, NamedTuple

import numpy as np
import jax
import jax.numpy as jnp
from jax import lax
from jax.experimental import pallas as pl
from jax.experimental.pallas import tpu as pltpu

f32 = jnp.float32
bf16 = jnp.bfloat16

D_MODEL = 2048
SEQ = 256
DEC_SEQ = 2048
GRID_W = 64
N_MOD = 9
D_FF = 5504
FFN_RES = 0.5
EPS = 1e-6
NEG_INF = -1e30

POOL_GROUPS = 4
POOL_WINDOWS = (2, 4, 8, 16)
POOL_WIDTH = 1024
POOL_GC = POOL_WIDTH // POOL_GROUPS

NA_HEADS = 8
NA_HEAD_DIM = 128
NA_WIDTH = NA_HEADS * NA_HEAD_DIM
NA_WIN_H = 8
NA_WIN_W = 16

GLA_HEADS = 4
GLA_DK = 128
GLA_DV = 256
GLA_KW = GLA_HEADS * GLA_DK
GLA_VW = GLA_HEADS * GLA_DV
GLA_RANK = 16
GLA_TAU = 16.0
GLA_CHUNK = 64
ROPE_BASE = 10000.0

BRANCH_W = 1024
N_BRANCH = 3
GATE_W = N_BRANCH * D_MODEL

LANE = 128
VMEM_LIMIT = 56 * 1024 * 1024

OFF_GL = 0
OFF_POOL = OFF_GL + GATE_W
OFF_NQ = OFF_POOL + POOL_WIDTH
OFF_NK = OFF_NQ + NA_WIDTH
OFF_NV = OFF_NK + NA_WIDTH
OFF_GQ = OFF_NV + NA_WIDTH
OFF_GK = OFF_GQ + GLA_KW
OFF_GV = OFF_GK + GLA_KW
OFF_GR = OFF_GV + GLA_VW
OFF_GZ = OFF_GR + GLA_VW
TN_IN = 2304
IN_COLS_P = -(-(OFF_GZ + LANE) // TN_IN) * TN_IN

TM = 512
TF = 512
D_FF_P = -(-D_FF // TF) * TF


def _cparams(sem):
    return pltpu.CompilerParams(dimension_semantics=sem, vmem_limit_bytes=VMEM_LIMIT)


def _cond_index(i, tm, cond):
    return cond[0] + (i * tm) // cond[1]


ROW_CHUNK = 16


def _row_sweep(nrows, fn, unroll=4):
    def trip(i, carry):
        fn(pl.ds(pl.multiple_of(i * ROW_CHUNK, ROW_CHUNK), ROW_CHUNK))
        return carry

    lax.fori_loop(0, nrows // ROW_CHUNK, trip, 0, unroll=unroll)


def _row_rsqrt(x_ref, r_scr):
    n = x_ref.shape[1]

    def fn(rows):
        x = x_ref[rows, :]
        ss = jnp.sum(_lane_fold(x * x, jnp.add), axis=-1, keepdims=True)
        r_scr[rows, :] = jnp.broadcast_to(lax.rsqrt(ss * (1.0 / n) + EPS), (ROW_CHUNK, LANE))

    _row_sweep(x_ref.shape[0], fn, unroll=16)


def _lanes(r, n):
    return jnp.concatenate([r] * (n // LANE), axis=1)


def _norm_modulate(x_ref, mod_ref, g_ref, h_ref, r_scr):
    d = D_MODEL
    shift = mod_ref[:, 0:d]
    w = g_ref[...] * (1.0 + mod_ref[:, d:2 * d])
    _row_rsqrt(x_ref, r_scr)

    def fn(rows):
        h_ref[rows, :] = (x_ref[rows, :] * _lanes(r_scr[rows, :], d) * w + shift).astype(h_ref.dtype)

    _row_sweep(x_ref.shape[0], fn)


def _norm_gate_residual(y_ref, x_ref, mod_ref, g_ref, o_ref, r_scr, res_weight):
    d = D_MODEL
    w = (res_weight * mod_ref[:, 2 * d:3 * d]) * g_ref[...]
    _row_rsqrt(y_ref, r_scr)

    def fn(rows):
        o_ref[rows, :] = x_ref[rows, :] + y_ref[rows, :] * _lanes(r_scr[rows, :], d) * w

    _row_sweep(x_ref.shape[0], fn)


def _dot(a, b):
    return jnp.dot(a, b, preferred_element_type=f32)


def _dot_nt(a, b):
    return lax.dot_general(a, b, (((1,), (1,)), ((), ())), preferred_element_type=f32)


def _mod_body(c_ref, w_ref, b_ref, o_ref):
    c = c_ref[...]
    s = c * jax.nn.sigmoid(c)
    o_ref[...] = _dot(s.astype(bf16), w_ref[...].astype(bf16)) + b_ref[...]


def _modulation(c_all, w_mod, b_mod):
    depth, d, n = w_mod.shape
    nc = c_all.shape[0]
    tn = 1024
    return pl.pallas_call(
        _mod_body,
        grid=(depth, n // tn),
        in_specs=[
            pl.BlockSpec((nc, d), lambda l, j: (0, 0)),
            pl.BlockSpec((None, d, tn), lambda l, j: (l, 0, j)),
            pl.BlockSpec((None, 1, tn), lambda l, j: (l, 0, j)),
        ],
        out_specs=pl.BlockSpec((None, nc, tn), lambda l, j: (l, 0, j)),
        out_shape=jax.ShapeDtypeStruct((depth, nc, n), f32),
        compiler_params=_cparams(("parallel", "parallel")),
        name="modulation",
    )(c_all, w_mod, b_mod.reshape(depth, 1, n))


TM_FFN = 1024


def _ffn_body(x_ref, mod_ref, pre_ref, post_ref, wg_ref, wu_ref, wo_ref, o_ref, h_scr, r_scr):
    f = pl.program_id(1)

    @pl.when(f == 0)
    def _():
        _norm_modulate(x_ref, mod_ref, pre_ref, h_scr, r_scr)
        o_ref[...] = jnp.zeros_like(o_ref)

    h = h_scr[...]
    gt = _dot(h, wg_ref[...])
    up = _dot(h, wu_ref[...])
    a = gt * jax.nn.sigmoid(gt) * up
    o_ref[...] += _dot(a.astype(bf16), wo_ref[...])

    @pl.when(f == pl.num_programs(1) - 1)
    def _():
        _norm_gate_residual(o_ref, x_ref, mod_ref, post_ref, o_ref, r_scr, FFN_RES)


def _ffn(x, mod_l, pre, post, w_in3, w_out2, sub, cond):
    t, d = x.shape
    fp = w_out2.shape[0]
    w_half = lambda half: pl.BlockSpec((None, d, TF), lambda i, f: (half, 0, f))
    return pl.pallas_call(
        _ffn_body,
        grid=(t // TM_FFN, fp // TF),
        in_specs=[
            pl.BlockSpec((TM_FFN, d), lambda i, f: (i, 0)),
            pl.BlockSpec((None, 1, 3 * d), lambda i, f: (_cond_index(i, TM_FFN, cond), 0, sub)),
            pl.BlockSpec((1, d), lambda i, f: (0, 0)),
            pl.BlockSpec((1, d), lambda i, f: (0, 0)),
            w_half(0),
            w_half(1),
            pl.BlockSpec((TF, d), lambda i, f: (f, 0)),
        ],
        out_specs=pl.BlockSpec((TM_FFN, d), lambda i, f: (i, 0)),
        out_shape=jax.ShapeDtypeStruct((t, d), f32),
        scratch_shapes=[pltpu.VMEM((TM_FFN, d), bf16), pltpu.VMEM((TM_FFN, LANE), f32)],
        compiler_params=_cparams(("parallel", "arbitrary")),
        name="ffn",
    )(x, mod_l, pre, post, w_in3, w_in3, w_out2)


def _inproj_body(x_ref, mod_ref, pre_ref, w_ref, o_ref, h_scr, r_scr):
    @pl.when(pl.program_id(1) == 0)
    def _():
        _norm_modulate(x_ref, mod_ref, pre_ref, h_scr, r_scr)

    o_ref[...] = _dot(h_scr[...], w_ref[...]).astype(o_ref.dtype)


def _inproj(x, mod_l, pre, w_in_p, cond, out_dtype):
    t, d = x.shape
    tm = TM * (4 // jnp.dtype(out_dtype).itemsize)
    n = w_in_p.shape[1]
    return pl.pallas_call(
        _inproj_body,
        grid=(t // tm, n // TN_IN),
        in_specs=[
            pl.BlockSpec((tm, d), lambda i, j: (i, 0)),
            pl.BlockSpec((None, 1, 3 * d), lambda i, j: (_cond_index(i, tm, cond), 0, 1)),
            pl.BlockSpec((1, d), lambda i, j: (0, 0)),
            pl.BlockSpec((d, TN_IN), lambda i, j: (0, j)),
        ],
        out_specs=pl.BlockSpec((tm, TN_IN), lambda i, j: (i, j)),
        out_shape=jax.ShapeDtypeStruct((t, n), out_dtype),
        scratch_shapes=[pltpu.VMEM((tm, d), bf16), pltpu.VMEM((tm, LANE), f32)],
        compiler_params=_cparams(("parallel", "arbitrary")),
        name="inproj",
    )(x, mod_l, pre, w_in_p)


POOL_PAD = 8


def _pool_body(u_ref, w_ref, sc_ref, o_ref, pad_scr, lvl_scr, *, seq):
    gc = POOL_GC
    pad = POOL_PAD
    n_lvl = seq + pad
    zeros = jnp.zeros((pad, POOL_WIDTH), f32)
    pad_scr[pl.ds(0, pad), :] = zeros
    pad_scr[pl.ds(pad + seq, pad), :] = zeros
    pad_scr[pl.ds(pad, seq), :] = u_ref[...].astype(f32)
    lvl_scr[:, pl.ds(n_lvl, pad), :] = jnp.zeros((2, pad, gc), f32)
    t = lax.broadcasted_iota(jnp.int32, (seq, 1), 0)
    for gi, win in enumerate(POOL_WINDOWS):
        cols = pl.ds(gi * gc, gc)
        read = lambda off, n: pad_scr[pl.ds(off, n), cols]
        k, slot = 1, 0
        while 2 * k < win:
            lvl_scr[slot, pl.ds(0, n_lvl), :] = read(0, n_lvl) + read(k, n_lvl)
            read = functools.partial(lambda s_, off, n: lvl_scr[s_, pl.ds(off, n), :], slot)
            k, slot = 2 * k, 1 - slot
        lo = jnp.maximum(t - win // 2, 0)
        hi = jnp.minimum(t + win - 1 - win // 2, seq - 1)
        inv_cnt = 1.0 / (hi - lo + 1).astype(f32)
        acc = read(pad - win // 2, seq) + read(pad, seq)
        pooled = acc * inv_cnt - pad_scr[pl.ds(pad, seq), cols]
        y = _dot(pooled.astype(bf16), w_ref[gi])
        o_ref[:, cols] = (y * sc_ref[:, cols]).astype(o_ref.dtype)


def _pool(p, pool_w, pool_scale, seq, row_block0, nseq):
    cb = OFF_POOL // POOL_WIDTH
    return pl.pallas_call(
        functools.partial(_pool_body, seq=seq),
        grid=(nseq,),
        in_specs=[
            pl.BlockSpec((seq, POOL_WIDTH), lambda s: (row_block0 + s, cb)),
            pl.BlockSpec((POOL_GROUPS, POOL_GC, POOL_GC), lambda s: (0, 0, 0)),
            pl.BlockSpec((1, POOL_WIDTH), lambda s: (0, 0)),
        ],
        out_specs=pl.BlockSpec((seq, POOL_WIDTH), lambda s: (s, 0)),
        out_shape=jax.ShapeDtypeStruct((nseq * seq, POOL_WIDTH), bf16),
        scratch_shapes=[pltpu.VMEM((seq + 2 * POOL_PAD, POOL_WIDTH), f32),
                        pltpu.VMEM((2, seq + 2 * POOL_PAD, POOL_GC), f32)],
        compiler_params=_cparams(("parallel",)),
        name="pool",
    )(p, pool_w, pool_scale)


def _lane_fold(x, op):
    parts = [x[:, i * LANE:(i + 1) * LANE] for i in range(x.shape[1] // LANE)]
    while len(parts) > 1:
        parts = [op(parts[i], parts[i + 1]) for i in range(0, len(parts) - 1, 2)] + parts[len(parts) & ~1:]
    return parts[0]


def _softmax_rows(s):
    m = jnp.max(_lane_fold(s, jnp.maximum), axis=-1, keepdims=True)
    e = jnp.exp(s - m)
    return e / jnp.sum(_lane_fold(e, jnp.add), axis=-1, keepdims=True)


def _ctx_attn_body(q_ref, k_ref, v_ref, *refs):
    o_ref, nk_ref, nv_ref = refs[-3:]
    hd = NA_HEAD_DIM
    for h in range(NA_HEADS):
        cols = pl.ds(h * hd, hd)
        kf = k_ref[:, cols]
        vf = v_ref[:, cols]
        nk_ref[h] = kf
        nv_ref[h] = vf
        p = _softmax_rows(_dot_nt(q_ref[:, cols].astype(bf16), kf.astype(bf16)) * (hd ** -0.5))
        o_ref[:, cols] = _dot(p.astype(bf16), vf.astype(bf16)).astype(o_ref.dtype)


def _ctx_attn(p, nseq, layer, depth, caches=None):
    spec = lambda off: pl.BlockSpec((SEQ, NA_WIDTH), lambda b: (b, off // NA_WIDTH))
    cache_spec = pl.BlockSpec((None, None, NA_HEADS, SEQ, NA_HEAD_DIM), lambda b: (b, layer, 0, 0, 0))
    cache_shape = jax.ShapeDtypeStruct((nseq, depth, NA_HEADS, SEQ, NA_HEAD_DIM), f32)
    in_specs = [spec(OFF_NQ), spec(OFF_NK), spec(OFF_NV)]
    args = [p, p, p]
    aliases = {}
    if caches is not None:
        in_specs += [pl.BlockSpec(memory_space=pl.ANY)] * 2
        args += list(caches)
        aliases = {3: 1, 4: 2}
    return pl.pallas_call(
        _ctx_attn_body,
        grid=(nseq,),
        in_specs=in_specs,
        out_specs=[pl.BlockSpec((SEQ, NA_WIDTH), lambda b: (b, 0)), cache_spec, cache_spec],
        out_shape=[jax.ShapeDtypeStruct((nseq * SEQ, NA_WIDTH), bf16), cache_shape, cache_shape],
        input_output_aliases=aliases,
        compiler_params=_cparams(("parallel",)),
        name="ctx_attn",
    )(*args)


LOG2E = float(np.log2(np.e))


def _na_bias_table(rpb):
    qc = np.arange(GRID_W)[:, None]
    kc = np.arange(GRID_W)[None, :]
    cs = np.clip(qc - NA_WIN_W // 2, 0, GRID_W - NA_WIN_W)
    ok = (kc >= cs) & (kc < cs + NA_WIN_W)
    cidx = np.clip(kc - qc + NA_WIN_W - 1, 0, 2 * NA_WIN_W - 2)
    onehot = jnp.asarray((cidx[None] == np.arange(2 * NA_WIN_W - 1)[:, None, None]) & ok[None], f32)
    toep = jnp.einsum('...rc,cqk->...rqk', rpb.astype(f32), onehot, precision=lax.Precision.HIGHEST)
    toep = jnp.where(ok, toep * LOG2E, NEG_INF)
    return jnp.concatenate([toep[..., :-1, :, :], toep[..., 1:, :, :]], axis=-1)


NA_UNROLL = 8
NA_CTX_ROWS = 256


def _na_body(*refs, rows, side):
    ns = len(side)
    qb_scr, kb_scr, vb_scr, ck_ref, cv_ref, bias_ref = refs[:6]
    o_ref = refs[6 + ns]
    sl_scr, sc_scr, el_scr, ec_scr, den_scr, oc_scr = refs[7 + 2 * ns:]
    step = pl.program_id(0) * pl.num_programs(1) + pl.program_id(1)
    for cast, src_ref, dst_ref in zip(side, refs[6:6 + ns], refs[7 + ns:7 + 2 * ns]):
        cast(step, src_ref, dst_ref)

    hd = NA_HEAD_DIM
    scale = hd ** -0.5 * LOG2E
    kh = min(NA_WIN_H, rows)
    nloc = kh * GRID_W
    n = rows * GRID_W
    assert qb_scr.dtype == bf16
    ck = ck_ref[...].astype(bf16)
    cv = cv_ref[...].astype(bf16)

    def row_slices(r):
        rs = jnp.clip(r - kh // 2, 0, rows - kh)
        q_rows = pl.ds(pl.multiple_of(r * GRID_W, GRID_W), GRID_W)
        k_rows = pl.ds(pl.multiple_of(rs * GRID_W, GRID_W), nloc)
        return rs, q_rows, k_rows

    def ctx_scores(i, carry):
        blk = pl.ds(pl.multiple_of(i * NA_CTX_ROWS, NA_CTX_ROWS), NA_CTX_ROWS)
        sc_scr[blk, :] = _dot_nt(qb_scr[blk, :], ck) * scale
        return carry

    lax.fori_loop(0, n // NA_CTX_ROWS, ctx_scores, 0, unroll=2)

    def loc_scores(r, carry):
        rs, q_rows, k_rows = row_slices(r)
        first = rs - r + NA_WIN_H - 1
        bias = jnp.concatenate([bias_ref[first + 2 * e] for e in range(kh // 2)], axis=1)
        sl_scr[q_rows, :] = _dot_nt(qb_scr[q_rows, :], kb_scr[k_rows, :]) * scale + bias
        return carry

    lax.fori_loop(0, rows, loc_scores, 0, unroll=NA_UNROLL)

    def numerators(r, carry):
        q_rows = pl.ds(pl.multiple_of(r * GRID_W, GRID_W), GRID_W)
        s_loc = sl_scr[q_rows, :]
        s_ctx = sc_scr[q_rows, :]
        m = jnp.max(jnp.maximum(_lane_fold(s_loc, jnp.maximum), _lane_fold(s_ctx, jnp.maximum)),
                    axis=-1, keepdims=True)
        e_loc = jnp.exp2(s_loc - m)
        e_ctx = jnp.exp2(s_ctx - m)
        den = jnp.sum(_lane_fold(e_loc, jnp.add) + _lane_fold(e_ctx, jnp.add), axis=-1, keepdims=True)
        el_scr[q_rows, :] = e_loc.astype(bf16)
        ec_scr[q_rows, :] = e_ctx.astype(bf16)
        den_scr[q_rows, :] = jnp.broadcast_to(den, (GRID_W, hd))
        return carry

    lax.fori_loop(0, rows, numerators, 0, unroll=NA_UNROLL)

    def ctx_values(i, carry):
        blk = pl.ds(pl.multiple_of(i * NA_CTX_ROWS, NA_CTX_ROWS), NA_CTX_ROWS)
        oc_scr[blk, :] = _dot(ec_scr[blk, :], cv)
        return carry

    lax.fori_loop(0, n // NA_CTX_ROWS, ctx_values, 0, unroll=2)

    def loc_values(r, carry):
        _, q_rows, k_rows = row_slices(r)
        o = _dot(el_scr[q_rows, :], vb_scr[k_rows, :]) + oc_scr[q_rows, :]
        o_ref[q_rows, :] = (o / den_scr[q_rows, :]).astype(o_ref.dtype)
        return carry

    lax.fori_loop(0, rows, loc_values, 0, unroll=NA_UNROLL)


def _na_latent(p, cache_k, cache_v, bias_tbl, layer, row_block0, nreq, make_side=()):
    n = DEC_SEQ
    hd = NA_HEAD_DIM
    past = cache_k.shape[3]
    rows = n // GRID_W
    side = [make(nreq * NA_HEADS, lambda b, h: b * NA_HEADS + h) for make in make_side]
    qkv = lambda off: pl.BlockSpec((n, hd), lambda b, h: (row_block0 + b, off // hd + h))
    cache = pl.BlockSpec((None, None, None, past, hd), lambda b, h: (b, layer, h, 0, 0))
    return pl.pallas_call(
        functools.partial(_na_body, rows=rows, side=tuple(j.body for j in side)),
        grid=(nreq, NA_HEADS),
        in_specs=[qkv(OFF_NQ), qkv(OFF_NK), qkv(OFF_NV), cache, cache,
                  pl.BlockSpec((None, None, 2 * NA_WIN_H - 2, GRID_W, 2 * GRID_W),
                               lambda b, h: (layer, h, 0, 0, 0))] + [j.in_spec for j in side],
        out_specs=[pl.BlockSpec((n, hd), lambda b, h: (b, h))] + [j.out_spec for j in side],
        out_shape=[jax.ShapeDtypeStruct((nreq * n, NA_WIDTH), bf16)] + [j.out_shape for j in side],
        scratch_shapes=[pltpu.VMEM((n, NA_WIN_H * GRID_W), f32), pltpu.VMEM((n, past), f32),
                        pltpu.VMEM((n, NA_WIN_H * GRID_W), bf16), pltpu.VMEM((n, past), bf16),
                        pltpu.VMEM((n, hd), f32), pltpu.VMEM((n, hd), f32)],
        compiler_params=_cparams(("arbitrary", "arbitrary")),
        name="na_latent",
    )(p, p, p, cache_k, cache_v, bias_tbl, *[j.array for j in side])


GLA_PAD = 32
GLA_UNROLL = 4


def _rope_tables(seq):
    t = np.arange(seq)
    half = GLA_DK // 2
    nf = half // 2
    inv = ROPE_BASE ** (-np.arange(nf, dtype=np.float64) / nf)
    cos, sin = [], []
    for pos in (t // GRID_W, t % GRID_W):
        ang = pos[:, None].astype(np.float64) * inv
        cos += [np.cos(ang), np.cos(ang)]
        sin += [-np.sin(ang), np.sin(ang)]
    return (jnp.asarray(np.concatenate(cos, axis=-1), f32), jnp.asarray(np.concatenate(sin, axis=-1), f32))


def _rope(x, cos, sin_signed):
    nf = GLA_DK // 4
    lane = lax.broadcasted_iota(jnp.int32, x.shape, 1)
    partner = jnp.where(lane % (2 * nf) < nf, pltpu.roll(x, GLA_DK - nf, 1), pltpu.roll(x, nf, 1))
    return x * cos + partner * sin_signed


def _log_sigmoid(x):
    return jnp.minimum(x, 0.0) - jnp.log1p(jnp.exp(-jnp.abs(x)))


def _gla_body(*refs, seq, rope, with_s0, with_sfin, n_carried, side):
    refs = list(refs)
    q_ref, k_ref, v_ref, r_ref, z_ref, wg_ref, bg_ref, ng_ref = refs[:8]
    refs = refs[8:]
    if rope:
        cos_ref, sin_ref = refs[:2]
        refs = refs[2:]
    if with_s0:
        s0_ref = refs[0]
        refs = refs[1:]
    refs = refs[n_carried:]
    side_src, refs = refs[:len(side)], refs[len(side):]
    o_ref = refs[0]
    refs = refs[1:]
    if with_sfin:
        sfin_ref = refs[0]
        refs = refs[1:]
    side_dst, refs = refs[:len(side)], refs[len(side):]
    step = pl.program_id(0) * pl.num_programs(1) + pl.program_id(1)
    for cast, src_ref, dst_ref in zip(side, side_src, side_dst):
        cast(step, src_ref, dst_ref)
    qi_scr, kn_scr, kd_scr, dec_scr, scan_scr, vb_scr, u_scr, sb_scr, o_scr, st_scr = refs

    ch = GLA_CHUNK
    nch = seq // ch
    dk, dv = GLA_DK, GLA_DV

    q = q_ref[...].astype(f32)
    k = k_ref[...].astype(f32)
    if rope:
        q = _rope(q, cos_ref[...], sin_ref[...])
        k = _rope(k, cos_ref[...], sin_ref[...])
    q = q * (dk ** -0.5)

    zb = z_ref[...].astype(bf16)
    pos = lax.broadcasted_iota(jnp.int32, (seq, 1), 0) % ch
    zpad = jnp.zeros((GLA_PAD, dk), f32)
    scan_scr[pl.ds(0, GLA_PAD), :] = zpad
    scan_scr[pl.ds(GLA_PAD + seq, GLA_PAD), :] = zpad
    for d in range(2):
        g = _log_sigmoid(_dot(zb, wg_ref[d]) + bg_ref[d]) / GLA_TAU
        b = g
        sh = 1
        while sh < ch:
            scan_scr[pl.ds(GLA_PAD, seq), :] = b
            if d == 0:
                b = b + jnp.where(pos >= sh, scan_scr[pl.ds(GLA_PAD - sh, seq), :], 0.0)
            else:
                b = b + jnp.where(pos < ch - sh, scan_scr[pl.ds(GLA_PAD + sh, seq), :], 0.0)
            sh *= 2
        b3 = b.reshape(nch, ch, dk)
        b_end = b3[:, ch - 1:ch, :] if d == 0 else b3[:, 0:1, :]
        lanes = pl.ds(d * dk, dk)
        qi_scr[:, lanes] = (q * jnp.exp(b)).astype(bf16)
        kn_scr[d] = (k * jnp.exp(-b)).astype(bf16)
        kd_scr[:, lanes] = (k.reshape(nch, ch, dk) * jnp.exp(b_end - b3)).reshape(seq, dk).astype(bf16)
        dec_scr[d] = jnp.exp(b_end)

    for d in range(2):
        if with_s0:
            st_scr[d] = s0_ref[d].T
        else:
            st_scr[d] = jnp.zeros((dv, dk), f32)

    ri = lax.broadcasted_iota(jnp.int32, (ch, ch), 0)
    ci = lax.broadcasted_iota(jnp.int32, (ch, ch), 1)

    vb_scr[...] = v_ref[...].astype(bf16)
    chunk_rows = lambda c: pl.ds(pl.multiple_of(c * ch, ch), ch)
    fwd, bwd = pl.ds(0, dk), pl.ds(dk, dk)

    def increments(c, carry):
        rows = chunk_rows(c)
        u_scr[c] = lax.dot_general(vb_scr[rows, :], kd_scr[rows, :], (((0,), (0,)), ((), ())),
                                   preferred_element_type=f32)
        return carry

    lax.fori_loop(0, nch, increments, 0, unroll=GLA_UNROLL)

    def states(i, carry):
        for d, c, lanes in ((0, i, fwd), (1, nch - 1 - i, bwd)):
            st = st_scr[d]
            sb_scr[c, :, lanes] = st.astype(bf16)
            st_scr[d] = st * dec_scr[d, c] + u_scr[c, :, lanes]
        return carry

    lax.fori_loop(0, nch, states, 0)

    if with_sfin:
        for d in range(2):
            sfin_ref[d] = st_scr[d].T

    def outputs(c, carry):
        rows = chunk_rows(c)
        qi = qi_scr[rows, :]
        pf = _dot_nt(qi[:, 0:dk], kn_scr[0, rows, :])
        pb = _dot_nt(qi[:, dk:2 * dk], kn_scr[1, rows, :])
        a = jnp.where(ci < ri, pf, jnp.where(ci > ri, pb, pf + pb))
        o_scr[rows, :] = _dot(a.astype(bf16), vb_scr[rows, :]) + _dot_nt(qi, sb_scr[c])
        return carry

    lax.fori_loop(0, nch, outputs, 0, unroll=GLA_UNROLL)

    o = o_scr[...]
    r = r_ref[...].astype(f32)
    o = o * lax.rsqrt(jnp.mean(o * o, axis=-1, keepdims=True) + EPS) * ng_ref[...]
    o_ref[...] = (o * (r * jax.nn.sigmoid(r))).astype(o_ref.dtype)


def _gla(p, wgate_p, b_gate, gla_norm, seq, row_block0, nreq, rope_tabs=None, state=None, layer=0,
         with_sfin=False, depth=1, carried=None, make_side=()):
    dk, dv = GLA_DK, GLA_DV
    nch = seq // GLA_CHUNK
    rope = rope_tabs is not None
    with_s0 = state is not None
    blk = lambda w, off: pl.BlockSpec((seq, w), lambda b, h: (row_block0 + b, off // w + h))
    in_specs = [blk(dk, OFF_GQ), blk(dk, OFF_GK), blk(dv, OFF_GV), blk(dv, OFF_GR),
                pl.BlockSpec((seq, LANE), lambda b, h: (row_block0 + b, OFF_GZ // LANE)),
                pl.BlockSpec((2, LANE, dk), lambda b, h: (0, 0, h)),
                pl.BlockSpec((2, 1, dk), lambda b, h: (0, 0, h)),
                pl.BlockSpec((1, dv), lambda b, h: (0, h))]
    args = [p, p, p, p, p, wgate_p, b_gate.reshape(2, 1, GLA_KW), gla_norm.reshape(1, GLA_VW)]
    if rope:
        in_specs += [pl.BlockSpec((seq, dk), lambda b, h: (0, 0))] * 2
        args += list(rope_tabs)
    if with_s0:
        in_specs.append(pl.BlockSpec((None, None, 2, None, dk, dv), lambda b, h: (b, layer, 0, h, 0, 0)))
        args.append(state)
    aliases = {}
    if carried is not None:
        aliases = {len(args): 1}
        in_specs.append(pl.BlockSpec(memory_space=pl.ANY))
        args.append(carried)
    side = [make(nreq * GLA_HEADS, lambda b, h: b * GLA_HEADS + h) for make in make_side]
    in_specs += [j.in_spec for j in side]
    args += [j.array for j in side]
    out_specs = [pl.BlockSpec((seq, dv), lambda b, h: (b, h))]
    out_shape = [jax.ShapeDtypeStruct((nreq * seq, GLA_VW), bf16)]
    if with_sfin:
        out_specs.append(pl.BlockSpec((None, None, 2, None, dk, dv), lambda b, h: (b, layer, 0, h, 0, 0)))
        out_shape.append(jax.ShapeDtypeStruct((nreq, depth, 2, GLA_HEADS, dk, dv), f32))
    out_specs += [j.out_spec for j in side]
    out_shape += [j.out_shape for j in side]
    scratch = [pltpu.VMEM((seq, 2 * dk), bf16), pltpu.VMEM((2, seq, dk), bf16), pltpu.VMEM((seq, 2 * dk), bf16),
               pltpu.VMEM((2, nch, 1, dk), f32), pltpu.VMEM((seq + 2 * GLA_PAD, dk), f32),
               pltpu.VMEM((seq, dv), bf16), pltpu.VMEM((nch, dv, 2 * dk), f32), pltpu.VMEM((nch, dv, 2 * dk), bf16),
               pltpu.VMEM((seq, dv), f32), pltpu.VMEM((2, dv, dk), f32)]
    return pl.pallas_call(
        functools.partial(_gla_body, seq=seq, rope=rope, with_s0=with_s0, with_sfin=with_sfin,
                          n_carried=len(aliases), side=tuple(j.body for j in side)),
        grid=(nreq, GLA_HEADS),
        in_specs=in_specs,
        out_specs=out_specs,
        out_shape=out_shape,
        input_output_aliases=aliases,
        scratch_shapes=scratch,
        compiler_params=_cparams(("arbitrary", "arbitrary")),
        name="gla",
    )(*args)


TM_MERGE = 512
MERGE_SUB = 2


def _merge_body(x_ref, mod_ref, post_ref, bp_ref, bn_ref, bg_ref, gl_ref, w_ref, o_ref, m_scr, mb_scr, r_scr):
    n = pl.program_id(1)
    m_ref = m_scr.at[pl.program_id(2)]
    mb_ref = mb_scr.at[pl.program_id(2)]

    for bi, br_ref in enumerate((bp_ref, bn_ref, bg_ref)):
        @pl.when(n == bi)
        def _():
            y = jax.nn.sigmoid(gl_ref[...].astype(f32)) * _dot(br_ref[...], w_ref[...])
            m_ref[...] = y if bi == 0 else m_ref[...] + y

    @pl.when(n == N_BRANCH)
    def _():
        mb_ref[...] = m_ref[...].astype(bf16)
        m_ref[...] = _dot(mb_ref[:, 0:BRANCH_W], w_ref[...])

    @pl.when(n == N_BRANCH + 1)
    def _():
        m_ref[...] += _dot(mb_ref[:, BRANCH_W:2 * BRANCH_W], w_ref[...])
        _norm_gate_residual(m_ref, x_ref, mod_ref, post_ref, o_ref, r_scr, 1.0)


def _merge(x, mod_l, post, y_pool, y_na, y_gla, p, w_stack, layer, cond):
    t, d = x.shape
    tm = TM_MERGE
    sub = MERGE_SUB
    nsteps = N_BRANCH + d // BRANCH_W
    tile = lambda i, s: i * sub + s

    def rows_at(first, last):
        def index(i, n, s):
            return jnp.where(n < first, jnp.maximum(tile(i, 0) - 1, 0),
                             jnp.where(n > last, tile(i, sub - 1), tile(i, s)))
        return index

    last = nsteps - 1
    br = lambda step: pl.BlockSpec((tm, BRANCH_W), lambda i, n, s: (rows_at(step, step)(i, n, s), 0))
    xo = pl.BlockSpec((tm, d), lambda i, n, s: (rows_at(last, last)(i, n, s), 0))
    return pl.pallas_call(
        _merge_body,
        grid=(t // (tm * sub), nsteps, sub),
        in_specs=[
            xo,
            pl.BlockSpec((None, 1, 3 * d), lambda i, n, s: (_cond_index(tile(i, s), tm, cond), 0, 1)),
            pl.BlockSpec((1, d), lambda i, n, s: (0, 0)),
            br(0), br(1), br(2),
            pl.BlockSpec((tm, d),
                         lambda i, n, s: (rows_at(0, N_BRANCH - 1)(i, n, s), jnp.minimum(n, N_BRANCH - 1))),
            pl.BlockSpec((None, None, BRANCH_W, d), lambda i, n, s: (layer, n, 0, 0)),
        ],
        out_specs=xo,
        out_shape=jax.ShapeDtypeStruct((t, d), f32),
        scratch_shapes=[pltpu.VMEM((sub, tm, d), f32), pltpu.VMEM((sub, tm, d), bf16), pltpu.VMEM((tm, LANE), f32)],
        compiler_params=_cparams(("arbitrary", "arbitrary", "arbitrary")),
        name="merge",
    )(x, mod_l, post, y_pool, y_na, y_gla, p, w_stack)


_IN_SPLITS = (POOL_WIDTH, NA_WIDTH, NA_WIDTH, NA_WIDTH, GLA_KW, GLA_KW, GLA_VW, 2 * GLA_RANK, GLA_VW, GATE_W)
_IN_OFFS = tuple(int(v) for v in np.cumsum((0,) + _IN_SPLITS))
_IN_RUNS = ((_IN_OFFS[9], _IN_OFFS[10]), (_IN_OFFS[0], _IN_OFFS[7]), (_IN_OFFS[8], _IN_OFFS[9]),
            (_IN_OFFS[7], _IN_OFFS[8]))


CT_IN = 512
_IN_TILE_STARTS = []
for _a, _b in _IN_RUNS:
    _IN_TILE_STARTS += [_a + CT_IN * _t for _t in range(-(-(_b - _a) // CT_IN))]
assert len(_IN_TILE_STARTS) * CT_IN == IN_COLS_P and all(v % 8 == 0 for v in _IN_TILE_STARTS)
_IN_LAST_VALID = (_IN_RUNS[-1][1] - _IN_RUNS[-1][0]) % CT_IN or CT_IN
assert all((b - a) % CT_IN == 0 for a, b in _IN_RUNS[:-1])


def _w_in_tile_start(j):
    out = jnp.int32(_IN_TILE_STARTS[0]) + CT_IN * j
    for t in range(1, len(_IN_TILE_STARTS)):
        if _IN_TILE_STARTS[t] != _IN_TILE_STARTS[t - 1] + CT_IN:
            out = jnp.where(j >= t, _IN_TILE_STARTS[t] + CT_IN * (j - t), out)
    return out


N_IN_TILES = IN_COLS_P // CT_IN


def _cast_w_in_body(step, w_ref, o_ref):
    @pl.when(step < N_IN_TILES)
    def _():
        col = lax.broadcasted_iota(jnp.int32, (1, CT_IN), 1)
        valid = jnp.where(step == N_IN_TILES - 1, _IN_LAST_VALID, CT_IN)
        o_ref[...] = jnp.where(col < valid, w_ref[...].T, 0.0).astype(bf16)


def _w_in_cast(wt, n_cols, layer, nsteps, step_of):
    assert nsteps >= N_IN_TILES
    d = wt.shape[1]
    tile = lambda *g: jnp.minimum(step_of(*g), N_IN_TILES - 1)
    return _SideCast(
        wt,
        pl.BlockSpec((pl.Element(CT_IN), pl.Element(d)),
                     lambda *g: (pl.multiple_of(layer * n_cols + _w_in_tile_start(tile(*g)), 8), 0)),
        pl.BlockSpec((d, CT_IN), lambda *g: (0, tile(*g))),
        jax.ShapeDtypeStruct((d, IN_COLS_P), bf16),
        _cast_w_in_body)


class _SideCast(NamedTuple):
    array: Any
    in_spec: pl.BlockSpec
    out_spec: pl.BlockSpec
    out_shape: jax.ShapeDtypeStruct
    body: Callable


def _cast_ffn_in_body(step, w_ref, o_ref):
    del step
    pad = jnp.zeros((w_ref.shape[0], D_FF_P - D_FF), bf16)
    for half in range(2):
        o_ref[half, :, 0:D_FF] = w_ref[:, half * D_FF:(half + 1) * D_FF].astype(bf16)
        o_ref[half, :, D_FF:D_FF_P] = pad


def _ffn_in_cast(w, layer, slot, nsteps, step_of):
    d = w.shape[2]
    rows = d // nsteps
    assert rows * nsteps == d and rows % 16 == 0
    return _SideCast(
        w,
        pl.BlockSpec((None, None, rows, 2 * D_FF), lambda *g: (layer, slot, step_of(*g), 0)),
        pl.BlockSpec((2, rows, D_FF_P), lambda *g: (0, step_of(*g), 0)),
        jax.ShapeDtypeStruct((2, d, D_FF_P), bf16),
        _cast_ffn_in_body)


def _cast_ffn_out_body(step, w_ref, o_ref):
    rows = w_ref.shape[0]
    row = step * rows + lax.broadcasted_iota(jnp.int32, (rows, 1), 0)
    o_ref[...] = jnp.where(row < D_FF, w_ref[...], 0.0).astype(bf16)


def _ffn_out_cast(w, layer, slot, nsteps, step_of):
    d = w.shape[3]
    rows = D_FF_P // nsteps
    assert rows * nsteps == D_FF_P and rows % 16 == 0 and rows * (nsteps - 1) < D_FF
    return _SideCast(
        w,
        pl.BlockSpec((None, None, rows, d), lambda *g: (layer, slot, step_of(*g), 0)),
        pl.BlockSpec((rows, d), lambda *g: (step_of(*g), 0)),
        jax.ShapeDtypeStruct((D_FF_P, d), bf16),
        _cast_ffn_out_body)


def _run_cast_body(w_ref, o_ref, *, cast):
    cast(pl.program_id(0), w_ref, o_ref)


def _run_cast(make_job, nsteps, name):
    job = make_job(nsteps, lambda r: r)
    return pl.pallas_call(
        functools.partial(_run_cast_body, cast=job.body),
        grid=(nsteps,),
        in_specs=[job.in_spec],
        out_specs=job.out_spec,
        out_shape=job.out_shape,
        compiler_params=_cparams(("parallel",)),
        name=name,
    )(job.array)


def _cast_w_stack_body(wb_ref, wo_ref, o_ref):
    is_branch = pl.program_id(1) < N_BRANCH

    @pl.when(is_branch)
    def _():
        o_ref[...] = wb_ref[...].astype(bf16)

    @pl.when(jnp.logical_not(is_branch))
    def _():
        o_ref[...] = wo_ref[...].astype(bf16)


def _cast_w_stack(w_branch, w_out):
    depth, nb, bw, d = w_branch.shape
    kparts = d // bw
    wo = w_out.reshape(depth, kparts, bw, d)
    blk = lambda index: pl.BlockSpec((None, None, bw, d), index)
    return pl.pallas_call(
        _cast_w_stack_body,
        grid=(depth, nb + kparts),
        in_specs=[blk(lambda l, j: (l, jnp.minimum(j, nb - 1), 0, 0)),
                  blk(lambda l, j: (l, jnp.maximum(j - nb, 0), 0, 0))],
        out_specs=blk(lambda l, j: (l, j, 0, 0)),
        out_shape=jax.ShapeDtypeStruct((depth, nb + kparts, bw, d), bf16),
        compiler_params=_cparams(("parallel", "arbitrary")),
        name="cast_w_stack",
    )(w_branch, wo)


def _prep_gate(w_gate):
    out = jnp.zeros((2, LANE, GLA_KW), f32)
    for d in range(2):
        out = out.at[d, d * GLA_RANK:(d + 1) * GLA_RANK].set(w_gate[d])
    return out.astype(bf16)


def kernel(x_prompt, x_sample, c, cache_na_k, cache_na_v, state_gla, c_ctx, w_mod, b_mod, norm_pre, norm_post,
           w_ffn_in, w_ffn_out, w_in, pool_w, pool_scale, na_rpb, gla_w_gate, gla_b_gate, gla_norm, w_branch,
           w_out):
    nb, seq, d = x_prompt.shape
    ndec, dseq, _ = x_sample.shape
    depth = w_mod.shape[0]
    n_ctx = nb * seq
    n_lat = ndec * dseq
    assert (seq, dseq, d) == (SEQ, DEC_SEQ, D_MODEL)
    assert n_ctx % (TM_MERGE * MERGE_SUB) == 0 and n_ctx % TM_FFN == 0 and dseq % TM_FFN == 0

    xs = [x_prompt.reshape(n_ctx, d), x_sample.reshape(n_lat, d)]
    conds = [(0, n_ctx), (1, dseq)]
    ncond = -(-(1 + ndec) // 8) * 8
    c_all = jnp.concatenate([c_ctx[None], c, jnp.zeros((ncond - 1 - ndec, d), f32)], axis=0)
    mod = _modulation(c_all, w_mod, b_mod)
    rope_tabs = _rope_tables(dseq)
    bias_tbl = _na_bias_table(na_rpb)

    in_cast = lambda l, s: functools.partial(_ffn_in_cast, w_ffn_in, l, s)
    out_cast = lambda l, s: functools.partial(_ffn_out_cast, w_ffn_out, l, s)
    ffn_w_in = {(0, 0): _run_cast(in_cast(0, 0), d // 256, "cast_ffn_in")}
    ffn_w_out = {(0, 0): _run_cast(out_cast(0, 0), D_FF_P // TF, "cast_ffn_out")}
    wt_in = jnp.swapaxes(w_in, 1, 2).reshape(depth * w_in.shape[2], d)
    w_in_cast = lambda l: functools.partial(_w_in_cast, wt_in, w_in.shape[2], l)
    w_in_p = {0: _run_cast(w_in_cast(0), N_IN_TILES, "cast_w_in")}
    w_stack_all = _cast_w_stack(w_branch, w_out)

    caches = None
    new_s = None
    for l in range(depth):
        mod_l = mod[l].reshape(ncond, 1, N_MOD * d)
        pre = norm_pre[l].reshape(3, 1, d)
        post = norm_post[l].reshape(3, 1, d)
        wgate_p = _prep_gate(gla_w_gate[l])
        pw = pool_w[l].astype(bf16)
        psc = pool_scale[l].reshape(1, POOL_WIDTH)
        later = [(l, 1)] + ([(l + 1, 0)] if l + 1 < depth else [])

        xs = [_ffn(x, mod_l, pre[0], post[0], ffn_w_in[l, 0], ffn_w_out[l, 0], 0, cond) for x, cond in zip(xs, conds)]
        p_ctx, p_lat = [_inproj(x, mod_l, pre[1], w_in_p[l], cond, dt) for x, cond, dt in zip(xs, conds, (f32, bf16))]

        y_pool = [_pool(p_ctx, pw, psc, seq, 0, nb), _pool(p_lat, pw, psc, dseq, 0, ndec)]
        na_ctx, new_k, new_v = _ctx_attn(p_ctx, nb, l, depth, caches)
        caches = (new_k, new_v)
        na_lat, *cast = _na_latent(p_lat, cache_na_k, cache_na_v, bias_tbl, l, 0, ndec,
                                   make_side=[in_cast(*ls) for ls in later])
        ffn_w_in.update(zip(later, cast))
        y_na = [na_ctx, na_lat]
        nxt = [l + 1] if l + 1 < depth else []
        g_ctx, new_s, *cast = _gla(p_ctx, wgate_p, gla_b_gate[l], gla_norm[l], seq, 0, nb, layer=l, with_sfin=True,
                                   depth=depth, carried=new_s, make_side=[w_in_cast(j) for j in nxt])
        w_in_p.update(zip(nxt, cast))
        g_lat, *cast = _gla(p_lat, wgate_p, gla_b_gate[l], gla_norm[l], dseq, 0, ndec, rope_tabs=rope_tabs,
                            state=state_gla, layer=l, make_side=[out_cast(*ls) for ls in later])
        ffn_w_out.update(zip(later, cast))
        y_gla = [g_ctx, g_lat]

        xs = [_merge(x, mod_l, post[1], yp, yn, yg, p, w_stack_all, l, cond)
              for x, yp, yn, yg, p, cond in zip(xs, y_pool, y_na, y_gla, (p_ctx, p_lat), conds)]
        xs = [_ffn(x, mod_l, pre[2], post[2], ffn_w_in[l, 1], ffn_w_out[l, 1], 2, cond) for x, cond in zip(xs, conds)]

    return (xs[0].reshape(nb, seq, d), xs[1].reshape(ndec, dseq, d), caches[0], caches[1], new_s)
```

```python
import functools
from typing import Any, Callable, NamedTuple

import numpy as np
import jax
import jax.numpy as jnp
from jax import lax
from jax.experimental import pallas as pl
from jax.experimental.pallas import tpu as pltpu

f32 = jnp.float32
bf16 = jnp.bfloat16

D_MODEL = 2048
SEQ = 256
DEC_SEQ = 2048
GRID_W = 64
N_MOD = 9
D_FF = 5504
FFN_RES = 0.5
EPS = 1e-6
NEG_INF = -1e30

POOL_GROUPS = 4
POOL_WINDOWS = (2, 4, 8, 16)
POOL_WIDTH = 1024
POOL_GC = POOL_WIDTH // POOL_GROUPS

NA_HEADS = 8
NA_HEAD_DIM = 128
NA_WIDTH = NA_HEADS * NA_HEAD_DIM
NA_WIN_H = 8
NA_WIN_W = 16

GLA_HEADS = 4
GLA_DK = 128
GLA_DV = 256
GLA_KW = GLA_HEADS * GLA_DK
GLA_VW = GLA_HEADS * GLA_DV
GLA_RANK = 16
GLA_TAU = 16.0
GLA_CHUNK = 64
ROPE_BASE = 10000.0

BRANCH_W = 1024
N_BRANCH = 3
GATE_W = N_BRANCH * D_MODEL

LANE = 128
VMEM_LIMIT = 56 * 1024 * 1024

OFF_GL = 0
OFF_POOL = OFF_GL + GATE_W
OFF_NQ = OFF_POOL + POOL_WIDTH
OFF_NK = OFF_NQ + NA_WIDTH
OFF_NV = OFF_NK + NA_WIDTH
OFF_GQ = OFF_NV + NA_WIDTH
OFF_GK = OFF_GQ + GLA_KW
OFF_GV = OFF_GK + GLA_KW
OFF_GR = OFF_GV + GLA_VW
OFF_GZ = OFF_GR + GLA_VW
TN_IN = 2304
IN_COLS_P = -(-(OFF_GZ + LANE) // TN_IN) * TN_IN

TM = 512
TF = 512
D_FF_P = -(-D_FF // TF) * TF


def _cparams(sem):
    return pltpu.CompilerParams(dimension_semantics=sem, vmem_limit_bytes=VMEM_LIMIT)


def _cond_index(i, tm, cond):
    return cond[0] + (i * tm) // cond[1]


ROW_CHUNK = 16


def _row_sweep(nrows, fn, unroll=4):
    def trip(i, carry):
        fn(pl.ds(pl.multiple_of(i * ROW_CHUNK, ROW_CHUNK), ROW_CHUNK))
        return carry

    lax.fori_loop(0, nrows // ROW_CHUNK, trip, 0, unroll=unroll)


def _row_rsqrt(x_ref, r_scr):
    n = x_ref.shape[1]

    def fn(rows):
        x = x_ref[rows, :]
        ss = jnp.sum(_lane_fold(x * x, jnp.add), axis=-1, keepdims=True)
        r_scr[rows, :] = jnp.broadcast_to(lax.rsqrt(ss * (1.0 / n) + EPS), (ROW_CHUNK, LANE))

    _row_sweep(x_ref.shape[0], fn, unroll=16)


def _lanes(r, n):
    return jnp.concatenate([r] * (n // LANE), axis=1)


def _norm_modulate(x_ref, mod_ref, g_ref, h_ref, r_scr):
    d = D_MODEL
    shift = mod_ref[:, 0:d]
    w = g_ref[...] * (1.0 + mod_ref[:, d:2 * d])
    _row_rsqrt(x_ref, r_scr)

    def fn(rows):
        h_ref[rows, :] = (x_ref[rows, :] * _lanes(r_scr[rows, :], d) * w + shift).astype(h_ref.dtype)

    _row_sweep(x_ref.shape[0], fn)


def _norm_gate_residual(y_ref, x_ref, mod_ref, g_ref, o_ref, r_scr, res_weight):
    d = D_MODEL
    w = (res_weight * mod_ref[:, 2 * d:3 * d]) * g_ref[...]
    _row_rsqrt(y_ref, r_scr)

    def fn(rows):
        o_ref[rows, :] = x_ref[rows, :] + y_ref[rows, :] * _lanes(r_scr[rows, :], d) * w

    _row_sweep(x_ref.shape[0], fn)


def _dot(a, b):
    return jnp.dot(a, b, preferred_element_type=f32)


def _dot_nt(a, b):
    return lax.dot_general(a, b, (((1,), (1,)), ((), ())), preferred_element_type=f32)


def _mod_body(c_ref, w_ref, b_ref, o_ref):
    c = c_ref[...]
    s = c * jax.nn.sigmoid(c)
    o_ref[...] = _dot(s.astype(bf16), w_ref[...].astype(bf16)) + b_ref[...]


def _modulation(c_all, w_mod, b_mod):
    depth, d, n = w_mod.shape
    nc = c_all.shape[0]
    tn = 1024
    return pl.pallas_call(
        _mod_body,
        grid=(depth, n // tn),
        in_specs=[
            pl.BlockSpec((nc, d), lambda l, j: (0, 0)),
            pl.BlockSpec((None, d, tn), lambda l, j: (l, 0, j)),
            pl.BlockSpec((None, 1, tn), lambda l, j: (l, 0, j)),
        ],
        out_specs=pl.BlockSpec((None, nc, tn), lambda l, j: (l, 0, j)),
        out_shape=jax.ShapeDtypeStruct((depth, nc, n), f32),
        compiler_params=_cparams(("parallel", "parallel")),
        name="modulation",
    )(c_all, w_mod, b_mod.reshape(depth, 1, n))


TM_FFN = 1024


def _ffn_body(x_ref, mod_ref, pre_ref, post_ref, wg_ref, wu_ref, wo_ref, o_ref, h_scr, r_scr):
    f = pl.program_id(1)

    @pl.when(f == 0)
    def _():
        _norm_modulate(x_ref, mod_ref, pre_ref, h_scr, r_scr)
        o_ref[...] = jnp.zeros_like(o_ref)

    h = h_scr[...]
    gt = _dot(h, wg_ref[...])
    up = _dot(h, wu_ref[...])
    a = gt * jax.nn.sigmoid(gt) * up
    o_ref[...] += _dot(a.astype(bf16), wo_ref[...])

    @pl.when(f == pl.num_programs(1) - 1)
    def _():
        _norm_gate_residual(o_ref, x_ref, mod_ref, post_ref, o_ref, r_scr, FFN_RES)


def _ffn(x, mod_l, pre, post, w_in3, w_out2, sub, cond):
    t, d = x.shape
    fp = w_out2.shape[0]
    w_half = lambda half: pl.BlockSpec((None, d, TF), lambda i, f: (half, 0, f))
    return pl.pallas_call(
        _ffn_body,
        grid=(t // TM_FFN, fp // TF),
        in_specs=[
            pl.BlockSpec((TM_FFN, d), lambda i, f: (i, 0)),
            pl.BlockSpec((None, 1, 3 * d), lambda i, f: (_cond_index(i, TM_FFN, cond), 0, sub)),
            pl.BlockSpec((1, d), lambda i, f: (0, 0)),
            pl.BlockSpec((1, d), lambda i, f: (0, 0)),
            w_half(0),
            w_half(1),
            pl.BlockSpec((TF, d), lambda i, f: (f, 0)),
        ],
        out_specs=pl.BlockSpec((TM_FFN, d), lambda i, f: (i, 0)),
        out_shape=jax.ShapeDtypeStruct((t, d), f32),
        scratch_shapes=[pltpu.VMEM((TM_FFN, d), bf16), pltpu.VMEM((TM_FFN, LANE), f32)],
        compiler_params=_cparams(("parallel", "arbitrary")),
        name="ffn",
    )(x, mod_l, pre, post, w_in3, w_in3, w_out2)


def _inproj_body(x_ref, mod_ref, pre_ref, w_ref, o_ref, h_scr, r_scr):
    @pl.when(pl.program_id(1) == 0)
    def _():
        _norm_modulate(x_ref, mod_ref, pre_ref, h_scr, r_scr)

    o_ref[...] = _dot(h_scr[...], w_ref[...]).astype(o_ref.dtype)


def _inproj(x, mod_l, pre, w_in_p, cond, out_dtype):
    t, d = x.shape
    tm = TM * (4 // jnp.dtype(out_dtype).itemsize)
    n = w_in_p.shape[1]
    return pl.pallas_call(
        _inproj_body,
        grid=(t // tm, n // TN_IN),
        in_specs=[
            pl.BlockSpec((tm, d), lambda i, j: (i, 0)),
            pl.BlockSpec((None, 1, 3 * d), lambda i, j: (_cond_index(i, tm, cond), 0, 1)),
            pl.BlockSpec((1, d), lambda i, j: (0, 0)),
            pl.BlockSpec((d, TN_IN), lambda i, j: (0, j)),
        ],
        out_specs=pl.BlockSpec((tm, TN_IN), lambda i, j: (i, j)),
        out_shape=jax.ShapeDtypeStruct((t, n), out_dtype),
        scratch_shapes=[pltpu.VMEM((tm, d), bf16), pltpu.VMEM((tm, LANE), f32)],
        compiler_params=_cparams(("parallel", "arbitrary")),
        name="inproj",
    )(x, mod_l, pre, w_in_p)


POOL_PAD = 8


def _pool_body(u_ref, w_ref, sc_ref, o_ref, pad_scr, lvl_scr, *, seq):
    gc = POOL_GC
    pad = POOL_PAD
    n_lvl = seq + pad
    zeros = jnp.zeros((pad, POOL_WIDTH), f32)
    pad_scr[pl.ds(0, pad), :] = zeros
    pad_scr[pl.ds(pad + seq, pad), :] = zeros
    pad_scr[pl.ds(pad, seq), :] = u_ref[...].astype(f32)
    lvl_scr[:, pl.ds(n_lvl, pad), :] = jnp.zeros((2, pad, gc), f32)
    t = lax.broadcasted_iota(jnp.int32, (seq, 1), 0)
    for gi, win in enumerate(POOL_WINDOWS):
        cols = pl.ds(gi * gc, gc)
        read = lambda off, n: pad_scr[pl.ds(off, n), cols]
        k, slot = 1, 0
        while 2 * k < win:
            lvl_scr[slot, pl.ds(0, n_lvl), :] = read(0, n_lvl) + read(k, n_lvl)
            read = functools.partial(lambda s_, off, n: lvl_scr[s_, pl.ds(off, n), :], slot)
            k, slot = 2 * k, 1 - slot
        lo = jnp.maximum(t - win // 2, 0)
        hi = jnp.minimum(t + win - 1 - win // 2, seq - 1)
        inv_cnt = 1.0 / (hi - lo + 1).astype(f32)
        acc = read(pad - win // 2, seq) + read(pad, seq)
        pooled = acc * inv_cnt - pad_scr[pl.ds(pad, seq), cols]
        y = _dot(pooled.astype(bf16), w_ref[gi])
        o_ref[:, cols] = (y * sc_ref[:, cols]).astype(o_ref.dtype)


def _pool(p, pool_w, pool_scale, seq, row_block0, nseq):
    cb = OFF_POOL // POOL_WIDTH
    return pl.pallas_call(
        functools.partial(_pool_body, seq=seq),
        grid=(nseq,),
        in_specs=[
            pl.BlockSpec((seq, POOL_WIDTH), lambda s: (row_block0 + s, cb)),
            pl.BlockSpec((POOL_GROUPS, POOL_GC, POOL_GC), lambda s: (0, 0, 0)),
            pl.BlockSpec((1, POOL_WIDTH), lambda s: (0, 0)),
        ],
        out_specs=pl.BlockSpec((seq, POOL_WIDTH), lambda s: (s, 0)),
        out_shape=jax.ShapeDtypeStruct((nseq * seq, POOL_WIDTH), bf16),
        scratch_shapes=[pltpu.VMEM((seq + 2 * POOL_PAD, POOL_WIDTH), f32),
                        pltpu.VMEM((2, seq + 2 * POOL_PAD, POOL_GC), f32)],
        compiler_params=_cparams(("parallel",)),
        name="pool",
    )(p, pool_w, pool_scale)


def _lane_fold(x, op):
    parts = [x[:, i * LANE:(i + 1) * LANE] for i in range(x.shape[1] // LANE)]
    while len(parts) > 1:
        parts = [op(parts[i], parts[i + 1]) for i in range(0, len(parts) - 1, 2)] + parts[len(parts) & ~1:]
    return parts[0]


def _softmax_rows(s):
    m = jnp.max(_lane_fold(s, jnp.maximum), axis=-1, keepdims=True)
    e = jnp.exp(s - m)
    return e / jnp.sum(_lane_fold(e, jnp.add), axis=-1, keepdims=True)


def _ctx_attn_body(q_ref, k_ref, v_ref, *refs):
    o_ref, nk_ref, nv_ref = refs[-3:]
    hd = NA_HEAD_DIM
    for h in range(NA_HEADS):
        cols = pl.ds(h * hd, hd)
        kf = k_ref[:, cols]
        vf = v_ref[:, cols]
        nk_ref[h] = kf
        nv_ref[h] = vf
        p = _softmax_rows(_dot_nt(q_ref[:, cols].astype(bf16), kf.astype(bf16)) * (hd ** -0.5))
        o_ref[:, cols] = _dot(p.astype(bf16), vf.astype(bf16)).astype(o_ref.dtype)


def _ctx_attn(p, nseq, layer, depth, caches=None):
    spec = lambda off: pl.BlockSpec((SEQ, NA_WIDTH), lambda b: (b, off // NA_WIDTH))
    cache_spec = pl.BlockSpec((None, None, NA_HEADS, SEQ, NA_HEAD_DIM), lambda b: (b, layer, 0, 0, 0))
    cache_shape = jax.ShapeDtypeStruct((nseq, depth, NA_HEADS, SEQ, NA_HEAD_DIM), f32)
    in_specs = [spec(OFF_NQ), spec(OFF_NK), spec(OFF_NV)]
    args = [p, p, p]
    aliases = {}
    if caches is not None:
        in_specs += [pl.BlockSpec(memory_space=pl.ANY)] * 2
        args += list(caches)
        aliases = {3: 1, 4: 2}
    return pl.pallas_call(
        _ctx_attn_body,
        grid=(nseq,),
        in_specs=in_specs,
        out_specs=[pl.BlockSpec((SEQ, NA_WIDTH), lambda b: (b, 0)), cache_spec, cache_spec],
        out_shape=[jax.ShapeDtypeStruct((nseq * SEQ, NA_WIDTH), bf16), cache_shape, cache_shape],
        input_output_aliases=aliases,
        compiler_params=_cparams(("parallel",)),
        name="ctx_attn",
    )(*args)


LOG2E = float(np.log2(np.e))


def _na_bias_table(rpb):
    qc = np.arange(GRID_W)[:, None]
    kc = np.arange(GRID_W)[None, :]
    cs = np.clip(qc - NA_WIN_W // 2, 0, GRID_W - NA_WIN_W)
    ok = (kc >= cs) & (kc < cs + NA_WIN_W)
    cidx = np.clip(kc - qc + NA_WIN_W - 1, 0, 2 * NA_WIN_W - 2)
    onehot = jnp.asarray((cidx[None] == np.arange(2 * NA_WIN_W - 1)[:, None, None]) & ok[None], f32)
    toep = jnp.einsum('...rc,cqk->...rqk', rpb.astype(f32), onehot, precision=lax.Precision.HIGHEST)
    toep = jnp.where(ok, toep * LOG2E, NEG_INF)
    return jnp.concatenate([toep[..., :-1, :, :], toep[..., 1:, :, :]], axis=-1)


NA_UNROLL = 8
NA_CTX_ROWS = 256


def _na_body(*refs, rows, side):
    ns = len(side)
    qb_scr, kb_scr, vb_scr, ck_ref, cv_ref, bias_ref = refs[:6]
    o_ref = refs[6 + ns]
    sl_scr, sc_scr, el_scr, ec_scr, den_scr, oc_scr = refs[7 + 2 * ns:]
    step = pl.program_id(0) * pl.num_programs(1) + pl.program_id(1)
    for cast, src_ref, dst_ref in zip(side, refs[6:6 + ns], refs[7 + ns:7 + 2 * ns]):
        cast(step, src_ref, dst_ref)

    hd = NA_HEAD_DIM
    scale = hd ** -0.5 * LOG2E
    kh = min(NA_WIN_H, rows)
    nloc = kh * GRID_W
    n = rows * GRID_W
    assert qb_scr.dtype == bf16
    ck = ck_ref[...].astype(bf16)
    cv = cv_ref[...].astype(bf16)

    def row_slices(r):
        rs = jnp.clip(r - kh // 2, 0, rows - kh)
        q_rows = pl.ds(pl.multiple_of(r * GRID_W, GRID_W), GRID_W)
        k_rows = pl.ds(pl.multiple_of(rs * GRID_W, GRID_W), nloc)
        return rs, q_rows, k_rows

    def ctx_scores(i, carry):
        blk = pl.ds(pl.multiple_of(i * NA_CTX_ROWS, NA_CTX_ROWS), NA_CTX_ROWS)
        sc_scr[blk, :] = _dot_nt(qb_scr[blk, :], ck) * scale
        return carry

    lax.fori_loop(0, n // NA_CTX_ROWS, ctx_scores, 0, unroll=2)

    def loc_scores(r, carry):
        rs, q_rows, k_rows = row_slices(r)
        first = rs - r + NA_WIN_H - 1
        bias = jnp.concatenate([bias_ref[first + 2 * e] for e in range(kh // 2)], axis=1)
        sl_scr[q_rows, :] = _dot_nt(qb_scr[q_rows, :], kb_scr[k_rows, :]) * scale + bias
        return carry

    lax.fori_loop(0, rows, loc_scores, 0, unroll=NA_UNROLL)

    def numerators(r, carry):
        q_rows = pl.ds(pl.multiple_of(r * GRID_W, GRID_W), GRID_W)
        s_loc = sl_scr[q_rows, :]
        s_ctx = sc_scr[q_rows, :]
        m = jnp.max(jnp.maximum(_lane_fold(s_loc, jnp.maximum), _lane_fold(s_ctx, jnp.maximum)),
                    axis=-1, keepdims=True)
        e_loc = jnp.exp2(s_loc - m)
        e_ctx = jnp.exp2(s_ctx - m)
        den = jnp.sum(_lane_fold(e_loc, jnp.add) + _lane_fold(e_ctx, jnp.add), axis=-1, keepdims=True)
        el_scr[q_rows, :] = e_loc.astype(bf16)
        ec_scr[q_rows, :] = e_ctx.astype(bf16)
        den_scr[q_rows, :] = jnp.broadcast_to(den, (GRID_W, hd))
        return carry

    lax.fori_loop(0, rows, numerators, 0, unroll=NA_UNROLL)

    def ctx_values(i, carry):
        blk = pl.ds(pl.multiple_of(i * NA_CTX_ROWS, NA_CTX_ROWS), NA_CTX_ROWS)
        oc_scr[blk, :] = _dot(ec_scr[blk, :], cv)
        return carry

    lax.fori_loop(0, n // NA_CTX_ROWS, ctx_values, 0, unroll=2)

    def loc_values(r, carry):
        _, q_rows, k_rows = row_slices(r)
        o = _dot(el_scr[q_rows, :], vb_scr[k_rows, :]) + oc_scr[q_rows, :]
        o_ref[q_rows, :] = (o / den_scr[q_rows, :]).astype(o_ref.dtype)
        return carry

    lax.fori_loop(0, rows, loc_values, 0, unroll=NA_UNROLL)


def _na_latent(p, cache_k, cache_v, bias_tbl, layer, row_block0, nreq, make_side=()):
    n = DEC_SEQ
    hd = NA_HEAD_DIM
    past = cache_k.shape[3]
    rows = n // GRID_W
    side = [make(nreq * NA_HEADS, lambda b, h: b * NA_HEADS + h) for make in make_side]
    qkv = lambda off: pl.BlockSpec((n, hd), lambda b, h: (row_block0 + b, off // hd + h))
    cache = pl.BlockSpec((None, None, None, past, hd), lambda b, h: (b, layer, h, 0, 0))
    return pl.pallas_call(
        functools.partial(_na_body, rows=rows, side=tuple(j.body for j in side)),
        grid=(nreq, NA_HEADS),
        in_specs=[qkv(OFF_NQ), qkv(OFF_NK), qkv(OFF_NV), cache, cache,
                  pl.BlockSpec((None, None, 2 * NA_WIN_H - 2, GRID_W, 2 * GRID_W),
                               lambda b, h: (layer, h, 0, 0, 0))] + [j.in_spec for j in side],
        out_specs=[pl.BlockSpec((n, hd), lambda b, h: (b, h))] + [j.out_spec for j in side],
        out_shape=[jax.ShapeDtypeStruct((nreq * n, NA_WIDTH), bf16)] + [j.out_shape for j in side],
        scratch_shapes=[pltpu.VMEM((n, NA_WIN_H * GRID_W), f32), pltpu.VMEM((n, past), f32),
                        pltpu.VMEM((n, NA_WIN_H * GRID_W), bf16), pltpu.VMEM((n, past), bf16),
                        pltpu.VMEM((n, hd), f32), pltpu.VMEM((n, hd), f32)],
        compiler_params=_cparams(("arbitrary", "arbitrary")),
        name="na_latent",
    )(p, p, p, cache_k, cache_v, bias_tbl, *[j.array for j in side])


GLA_PAD = 32
GLA_UNROLL = 4


def _rope_tables(seq):
    t = np.arange(seq)
    half = GLA_DK // 2
    nf = half // 2
    inv = ROPE_BASE ** (-np.arange(nf, dtype=np.float64) / nf)
    cos, sin = [], []
    for pos in (t // GRID_W, t % GRID_W):
        ang = pos[:, None].astype(np.float64) * inv
        cos += [np.cos(ang), np.cos(ang)]
        sin += [-np.sin(ang), np.sin(ang)]
    return (jnp.asarray(np.concatenate(cos, axis=-1), f32), jnp.asarray(np.concatenate(sin, axis=-1), f32))


def _rope(x, cos, sin_signed):
    nf = GLA_DK // 4
    lane = lax.broadcasted_iota(jnp.int32, x.shape, 1)
    partner = jnp.where(lane % (2 * nf) < nf, pltpu.roll(x, GLA_DK - nf, 1), pltpu.roll(x, nf, 1))
    return x * cos + partner * sin_signed


def _log_sigmoid(x):
    return jnp.minimum(x, 0.0) - jnp.log1p(jnp.exp(-jnp.abs(x)))


def _gla_body(*refs, seq, rope, with_s0, with_sfin, n_carried, side):
    refs = list(refs)
    q_ref, k_ref, v_ref, r_ref, z_ref, wg_ref, bg_ref, ng_ref = refs[:8]
    refs = refs[8:]
    if rope:
        cos_ref, sin_ref = refs[:2]
        refs = refs[2:]
    if with_s0:
        s0_ref = refs[0]
        refs = refs[1:]
    refs = refs[n_carried:]
    side_src, refs = refs[:len(side)], refs[len(side):]
    o_ref = refs[0]
    refs = refs[1:]
    if with_sfin:
        sfin_ref = refs[0]
        refs = refs[1:]
    side_dst, refs = refs[:len(side)], refs[len(side):]
    step = pl.program_id(0) * pl.num_programs(1) + pl.program_id(1)
    for cast, src_ref, dst_ref in zip(side, side_src, side_dst):
        cast(step, src_ref, dst_ref)
    qi_scr, kn_scr, kd_scr, dec_scr, scan_scr, vb_scr, u_scr, sb_scr, o_scr, st_scr = refs

    ch = GLA_CHUNK
    nch = seq // ch
    dk, dv = GLA_DK, GLA_DV

    q = q_ref[...].astype(f32)
    k = k_ref[...].astype(f32)
    if rope:
        q = _rope(q, cos_ref[...], sin_ref[...])
        k = _rope(k, cos_ref[...], sin_ref[...])
    q = q * (dk ** -0.5)

    zb = z_ref[...].astype(bf16)
    pos = lax.broadcasted_iota(jnp.int32, (seq, 1), 0) % ch
    zpad = jnp.zeros((GLA_PAD, dk), f32)
    scan_scr[pl.ds(0, GLA_PAD), :] = zpad
    scan_scr[pl.ds(GLA_PAD + seq, GLA_PAD), :] = zpad
    for d in range(2):
        g = _log_sigmoid(_dot(zb, wg_ref[d]) + bg_ref[d]) / GLA_TAU
        b = g
        sh = 1
        while sh < ch:
            scan_scr[pl.ds(GLA_PAD, seq), :] = b
            if d == 0:
                b = b + jnp.where(pos >= sh, scan_scr[pl.ds(GLA_PAD - sh, seq), :], 0.0)
            else:
                b = b + jnp.where(pos < ch - sh, scan_scr[pl.ds(GLA_PAD + sh, seq), :], 0.0)
            sh *= 2
        b3 = b.reshape(nch, ch, dk)
        b_end = b3[:, ch - 1:ch, :] if d == 0 else b3[:, 0:1, :]
        lanes = pl.ds(d * dk, dk)
        qi_scr[:, lanes] = (q * jnp.exp(b)).astype(bf16)
        kn_scr[d] = (k * jnp.exp(-b)).astype(bf16)
        kd_scr[:, lanes] = (k.reshape(nch, ch, dk) * jnp.exp(b_end - b3)).reshape(seq, dk).astype(bf16)
        dec_scr[d] = jnp.exp(b_end)

    for d in range(2):
        if with_s0:
            st_scr[d] = s0_ref[d].T
        else:
            st_scr[d] = jnp.zeros((dv, dk), f32)

    ri = lax.broadcasted_iota(jnp.int32, (ch, ch), 0)
    ci = lax.broadcasted_iota(jnp.int32, (ch, ch), 1)

    vb_scr[...] = v_ref[...].astype(bf16)
    chunk_rows = lambda c: pl.ds(pl.multiple_of(c * ch, ch), ch)
    fwd, bwd = pl.ds(0, dk), pl.ds(dk, dk)

    def increments(c, carry):
        rows = chunk_rows(c)
        u_scr[c] = lax.dot_general(vb_scr[rows, :], kd_scr[rows, :], (((0,), (0,)), ((), ())),
                                   preferred_element_type=f32)
        return carry

    lax.fori_loop(0, nch, increments, 0, unroll=GLA_UNROLL)

    def states(i, carry):
        for d, c, lanes in ((0, i, fwd), (1, nch - 1 - i, bwd)):
            st = st_scr[d]
            sb_scr[c, :, lanes] = st.astype(bf16)
            st_scr[d] = st * dec_scr[d, c] + u_scr[c, :, lanes]
        return carry

    lax.fori_loop(0, nch, states, 0)

    if with_sfin:
        for d in range(2):
            sfin_ref[d] = st_scr[d].T

    def outputs(c, carry):
        rows = chunk_rows(c)
        qi = qi_scr[rows, :]
        pf = _dot_nt(qi[:, 0:dk], kn_scr[0, rows, :])
        pb = _dot_nt(qi[:, dk:2 * dk], kn_scr[1, rows, :])
        a = jnp.where(ci < ri, pf, jnp.where(ci > ri, pb, pf + pb))
        o_scr[rows, :] = _dot(a.astype(bf16), vb_scr[rows, :]) + _dot_nt(qi, sb_scr[c])
        return carry

    lax.fori_loop(0, nch, outputs, 0, unroll=GLA_UNROLL)

    o = o_scr[...]
    r = r_ref[...].astype(f32)
    o = o * lax.rsqrt(jnp.mean(o * o, axis=-1, keepdims=True) + EPS) * ng_ref[...]
    o_ref[...] = (o * (r * jax.nn.sigmoid(r))).astype(o_ref.dtype)


def _gla(p, wgate_p, b_gate, gla_norm, seq, row_block0, nreq, rope_tabs=None, state=None, layer=0,
         with_sfin=False, depth=1, carried=None, make_side=()):
    dk, dv = GLA_DK, GLA_DV
    nch = seq // GLA_CHUNK
    rope = rope_tabs is not None
    with_s0 = state is not None
    blk = lambda w, off: pl.BlockSpec((seq, w), lambda b, h: (row_block0 + b, off // w + h))
    in_specs = [blk(dk, OFF_GQ), blk(dk, OFF_GK), blk(dv, OFF_GV), blk(dv, OFF_GR),
                pl.BlockSpec((seq, LANE), lambda b, h: (row_block0 + b, OFF_GZ // LANE)),
                pl.BlockSpec((2, LANE, dk), lambda b, h: (0, 0, h)),
                pl.BlockSpec((2, 1, dk), lambda b, h: (0, 0, h)),
                pl.BlockSpec((1, dv), lambda b, h: (0, h))]
    args = [p, p, p, p, p, wgate_p, b_gate.reshape(2, 1, GLA_KW), gla_norm.reshape(1, GLA_VW)]
    if rope:
        in_specs += [pl.BlockSpec((seq, dk), lambda b, h: (0, 0))] * 2
        args += list(rope_tabs)
    if with_s0:
        in_specs.append(pl.BlockSpec((None, None, 2, None, dk, dv), lambda b, h: (b, layer, 0, h, 0, 0)))
        args.append(state)
    aliases = {}
    if carried is not None:
        aliases = {len(args): 1}
        in_specs.append(pl.BlockSpec(memory_space=pl.ANY))
        args.append(carried)
    side = [make(nreq * GLA_HEADS, lambda b, h: b * GLA_HEADS + h) for make in make_side]
    in_specs += [j.in_spec for j in side]
    args += [j.array for j in side]
    out_specs = [pl.BlockSpec((seq, dv), lambda b, h: (b, h))]
    out_shape = [jax.ShapeDtypeStruct((nreq * seq, GLA_VW), bf16)]
    if with_sfin:
        out_specs.append(pl.BlockSpec((None, None, 2, None, dk, dv), lambda b, h: (b, layer, 0, h, 0, 0)))
        out_shape.append(jax.ShapeDtypeStruct((nreq, depth, 2, GLA_HEADS, dk, dv), f32))
    out_specs += [j.out_spec for j in side]
    out_shape += [j.out_shape for j in side]
    scratch = [pltpu.VMEM((seq, 2 * dk), bf16), pltpu.VMEM((2, seq, dk), bf16), pltpu.VMEM((seq, 2 * dk), bf16),
               pltpu.VMEM((2, nch, 1, dk), f32), pltpu.VMEM((seq + 2 * GLA_PAD, dk), f32),
               pltpu.VMEM((seq, dv), bf16), pltpu.VMEM((nch, dv, 2 * dk), f32), pltpu.VMEM((nch, dv, 2 * dk), bf16),
               pltpu.VMEM((seq, dv), f32), pltpu.VMEM((2, dv, dk), f32)]
    return pl.pallas_call(
        functools.partial(_gla_body, seq=seq, rope=rope, with_s0=with_s0, with_sfin=with_sfin,
                          n_carried=len(aliases), side=tuple(j.body for j in side)),
        grid=(nreq, GLA_HEADS),
        in_specs=in_specs,
        out_specs=out_specs,
        out_shape=out_shape,
        input_output_aliases=aliases,
        scratch_shapes=scratch,
        compiler_params=_cparams(("arbitrary", "arbitrary")),
        name="gla",
    )(*args)


TM_MERGE = 512
MERGE_SUB = 2


def _merge_body(x_ref, mod_ref, post_ref, bp_ref, bn_ref, bg_ref, gl_ref, w_ref, o_ref, m_scr, mb_scr, r_scr):
    n = pl.program_id(1)
    m_ref = m_scr.at[pl.program_id(2)]
    mb_ref = mb_scr.at[pl.program_id(2)]

    for bi, br_ref in enumerate((bp_ref, bn_ref, bg_ref)):
        @pl.when(n == bi)
        def _():
            y = jax.nn.sigmoid(gl_ref[...].astype(f32)) * _dot(br_ref[...], w_ref[...])
            m_ref[...] = y if bi == 0 else m_ref[...] + y

    @pl.when(n == N_BRANCH)
    def _():
        mb_ref[...] = m_ref[...].astype(bf16)
        m_ref[...] = _dot(mb_ref[:, 0:BRANCH_W], w_ref[...])

    @pl.when(n == N_BRANCH + 1)
    def _():
        m_ref[...] += _dot(mb_ref[:, BRANCH_W:2 * BRANCH_W], w_ref[...])
        _norm_gate_residual(m_ref, x_ref, mod_ref, post_ref, o_ref, r_scr, 1.0)


def _merge(x, mod_l, post, y_pool, y_na, y_gla, p, w_stack, layer, cond):
    t, d = x.shape
    tm = TM_MERGE
    sub = MERGE_SUB
    nsteps = N_BRANCH + d // BRANCH_W
    tile = lambda i, s: i * sub + s

    def rows_at(first, last):
        def index(i, n, s):
            return jnp.where(n < first, jnp.maximum(tile(i, 0) - 1, 0),
                             jnp.where(n > last, tile(i, sub - 1), tile(i, s)))
        return index

    last = nsteps - 1
    br = lambda step: pl.BlockSpec((tm, BRANCH_W), lambda i, n, s: (rows_at(step, step)(i, n, s), 0))
    xo = pl.BlockSpec((tm, d), lambda i, n, s: (rows_at(last, last)(i, n, s), 0))
    return pl.pallas_call(
        _merge_body,
        grid=(t // (tm * sub), nsteps, sub),
        in_specs=[
            xo,
            pl.BlockSpec((None, 1, 3 * d), lambda i, n, s: (_cond_index(tile(i, s), tm, cond), 0, 1)),
            pl.BlockSpec((1, d), lambda i, n, s: (0, 0)),
            br(0), br(1), br(2),
            pl.BlockSpec((tm, d),
                         lambda i, n, s: (rows_at(0, N_BRANCH - 1)(i, n, s), jnp.minimum(n, N_BRANCH - 1))),
            pl.BlockSpec((None, None, BRANCH_W, d), lambda i, n, s: (layer, n, 0, 0)),
        ],
        out_specs=xo,
        out_shape=jax.ShapeDtypeStruct((t, d), f32),
        scratch_shapes=[pltpu.VMEM((sub, tm, d), f32), pltpu.VMEM((sub, tm, d), bf16), pltpu.VMEM((tm, LANE), f32)],
        compiler_params=_cparams(("arbitrary", "arbitrary", "arbitrary")),
        name="merge",
    )(x, mod_l, post, y_pool, y_na, y_gla, p, w_stack)


_IN_SPLITS = (POOL_WIDTH, NA_WIDTH, NA_WIDTH, NA_WIDTH, GLA_KW, GLA_KW, GLA_VW, 2 * GLA_RANK, GLA_VW, GATE_W)
_IN_OFFS = tuple(int(v) for v in np.cumsum((0,) + _IN_SPLITS))
_IN_RUNS = ((_IN_OFFS[9], _IN_OFFS[10]), (_IN_OFFS[0], _IN_OFFS[7]), (_IN_OFFS[8], _IN_OFFS[9]),
            (_IN_OFFS[7], _IN_OFFS[8]))


CT_IN = 512
_IN_TILE_STARTS = []
for _a, _b in _IN_RUNS:
    _IN_TILE_STARTS += [_a + CT_IN * _t for _t in range(-(-(_b - _a) // CT_IN))]
assert len(_IN_TILE_STARTS) * CT_IN == IN_COLS_P and all(v % 8 == 0 for v in _IN_TILE_STARTS)
_IN_LAST_VALID = (_IN_RUNS[-1][1] - _IN_RUNS[-1][0]) % CT_IN or CT_IN
assert all((b - a) % CT_IN == 0 for a, b in _IN_RUNS[:-1])


def _w_in_tile_start(j):
    out = jnp.int32(_IN_TILE_STARTS[0]) + CT_IN * j
    for t in range(1, len(_IN_TILE_STARTS)):
        if _IN_TILE_STARTS[t] != _IN_TILE_STARTS[t - 1] + CT_IN:
            out = jnp.where(j >= t, _IN_TILE_STARTS[t] + CT_IN * (j - t), out)
    return out


N_IN_TILES = IN_COLS_P // CT_IN


def _cast_w_in_body(step, w_ref, o_ref):
    @pl.when(step < N_IN_TILES)
    def _():
        col = lax.broadcasted_iota(jnp.int32, (1, CT_IN), 1)
        valid = jnp.where(step == N_IN_TILES - 1, _IN_LAST_VALID, CT_IN)
        o_ref[...] = jnp.where(col < valid, w_ref[...].T, 0.0).astype(bf16)


def _w_in_cast(wt, n_cols, layer, nsteps, step_of):
    assert nsteps >= N_IN_TILES
    d = wt.shape[1]
    tile = lambda *g: jnp.minimum(step_of(*g), N_IN_TILES - 1)
    return _SideCast(
        wt,
        pl.BlockSpec((pl.Element(CT_IN), pl.Element(d)),
                     lambda *g: (pl.multiple_of(layer * n_cols + _w_in_tile_start(tile(*g)), 8), 0)),
        pl.BlockSpec((d, CT_IN), lambda *g: (0, tile(*g))),
        jax.ShapeDtypeStruct((d, IN_COLS_P), bf16),
        _cast_w_in_body)


class _SideCast(NamedTuple):
    array: Any
    in_spec: pl.BlockSpec
    out_spec: pl.BlockSpec
    out_shape: jax.ShapeDtypeStruct
    body: Callable


def _cast_ffn_in_body(step, w_ref, o_ref):
    del step
    pad = jnp.zeros((w_ref.shape[0], D_FF_P - D_FF), bf16)
    for half in range(2):
        o_ref[half, :, 0:D_FF] = w_ref[:, half * D_FF:(half + 1) * D_FF].astype(bf16)
        o_ref[half, :, D_FF:D_FF_P] = pad


def _ffn_in_cast(w, layer, slot, nsteps, step_of):
    d = w.shape[2]
    rows = d // nsteps
    assert rows * nsteps == d and rows % 16 == 0
    return _SideCast(
        w,
        pl.BlockSpec((None, None, rows, 2 * D_FF), lambda *g: (layer, slot, step_of(*g), 0)),
        pl.BlockSpec((2, rows, D_FF_P), lambda *g: (0, step_of(*g), 0)),
        jax.ShapeDtypeStruct((2, d, D_FF_P), bf16),
        _cast_ffn_in_body)


def _cast_ffn_out_body(step, w_ref, o_ref):
    rows = w_ref.shape[0]
    row = step * rows + lax.broadcasted_iota(jnp.int32, (rows, 1), 0)
    o_ref[...] = jnp.where(row < D_FF, w_ref[...], 0.0).astype(bf16)


def _ffn_out_cast(w, layer, slot, nsteps, step_of):
    d = w.shape[3]
    rows = D_FF_P // nsteps
    assert rows * nsteps == D_FF_P and rows % 16 == 0 and rows * (nsteps - 1) < D_FF
    return _SideCast(
        w,
        pl.BlockSpec((None, None, rows, d), lambda *g: (layer, slot, step_of(*g), 0)),
        pl.BlockSpec((rows, d), lambda *g: (step_of(*g), 0)),
        jax.ShapeDtypeStruct((D_FF_P, d), bf16),
        _cast_ffn_out_body)


def _run_cast_body(w_ref, o_ref, *, cast):
    cast(pl.program_id(0), w_ref, o_ref)


def _run_cast(make_job, nsteps, name):
    job = make_job(nsteps, lambda r: r)
    return pl.pallas_call(
        functools.partial(_run_cast_body, cast=job.body),
        grid=(nsteps,),
        in_specs=[job.in_spec],
        out_specs=job.out_spec,
        out_shape=job.out_shape,
        compiler_params=_cparams(("parallel",)),
        name=name,
    )(job.array)


def _cast_w_stack_body(wb_ref, wo_ref, o_ref):
    is_branch = pl.program_id(1) < N_BRANCH

    @pl.when(is_branch)
    def _():
        o_ref[...] = wb_ref[...].astype(bf16)

    @pl.when(jnp.logical_not(is_branch))
    def _():
        o_ref[...] = wo_ref[...].astype(bf16)


def _cast_w_stack(w_branch, w_out):
    depth, nb, bw, d = w_branch.shape
    kparts = d // bw
    wo = w_out.reshape(depth, kparts, bw, d)
    blk = lambda index: pl.BlockSpec((None, None, bw, d), index)
    return pl.pallas_call(
        _cast_w_stack_body,
        grid=(depth, nb + kparts),
        in_specs=[blk(lambda l, j: (l, jnp.minimum(j, nb - 1), 0, 0)),
                  blk(lambda l, j: (l, jnp.maximum(j - nb, 0), 0, 0))],
        out_specs=blk(lambda l, j: (l, j, 0, 0)),
        out_shape=jax.ShapeDtypeStruct((depth, nb + kparts, bw, d), bf16),
        compiler_params=_cparams(("parallel", "arbitrary")),
        name="cast_w_stack",
    )(w_branch, wo)


def _prep_gate(w_gate):
    out = jnp.zeros((2, LANE, GLA_KW), f32)
    for d in range(2):
        out = out.at[d, d * GLA_RANK:(d + 1) * GLA_RANK].set(w_gate[d])
    return out.astype(bf16)


def kernel(x_prompt, x_sample, c, cache_na_k, cache_na_v, state_gla, c_ctx, w_mod, b_mod, norm_pre, norm_post,
           w_ffn_in, w_ffn_out, w_in, pool_w, pool_scale, na_rpb, gla_w_gate, gla_b_gate, gla_norm, w_branch,
           w_out):
    nb, seq, d = x_prompt.shape
    ndec, dseq, _ = x_sample.shape
    depth = w_mod.shape[0]
    n_ctx = nb * seq
    n_lat = ndec * dseq
    assert (seq, dseq, d) == (SEQ, DEC_SEQ, D_MODEL)
    assert n_ctx % (TM_MERGE * MERGE_SUB) == 0 and n_ctx % TM_FFN == 0 and dseq % TM_FFN == 0

    xs = [x_prompt.reshape(n_ctx, d), x_sample.reshape(n_lat, d)]
    conds = [(0, n_ctx), (1, dseq)]
    ncond = -(-(1 + ndec) // 8) * 8
    c_all = jnp.concatenate([c_ctx[None], c, jnp.zeros((ncond - 1 - ndec, d), f32)], axis=0)
    mod = _modulation(c_all, w_mod, b_mod)
    rope_tabs = _rope_tables(dseq)
    bias_tbl = _na_bias_table(na_rpb)

    in_cast = lambda l, s: functools.partial(_ffn_in_cast, w_ffn_in, l, s)
    out_cast = lambda l, s: functools.partial(_ffn_out_cast, w_ffn_out, l, s)
    ffn_w_in = {(0, 0): _run_cast(in_cast(0, 0), d // 256, "cast_ffn_in")}
    ffn_w_out = {(0, 0): _run_cast(out_cast(0, 0), D_FF_P // TF, "cast_ffn_out")}
    wt_in = jnp.swapaxes(w_in, 1, 2).reshape(depth * w_in.shape[2], d)
    w_in_cast = lambda l: functools.partial(_w_in_cast, wt_in, w_in.shape[2], l)
    w_in_p = {0: _run_cast(w_in_cast(0), N_IN_TILES, "cast_w_in")}
    w_stack_all = _cast_w_stack(w_branch, w_out)

    caches = None
    new_s = None
    for l in range(depth):
        mod_l = mod[l].reshape(ncond, 1, N_MOD * d)
        pre = norm_pre[l].reshape(3, 1, d)
        post = norm_post[l].reshape(3, 1, d)
        wgate_p = _prep_gate(gla_w_gate[l])
        pw = pool_w[l].astype(bf16)
        psc = pool_scale[l].reshape(1, POOL_WIDTH)
        later = [(l, 1)] + ([(l + 1, 0)] if l + 1 < depth else [])

        xs = [_ffn(x, mod_l, pre[0], post[0], ffn_w_in[l, 0], ffn_w_out[l, 0], 0, cond) for x, cond in zip(xs, conds)]
        p_ctx, p_lat = [_inproj(x, mod_l, pre[1], w_in_p[l], cond, dt) for x, cond, dt in zip(xs, conds, (f32, bf16))]

        y_pool = [_pool(p_ctx, pw, psc, seq, 0, nb), _pool(p_lat, pw, psc, dseq, 0, ndec)]
        na_ctx, new_k, new_v = _ctx_attn(p_ctx, nb, l, depth, caches)
        caches = (new_k, new_v)
        nxt = [l + 1] if l + 1 < depth else []
        on_na = nxt if ndec * NA_HEADS >= N_IN_TILES else []
        na_lat, *cast = _na_latent(p_lat, cache_na_k, cache_na_v, bias_tbl, l, 0, ndec,
                                   make_side=[in_cast(*ls) for ls in later] + [w_in_cast(j) for j in on_na])
        ffn_w_in.update(zip(later, cast))
        w_in_p.update(zip(on_na, cast[len(later):]))
        y_na = [na_ctx, na_lat]
        on_gla = [j for j in nxt if j not in on_na]
        g_ctx, new_s, *cast = _gla(p_ctx, wgate_p, gla_b_gate[l], gla_norm[l], seq, 0, nb, layer=l, with_sfin=True,
                                   depth=depth, carried=new_s, make_side=[w_in_cast(j) for j in on_gla])
        w_in_p.update(zip(on_gla, cast))
        g_lat, *cast = _gla(p_lat, wgate_p, gla_b_gate[l], gla_norm[l], dseq, 0, ndec, rope_tabs=rope_tabs,
                            state=state_gla, layer=l, make_side=[out_cast(*ls) for ls in later])
        ffn_w_out.update(zip(later, cast))
        y_gla = [g_ctx, g_lat]

        xs = [_merge(x, mod_l, post[1], yp, yn, yg, p, w_stack_all, l, cond)
              for x, yp, yn, yg, p, cond in zip(xs, y_pool, y_na, y_gla, (p_ctx, p_lat), conds)]
        xs = [_ffn(x, mod_l, pre[2], post[2], ffn_w_in[l, 1], ffn_w_out[l, 1], 2, cond) for x, cond in zip(xs, conds)]

    return (xs[0].reshape(nb, seq, d), xs[1].reshape(ndec, dseq, d), caches[0], caches[1], new_s)
```

```python
import functools
from typing import Any, Callable, NamedTuple

import numpy as np
import jax
import jax.numpy as jnp
from jax import lax
from jax.experimental import pallas as pl
from jax.experimental.pallas import tpu as pltpu

f32 = jnp.float32
bf16 = jnp.bfloat16

D_MODEL = 2048
SEQ = 256
DEC_SEQ = 2048
GRID_W = 64
N_MOD = 9
D_FF = 5504
FFN_RES = 0.5
EPS = 1e-6
NEG_INF = -1e30

POOL_GROUPS = 4
POOL_WINDOWS = (2, 4, 8, 16)
POOL_WIDTH = 1024
POOL_GC = POOL_WIDTH // POOL_GROUPS

NA_HEADS = 8
NA_HEAD_DIM = 128
NA_WIDTH = NA_HEADS * NA_HEAD_DIM
NA_WIN_H = 8
NA_WIN_W = 16

GLA_HEADS = 4
GLA_DK = 128
GLA_DV = 256
GLA_KW = GLA_HEADS * GLA_DK
GLA_VW = GLA_HEADS * GLA_DV
GLA_RANK = 16
GLA_TAU = 16.0
GLA_CHUNK = 64
ROPE_BASE = 10000.0

BRANCH_W = 1024
N_BRANCH = 3
GATE_W = N_BRANCH * D_MODEL

LANE = 128
VMEM_LIMIT = 56 * 1024 * 1024

OFF_GL = 0
OFF_POOL = OFF_GL + GATE_W
OFF_NQ = OFF_POOL + POOL_WIDTH
OFF_NK = OFF_NQ + NA_WIDTH
OFF_NV = OFF_NK + NA_WIDTH
OFF_GQ = OFF_NV + NA_WIDTH
OFF_GK = OFF_GQ + GLA_KW
OFF_GV = OFF_GK + GLA_KW
OFF_GR = OFF_GV + GLA_VW
OFF_GZ = OFF_GR + GLA_VW
TN_IN = 2304
IN_COLS_P = -(-(OFF_GZ + LANE) // TN_IN) * TN_IN

TM = 512
TF = 512
D_FF_P = -(-D_FF // TF) * TF


def _cparams(sem):
    return pltpu.CompilerParams(dimension_semantics=sem, vmem_limit_bytes=VMEM_LIMIT)


def _cond_index(i, tm, cond):
    return cond[0] + (i * tm) // cond[1]


ROW_CHUNK = 16


def _row_sweep(nrows, fn, unroll=4):
    def trip(i, carry):
        fn(pl.ds(pl.multiple_of(i * ROW_CHUNK, ROW_CHUNK), ROW_CHUNK))
        return carry

    lax.fori_loop(0, nrows // ROW_CHUNK, trip, 0, unroll=unroll)


def _row_rsqrt(x_ref, r_scr):
    n = x_ref.shape[1]

    def fn(rows):
        x = x_ref[rows, :]
        ss = jnp.sum(_lane_fold(x * x, jnp.add), axis=-1, keepdims=True)
        r_scr[rows, :] = jnp.broadcast_to(lax.rsqrt(ss * (1.0 / n) + EPS), (ROW_CHUNK, LANE))

    _row_sweep(x_ref.shape[0], fn, unroll=16)


def _lanes(r, n):
    return jnp.concatenate([r] * (n // LANE), axis=1)


def _norm_modulate(x_ref, mod_ref, g_ref, h_ref, r_scr):
    d = D_MODEL
    shift = mod_ref[:, 0:d]
    w = g_ref[...] * (1.0 + mod_ref[:, d:2 * d])
    _row_rsqrt(x_ref, r_scr)

    def fn(rows):
        h_ref[rows, :] = (x_ref[rows, :] * _lanes(r_scr[rows, :], d) * w + shift).astype(h_ref.dtype)

    _row_sweep(x_ref.shape[0], fn)


def _norm_gate_residual(y_ref, x_ref, mod_ref, g_ref, o_ref, r_scr, res_weight):
    d = D_MODEL
    w = (res_weight * mod_ref[:, 2 * d:3 * d]) * g_ref[...]
    _row_rsqrt(y_ref, r_scr)

    def fn(rows):
        o_ref[rows, :] = x_ref[rows, :] + y_ref[rows, :] * _lanes(r_scr[rows, :], d) * w

    _row_sweep(x_ref.shape[0], fn)


def _dot(a, b):
    return jnp.dot(a, b, preferred_element_type=f32)


def _dot_nt(a, b):
    return lax.dot_general(a, b, (((1,), (1,)), ((), ())), preferred_element_type=f32)


def _mod_body(c_ref, w_ref, b_ref, o_ref):
    c = c_ref[...]
    s = c * jax.nn.sigmoid(c)
    o_ref[...] = _dot(s.astype(bf16), w_ref[...].astype(bf16)) + b_ref[...]


def _modulation(c_all, w_mod, b_mod):
    depth, d, n = w_mod.shape
    nc = c_all.shape[0]
    tn = 1024
    return pl.pallas_call(
        _mod_body,
        grid=(depth, n // tn),
        in_specs=[
            pl.BlockSpec((nc, d), lambda l, j: (0, 0)),
            pl.BlockSpec((None, d, tn), lambda l, j: (l, 0, j)),
            pl.BlockSpec((None, 1, tn), lambda l, j: (l, 0, j)),
        ],
        out_specs=pl.BlockSpec((None, nc, tn), lambda l, j: (l, 0, j)),
        out_shape=jax.ShapeDtypeStruct((depth, nc, n), f32),
        compiler_params=_cparams(("parallel", "parallel")),
        name="modulation",
    )(c_all, w_mod, b_mod.reshape(depth, 1, n))


TM_FFN = 1024


def _ffn_body(x_ref, mod_ref, pre_ref, post_ref, wgu_ref, wo_ref, o_ref, h_scr, r_scr):
    f = pl.program_id(1)

    @pl.when(f == 0)
    def _():
        _norm_modulate(x_ref, mod_ref, pre_ref, h_scr, r_scr)
        o_ref[...] = jnp.zeros_like(o_ref)

    h = h_scr[...]
    gu = _dot(h, wgu_ref[...])
    gt, up = gu[:, 0:TF], gu[:, TF:2 * TF]
    a = gt * jax.nn.sigmoid(gt) * up
    o_ref[...] += _dot(a.astype(bf16), wo_ref[...])

    @pl.when(f == pl.num_programs(1) - 1)
    def _():
        _norm_gate_residual(o_ref, x_ref, mod_ref, post_ref, o_ref, r_scr, FFN_RES)


def _ffn(x, mod_l, pre, post, w_gu, w_out2, sub, cond):
    t, d = x.shape
    fp = w_out2.shape[0]
    return pl.pallas_call(
        _ffn_body,
        grid=(t // TM_FFN, fp // TF),
        in_specs=[
            pl.BlockSpec((TM_FFN, d), lambda i, f: (i, 0)),
            pl.BlockSpec((None, 1, 3 * d), lambda i, f: (_cond_index(i, TM_FFN, cond), 0, sub)),
            pl.BlockSpec((1, d), lambda i, f: (0, 0)),
            pl.BlockSpec((1, d), lambda i, f: (0, 0)),
            pl.BlockSpec((d, 2 * TF), lambda i, f: (0, f)),
            pl.BlockSpec((TF, d), lambda i, f: (f, 0)),
        ],
        out_specs=pl.BlockSpec((TM_FFN, d), lambda i, f: (i, 0)),
        out_shape=jax.ShapeDtypeStruct((t, d), f32),
        scratch_shapes=[pltpu.VMEM((TM_FFN, d), bf16), pltpu.VMEM((TM_FFN, LANE), f32)],
        compiler_params=_cparams(("parallel", "arbitrary")),
        name="ffn",
    )(x, mod_l, pre, post, w_gu, w_out2)


def _inproj_body(x_ref, mod_ref, pre_ref, w_ref, o_ref, h_scr, r_scr):
    @pl.when(pl.program_id(1) == 0)
    def _():
        _norm_modulate(x_ref, mod_ref, pre_ref, h_scr, r_scr)

    o_ref[...] = _dot(h_scr[...], w_ref[...]).astype(o_ref.dtype)


def _inproj(x, mod_l, pre, w_in_p, cond, out_dtype):
    t, d = x.shape
    tm = TM * (4 // jnp.dtype(out_dtype).itemsize)
    n = w_in_p.shape[1]
    return pl.pallas_call(
        _inproj_body,
        grid=(t // tm, n // TN_IN),
        in_specs=[
            pl.BlockSpec((tm, d), lambda i, j: (i, 0)),
            pl.BlockSpec((None, 1, 3 * d), lambda i, j: (_cond_index(i, tm, cond), 0, 1)),
            pl.BlockSpec((1, d), lambda i, j: (0, 0)),
            pl.BlockSpec((d, TN_IN), lambda i, j: (0, j)),
        ],
        out_specs=pl.BlockSpec((tm, TN_IN), lambda i, j: (i, j)),
        out_shape=jax.ShapeDtypeStruct((t, n), out_dtype),
        scratch_shapes=[pltpu.VMEM((tm, d), bf16), pltpu.VMEM((tm, LANE), f32)],
        compiler_params=_cparams(("parallel", "arbitrary")),
        name="inproj",
    )(x, mod_l, pre, w_in_p)


POOL_PAD = 8


def _pool_body(u_ref, w_ref, sc_ref, o_ref, pad_scr, lvl_scr, *, seq):
    gc = POOL_GC
    pad = POOL_PAD
    n_lvl = seq + pad
    zeros = jnp.zeros((pad, POOL_WIDTH), f32)
    pad_scr[pl.ds(0, pad), :] = zeros
    pad_scr[pl.ds(pad + seq, pad), :] = zeros
    pad_scr[pl.ds(pad, seq), :] = u_ref[...].astype(f32)
    lvl_scr[:, pl.ds(n_lvl, pad), :] = jnp.zeros((2, pad, gc), f32)
    t = lax.broadcasted_iota(jnp.int32, (seq, 1), 0)
    for gi, win in enumerate(POOL_WINDOWS):
        cols = pl.ds(gi * gc, gc)
        read = lambda off, n: pad_scr[pl.ds(off, n), cols]
        k, slot = 1, 0
        while 2 * k < win:
            lvl_scr[slot, pl.ds(0, n_lvl), :] = read(0, n_lvl) + read(k, n_lvl)
            read = functools.partial(lambda s_, off, n: lvl_scr[s_, pl.ds(off, n), :], slot)
            k, slot = 2 * k, 1 - slot
        lo = jnp.maximum(t - win // 2, 0)
        hi = jnp.minimum(t + win - 1 - win // 2, seq - 1)
        inv_cnt = 1.0 / (hi - lo + 1).astype(f32)
        acc = read(pad - win // 2, seq) + read(pad, seq)
        pooled = acc * inv_cnt - pad_scr[pl.ds(pad, seq), cols]
        y = _dot(pooled.astype(bf16), w_ref[gi])
        o_ref[:, cols] = (y * sc_ref[:, cols]).astype(o_ref.dtype)


def _pool(p, pool_w, pool_scale, seq, row_block0, nseq):
    cb = OFF_POOL // POOL_WIDTH
    return pl.pallas_call(
        functools.partial(_pool_body, seq=seq),
        grid=(nseq,),
        in_specs=[
            pl.BlockSpec((seq, POOL_WIDTH), lambda s: (row_block0 + s, cb)),
            pl.BlockSpec((POOL_GROUPS, POOL_GC, POOL_GC), lambda s: (0, 0, 0)),
            pl.BlockSpec((1, POOL_WIDTH), lambda s: (0, 0)),
        ],
        out_specs=pl.BlockSpec((seq, POOL_WIDTH), lambda s: (s, 0)),
        out_shape=jax.ShapeDtypeStruct((nseq * seq, POOL_WIDTH), bf16),
        scratch_shapes=[pltpu.VMEM((seq + 2 * POOL_PAD, POOL_WIDTH), f32),
                        pltpu.VMEM((2, seq + 2 * POOL_PAD, POOL_GC), f32)],
        compiler_params=_cparams(("parallel",)),
        name="pool",
    )(p, pool_w, pool_scale)


def _lane_fold(x, op):
    parts = [x[:, i * LANE:(i + 1) * LANE] for i in range(x.shape[1] // LANE)]
    while len(parts) > 1:
        parts = [op(parts[i], parts[i + 1]) for i in range(0, len(parts) - 1, 2)] + parts[len(parts) & ~1:]
    return parts[0]


def _softmax_rows(s):
    m = jnp.max(_lane_fold(s, jnp.maximum), axis=-1, keepdims=True)
    e = jnp.exp(s - m)
    return e / jnp.sum(_lane_fold(e, jnp.add), axis=-1, keepdims=True)


def _ctx_attn_body(q_ref, k_ref, v_ref, *refs):
    o_ref, nk_ref, nv_ref = refs[-3:]
    hd = NA_HEAD_DIM
    for h in range(NA_HEADS):
        cols = pl.ds(h * hd, hd)
        kf = k_ref[:, cols]
        vf = v_ref[:, cols]
        nk_ref[h] = kf
        nv_ref[h] = vf
        p = _softmax_rows(_dot_nt(q_ref[:, cols].astype(bf16), kf.astype(bf16)) * (hd ** -0.5))
        o_ref[:, cols] = _dot(p.astype(bf16), vf.astype(bf16)).astype(o_ref.dtype)


def _ctx_attn(p, nseq, layer, depth, caches=None):
    spec = lambda off: pl.BlockSpec((SEQ, NA_WIDTH), lambda b: (b, off // NA_WIDTH))
    cache_spec = pl.BlockSpec((None, None, NA_HEADS, SEQ, NA_HEAD_DIM), lambda b: (b, layer, 0, 0, 0))
    cache_shape = jax.ShapeDtypeStruct((nseq, depth, NA_HEADS, SEQ, NA_HEAD_DIM), f32)
    in_specs = [spec(OFF_NQ), spec(OFF_NK), spec(OFF_NV)]
    args = [p, p, p]
    aliases = {}
    if caches is not None:
        in_specs += [pl.BlockSpec(memory_space=pl.ANY)] * 2
        args += list(caches)
        aliases = {3: 1, 4: 2}
    return pl.pallas_call(
        _ctx_attn_body,
        grid=(nseq,),
        in_specs=in_specs,
        out_specs=[pl.BlockSpec((SEQ, NA_WIDTH), lambda b: (b, 0)), cache_spec, cache_spec],
        out_shape=[jax.ShapeDtypeStruct((nseq * SEQ, NA_WIDTH), bf16), cache_shape, cache_shape],
        input_output_aliases=aliases,
        compiler_params=_cparams(("parallel",)),
        name="ctx_attn",
    )(*args)


LOG2E = float(np.log2(np.e))


def _na_bias_table(rpb):
    qc = np.arange(GRID_W)[:, None]
    kc = np.arange(GRID_W)[None, :]
    cs = np.clip(qc - NA_WIN_W // 2, 0, GRID_W - NA_WIN_W)
    ok = (kc >= cs) & (kc < cs + NA_WIN_W)
    cidx = np.clip(kc - qc + NA_WIN_W - 1, 0, 2 * NA_WIN_W - 2)
    onehot = jnp.asarray((cidx[None] == np.arange(2 * NA_WIN_W - 1)[:, None, None]) & ok[None], f32)
    toep = jnp.einsum('...rc,cqk->...rqk', rpb.astype(f32), onehot, precision=lax.Precision.HIGHEST)
    toep = jnp.where(ok, toep * LOG2E, NEG_INF)
    return jnp.concatenate([toep[..., :-1, :, :], toep[..., 1:, :, :]], axis=-1)


NA_UNROLL = 8
NA_CTX_ROWS = 256


def _na_body(*refs, rows, side):
    ns = len(side)
    qb_scr, kb_scr, vb_scr, ck_ref, cv_ref, bias_ref = refs[:6]
    o_ref = refs[6 + ns]
    sl_scr, sc_scr, el_scr, ec_scr, den_scr, oc_scr = refs[7 + 2 * ns:]
    step = pl.program_id(0) * pl.num_programs(1) + pl.program_id(1)
    for cast, src_ref, dst_ref in zip(side, refs[6:6 + ns], refs[7 + ns:7 + 2 * ns]):
        cast(step, src_ref, dst_ref)

    hd = NA_HEAD_DIM
    scale = hd ** -0.5 * LOG2E
    kh = min(NA_WIN_H, rows)
    nloc = kh * GRID_W
    n = rows * GRID_W
    assert qb_scr.dtype == bf16
    ck = ck_ref[...].astype(bf16)
    cv = cv_ref[...].astype(bf16)

    def row_slices(r):
        rs = jnp.clip(r - kh // 2, 0, rows - kh)
        q_rows = pl.ds(pl.multiple_of(r * GRID_W, GRID_W), GRID_W)
        k_rows = pl.ds(pl.multiple_of(rs * GRID_W, GRID_W), nloc)
        return rs, q_rows, k_rows

    def ctx_scores(i, carry):
        blk = pl.ds(pl.multiple_of(i * NA_CTX_ROWS, NA_CTX_ROWS), NA_CTX_ROWS)
        sc_scr[blk, :] = _dot_nt(qb_scr[blk, :], ck) * scale
        return carry

    lax.fori_loop(0, n // NA_CTX_ROWS, ctx_scores, 0, unroll=2)

    def loc_scores(r, carry):
        rs, q_rows, k_rows = row_slices(r)
        first = rs - r + NA_WIN_H - 1
        bias = jnp.concatenate([bias_ref[first + 2 * e] for e in range(kh // 2)], axis=1)
        sl_scr[q_rows, :] = _dot_nt(qb_scr[q_rows, :], kb_scr[k_rows, :]) * scale + bias
        return carry

    lax.fori_loop(0, rows, loc_scores, 0, unroll=NA_UNROLL)

    def numerators(r, carry):
        q_rows = pl.ds(pl.multiple_of(r * GRID_W, GRID_W), GRID_W)
        s_loc = sl_scr[q_rows, :]
        s_ctx = sc_scr[q_rows, :]
        m = jnp.max(jnp.maximum(_lane_fold(s_loc, jnp.maximum), _lane_fold(s_ctx, jnp.maximum)),
                    axis=-1, keepdims=True)
        e_loc = jnp.exp2(s_loc - m)
        e_ctx = jnp.exp2(s_ctx - m)
        den = jnp.sum(_lane_fold(e_loc, jnp.add) + _lane_fold(e_ctx, jnp.add), axis=-1, keepdims=True)
        el_scr[q_rows, :] = e_loc.astype(bf16)
        ec_scr[q_rows, :] = e_ctx.astype(bf16)
        den_scr[q_rows, :] = jnp.broadcast_to(den, (GRID_W, hd))
        return carry

    lax.fori_loop(0, rows, numerators, 0, unroll=NA_UNROLL)

    def ctx_values(i, carry):
        blk = pl.ds(pl.multiple_of(i * NA_CTX_ROWS, NA_CTX_ROWS), NA_CTX_ROWS)
        oc_scr[blk, :] = _dot(ec_scr[blk, :], cv)
        return carry

    lax.fori_loop(0, n // NA_CTX_ROWS, ctx_values, 0, unroll=2)

    def loc_values(r, carry):
        _, q_rows, k_rows = row_slices(r)
        o = _dot(el_scr[q_rows, :], vb_scr[k_rows, :]) + oc_scr[q_rows, :]
        o_ref[q_rows, :] = (o / den_scr[q_rows, :]).astype(o_ref.dtype)
        return carry

    lax.fori_loop(0, rows, loc_values, 0, unroll=NA_UNROLL)


def _na_latent(p, cache_k, cache_v, bias_tbl, layer, row_block0, nreq, make_side=()):
    n = DEC_SEQ
    hd = NA_HEAD_DIM
    past = cache_k.shape[3]
    rows = n // GRID_W
    side = [make(nreq * NA_HEADS, lambda b, h: b * NA_HEADS + h) for make in make_side]
    qkv = lambda off: pl.BlockSpec((n, hd), lambda b, h: (row_block0 + b, off // hd + h))
    cache = pl.BlockSpec((None, None, None, past, hd), lambda b, h: (b, layer, h, 0, 0))
    return pl.pallas_call(
        functools.partial(_na_body, rows=rows, side=tuple(j.body for j in side)),
        grid=(nreq, NA_HEADS),
        in_specs=[qkv(OFF_NQ), qkv(OFF_NK), qkv(OFF_NV), cache, cache,
                  pl.BlockSpec((None, None, 2 * NA_WIN_H - 2, GRID_W, 2 * GRID_W),
                               lambda b, h: (layer, h, 0, 0, 0))] + [j.in_spec for j in side],
        out_specs=[pl.BlockSpec((n, hd), lambda b, h: (b, h))] + [j.out_spec for j in side],
        out_shape=[jax.ShapeDtypeStruct((nreq * n, NA_WIDTH), bf16)] + [j.out_shape for j in side],
        scratch_shapes=[pltpu.VMEM((n, NA_WIN_H * GRID_W), f32), pltpu.VMEM((n, past), f32),
                        pltpu.VMEM((n, NA_WIN_H * GRID_W), bf16), pltpu.VMEM((n, past), bf16),
                        pltpu.VMEM((n, hd), f32), pltpu.VMEM((n, hd), f32)],
        compiler_params=_cparams(("arbitrary", "arbitrary")),
        name="na_latent",
    )(p, p, p, cache_k, cache_v, bias_tbl, *[j.array for j in side])


GLA_PAD = 32
GLA_UNROLL = 4


def _rope_tables(seq):
    t = np.arange(seq)
    half = GLA_DK // 2
    nf = half // 2
    inv = ROPE_BASE ** (-np.arange(nf, dtype=np.float64) / nf)
    cos, sin = [], []
    for pos in (t // GRID_W, t % GRID_W):
        ang = pos[:, None].astype(np.float64) * inv
        cos += [np.cos(ang), np.cos(ang)]
        sin += [-np.sin(ang), np.sin(ang)]
    return (jnp.asarray(np.concatenate(cos, axis=-1), f32), jnp.asarray(np.concatenate(sin, axis=-1), f32))


def _rope(x, cos, sin_signed):
    nf = GLA_DK // 4
    lane = lax.broadcasted_iota(jnp.int32, x.shape, 1)
    partner = jnp.where(lane % (2 * nf) < nf, pltpu.roll(x, GLA_DK - nf, 1), pltpu.roll(x, nf, 1))
    return x * cos + partner * sin_signed


def _log_sigmoid(x):
    return jnp.minimum(x, 0.0) - jnp.log1p(jnp.exp(-jnp.abs(x)))


def _gla_body(*refs, seq, rope, with_s0, with_sfin, n_carried, side):
    refs = list(refs)
    q_ref, k_ref, v_ref, r_ref, z_ref, wg_ref, bg_ref, ng_ref = refs[:8]
    refs = refs[8:]
    if rope:
        cos_ref, sin_ref = refs[:2]
        refs = refs[2:]
    if with_s0:
        s0_ref = refs[0]
        refs = refs[1:]
    refs = refs[n_carried:]
    side_src, refs = refs[:len(side)], refs[len(side):]
    o_ref = refs[0]
    refs = refs[1:]
    if with_sfin:
        sfin_ref = refs[0]
        refs = refs[1:]
    side_dst, refs = refs[:len(side)], refs[len(side):]
    step = pl.program_id(0) * pl.num_programs(1) + pl.program_id(1)
    for cast, src_ref, dst_ref in zip(side, side_src, side_dst):
        cast(step, src_ref, dst_ref)
    qi_scr, kn_scr, kd_scr, dec_scr, scan_scr, vb_scr, u_scr, sb_scr, o_scr, st_scr = refs

    ch = GLA_CHUNK
    nch = seq // ch
    dk, dv = GLA_DK, GLA_DV

    q = q_ref[...].astype(f32)
    k = k_ref[...].astype(f32)
    if rope:
        q = _rope(q, cos_ref[...], sin_ref[...])
        k = _rope(k, cos_ref[...], sin_ref[...])
    q = q * (dk ** -0.5)

    zb = z_ref[...].astype(bf16)
    pos = lax.broadcasted_iota(jnp.int32, (seq, 1), 0) % ch
    zpad = jnp.zeros((GLA_PAD, dk), f32)
    scan_scr[pl.ds(0, GLA_PAD), :] = zpad
    scan_scr[pl.ds(GLA_PAD + seq, GLA_PAD), :] = zpad
    for d in range(2):
        g = _log_sigmoid(_dot(zb, wg_ref[d]) + bg_ref[d]) / GLA_TAU
        b = g
        sh = 1
        while sh < ch:
            scan_scr[pl.ds(GLA_PAD, seq), :] = b
            if d == 0:
                b = b + jnp.where(pos >= sh, scan_scr[pl.ds(GLA_PAD - sh, seq), :], 0.0)
            else:
                b = b + jnp.where(pos < ch - sh, scan_scr[pl.ds(GLA_PAD + sh, seq), :], 0.0)
            sh *= 2
        b3 = b.reshape(nch, ch, dk)
        b_end = b3[:, ch - 1:ch, :] if d == 0 else b3[:, 0:1, :]
        lanes = pl.ds(d * dk, dk)
        qi_scr[:, lanes] = (q * jnp.exp(b)).astype(bf16)
        kn_scr[d] = (k * jnp.exp(-b)).astype(bf16)
        kd_scr[:, lanes] = (k.reshape(nch, ch, dk) * jnp.exp(b_end - b3)).reshape(seq, dk).astype(bf16)
        dec_scr[d] = jnp.exp(b_end)

    for d in range(2):
        if with_s0:
            st_scr[d] = s0_ref[d].T
        else:
            st_scr[d] = jnp.zeros((dv, dk), f32)

    ri = lax.broadcasted_iota(jnp.int32, (ch, ch), 0)
    ci = lax.broadcasted_iota(jnp.int32, (ch, ch), 1)

    vb_scr[...] = v_ref[...].astype(bf16)
    chunk_rows = lambda c: pl.ds(pl.multiple_of(c * ch, ch), ch)
    fwd, bwd = pl.ds(0, dk), pl.ds(dk, dk)

    def increments(c, carry):
        rows = chunk_rows(c)
        u_scr[c] = lax.dot_general(vb_scr[rows, :], kd_scr[rows, :], (((0,), (0,)), ((), ())),
                                   preferred_element_type=f32)
        return carry

    lax.fori_loop(0, nch, increments, 0, unroll=GLA_UNROLL)

    def states(i, carry):
        for d, c, lanes in ((0, i, fwd), (1, nch - 1 - i, bwd)):
            st = st_scr[d]
            sb_scr[c, :, lanes] = st.astype(bf16)
            st_scr[d] = st * dec_scr[d, c] + u_scr[c, :, lanes]
        return carry

    lax.fori_loop(0, nch, states, 0)

    if with_sfin:
        for d in range(2):
            sfin_ref[d] = st_scr[d].T

    def outputs(c, carry):
        rows = chunk_rows(c)
        qi = qi_scr[rows, :]
        pf = _dot_nt(qi[:, 0:dk], kn_scr[0, rows, :])
        pb = _dot_nt(qi[:, dk:2 * dk], kn_scr[1, rows, :])
        a = jnp.where(ci < ri, pf, jnp.where(ci > ri, pb, pf + pb))
        o_scr[rows, :] = _dot(a.astype(bf16), vb_scr[rows, :]) + _dot_nt(qi, sb_scr[c])
        return carry

    lax.fori_loop(0, nch, outputs, 0, unroll=GLA_UNROLL)

    o = o_scr[...]
    r = r_ref[...].astype(f32)
    o = o * lax.rsqrt(jnp.mean(o * o, axis=-1, keepdims=True) + EPS) * ng_ref[...]
    o_ref[...] = (o * (r * jax.nn.sigmoid(r))).astype(o_ref.dtype)


def _gla(p, wgate_p, b_gate, gla_norm, seq, row_block0, nreq, rope_tabs=None, state=None, layer=0,
         with_sfin=False, depth=1, carried=None, make_side=()):
    dk, dv = GLA_DK, GLA_DV
    nch = seq // GLA_CHUNK
    rope = rope_tabs is not None
    with_s0 = state is not None
    blk = lambda w, off: pl.BlockSpec((seq, w), lambda b, h: (row_block0 + b, off // w + h))
    in_specs = [blk(dk, OFF_GQ), blk(dk, OFF_GK), blk(dv, OFF_GV), blk(dv, OFF_GR),
                pl.BlockSpec((seq, LANE), lambda b, h: (row_block0 + b, OFF_GZ // LANE)),
                pl.BlockSpec((2, LANE, dk), lambda b, h: (0, 0, h)),
                pl.BlockSpec((2, 1, dk), lambda b, h: (0, 0, h)),
                pl.BlockSpec((1, dv), lambda b, h: (0, h))]
    args = [p, p, p, p, p, wgate_p, b_gate.reshape(2, 1, GLA_KW), gla_norm.reshape(1, GLA_VW)]
    if rope:
        in_specs += [pl.BlockSpec((seq, dk), lambda b, h: (0, 0))] * 2
        args += list(rope_tabs)
    if with_s0:
        in_specs.append(pl.BlockSpec((None, None, 2, None, dk, dv), lambda b, h: (b, layer, 0, h, 0, 0)))
        args.append(state)
    aliases = {}
    if carried is not None:
        aliases = {len(args): 1}
        in_specs.append(pl.BlockSpec(memory_space=pl.ANY))
        args.append(carried)
    side = [make(nreq * GLA_HEADS, lambda b, h: b * GLA_HEADS + h) for make in make_side]
    in_specs += [j.in_spec for j in side]
    args += [j.array for j in side]
    out_specs = [pl.BlockSpec((seq, dv), lambda b, h: (b, h))]
    out_shape = [jax.ShapeDtypeStruct((nreq * seq, GLA_VW), bf16)]
    if with_sfin:
        out_specs.append(pl.BlockSpec((None, None, 2, None, dk, dv), lambda b, h: (b, layer, 0, h, 0, 0)))
        out_shape.append(jax.ShapeDtypeStruct((nreq, depth, 2, GLA_HEADS, dk, dv), f32))
    out_specs += [j.out_spec for j in side]
    out_shape += [j.out_shape for j in side]
    scratch = [pltpu.VMEM((seq, 2 * dk), bf16), pltpu.VMEM((2, seq, dk), bf16), pltpu.VMEM((seq, 2 * dk), bf16),
               pltpu.VMEM((2, nch, 1, dk), f32), pltpu.VMEM((seq + 2 * GLA_PAD, dk), f32),
               pltpu.VMEM((seq, dv), bf16), pltpu.VMEM((nch, dv, 2 * dk), f32), pltpu.VMEM((nch, dv, 2 * dk), bf16),
               pltpu.VMEM((seq, dv), f32), pltpu.VMEM((2, dv, dk), f32)]
    return pl.pallas_call(
        functools.partial(_gla_body, seq=seq, rope=rope, with_s0=with_s0, with_sfin=with_sfin,
                          n_carried=len(aliases), side=tuple(j.body for j in side)),
        grid=(nreq, GLA_HEADS),
        in_specs=in_specs,
        out_specs=out_specs,
        out_shape=out_shape,
        input_output_aliases=aliases,
        scratch_shapes=scratch,
        compiler_params=_cparams(("arbitrary", "arbitrary")),
        name="gla",
    )(*args)


TM_MERGE = 512
MERGE_SUB = 2


def _merge_body(x_ref, mod_ref, post_ref, bp_ref, bn_ref, bg_ref, gl_ref, w_ref, o_ref, m_scr, mb_scr, r_scr):
    n = pl.program_id(1)
    m_ref = m_scr.at[pl.program_id(2)]
    mb_ref = mb_scr.at[pl.program_id(2)]

    for bi, br_ref in enumerate((bp_ref, bn_ref, bg_ref)):
        @pl.when(n == bi)
        def _():
            y = jax.nn.sigmoid(gl_ref[...].astype(f32)) * _dot(br_ref[...], w_ref[...])
            m_ref[...] = y if bi == 0 else m_ref[...] + y

    @pl.when(n == N_BRANCH)
    def _():
        mb_ref[...] = m_ref[...].astype(bf16)
        m_ref[...] = _dot(mb_ref[:, 0:BRANCH_W], w_ref[...])

    @pl.when(n == N_BRANCH + 1)
    def _():
        m_ref[...] += _dot(mb_ref[:, BRANCH_W:2 * BRANCH_W], w_ref[...])
        _norm_gate_residual(m_ref, x_ref, mod_ref, post_ref, o_ref, r_scr, 1.0)


def _merge(x, mod_l, post, y_pool, y_na, y_gla, p, w_stack, layer, cond):
    t, d = x.shape
    tm = TM_MERGE
    sub = MERGE_SUB
    nsteps = N_BRANCH + d // BRANCH_W
    tile = lambda i, s: i * sub + s

    def rows_at(first, last):
        def index(i, n, s):
            return jnp.where(n < first, jnp.maximum(tile(i, 0) - 1, 0),
                             jnp.where(n > last, tile(i, sub - 1), tile(i, s)))
        return index

    last = nsteps - 1
    br = lambda step: pl.BlockSpec((tm, BRANCH_W), lambda i, n, s: (rows_at(step, step)(i, n, s), 0))
    xo = pl.BlockSpec((tm, d), lambda i, n, s: (rows_at(last, last)(i, n, s), 0))
    return pl.pallas_call(
        _merge_body,
        grid=(t // (tm * sub), nsteps, sub),
        in_specs=[
            xo,
            pl.BlockSpec((None, 1, 3 * d), lambda i, n, s: (_cond_index(tile(i, s), tm, cond), 0, 1)),
            pl.BlockSpec((1, d), lambda i, n, s: (0, 0)),
            br(0), br(1), br(2),
            pl.BlockSpec((tm, d),
                         lambda i, n, s: (rows_at(0, N_BRANCH - 1)(i, n, s), jnp.minimum(n, N_BRANCH - 1))),
            pl.BlockSpec((None, None, BRANCH_W, d), lambda i, n, s: (layer, n, 0, 0)),
        ],
        out_specs=xo,
        out_shape=jax.ShapeDtypeStruct((t, d), f32),
        scratch_shapes=[pltpu.VMEM((sub, tm, d), f32), pltpu.VMEM((sub, tm, d), bf16), pltpu.VMEM((tm, LANE), f32)],
        compiler_params=_cparams(("arbitrary", "arbitrary", "arbitrary")),
        name="merge",
    )(x, mod_l, post, y_pool, y_na, y_gla, p, w_stack)


_IN_SPLITS = (POOL_WIDTH, NA_WIDTH, NA_WIDTH, NA_WIDTH, GLA_KW, GLA_KW, GLA_VW, 2 * GLA_RANK, GLA_VW, GATE_W)
_IN_OFFS = tuple(int(v) for v in np.cumsum((0,) + _IN_SPLITS))
_IN_RUNS = ((_IN_OFFS[9], _IN_OFFS[10]), (_IN_OFFS[0], _IN_OFFS[7]), (_IN_OFFS[8], _IN_OFFS[9]),
            (_IN_OFFS[7], _IN_OFFS[8]))


CT_IN = 512
_IN_TILE_STARTS = []
for _a, _b in _IN_RUNS:
    _IN_TILE_STARTS += [_a + CT_IN * _t for _t in range(-(-(_b - _a) // CT_IN))]
assert len(_IN_TILE_STARTS) * CT_IN == IN_COLS_P and all(v % 8 == 0 for v in _IN_TILE_STARTS)
_IN_LAST_VALID = (_IN_RUNS[-1][1] - _IN_RUNS[-1][0]) % CT_IN or CT_IN
assert all((b - a) % CT_IN == 0 for a, b in _IN_RUNS[:-1])


def _w_in_tile_start(j):
    out = jnp.int32(_IN_TILE_STARTS[0]) + CT_IN * j
    for t in range(1, len(_IN_TILE_STARTS)):
        if _IN_TILE_STARTS[t] != _IN_TILE_STARTS[t - 1] + CT_IN:
            out = jnp.where(j >= t, _IN_TILE_STARTS[t] + CT_IN * (j - t), out)
    return out


N_IN_TILES = IN_COLS_P // CT_IN


def _cast_w_in_body(step, w_ref, o_ref):
    @pl.when(step < N_IN_TILES)
    def _():
        col = lax.broadcasted_iota(jnp.int32, (1, CT_IN), 1)
        valid = jnp.where(step == N_IN_TILES - 1, _IN_LAST_VALID, CT_IN)
        o_ref[...] = jnp.where(col < valid, w_ref[...].T, 0.0).astype(bf16)


def _w_in_cast(wt, n_cols, layer, nsteps, step_of):
    assert nsteps >= N_IN_TILES
    d = wt.shape[1]
    tile = lambda *g: jnp.minimum(step_of(*g), N_IN_TILES - 1)
    return _SideCast(
        wt,
        pl.BlockSpec((pl.Element(CT_IN), pl.Element(d)),
                     lambda *g: (pl.multiple_of(layer * n_cols + _w_in_tile_start(tile(*g)), 8), 0)),
        pl.BlockSpec((d, CT_IN), lambda *g: (0, tile(*g))),
        jax.ShapeDtypeStruct((d, IN_COLS_P), bf16),
        _cast_w_in_body)


class _SideCast(NamedTuple):
    array: Any
    in_spec: pl.BlockSpec
    out_spec: pl.BlockSpec
    out_shape: jax.ShapeDtypeStruct
    body: Callable


def _cast_ffn_in_body(step, w_ref, o_ref):
    del step
    for f in range(D_FF_P // TF):
        n = min(TF, D_FF - f * TF)
        for half in range(2):
            dst = (2 * f + half) * TF
            o_ref[:, dst:dst + n] = w_ref[:, half * D_FF + f * TF:half * D_FF + f * TF + n].astype(bf16)
            if n < TF:
                o_ref[:, dst + n:dst + TF] = jnp.zeros((w_ref.shape[0], TF - n), bf16)


def _ffn_in_cast(w, layer, slot, nsteps, step_of):
    d = w.shape[2]
    rows = d // nsteps
    assert rows * nsteps == d and rows % 16 == 0
    return _SideCast(
        w,
        pl.BlockSpec((None, None, rows, 2 * D_FF), lambda *g: (layer, slot, step_of(*g), 0)),
        pl.BlockSpec((rows, 2 * D_FF_P), lambda *g: (step_of(*g), 0)),
        jax.ShapeDtypeStruct((d, 2 * D_FF_P), bf16),
        _cast_ffn_in_body)


def _cast_ffn_out_body(step, w_ref, o_ref):
    rows = w_ref.shape[0]
    row = step * rows + lax.broadcasted_iota(jnp.int32, (rows, 1), 0)
    o_ref[...] = jnp.where(row < D_FF, w_ref[...], 0.0).astype(bf16)


def _ffn_out_cast(w, layer, slot, nsteps, step_of):
    d = w.shape[3]
    rows = D_FF_P // nsteps
    assert rows * nsteps == D_FF_P and rows % 16 == 0 and rows * (nsteps - 1) < D_FF
    return _SideCast(
        w,
        pl.BlockSpec((None, None, rows, d), lambda *g: (layer, slot, step_of(*g), 0)),
        pl.BlockSpec((rows, d), lambda *g: (step_of(*g), 0)),
        jax.ShapeDtypeStruct((D_FF_P, d), bf16),
        _cast_ffn_out_body)


def _run_cast_body(w_ref, o_ref, *, cast):
    cast(pl.program_id(0), w_ref, o_ref)


def _run_cast(make_job, nsteps, name):
    job = make_job(nsteps, lambda r: r)
    return pl.pallas_call(
        functools.partial(_run_cast_body, cast=job.body),
        grid=(nsteps,),
        in_specs=[job.in_spec],
        out_specs=job.out_spec,
        out_shape=job.out_shape,
        compiler_params=_cparams(("parallel",)),
        name=name,
    )(job.array)


def _cast_w_stack_body(wb_ref, wo_ref, o_ref):
    is_branch = pl.program_id(1) < N_BRANCH

    @pl.when(is_branch)
    def _():
        o_ref[...] = wb_ref[...].astype(bf16)

    @pl.when(jnp.logical_not(is_branch))
    def _():
        o_ref[...] = wo_ref[...].astype(bf16)


def _cast_w_stack(w_branch, w_out):
    depth, nb, bw, d = w_branch.shape
    kparts = d // bw
    wo = w_out.reshape(depth, kparts, bw, d)
    blk = lambda index: pl.BlockSpec((None, None, bw, d), index)
    return pl.pallas_call(
        _cast_w_stack_body,
        grid=(depth, nb + kparts),
        in_specs=[blk(lambda l, j: (l, jnp.minimum(j, nb - 1), 0, 0)),
                  blk(lambda l, j: (l, jnp.maximum(j - nb, 0), 0, 0))],
        out_specs=blk(lambda l, j: (l, j, 0, 0)),
        out_shape=jax.ShapeDtypeStruct((depth, nb + kparts, bw, d), bf16),
        compiler_params=_cparams(("parallel", "arbitrary")),
        name="cast_w_stack",
    )(w_branch, wo)


def _prep_gate(w_gate):
    out = jnp.zeros((2, LANE, GLA_KW), f32)
    for d in range(2):
        out = out.at[d, d * GLA_RANK:(d + 1) * GLA_RANK].set(w_gate[d])
    return out.astype(bf16)


def kernel(x_prompt, x_sample, c, cache_na_k, cache_na_v, state_gla, c_ctx, w_mod, b_mod, norm_pre, norm_post,
           w_ffn_in, w_ffn_out, w_in, pool_w, pool_scale, na_rpb, gla_w_gate, gla_b_gate, gla_norm, w_branch,
           w_out):
    nb, seq, d = x_prompt.shape
    ndec, dseq, _ = x_sample.shape
    depth = w_mod.shape[0]
    n_ctx = nb * seq
    n_lat = ndec * dseq
    assert (seq, dseq, d) == (SEQ, DEC_SEQ, D_MODEL)
    assert n_ctx % (TM_MERGE * MERGE_SUB) == 0 and n_ctx % TM_FFN == 0 and dseq % TM_FFN == 0

    xs = [x_prompt.reshape(n_ctx, d), x_sample.reshape(n_lat, d)]
    conds = [(0, n_ctx), (1, dseq)]
    ncond = -(-(1 + ndec) // 8) * 8
    c_all = jnp.concatenate([c_ctx[None], c, jnp.zeros((ncond - 1 - ndec, d), f32)], axis=0)
    mod = _modulation(c_all, w_mod, b_mod)
    rope_tabs = _rope_tables(dseq)
    bias_tbl = _na_bias_table(na_rpb)

    in_cast = lambda l, s: functools.partial(_ffn_in_cast, w_ffn_in, l, s)
    out_cast = lambda l, s: functools.partial(_ffn_out_cast, w_ffn_out, l, s)
    ffn_w_in = {(0, 0): _run_cast(in_cast(0, 0), d // 256, "cast_ffn_in")}
    ffn_w_out = {(0, 0): _run_cast(out_cast(0, 0), D_FF_P // TF, "cast_ffn_out")}
    wt_in = jnp.swapaxes(w_in, 1, 2).reshape(depth * w_in.shape[2], d)
    w_in_cast = lambda l: functools.partial(_w_in_cast, wt_in, w_in.shape[2], l)
    w_in_p = {0: _run_cast(w_in_cast(0), N_IN_TILES, "cast_w_in")}
    w_stack_all = _cast_w_stack(w_branch, w_out)

    caches = None
    new_s = None
    for l in range(depth):
        mod_l = mod[l].reshape(ncond, 1, N_MOD * d)
        pre = norm_pre[l].reshape(3, 1, d)
        post = norm_post[l].reshape(3, 1, d)
        wgate_p = _prep_gate(gla_w_gate[l])
        pw = pool_w[l].astype(bf16)
        psc = pool_scale[l].reshape(1, POOL_WIDTH)
        later = [(l, 1)] + ([(l + 1, 0)] if l + 1 < depth else [])

        xs = [_ffn(x, mod_l, pre[0], post[0], ffn_w_in[l, 0], ffn_w_out[l, 0], 0, cond) for x, cond in zip(xs, conds)]
        p_ctx, p_lat = [_inproj(x, mod_l, pre[1], w_in_p[l], cond, dt) for x, cond, dt in zip(xs, conds, (f32, bf16))]

        y_pool = [_pool(p_ctx, pw, psc, seq, 0, nb), _pool(p_lat, pw, psc, dseq, 0, ndec)]
        na_ctx, new_k, new_v = _ctx_attn(p_ctx, nb, l, depth, caches)
        caches = (new_k, new_v)
        nxt = [l + 1] if l + 1 < depth else []
        on_na = nxt if ndec * NA_HEADS >= N_IN_TILES else []
        na_lat, *cast = _na_latent(p_lat, cache_na_k, cache_na_v, bias_tbl, l, 0, ndec,
                                   make_side=[in_cast(*ls) for ls in later] + [w_in_cast(j) for j in on_na])
        ffn_w_in.update(zip(later, cast))
        w_in_p.update(zip(on_na, cast[len(later):]))
        y_na = [na_ctx, na_lat]
        on_gla = [j for j in nxt if j not in on_na]
        g_ctx, new_s, *cast = _gla(p_ctx, wgate_p, gla_b_gate[l], gla_norm[l], seq, 0, nb, layer=l, with_sfin=True,
                                   depth=depth, carried=new_s, make_side=[w_in_cast(j) for j in on_gla])
        w_in_p.update(zip(on_gla, cast))
        g_lat, *cast = _gla(p_lat, wgate_p, gla_b_gate[l], gla_norm[l], dseq, 0, ndec, rope_tabs=rope_tabs,
                            state=state_gla, layer=l, make_side=[out_cast(*ls) for ls in later])
        ffn_w_out.update(zip(later, cast))
        y_gla = [g_ctx, g_lat]

        xs = [_merge(x, mod_l, post[1], yp, yn, yg, p, w_stack_all, l, cond)
              for x, yp, yn, yg, p, cond in zip(xs, y_pool, y_na, y_gla, (p_ctx, p_lat), conds)]
        xs = [_ffn(x, mod_l, pre[2], post[2], ffn_w_in[l, 1], ffn_w_out[l, 1], 2, cond) for x, cond in zip(xs, conds)]

    return (xs[0].reshape(nb, seq, d), xs[1].reshape(ndec, dseq, d), caches[0], caches[1], new_s)
```

```python
import functools
from typing import Any, Callable, NamedTuple

import numpy as np
import jax
import jax.numpy as jnp
from jax import lax
from jax.experimental import pallas as pl
from jax.experimental.pallas import tpu as pltpu

f32 = jnp.float32
bf16 = jnp.bfloat16

D_MODEL = 2048
SEQ = 256
DEC_SEQ = 2048
GRID_W = 64
N_MOD = 9
D_FF = 5504
FFN_RES = 0.5
EPS = 1e-6
NEG_INF = -1e30

POOL_GROUPS = 4
POOL_WINDOWS = (2, 4, 8, 16)
POOL_WIDTH = 1024
POOL_GC = POOL_WIDTH // POOL_GROUPS

NA_HEADS = 8
NA_HEAD_DIM = 128
NA_WIDTH = NA_HEADS * NA_HEAD_DIM
NA_WIN_H = 8
NA_WIN_W = 16

GLA_HEADS = 4
GLA_DK = 128
GLA_DV = 256
GLA_KW = GLA_HEADS * GLA_DK
GLA_VW = GLA_HEADS * GLA_DV
GLA_RANK = 16
GLA_TAU = 16.0
GLA_CHUNK = 64
ROPE_BASE = 10000.0

BRANCH_W = 1024
N_BRANCH = 3
GATE_W = N_BRANCH * D_MODEL

LANE = 128
VMEM_LIMIT = 56 * 1024 * 1024

OFF_GL = 0
OFF_POOL = OFF_GL + GATE_W
OFF_NQ = OFF_POOL + POOL_WIDTH
OFF_NK = OFF_NQ + NA_WIDTH
OFF_NV = OFF_NK + NA_WIDTH
OFF_GQ = OFF_NV + NA_WIDTH
OFF_GK = OFF_GQ + GLA_KW
OFF_GV = OFF_GK + GLA_KW
OFF_GR = OFF_GV + GLA_VW
OFF_GZ = OFF_GR + GLA_VW
TN_IN = 2304
IN_COLS_P = -(-(OFF_GZ + LANE) // TN_IN) * TN_IN

TM = 512
TF = 512
D_FF_P = -(-D_FF // TF) * TF


def _cparams(sem):
    return pltpu.CompilerParams(dimension_semantics=sem, vmem_limit_bytes=VMEM_LIMIT)


def _cond_index(i, tm, cond):
    return cond[0] + (i * tm) // cond[1]


ROW_CHUNK = 16


def _row_sweep(nrows, fn, unroll=4):
    def trip(i, carry):
        fn(pl.ds(pl.multiple_of(i * ROW_CHUNK, ROW_CHUNK), ROW_CHUNK))
        return carry

    lax.fori_loop(0, nrows // ROW_CHUNK, trip, 0, unroll=unroll)


def _row_rsqrt(x_ref, r_scr):
    n = x_ref.shape[1]

    def fn(rows):
        x = x_ref[rows, :]
        ss = jnp.sum(_lane_fold(x * x, jnp.add), axis=-1, keepdims=True)
        r_scr[rows, :] = jnp.broadcast_to(lax.rsqrt(ss * (1.0 / n) + EPS), (ROW_CHUNK, LANE))

    _row_sweep(x_ref.shape[0], fn, unroll=16)


def _lanes(r, n):
    return jnp.concatenate([r] * (n // LANE), axis=1)


def _norm_modulate(x_ref, mod_ref, g_ref, h_ref, r_scr):
    d = D_MODEL
    shift = mod_ref[:, 0:d]
    w = g_ref[...] * (1.0 + mod_ref[:, d:2 * d])
    _row_rsqrt(x_ref, r_scr)

    def fn(rows):
        h_ref[rows, :] = (x_ref[rows, :] * _lanes(r_scr[rows, :], d) * w + shift).astype(h_ref.dtype)

    _row_sweep(x_ref.shape[0], fn)


def _norm_gate_residual(y_ref, x_ref, mod_ref, g_ref, o_ref, r_scr, res_weight):
    d = D_MODEL
    w = (res_weight * mod_ref[:, 2 * d:3 * d]) * g_ref[...]
    _row_rsqrt(y_ref, r_scr)

    def fn(rows):
        o_ref[rows, :] = x_ref[rows, :] + y_ref[rows, :] * _lanes(r_scr[rows, :], d) * w

    _row_sweep(x_ref.shape[0], fn)


def _dot(a, b):
    return jnp.dot(a, b, preferred_element_type=f32)


def _dot_nt(a, b):
    return lax.dot_general(a, b, (((1,), (1,)), ((), ())), preferred_element_type=f32)


def _mod_body(step, c_ref, w_ref, b_ref, o_ref):
    del step
    c = c_ref[...]
    s = c * jax.nn.sigmoid(c)
    o_ref[...] = _dot(s.astype(bf16), w_ref[...].astype(bf16)) + b_ref[...]


def _mod_job(c_all, w_mod, b_mod, layer, nsteps, step_of):
    depth, d, n = w_mod.shape
    nc = c_all.shape[0]
    tn = n // nsteps
    assert tn * nsteps == n and tn % LANE == 0
    return _SideCast(
        (c_all, w_mod, b_mod.reshape(depth, 1, n)),
        (pl.BlockSpec((nc, d), lambda *g: (0, 0)),
         pl.BlockSpec((None, d, tn), lambda *g: (layer, 0, step_of(*g))),
         pl.BlockSpec((None, 1, tn), lambda *g: (layer, 0, step_of(*g)))),
        pl.BlockSpec((nc, tn), lambda *g: (0, step_of(*g))),
        jax.ShapeDtypeStruct((nc, n), f32),
        _mod_body)


TM_FFN = 1024


def _ffn_body(x_ref, mod_ref, pre_ref, post_ref, wgu_ref, wo_ref, o_ref, h_scr, r_scr):
    f = pl.program_id(1)

    @pl.when(f == 0)
    def _():
        _norm_modulate(x_ref, mod_ref, pre_ref, h_scr, r_scr)
        o_ref[...] = jnp.zeros_like(o_ref)

    h = h_scr[...]
    gu = _dot(h, wgu_ref[...])
    gt, up = gu[:, 0:TF], gu[:, TF:2 * TF]
    a = gt * jax.nn.sigmoid(gt) * up
    o_ref[...] += _dot(a.astype(bf16), wo_ref[...])

    @pl.when(f == pl.num_programs(1) - 1)
    def _():
        _norm_gate_residual(o_ref, x_ref, mod_ref, post_ref, o_ref, r_scr, FFN_RES)


def _ffn(x, mod_l, pre, post, w_gu, w_out2, sub, cond):
    t, d = x.shape
    fp = w_out2.shape[0]
    return pl.pallas_call(
        _ffn_body,
        grid=(t // TM_FFN, fp // TF),
        in_specs=[
            pl.BlockSpec((TM_FFN, d), lambda i, f: (i, 0)),
            pl.BlockSpec((None, 1, 3 * d), lambda i, f: (_cond_index(i, TM_FFN, cond), 0, sub)),
            pl.BlockSpec((1, d), lambda i, f: (0, 0)),
            pl.BlockSpec((1, d), lambda i, f: (0, 0)),
            pl.BlockSpec((d, 2 * TF), lambda i, f: (0, f)),
            pl.BlockSpec((TF, d), lambda i, f: (f, 0)),
        ],
        out_specs=pl.BlockSpec((TM_FFN, d), lambda i, f: (i, 0)),
        out_shape=jax.ShapeDtypeStruct((t, d), f32),
        scratch_shapes=[pltpu.VMEM((TM_FFN, d), bf16), pltpu.VMEM((TM_FFN, LANE), f32)],
        compiler_params=_cparams(("parallel", "arbitrary")),
        name="ffn",
    )(x, mod_l, pre, post, w_gu, w_out2)


def _inproj_body(x_ref, mod_ref, pre_ref, w_ref, o_ref, h_scr, r_scr):
    @pl.when(pl.program_id(1) == 0)
    def _():
        _norm_modulate(x_ref, mod_ref, pre_ref, h_scr, r_scr)

    o_ref[...] = _dot(h_scr[...], w_ref[...]).astype(o_ref.dtype)


def _inproj(x, mod_l, pre, w_in_p, cond, out_dtype):
    t, d = x.shape
    tm = TM * (4 // jnp.dtype(out_dtype).itemsize)
    n = w_in_p.shape[1]
    return pl.pallas_call(
        _inproj_body,
        grid=(t // tm, n // TN_IN),
        in_specs=[
            pl.BlockSpec((tm, d), lambda i, j: (i, 0)),
            pl.BlockSpec((None, 1, 3 * d), lambda i, j: (_cond_index(i, tm, cond), 0, 1)),
            pl.BlockSpec((1, d), lambda i, j: (0, 0)),
            pl.BlockSpec((d, TN_IN), lambda i, j: (0, j)),
        ],
        out_specs=pl.BlockSpec((tm, TN_IN), lambda i, j: (i, j)),
        out_shape=jax.ShapeDtypeStruct((t, n), out_dtype),
        scratch_shapes=[pltpu.VMEM((tm, d), bf16), pltpu.VMEM((tm, LANE), f32)],
        compiler_params=_cparams(("parallel", "arbitrary")),
        name="inproj",
    )(x, mod_l, pre, w_in_p)


POOL_PAD = 8


def _pool_body(u_ref, w_ref, sc_ref, o_ref, pad_scr, lvl_scr, *, seq):
    gc = POOL_GC
    pad = POOL_PAD
    n_lvl = seq + pad
    zeros = jnp.zeros((pad, POOL_WIDTH), f32)
    pad_scr[pl.ds(0, pad), :] = zeros
    pad_scr[pl.ds(pad + seq, pad), :] = zeros
    pad_scr[pl.ds(pad, seq), :] = u_ref[...].astype(f32)
    lvl_scr[:, pl.ds(n_lvl, pad), :] = jnp.zeros((2, pad, gc), f32)
    t = lax.broadcasted_iota(jnp.int32, (seq, 1), 0)
    for gi, win in enumerate(POOL_WINDOWS):
        cols = pl.ds(gi * gc, gc)
        read = lambda off, n: pad_scr[pl.ds(off, n), cols]
        k, slot = 1, 0
        while 2 * k < win:
            lvl_scr[slot, pl.ds(0, n_lvl), :] = read(0, n_lvl) + read(k, n_lvl)
            read = functools.partial(lambda s_, off, n: lvl_scr[s_, pl.ds(off, n), :], slot)
            k, slot = 2 * k, 1 - slot
        lo = jnp.maximum(t - win // 2, 0)
        hi = jnp.minimum(t + win - 1 - win // 2, seq - 1)
        inv_cnt = 1.0 / (hi - lo + 1).astype(f32)
        acc = read(pad - win // 2, seq) + read(pad, seq)
        pooled = acc * inv_cnt - pad_scr[pl.ds(pad, seq), cols]
        y = _dot(pooled.astype(bf16), w_ref[gi])
        o_ref[:, cols] = (y * sc_ref[:, cols]).astype(o_ref.dtype)


def _pool(p, pool_w, pool_scale, seq, row_block0, nseq):
    cb = OFF_POOL // POOL_WIDTH
    return pl.pallas_call(
        functools.partial(_pool_body, seq=seq),
        grid=(nseq,),
        in_specs=[
            pl.BlockSpec((seq, POOL_WIDTH), lambda s: (row_block0 + s, cb)),
            pl.BlockSpec((POOL_GROUPS, POOL_GC, POOL_GC), lambda s: (0, 0, 0)),
            pl.BlockSpec((1, POOL_WIDTH), lambda s: (0, 0)),
        ],
        out_specs=pl.BlockSpec((seq, POOL_WIDTH), lambda s: (s, 0)),
        out_shape=jax.ShapeDtypeStruct((nseq * seq, POOL_WIDTH), bf16),
        scratch_shapes=[pltpu.VMEM((seq + 2 * POOL_PAD, POOL_WIDTH), f32),
                        pltpu.VMEM((2, seq + 2 * POOL_PAD, POOL_GC), f32)],
        compiler_params=_cparams(("parallel",)),
        name="pool",
    )(p, pool_w, pool_scale)


def _lane_fold(x, op):
    parts = [x[:, i * LANE:(i + 1) * LANE] for i in range(x.shape[1] // LANE)]
    while len(parts) > 1:
        parts = [op(parts[i], parts[i + 1]) for i in range(0, len(parts) - 1, 2)] + parts[len(parts) & ~1:]
    return parts[0]


def _softmax_rows(s):
    m = jnp.max(_lane_fold(s, jnp.maximum), axis=-1, keepdims=True)
    e = jnp.exp(s - m)
    return e / jnp.sum(_lane_fold(e, jnp.add), axis=-1, keepdims=True)


def _ctx_attn_body(q_ref, k_ref, v_ref, *refs, n_alias, side, layer):
    refs = refs[n_alias:]
    n_src = sum(n_in for _, n_in in side)
    o_ref, nk_ref, nv_ref = refs[n_src:n_src + 3]
    for job, src_refs, dst_ref in _side_refs(side, refs[:n_src], refs[n_src + 3:]):
        job(pl.program_id(0), *src_refs, dst_ref)
    if not n_alias:
        for ref in (nk_ref, nv_ref):
            for other in range(ref.shape[0]):
                if other != layer:
                    ref[other] = jnp.zeros(ref.shape[1:], ref.dtype)
        nk_ref, nv_ref = nk_ref.at[layer], nv_ref.at[layer]
    hd = NA_HEAD_DIM
    for h in range(NA_HEADS):
        cols = pl.ds(h * hd, hd)
        kf = k_ref[:, cols]
        vf = v_ref[:, cols]
        nk_ref[h] = kf
        nv_ref[h] = vf
        p = _softmax_rows(_dot_nt(q_ref[:, cols].astype(bf16), kf.astype(bf16)) * (hd ** -0.5))
        o_ref[:, cols] = _dot(p.astype(bf16), vf.astype(bf16)).astype(o_ref.dtype)


def _ctx_attn(p, nseq, layer, depth, caches=None, make_side=()):
    spec = lambda off: pl.BlockSpec((SEQ, NA_WIDTH), lambda b: (b, off // NA_WIDTH))
    cache_shape = jax.ShapeDtypeStruct((nseq, depth, NA_HEADS, SEQ, NA_HEAD_DIM), f32)
    in_specs = [spec(OFF_NQ), spec(OFF_NK), spec(OFF_NV)]
    args = [p, p, p]
    aliases = {}
    if caches is None:
        cache_spec = pl.BlockSpec((None, depth, NA_HEADS, SEQ, NA_HEAD_DIM), lambda b: (b, 0, 0, 0, 0))
    else:
        cache_spec = pl.BlockSpec((None, None, NA_HEADS, SEQ, NA_HEAD_DIM), lambda b: (b, layer, 0, 0, 0))
        in_specs += [pl.BlockSpec(memory_space=pl.ANY)] * 2
        args += list(caches)
        aliases = {3: 1, 4: 2}
    side_in, side_args, side_out, side_shape, side = _side_io([make(nseq, lambda b: b) for make in make_side])
    return pl.pallas_call(
        functools.partial(_ctx_attn_body, n_alias=len(aliases), side=side, layer=layer),
        grid=(nseq,),
        in_specs=in_specs + side_in,
        out_specs=[pl.BlockSpec((SEQ, NA_WIDTH), lambda b: (b, 0)), cache_spec, cache_spec] + side_out,
        out_shape=[jax.ShapeDtypeStruct((nseq * SEQ, NA_WIDTH), bf16), cache_shape, cache_shape] + side_shape,
        input_output_aliases=aliases,
        compiler_params=_cparams(("arbitrary",)),
        name="ctx_attn",
    )(*args, *side_args)


LOG2E = float(np.log2(np.e))


def _na_bias_table(rpb):
    qc = np.arange(GRID_W)[:, None]
    kc = np.arange(GRID_W)[None, :]
    cs = np.clip(qc - NA_WIN_W // 2, 0, GRID_W - NA_WIN_W)
    ok = (kc >= cs) & (kc < cs + NA_WIN_W)
    cidx = np.clip(kc - qc + NA_WIN_W - 1, 0, 2 * NA_WIN_W - 2)
    onehot = jnp.asarray((cidx[None] == np.arange(2 * NA_WIN_W - 1)[:, None, None]) & ok[None], f32)
    toep = jnp.einsum('...rc,cqk->...rqk', rpb.astype(f32), onehot, precision=lax.Precision.HIGHEST)
    toep = jnp.where(ok, toep * LOG2E, NEG_INF)
    return jnp.concatenate([toep[..., :-1, :, :], toep[..., 1:, :, :]], axis=-1)


NA_UNROLL = 16
NA_SOFTMAX_UNROLL = 8
NA_CTX_ROWS = 256


def _na_body(*refs, rows, side):
    n_src = sum(n_in for _, n_in in side)
    qb_scr, kb_scr, vb_scr, ck_ref, cv_ref, bias_ref = refs[:6]
    o_ref = refs[6 + n_src]
    sl_scr, sc_scr, el_scr, ec_scr, den_scr, oc_scr = refs[7 + n_src + len(side):]
    step = pl.program_id(0) * pl.num_programs(1) + pl.program_id(1)
    for job, src_refs, dst_ref in _side_refs(side, refs[6:6 + n_src], refs[7 + n_src:7 + n_src + len(side)]):
        job(step, *src_refs, dst_ref)

    hd = NA_HEAD_DIM
    scale = hd ** -0.5 * LOG2E
    kh = min(NA_WIN_H, rows)
    nloc = kh * GRID_W
    n = rows * GRID_W
    assert qb_scr.dtype == bf16
    ck = ck_ref[...].astype(bf16)
    cv = cv_ref[...].astype(bf16)

    def row_slices(r):
        rs = jnp.clip(r - kh // 2, 0, rows - kh)
        q_rows = pl.ds(pl.multiple_of(r * GRID_W, GRID_W), GRID_W)
        k_rows = pl.ds(pl.multiple_of(rs * GRID_W, GRID_W), nloc)
        return rs, q_rows, k_rows

    def ctx_scores(i, carry):
        blk = pl.ds(pl.multiple_of(i * NA_CTX_ROWS, NA_CTX_ROWS), NA_CTX_ROWS)
        sc_scr[blk, :] = _dot_nt(qb_scr[blk, :], ck) * scale
        return carry

    lax.fori_loop(0, n // NA_CTX_ROWS, ctx_scores, 0, unroll=2)

    def loc_scores(r, carry):
        rs, q_rows, k_rows = row_slices(r)
        first = rs - r + NA_WIN_H - 1
        bias = jnp.concatenate([bias_ref[first + 2 * e] for e in range(kh // 2)], axis=1)
        sl_scr[q_rows, :] = _dot_nt(qb_scr[q_rows, :], kb_scr[k_rows, :]) * scale + bias
        return carry

    lax.fori_loop(0, rows, loc_scores, 0, unroll=NA_UNROLL)

    def numerators(r, carry):
        q_rows = pl.ds(pl.multiple_of(r * GRID_W, GRID_W), GRID_W)
        s_loc = sl_scr[q_rows, :]
        s_ctx = sc_scr[q_rows, :]
        m = jnp.max(jnp.maximum(_lane_fold(s_loc, jnp.maximum), _lane_fold(s_ctx, jnp.maximum)),
                    axis=-1, keepdims=True)
        e_loc = jnp.exp2(s_loc - m)
        e_ctx = jnp.exp2(s_ctx - m)
        den = jnp.sum(_lane_fold(e_loc, jnp.add) + _lane_fold(e_ctx, jnp.add), axis=-1, keepdims=True)
        el_scr[q_rows, :] = e_loc.astype(bf16)
        ec_scr[q_rows, :] = e_ctx.astype(bf16)
        den_scr[q_rows, :] = jnp.broadcast_to(den, (GRID_W, hd))
        return carry

    lax.fori_loop(0, rows, numerators, 0, unroll=NA_SOFTMAX_UNROLL)

    def ctx_values(i, carry):
        blk = pl.ds(pl.multiple_of(i * NA_CTX_ROWS, NA_CTX_ROWS), NA_CTX_ROWS)
        oc_scr[blk, :] = _dot(ec_scr[blk, :], cv)
        return carry

    lax.fori_loop(0, n // NA_CTX_ROWS, ctx_values, 0, unroll=2)

    def loc_values(r, carry):
        _, q_rows, k_rows = row_slices(r)
        o = _dot(el_scr[q_rows, :], vb_scr[k_rows, :]) + oc_scr[q_rows, :]
        o_ref[q_rows, :] = (o / den_scr[q_rows, :]).astype(o_ref.dtype)
        return carry

    lax.fori_loop(0, rows, loc_values, 0, unroll=NA_UNROLL)


def _na_latent(p, cache_k, cache_v, bias_tbl, layer, row_block0, nreq, make_side=()):
    n = DEC_SEQ
    hd = NA_HEAD_DIM
    past = cache_k.shape[3]
    rows = n // GRID_W
    side_in, side_args, side_out, side_shape, side = _side_io(
        [make(nreq * NA_HEADS, lambda b, h: b * NA_HEADS + h) for make in make_side])
    qkv = lambda off: pl.BlockSpec((n, hd), lambda b, h: (row_block0 + b, off // hd + h))
    cache = pl.BlockSpec((None, None, None, past, hd), lambda b, h: (b, layer, h, 0, 0))
    return pl.pallas_call(
        functools.partial(_na_body, rows=rows, side=side),
        grid=(nreq, NA_HEADS),
        in_specs=[qkv(OFF_NQ), qkv(OFF_NK), qkv(OFF_NV), cache, cache,
                  pl.BlockSpec((None, None, 2 * NA_WIN_H - 2, GRID_W, 2 * GRID_W),
                               lambda b, h: (layer, h, 0, 0, 0))] + side_in,
        out_specs=[pl.BlockSpec((n, hd), lambda b, h: (b, h))] + side_out,
        out_shape=[jax.ShapeDtypeStruct((nreq * n, NA_WIDTH), bf16)] + side_shape,
        scratch_shapes=[pltpu.VMEM((n, NA_WIN_H * GRID_W), f32), pltpu.VMEM((n, past), f32),
                        pltpu.VMEM((n, NA_WIN_H * GRID_W), bf16), pltpu.VMEM((n, past), bf16),
                        pltpu.VMEM((n, hd), f32), pltpu.VMEM((n, hd), f32)],
        compiler_params=_cparams(("arbitrary", "arbitrary")),
        name="na_latent",
    )(p, p, p, cache_k, cache_v, bias_tbl, *side_args)


GLA_PAD = 32
GLA_UNROLL = 16


def _rope_tables(seq):
    t = np.arange(seq)
    half = GLA_DK // 2
    nf = half // 2
    inv = ROPE_BASE ** (-np.arange(nf, dtype=np.float64) / nf)
    cos, sin = [], []
    for pos in (t // GRID_W, t % GRID_W):
        ang = pos[:, None].astype(np.float64) * inv
        cos += [np.cos(ang), np.cos(ang)]
        sin += [-np.sin(ang), np.sin(ang)]
    return (jnp.asarray(np.concatenate(cos, axis=-1), f32), jnp.asarray(np.concatenate(sin, axis=-1), f32))


def _rope(x, cos, sin_signed):
    nf = GLA_DK // 4
    lane = lax.broadcasted_iota(jnp.int32, x.shape, 1)
    partner = jnp.where(lane % (2 * nf) < nf, pltpu.roll(x, GLA_DK - nf, 1), pltpu.roll(x, nf, 1))
    return x * cos + partner * sin_signed


def _log_sigmoid(x):
    return jnp.minimum(x, 0.0) - jnp.log1p(jnp.exp(-jnp.abs(x)))


def _gla_body(*refs, seq, rope, with_s0, with_sfin, sfin_layer, n_carried, side):
    refs = list(refs)
    q_ref, k_ref, v_ref, r_ref, z_ref, wg_ref, bg_ref, ng_ref = refs[:8]
    refs = refs[8:]
    if rope:
        cos_ref, sin_ref = refs[:2]
        refs = refs[2:]
    if with_s0:
        s0_ref = refs[0]
        refs = refs[1:]
    refs = refs[n_carried:]
    n_src = sum(n_in for _, n_in in side)
    side_src, refs = refs[:n_src], refs[n_src:]
    o_ref = refs[0]
    refs = refs[1:]
    if with_sfin:
        sfin_ref = refs[0]
        refs = refs[1:]
    side_dst, refs = refs[:len(side)], refs[len(side):]
    step = pl.program_id(0) * pl.num_programs(1) + pl.program_id(1)
    for job, src_refs, dst_ref in _side_refs(side, side_src, side_dst):
        job(step, *src_refs, dst_ref)
    qi_scr, kn_scr, kd_scr, dec_scr, scan_scr, vb_scr, u_scr, sb_scr, o_scr, st_scr = refs

    ch = GLA_CHUNK
    nch = seq // ch
    dk, dv = GLA_DK, GLA_DV

    q = q_ref[...].astype(f32)
    k = k_ref[...].astype(f32)
    if rope:
        q = _rope(q, cos_ref[...], sin_ref[...])
        k = _rope(k, cos_ref[...], sin_ref[...])
    q = q * (dk ** -0.5)

    zb = z_ref[...].astype(bf16)
    pos = lax.broadcasted_iota(jnp.int32, (seq, 1), 0) % ch
    zpad = jnp.zeros((GLA_PAD, dk), f32)
    scan_scr[pl.ds(0, GLA_PAD), :] = zpad
    scan_scr[pl.ds(GLA_PAD + seq, GLA_PAD), :] = zpad
    for d in range(2):
        g = _log_sigmoid(_dot(zb, wg_ref[d]) + bg_ref[d]) / GLA_TAU
        b = g
        sh = 1
        while sh < ch:
            scan_scr[pl.ds(GLA_PAD, seq), :] = b
            if d == 0:
                b = b + jnp.where(pos >= sh, scan_scr[pl.ds(GLA_PAD - sh, seq), :], 0.0)
            else:
                b = b + jnp.where(pos < ch - sh, scan_scr[pl.ds(GLA_PAD + sh, seq), :], 0.0)
            sh *= 2
        b3 = b.reshape(nch, ch, dk)
        b_end = b3[:, ch - 1:ch, :] if d == 0 else b3[:, 0:1, :]
        lanes = pl.ds(d * dk, dk)
        qi_scr[:, lanes] = (q * jnp.exp(b)).astype(bf16)
        kn_scr[d] = (k * jnp.exp(-b)).astype(bf16)
        kd_scr[:, lanes] = (k.reshape(nch, ch, dk) * jnp.exp(b_end - b3)).reshape(seq, dk).astype(bf16)
        dec_scr[d] = jnp.exp(b_end)

    for d in range(2):
        if with_s0:
            st_scr[d] = s0_ref[d].T
        else:
            st_scr[d] = jnp.zeros((dv, dk), f32)

    ri = lax.broadcasted_iota(jnp.int32, (ch, ch), 0)
    ci = lax.broadcasted_iota(jnp.int32, (ch, ch), 1)

    vb_scr[...] = v_ref[...].astype(bf16)
    chunk_rows = lambda c: pl.ds(pl.multiple_of(c * ch, ch), ch)
    fwd, bwd = pl.ds(0, dk), pl.ds(dk, dk)

    def increments(c, carry):
        rows = chunk_rows(c)
        u_scr[c] = lax.dot_general(vb_scr[rows, :], kd_scr[rows, :], (((0,), (0,)), ((), ())),
                                   preferred_element_type=f32)
        return carry

    lax.fori_loop(0, nch, increments, 0, unroll=min(GLA_UNROLL, nch))

    def states(i, carry):
        for d, c, lanes in ((0, i, fwd), (1, nch - 1 - i, bwd)):
            st = st_scr[d]
            sb_scr[c, :, lanes] = st.astype(bf16)
            st_scr[d] = st * dec_scr[d, c] + u_scr[c, :, lanes]
        return carry

    lax.fori_loop(0, nch, states, 0)

    if with_sfin:
        if not n_carried:
            for other in range(sfin_ref.shape[0]):
                if other != sfin_layer:
                    sfin_ref[other] = jnp.zeros(sfin_ref.shape[1:], sfin_ref.dtype)
            sfin_ref = sfin_ref.at[sfin_layer]
        for d in range(2):
            sfin_ref[d] = st_scr[d].T

    def outputs(c, carry):
        rows = chunk_rows(c)
        qi = qi_scr[rows, :]
        pf = _dot_nt(qi[:, 0:dk], kn_scr[0, rows, :])
        pb = _dot_nt(qi[:, dk:2 * dk], kn_scr[1, rows, :])
        a = jnp.where(ci < ri, pf, jnp.where(ci > ri, pb, pf + pb))
        o_scr[rows, :] = _dot(a.astype(bf16), vb_scr[rows, :]) + _dot_nt(qi, sb_scr[c])
        return carry

    lax.fori_loop(0, nch, outputs, 0, unroll=min(GLA_UNROLL, nch))

    o = o_scr[...]
    r = r_ref[...].astype(f32)
    o = o * lax.rsqrt(jnp.mean(o * o, axis=-1, keepdims=True) + EPS) * ng_ref[...]
    o_ref[...] = (o * (r * jax.nn.sigmoid(r))).astype(o_ref.dtype)


def _gla(p, wgate_p, b_gate, gla_norm, seq, row_block0, nreq, rope_tabs=None, state=None, layer=0,
         with_sfin=False, depth=1, carried=None, make_side=()):
    dk, dv = GLA_DK, GLA_DV
    nch = seq // GLA_CHUNK
    rope = rope_tabs is not None
    with_s0 = state is not None
    blk = lambda w, off: pl.BlockSpec((seq, w), lambda b, h: (row_block0 + b, off // w + h))
    in_specs = [blk(dk, OFF_GQ), blk(dk, OFF_GK), blk(dv, OFF_GV), blk(dv, OFF_GR),
                pl.BlockSpec((seq, LANE), lambda b, h: (row_block0 + b, OFF_GZ // LANE)),
                pl.BlockSpec((2, LANE, dk), lambda b, h: (0, 0, h)),
                pl.BlockSpec((2, 1, dk), lambda b, h: (0, 0, h)),
                pl.BlockSpec((1, dv), lambda b, h: (0, h))]
    args = [p, p, p, p, p, wgate_p, b_gate.reshape(2, 1, GLA_KW), gla_norm.reshape(1, GLA_VW)]
    if rope:
        in_specs += [pl.BlockSpec((seq, dk), lambda b, h: (0, 0))] * 2
        args += list(rope_tabs)
    if with_s0:
        in_specs.append(pl.BlockSpec((None, None, 2, None, dk, dv), lambda b, h: (b, layer, 0, h, 0, 0)))
        args.append(state)
    aliases = {}
    if carried is not None:
        aliases = {len(args): 1}
        in_specs.append(pl.BlockSpec(memory_space=pl.ANY))
        args.append(carried)
    side_in, side_args, side_out, side_shape, side = _side_io(
        [make(nreq * GLA_HEADS, lambda b, h: b * GLA_HEADS + h) for make in make_side])
    in_specs += side_in
    args += side_args
    out_specs = [pl.BlockSpec((seq, dv), lambda b, h: (b, h))]
    out_shape = [jax.ShapeDtypeStruct((nreq * seq, GLA_VW), bf16)]
    if with_sfin:
        if carried is None:
            out_specs.append(pl.BlockSpec((None, depth, 2, None, dk, dv), lambda b, h: (b, 0, 0, h, 0, 0)))
        else:
            out_specs.append(pl.BlockSpec((None, None, 2, None, dk, dv), lambda b, h: (b, layer, 0, h, 0, 0)))
        out_shape.append(jax.ShapeDtypeStruct((nreq, depth, 2, GLA_HEADS, dk, dv), f32))
    out_specs += side_out
    out_shape += side_shape
    scratch = [pltpu.VMEM((seq, 2 * dk), bf16), pltpu.VMEM((2, seq, dk), bf16), pltpu.VMEM((seq, 2 * dk), bf16),
               pltpu.VMEM((2, nch, 1, dk), f32), pltpu.VMEM((seq + 2 * GLA_PAD, dk), f32),
               pltpu.VMEM((seq, dv), bf16), pltpu.VMEM((nch, dv, 2 * dk), f32), pltpu.VMEM((nch, dv, 2 * dk), bf16),
               pltpu.VMEM((seq, dv), f32), pltpu.VMEM((2, dv, dk), f32)]
    return pl.pallas_call(
        functools.partial(_gla_body, seq=seq, rope=rope, with_s0=with_s0, with_sfin=with_sfin,
                          sfin_layer=layer, n_carried=len(aliases), side=side),
        grid=(nreq, GLA_HEADS),
        in_specs=in_specs,
        out_specs=out_specs,
        out_shape=out_shape,
        input_output_aliases=aliases,
        scratch_shapes=scratch,
        compiler_params=_cparams(("arbitrary", "arbitrary")),
        name="gla",
    )(*args)


TM_MERGE = 512
MERGE_SUB = 2


def _merge_body(x_ref, mod_ref, post_ref, bp_ref, bn_ref, bg_ref, gl_ref, w_ref, o_ref, m_scr, mb_scr, r_scr):
    n = pl.program_id(1)
    m_ref = m_scr.at[pl.program_id(2)]
    mb_ref = mb_scr.at[pl.program_id(2)]

    for bi, br_ref in enumerate((bp_ref, bn_ref, bg_ref)):
        @pl.when(n == bi)
        def _():
            y = jax.nn.sigmoid(gl_ref[...].astype(f32)) * _dot(br_ref[...], w_ref[...])
            m_ref[...] = y if bi == 0 else m_ref[...] + y

    @pl.when(n == N_BRANCH)
    def _():
        mb_ref[...] = m_ref[...].astype(bf16)
        m_ref[...] = _dot(mb_ref[:, 0:BRANCH_W], w_ref[...])

    @pl.when(n == N_BRANCH + 1)
    def _():
        m_ref[...] += _dot(mb_ref[:, BRANCH_W:2 * BRANCH_W], w_ref[...])
        _norm_gate_residual(m_ref, x_ref, mod_ref, post_ref, o_ref, r_scr, 1.0)


def _merge(x, mod_l, post, y_pool, y_na, y_gla, p, w_stack, cond):
    t, d = x.shape
    tm = TM_MERGE
    sub = MERGE_SUB
    nsteps = N_BRANCH + d // BRANCH_W
    tile = lambda i, s: i * sub + s

    def rows_at(first, last):
        def index(i, n, s):
            return jnp.where(n < first, jnp.maximum(tile(i, 0) - 1, 0),
                             jnp.where(n > last, tile(i, sub - 1), tile(i, s)))
        return index

    last = nsteps - 1
    br = lambda step: pl.BlockSpec((tm, BRANCH_W), lambda i, n, s: (rows_at(step, step)(i, n, s), 0))
    xo = pl.BlockSpec((tm, d), lambda i, n, s: (rows_at(last, last)(i, n, s), 0))
    return pl.pallas_call(
        _merge_body,
        grid=(t // (tm * sub), nsteps, sub),
        in_specs=[
            xo,
            pl.BlockSpec((None, 1, 3 * d), lambda i, n, s: (_cond_index(tile(i, s), tm, cond), 0, 1)),
            pl.BlockSpec((1, d), lambda i, n, s: (0, 0)),
            br(0), br(1), br(2),
            pl.BlockSpec((tm, d),
                         lambda i, n, s: (rows_at(0, N_BRANCH - 1)(i, n, s), jnp.minimum(n, N_BRANCH - 1))),
            pl.BlockSpec((BRANCH_W, d), lambda i, n, s: (n, 0)),
        ],
        out_specs=xo,
        out_shape=jax.ShapeDtypeStruct((t, d), f32),
        scratch_shapes=[pltpu.VMEM((sub, tm, d), f32), pltpu.VMEM((sub, tm, d), bf16), pltpu.VMEM((tm, LANE), f32)],
        compiler_params=_cparams(("arbitrary", "arbitrary", "arbitrary")),
        name="merge",
    )(x, mod_l, post, y_pool, y_na, y_gla, p, w_stack)


_IN_SPLITS = (POOL_WIDTH, NA_WIDTH, NA_WIDTH, NA_WIDTH, GLA_KW, GLA_KW, GLA_VW, 2 * GLA_RANK, GLA_VW, GATE_W)
_IN_OFFS = tuple(int(v) for v in np.cumsum((0,) + _IN_SPLITS))
_IN_RUNS = ((_IN_OFFS[9], _IN_OFFS[10]), (_IN_OFFS[0], _IN_OFFS[7]), (_IN_OFFS[8], _IN_OFFS[9]),
            (_IN_OFFS[7], _IN_OFFS[8]))


CT_IN = 512
_IN_TILE_STARTS = []
for _a, _b in _IN_RUNS:
    _IN_TILE_STARTS += [_a + CT_IN * _t for _t in range(-(-(_b - _a) // CT_IN))]
assert len(_IN_TILE_STARTS) * CT_IN == IN_COLS_P and all(v % 8 == 0 for v in _IN_TILE_STARTS)
_IN_LAST_VALID = (_IN_RUNS[-1][1] - _IN_RUNS[-1][0]) % CT_IN or CT_IN
assert all((b - a) % CT_IN == 0 for a, b in _IN_RUNS[:-1])


def _w_in_tile_start(j):
    out = jnp.int32(_IN_TILE_STARTS[0]) + CT_IN * j
    for t in range(1, len(_IN_TILE_STARTS)):
        if _IN_TILE_STARTS[t] != _IN_TILE_STARTS[t - 1] + CT_IN:
            out = jnp.where(j >= t, _IN_TILE_STARTS[t] + CT_IN * (j - t), out)
    return out


N_IN_TILES = IN_COLS_P // CT_IN


def _cast_w_in_body(step, w_ref, o_ref):
    @pl.when(step < N_IN_TILES)
    def _():
        col = lax.broadcasted_iota(jnp.int32, (1, CT_IN), 1)
        valid = jnp.where(step == N_IN_TILES - 1, _IN_LAST_VALID, CT_IN)
        o_ref[...] = jnp.where(col < valid, w_ref[...].T, 0.0).astype(bf16)


def _w_in_cast(wt, n_cols, layer, nsteps, step_of):
    assert nsteps >= N_IN_TILES
    d = wt.shape[1]
    tile = lambda *g: jnp.minimum(step_of(*g), N_IN_TILES - 1)
    return _SideCast(
        (wt,),
        (pl.BlockSpec((pl.Element(CT_IN), pl.Element(d)),
                      lambda *g: (pl.multiple_of(layer * n_cols + _w_in_tile_start(tile(*g)), 8), 0)),),
        pl.BlockSpec((d, CT_IN), lambda *g: (0, tile(*g))),
        jax.ShapeDtypeStruct((d, IN_COLS_P), bf16),
        _cast_w_in_body)


class _SideCast(NamedTuple):
    arrays: tuple
    in_specs: tuple
    out_spec: pl.BlockSpec
    out_shape: jax.ShapeDtypeStruct
    body: Callable


def _side_refs(side, src_refs, dst_refs):
    jobs, k = [], 0
    for (body, n_in), dst in zip(side, dst_refs):
        jobs.append((body, src_refs[k:k + n_in], dst))
        k += n_in
    return jobs


def _side_io(side):
    return ([sp for j in side for sp in j.in_specs], [a for j in side for a in j.arrays],
            [j.out_spec for j in side], [j.out_shape for j in side], tuple((j.body, len(j.arrays)) for j in side))


def _cast_ffn_in_body(step, w_ref, o_ref):
    del step
    for f in range(D_FF_P // TF):
        n = min(TF, D_FF - f * TF)
        for half in range(2):
            dst = (2 * f + half) * TF
            o_ref[:, dst:dst + n] = w_ref[:, half * D_FF + f * TF:half * D_FF + f * TF + n].astype(bf16)
            if n < TF:
                o_ref[:, dst + n:dst + TF] = jnp.zeros((w_ref.shape[0], TF - n), bf16)


def _ffn_in_cast(w, layer, slot, nsteps, step_of):
    d = w.shape[2]
    rows = d // nsteps
    assert rows * nsteps == d and rows % 16 == 0
    return _SideCast(
        (w,),
        (pl.BlockSpec((None, None, rows, 2 * D_FF), lambda *g: (layer, slot, step_of(*g), 0)),),
        pl.BlockSpec((rows, 2 * D_FF_P), lambda *g: (step_of(*g), 0)),
        jax.ShapeDtypeStruct((d, 2 * D_FF_P), bf16),
        _cast_ffn_in_body)


def _cast_ffn_out_body(step, w_ref, o_ref):
    rows = w_ref.shape[0]
    row = step * rows + lax.broadcasted_iota(jnp.int32, (rows, 1), 0)
    o_ref[...] = jnp.where(row < D_FF, w_ref[...], 0.0).astype(bf16)


def _ffn_out_cast(w, layer, slot, nsteps, step_of):
    d = w.shape[3]
    rows = D_FF_P // nsteps
    assert rows * nsteps == D_FF_P and rows % 16 == 0 and rows * (nsteps - 1) < D_FF
    return _SideCast(
        (w,),
        (pl.BlockSpec((None, None, rows, d), lambda *g: (layer, slot, step_of(*g), 0)),),
        pl.BlockSpec((rows, d), lambda *g: (step_of(*g), 0)),
        jax.ShapeDtypeStruct((D_FF_P, d), bf16),
        _cast_ffn_out_body)


def _run_cast_body(*refs, cast):
    cast(pl.program_id(0), *refs)


def _run_cast(make_job, nsteps, name):
    job = make_job(nsteps, lambda r: r)
    return pl.pallas_call(
        functools.partial(_run_cast_body, cast=job.body),
        grid=(nsteps,),
        in_specs=list(job.in_specs),
        out_specs=job.out_spec,
        out_shape=job.out_shape,
        compiler_params=_cparams(("parallel",)),
        name=name,
    )(*job.arrays)


W_STACK_ROWS = 256


def _cast_w_stack_body(step, wb_ref, wo_ref, o_ref, *, n_branch, n_out):
    @pl.when(step < n_branch)
    def _():
        o_ref[...] = wb_ref[...].astype(bf16)

    @pl.when(jnp.logical_and(step >= n_branch, step < n_branch + n_out))
    def _():
        o_ref[...] = wo_ref[...].astype(bf16)


def _w_stack_cast(w_branch, w_out, layer, nsteps, step_of):
    depth, nb, bw, d = w_branch.shape
    r = W_STACK_ROWS
    n_branch, n_out = nb * bw // r, d // r
    assert nsteps >= n_branch + n_out and bw % r == 0
    t = lambda *g: step_of(*g)
    return _SideCast(
        (w_branch.reshape(depth * nb * bw, d), w_out.reshape(depth * d, d)),
        (pl.BlockSpec((r, d), lambda *g: (layer * n_branch + jnp.minimum(t(*g), n_branch - 1), 0)),
         pl.BlockSpec((r, d), lambda *g: (layer * n_out + jnp.clip(t(*g) - n_branch, 0, n_out - 1), 0))),
        pl.BlockSpec((r, d), lambda *g: (jnp.minimum(t(*g), n_branch + n_out - 1), 0)),
        jax.ShapeDtypeStruct((nb * bw + d, d), bf16),
        functools.partial(_cast_w_stack_body, n_branch=n_branch, n_out=n_out))


def _prep_gate(w_gate):
    out = jnp.zeros((2, LANE, GLA_KW), f32)
    for d in range(2):
        out = out.at[d, d * GLA_RANK:(d + 1) * GLA_RANK].set(w_gate[d])
    return out.astype(bf16)


def kernel(x_prompt, x_sample, c, cache_na_k, cache_na_v, state_gla, c_ctx, w_mod, b_mod, norm_pre, norm_post,
           w_ffn_in, w_ffn_out, w_in, pool_w, pool_scale, na_rpb, gla_w_gate, gla_b_gate, gla_norm, w_branch,
           w_out):
    nb, seq, d = x_prompt.shape
    ndec, dseq, _ = x_sample.shape
    depth = w_mod.shape[0]
    n_ctx = nb * seq
    n_lat = ndec * dseq
    assert (seq, dseq, d) == (SEQ, DEC_SEQ, D_MODEL)
    assert n_ctx % (TM_MERGE * MERGE_SUB) == 0 and n_ctx % TM_FFN == 0 and dseq % TM_FFN == 0

    xs = [x_prompt.reshape(n_ctx, d), x_sample.reshape(n_lat, d)]
    conds = [(0, n_ctx), (1, dseq)]
    ncond = -(-(1 + ndec) // 8) * 8
    c_all = jnp.concatenate([c_ctx[None], c, jnp.zeros((ncond - 1 - ndec, d), f32)], axis=0)
    rope_tabs = _rope_tables(dseq)
    bias_tbl = _na_bias_table(na_rpb)

    mod_job = lambda l: functools.partial(_mod_job, c_all, w_mod, b_mod, l)
    in_cast = lambda l, s: functools.partial(_ffn_in_cast, w_ffn_in, l, s)
    out_cast = lambda l, s: functools.partial(_ffn_out_cast, w_ffn_out, l, s)
    wt_in = jnp.swapaxes(w_in, 1, 2).reshape(depth * w_in.shape[2], d)
    w_in_cast = lambda l: functools.partial(_w_in_cast, wt_in, w_in.shape[2], l)
    w_stack_cast = lambda l: functools.partial(_w_stack_cast, w_branch, w_out, l)
    mods = {0: _run_cast(mod_job(0), N_MOD * d // 1024, "modulation")}
    ffn_w_in = {(0, 0): _run_cast(in_cast(0, 0), d // 256, "cast_ffn_in")}
    ffn_w_out = {(0, 0): _run_cast(out_cast(0, 0), D_FF_P // TF, "cast_ffn_out")}
    w_in_p = {0: _run_cast(w_in_cast(0), N_IN_TILES, "cast_w_in")}

    caches = None
    new_s = None
    for l in range(depth):
        mod_l = mods[l].reshape(ncond, 1, N_MOD * d)
        pre = norm_pre[l].reshape(3, 1, d)
        post = norm_post[l].reshape(3, 1, d)
        wgate_p = _prep_gate(gla_w_gate[l])
        pw = pool_w[l].astype(bf16)
        psc = pool_scale[l].reshape(1, POOL_WIDTH)
        later = [(l, 1)] + ([(l + 1, 0)] if l + 1 < depth else [])
        nxt = [l + 1] if l + 1 < depth else []

        xs = [_ffn(x, mod_l, pre[0], post[0], ffn_w_in[l, 0], ffn_w_out[l, 0], 0, cond) for x, cond in zip(xs, conds)]
        p_ctx, p_lat = [_inproj(x, mod_l, pre[1], w_in_p[l], cond, dt) for x, cond, dt in zip(xs, conds, (f32, bf16))]

        y_pool = [_pool(p_ctx, pw, psc, seq, 0, nb), _pool(p_lat, pw, psc, dseq, 0, ndec)]
        na_ctx, new_k, new_v, *side = _ctx_attn(p_ctx, nb, l, depth, caches, make_side=[mod_job(j) for j in nxt])
        mods.update(zip(nxt, side))
        caches = (new_k, new_v)
        on_na = nxt if ndec * NA_HEADS >= N_IN_TILES else []
        na_lat, *side = _na_latent(p_lat, cache_na_k, cache_na_v, bias_tbl, l, 0, ndec,
                                   make_side=[in_cast(*ls) for ls in later] + [w_in_cast(j) for j in on_na])
        ffn_w_in.update(zip(later, side))
        w_in_p.update(zip(on_na, side[len(later):]))
        y_na = [na_ctx, na_lat]
        on_gla = [j for j in nxt if j not in on_na]
        g_ctx, new_s, w_stack, *side = _gla(p_ctx, wgate_p, gla_b_gate[l], gla_norm[l], seq, 0, nb, layer=l,
                                            with_sfin=True, depth=depth, carried=new_s,
                                            make_side=[w_stack_cast(l)] + [w_in_cast(j) for j in on_gla])
        w_in_p.update(zip(on_gla, side))
        g_lat, *side = _gla(p_lat, wgate_p, gla_b_gate[l], gla_norm[l], dseq, 0, ndec, rope_tabs=rope_tabs,
                            state=state_gla, layer=l, make_side=[out_cast(*ls) for ls in later])
        ffn_w_out.update(zip(later, side))
        y_gla = [g_ctx, g_lat]

        xs = [_merge(x, mod_l, post[1], yp, yn, yg, p, w_stack, cond)
              for x, yp, yn, yg, p, cond in zip(xs, y_pool, y_na, y_gla, (p_ctx, p_lat), conds)]
        xs = [_ffn(x, mod_l, pre[2], post[2], ffn_w_in[l, 1], ffn_w_out[l, 1], 2, cond) for x, cond in zip(xs, conds)]

    return (xs[0].reshape(nb, seq, d), xs[1].reshape(ndec, dseq, d), caches[0], caches[1], new_s)
```

```python
import functools
from typing import Any, Callable, NamedTuple

import numpy as np
import jax
import jax.numpy as jnp
from jax import lax
from jax.experimental import pallas as pl
from jax.experimental.pallas import tpu as pltpu

f32 = jnp.float32
bf16 = jnp.bfloat16

D_MODEL = 2048
SEQ = 256
DEC_SEQ = 2048
GRID_W = 64
N_MOD = 9
D_FF = 5504
FFN_RES = 0.5
EPS = 1e-6
NEG_INF = -1e30

POOL_GROUPS = 4
POOL_WINDOWS = (2, 4, 8, 16)
POOL_WIDTH = 1024
POOL_GC = POOL_WIDTH // POOL_GROUPS

NA_HEADS = 8
NA_HEAD_DIM = 128
NA_WIDTH = NA_HEADS * NA_HEAD_DIM
NA_WIN_H = 8
NA_WIN_W = 16

GLA_HEADS = 4
GLA_DK = 128
GLA_DV = 256
GLA_KW = GLA_HEADS * GLA_DK
GLA_VW = GLA_HEADS * GLA_DV
GLA_RANK = 16
GLA_TAU = 16.0
GLA_CHUNK = 64
ROPE_BASE = 10000.0

BRANCH_W = 1024
N_BRANCH = 3
GATE_W = N_BRANCH * D_MODEL

LANE = 128
VMEM_LIMIT = 56 * 1024 * 1024

OFF_GL = 0
OFF_POOL = OFF_GL + GATE_W
OFF_NQ = OFF_POOL + POOL_WIDTH
OFF_NK = OFF_NQ + NA_WIDTH
OFF_NV = OFF_NK + NA_WIDTH
OFF_GQ = OFF_NV + NA_WIDTH
OFF_GK = OFF_GQ + GLA_KW
OFF_GV = OFF_GK + GLA_KW
OFF_GR = OFF_GV + GLA_VW
OFF_GZ = OFF_GR + GLA_VW
TN_IN = 2304
IN_COLS_P = -(-(OFF_GZ + LANE) // TN_IN) * TN_IN

TM = 512
TF = 512
D_FF_P = -(-D_FF // TF) * TF


def _cparams(sem):
    return pltpu.CompilerParams(dimension_semantics=sem, vmem_limit_bytes=VMEM_LIMIT)


def _cond_index(i, tm, cond):
    return cond[0] + (i * tm) // cond[1]


ROW_CHUNK = 16


def _row_sweep(nrows, fn, unroll=4):
    def trip(i, carry):
        fn(pl.ds(pl.multiple_of(i * ROW_CHUNK, ROW_CHUNK), ROW_CHUNK))
        return carry

    lax.fori_loop(0, nrows // ROW_CHUNK, trip, 0, unroll=unroll)


def _row_rsqrt(x_ref, r_scr):
    n = x_ref.shape[1]

    def fn(rows):
        x = x_ref[rows, :]
        ss = jnp.sum(_lane_fold(x * x, jnp.add), axis=-1, keepdims=True)
        r_scr[rows, :] = jnp.broadcast_to(lax.rsqrt(ss * (1.0 / n) + EPS), (ROW_CHUNK, LANE))

    _row_sweep(x_ref.shape[0], fn, unroll=16)


def _lanes(r, n):
    return jnp.concatenate([r] * (n // LANE), axis=1)


def _norm_modulate(x_ref, mod_ref, g_ref, h_ref, r_scr):
    d = D_MODEL
    shift = mod_ref[:, 0:d]
    w = g_ref[...] * (1.0 + mod_ref[:, d:2 * d])
    _row_rsqrt(x_ref, r_scr)

    def fn(rows):
        h_ref[rows, :] = (x_ref[rows, :] * _lanes(r_scr[rows, :], d) * w + shift).astype(h_ref.dtype)

    _row_sweep(x_ref.shape[0], fn)


def _norm_gate_residual(y_ref, x_ref, mod_ref, g_ref, o_ref, r_scr, res_weight):
    d = D_MODEL
    w = (res_weight * mod_ref[:, 2 * d:3 * d]) * g_ref[...]
    _row_rsqrt(y_ref, r_scr)

    def fn(rows):
        o_ref[rows, :] = x_ref[rows, :] + y_ref[rows, :] * _lanes(r_scr[rows, :], d) * w

    _row_sweep(x_ref.shape[0], fn)


def _dot(a, b):
    return jnp.dot(a, b, preferred_element_type=f32)


def _dot_nt(a, b):
    return lax.dot_general(a, b, (((1,), (1,)), ((), ())), preferred_element_type=f32)


def _mod_body(step, c_ref, w_ref, b_ref, o_ref):
    del step
    c = c_ref[...]
    s = c * jax.nn.sigmoid(c)
    o_ref[...] = _dot(s.astype(bf16), w_ref[...].astype(bf16)) + b_ref[...]


def _mod_job(c_all, w_mod, b_mod, layer, nsteps, step_of):
    depth, d, n = w_mod.shape
    nc = c_all.shape[0]
    tn = n // nsteps
    assert tn * nsteps == n and tn % LANE == 0
    return _SideCast(
        (c_all, w_mod, b_mod.reshape(depth, 1, n)),
        (pl.BlockSpec((nc, d), lambda *g: (0, 0)),
         pl.BlockSpec((None, d, tn), lambda *g: (layer, 0, step_of(*g))),
         pl.BlockSpec((None, 1, tn), lambda *g: (layer, 0, step_of(*g)))),
        pl.BlockSpec((nc, tn), lambda *g: (0, step_of(*g))),
        jax.ShapeDtypeStruct((nc, n), f32),
        _mod_body)


TM_FFN = 1024


def _ffn_body(x_ref, mod_ref, pre_ref, post_ref, wgu_ref, wo_ref, o_ref, h_scr, r_scr):
    f = pl.program_id(1)

    @pl.when(f == 0)
    def _():
        _norm_modulate(x_ref, mod_ref, pre_ref, h_scr, r_scr)
        o_ref[...] = jnp.zeros_like(o_ref)

    h = h_scr[...]
    gu = _dot(h, wgu_ref[...])
    gt, up = gu[:, 0:TF], gu[:, TF:2 * TF]
    a = gt * jax.nn.sigmoid(gt) * up
    o_ref[...] += _dot(a.astype(bf16), wo_ref[...])

    @pl.when(f == pl.num_programs(1) - 1)
    def _():
        _norm_gate_residual(o_ref, x_ref, mod_ref, post_ref, o_ref, r_scr, FFN_RES)


def _ffn(x, mod_l, pre, post, w_gu, w_out2, sub, cond):
    t, d = x.shape
    fp = w_out2.shape[0]
    return pl.pallas_call(
        _ffn_body,
        grid=(t // TM_FFN, fp // TF),
        in_specs=[
            pl.BlockSpec((TM_FFN, d), lambda i, f: (i, 0)),
            pl.BlockSpec((None, 1, 3 * d), lambda i, f: (_cond_index(i, TM_FFN, cond), 0, sub)),
            pl.BlockSpec((1, d), lambda i, f: (0, 0)),
            pl.BlockSpec((1, d), lambda i, f: (0, 0)),
            pl.BlockSpec((d, 2 * TF), lambda i, f: (0, f)),
            pl.BlockSpec((TF, d), lambda i, f: (f, 0)),
        ],
        out_specs=pl.BlockSpec((TM_FFN, d), lambda i, f: (i, 0)),
        out_shape=jax.ShapeDtypeStruct((t, d), f32),
        scratch_shapes=[pltpu.VMEM((TM_FFN, d), bf16), pltpu.VMEM((TM_FFN, LANE), f32)],
        compiler_params=_cparams(("parallel", "arbitrary")),
        name="ffn",
    )(x, mod_l, pre, post, w_gu, w_out2)


def _inproj_body(x_ref, mod_ref, pre_ref, w_ref, o_ref, h_scr, r_scr):
    @pl.when(pl.program_id(1) == 0)
    def _():
        _norm_modulate(x_ref, mod_ref, pre_ref, h_scr, r_scr)

    o_ref[...] = _dot(h_scr[...], w_ref[...]).astype(o_ref.dtype)


def _inproj(x, mod_l, pre, w_in_p, cond, out_dtype):
    t, d = x.shape
    tm = TM * (4 // jnp.dtype(out_dtype).itemsize)
    n = w_in_p.shape[1]
    return pl.pallas_call(
        _inproj_body,
        grid=(t // tm, n // TN_IN),
        in_specs=[
            pl.BlockSpec((tm, d), lambda i, j: (i, 0)),
            pl.BlockSpec((None, 1, 3 * d), lambda i, j: (_cond_index(i, tm, cond), 0, 1)),
            pl.BlockSpec((1, d), lambda i, j: (0, 0)),
            pl.BlockSpec((d, TN_IN), lambda i, j: (0, j)),
        ],
        out_specs=pl.BlockSpec((tm, TN_IN), lambda i, j: (i, j)),
        out_shape=jax.ShapeDtypeStruct((t, n), out_dtype),
        scratch_shapes=[pltpu.VMEM((tm, d), bf16), pltpu.VMEM((tm, LANE), f32)],
        compiler_params=_cparams(("parallel", "arbitrary")),
        name="inproj",
    )(x, mod_l, pre, w_in_p)


POOL_PAD = 8


def _pool_body(u_ref, w_ref, sc_ref, o_ref, pad_scr, lvl_scr, *, seq):
    gc = POOL_GC
    pad = POOL_PAD
    n_lvl = seq + pad
    zeros = jnp.zeros((pad, POOL_WIDTH), f32)
    pad_scr[pl.ds(0, pad), :] = zeros
    pad_scr[pl.ds(pad + seq, pad), :] = zeros
    pad_scr[pl.ds(pad, seq), :] = u_ref[...].astype(f32)
    lvl_scr[:, pl.ds(n_lvl, pad), :] = jnp.zeros((2, pad, gc), f32)
    t = lax.broadcasted_iota(jnp.int32, (seq, 1), 0)
    for gi, win in enumerate(POOL_WINDOWS):
        cols = pl.ds(gi * gc, gc)
        read = lambda off, n: pad_scr[pl.ds(off, n), cols]
        k, slot = 1, 0
        while 2 * k < win:
            lvl_scr[slot, pl.ds(0, n_lvl), :] = read(0, n_lvl) + read(k, n_lvl)
            read = functools.partial(lambda s_, off, n: lvl_scr[s_, pl.ds(off, n), :], slot)
            k, slot = 2 * k, 1 - slot
        lo = jnp.maximum(t - win // 2, 0)
        hi = jnp.minimum(t + win - 1 - win // 2, seq - 1)
        inv_cnt = 1.0 / (hi - lo + 1).astype(f32)
        acc = read(pad - win // 2, seq) + read(pad, seq)
        pooled = acc * inv_cnt - pad_scr[pl.ds(pad, seq), cols]
        y = _dot(pooled.astype(bf16), w_ref[gi])
        o_ref[:, cols] = (y * sc_ref[:, cols]).astype(o_ref.dtype)


def _pool(p, pool_w, pool_scale, seq, row_block0, nseq):
    cb = OFF_POOL // POOL_WIDTH
    return pl.pallas_call(
        functools.partial(_pool_body, seq=seq),
        grid=(nseq,),
        in_specs=[
            pl.BlockSpec((seq, POOL_WIDTH), lambda s: (row_block0 + s, cb)),
            pl.BlockSpec((POOL_GROUPS, POOL_GC, POOL_GC), lambda s: (0, 0, 0)),
            pl.BlockSpec((1, POOL_WIDTH), lambda s: (0, 0)),
        ],
        out_specs=pl.BlockSpec((seq, POOL_WIDTH), lambda s: (s, 0)),
        out_shape=jax.ShapeDtypeStruct((nseq * seq, POOL_WIDTH), bf16),
        scratch_shapes=[pltpu.VMEM((seq + 2 * POOL_PAD, POOL_WIDTH), f32),
                        pltpu.VMEM((2, seq + 2 * POOL_PAD, POOL_GC), f32)],
        compiler_params=_cparams(("parallel",)),
        name="pool",
    )(p, pool_w, pool_scale)


def _lane_fold(x, op):
    parts = [x[:, i * LANE:(i + 1) * LANE] for i in range(x.shape[1] // LANE)]
    while len(parts) > 1:
        parts = [op(parts[i], parts[i + 1]) for i in range(0, len(parts) - 1, 2)] + parts[len(parts) & ~1:]
    return parts[0]


def _softmax_rows(s):
    m = jnp.max(_lane_fold(s, jnp.maximum), axis=-1, keepdims=True)
    e = jnp.exp(s - m)
    return e / jnp.sum(_lane_fold(e, jnp.add), axis=-1, keepdims=True)


def _ctx_attn_body(q_ref, k_ref, v_ref, *refs, n_alias, side, layer):
    refs = refs[n_alias:]
    n_src = sum(n_in for _, n_in in side)
    o_ref, nk_ref, nv_ref = refs[n_src:n_src + 3]
    for job, src_refs, dst_ref in _side_refs(side, refs[:n_src], refs[n_src + 3:]):
        job(pl.program_id(0), *src_refs, dst_ref)
    if not n_alias:
        for ref in (nk_ref, nv_ref):
            for other in range(ref.shape[0]):
                if other != layer:
                    ref[other] = jnp.zeros(ref.shape[1:], ref.dtype)
        nk_ref, nv_ref = nk_ref.at[layer], nv_ref.at[layer]
    hd = NA_HEAD_DIM
    for h in range(NA_HEADS):
        cols = pl.ds(h * hd, hd)
        kf = k_ref[:, cols]
        vf = v_ref[:, cols]
        nk_ref[h] = kf
        nv_ref[h] = vf
        p = _softmax_rows(_dot_nt(q_ref[:, cols].astype(bf16), kf.astype(bf16)) * (hd ** -0.5))
        o_ref[:, cols] = _dot(p.astype(bf16), vf.astype(bf16)).astype(o_ref.dtype)


def _ctx_attn(p, nseq, layer, depth, caches=None, make_side=()):
    spec = lambda off: pl.BlockSpec((SEQ, NA_WIDTH), lambda b: (b, off // NA_WIDTH))
    cache_shape = jax.ShapeDtypeStruct((nseq, depth, NA_HEADS, SEQ, NA_HEAD_DIM), f32)
    in_specs = [spec(OFF_NQ), spec(OFF_NK), spec(OFF_NV)]
    args = [p, p, p]
    aliases = {}
    if caches is None:
        cache_spec = pl.BlockSpec((None, depth, NA_HEADS, SEQ, NA_HEAD_DIM), lambda b: (b, 0, 0, 0, 0))
    else:
        cache_spec = pl.BlockSpec((None, None, NA_HEADS, SEQ, NA_HEAD_DIM), lambda b: (b, layer, 0, 0, 0))
        in_specs += [pl.BlockSpec(memory_space=pl.ANY)] * 2
        args += list(caches)
        aliases = {3: 1, 4: 2}
    side_in, side_args, side_out, side_shape, side = _side_io([make(nseq, lambda b: b) for make in make_side])
    return pl.pallas_call(
        functools.partial(_ctx_attn_body, n_alias=len(aliases), side=side, layer=layer),
        grid=(nseq,),
        in_specs=in_specs + side_in,
        out_specs=[pl.BlockSpec((SEQ, NA_WIDTH), lambda b: (b, 0)), cache_spec, cache_spec] + side_out,
        out_shape=[jax.ShapeDtypeStruct((nseq * SEQ, NA_WIDTH), bf16), cache_shape, cache_shape] + side_shape,
        input_output_aliases=aliases,
        compiler_params=_cparams(("arbitrary",)),
        name="ctx_attn",
    )(*args, *side_args)


LOG2E = float(np.log2(np.e))


def _na_bias_table(rpb):
    qc = np.arange(GRID_W)[:, None]
    kc = np.arange(GRID_W)[None, :]
    cs = np.clip(qc - NA_WIN_W // 2, 0, GRID_W - NA_WIN_W)
    ok = (kc >= cs) & (kc < cs + NA_WIN_W)
    cidx = np.clip(kc - qc + NA_WIN_W - 1, 0, 2 * NA_WIN_W - 2)
    onehot = jnp.asarray((cidx[None] == np.arange(2 * NA_WIN_W - 1)[:, None, None]) & ok[None], f32)
    toep = jnp.einsum('...rc,cqk->...rqk', rpb.astype(f32), onehot, precision=lax.Precision.HIGHEST)
    toep = jnp.where(ok, toep * LOG2E, NEG_INF)
    return jnp.concatenate([toep[..., :-1, :, :], toep[..., 1:, :, :]], axis=-1)


NA_UNROLL = 32
NA_SOFTMAX_UNROLL = 8
NA_CTX_ROWS = 256


def _na_body(*refs, rows, side):
    n_src = sum(n_in for _, n_in in side)
    qb_scr, kb_scr, vb_scr, ck_ref, cv_ref, bias_ref = refs[:6]
    o_ref = refs[6 + n_src]
    sl_scr, sc_scr, el_scr, ec_scr, den_scr, oc_scr = refs[7 + n_src + len(side):]
    step = pl.program_id(0) * pl.num_programs(1) + pl.program_id(1)
    for job, src_refs, dst_ref in _side_refs(side, refs[6:6 + n_src], refs[7 + n_src:7 + n_src + len(side)]):
        job(step, *src_refs, dst_ref)

    hd = NA_HEAD_DIM
    scale = hd ** -0.5 * LOG2E
    kh = min(NA_WIN_H, rows)
    nloc = kh * GRID_W
    n = rows * GRID_W
    assert qb_scr.dtype == bf16
    ck = ck_ref[...].astype(bf16)
    cv = cv_ref[...].astype(bf16)

    def row_slices(r):
        rs = jnp.clip(r - kh // 2, 0, rows - kh)
        q_rows = pl.ds(pl.multiple_of(r * GRID_W, GRID_W), GRID_W)
        k_rows = pl.ds(pl.multiple_of(rs * GRID_W, GRID_W), nloc)
        return rs, q_rows, k_rows

    def ctx_scores(i, carry):
        blk = pl.ds(pl.multiple_of(i * NA_CTX_ROWS, NA_CTX_ROWS), NA_CTX_ROWS)
        sc_scr[blk, :] = _dot_nt(qb_scr[blk, :], ck) * scale
        return carry

    lax.fori_loop(0, n // NA_CTX_ROWS, ctx_scores, 0, unroll=8)

    def loc_scores(r, carry):
        rs, q_rows, k_rows = row_slices(r)
        first = rs - r + NA_WIN_H - 1
        bias = jnp.concatenate([bias_ref[first + 2 * e] for e in range(kh // 2)], axis=1)
        sl_scr[q_rows, :] = _dot_nt(qb_scr[q_rows, :], kb_scr[k_rows, :]) * scale + bias
        return carry

    lax.fori_loop(0, rows, loc_scores, 0, unroll=NA_UNROLL)

    def numerators(r, carry):
        q_rows = pl.ds(pl.multiple_of(r * GRID_W, GRID_W), GRID_W)
        s_loc = sl_scr[q_rows, :]
        s_ctx = sc_scr[q_rows, :]
        m = jnp.max(jnp.maximum(_lane_fold(s_loc, jnp.maximum), _lane_fold(s_ctx, jnp.maximum)),
                    axis=-1, keepdims=True)
        e_loc = jnp.exp2(s_loc - m)
        e_ctx = jnp.exp2(s_ctx - m)
        den = jnp.sum(_lane_fold(e_loc, jnp.add) + _lane_fold(e_ctx, jnp.add), axis=-1, keepdims=True)
        el_scr[q_rows, :] = e_loc.astype(bf16)
        ec_scr[q_rows, :] = e_ctx.astype(bf16)
        den_scr[q_rows, :] = jnp.broadcast_to(den, (GRID_W, hd))
        return carry

    lax.fori_loop(0, rows, numerators, 0, unroll=NA_SOFTMAX_UNROLL)

    def ctx_values(i, carry):
        blk = pl.ds(pl.multiple_of(i * NA_CTX_ROWS, NA_CTX_ROWS), NA_CTX_ROWS)
        oc_scr[blk, :] = _dot(ec_scr[blk, :], cv)
        return carry

    lax.fori_loop(0, n // NA_CTX_ROWS, ctx_values, 0, unroll=8)

    def loc_values(r, carry):
        _, q_rows, k_rows = row_slices(r)
        o = _dot(el_scr[q_rows, :], vb_scr[k_rows, :]) + oc_scr[q_rows, :]
        o_ref[q_rows, :] = (o / den_scr[q_rows, :]).astype(o_ref.dtype)
        return carry

    lax.fori_loop(0, rows, loc_values, 0, unroll=NA_UNROLL)


def _na_latent(p, cache_k, cache_v, bias_tbl, layer, row_block0, nreq, make_side=()):
    n = DEC_SEQ
    hd = NA_HEAD_DIM
    past = cache_k.shape[3]
    rows = n // GRID_W
    side_in, side_args, side_out, side_shape, side = _side_io(
        [make(nreq * NA_HEADS, lambda b, h: b * NA_HEADS + h) for make in make_side])
    qkv = lambda off: pl.BlockSpec((n, hd), lambda b, h: (row_block0 + b, off // hd + h))
    cache = pl.BlockSpec((None, None, None, past, hd), lambda b, h: (b, layer, h, 0, 0))
    return pl.pallas_call(
        functools.partial(_na_body, rows=rows, side=side),
        grid=(nreq, NA_HEADS),
        in_specs=[qkv(OFF_NQ), qkv(OFF_NK), qkv(OFF_NV), cache, cache,
                  pl.BlockSpec((None, None, 2 * NA_WIN_H - 2, GRID_W, 2 * GRID_W),
                               lambda b, h: (layer, h, 0, 0, 0))] + side_in,
        out_specs=[pl.BlockSpec((n, hd), lambda b, h: (b, h))] + side_out,
        out_shape=[jax.ShapeDtypeStruct((nreq * n, NA_WIDTH), bf16)] + side_shape,
        scratch_shapes=[pltpu.VMEM((n, NA_WIN_H * GRID_W), f32), pltpu.VMEM((n, past), f32),
                        pltpu.VMEM((n, NA_WIN_H * GRID_W), bf16), pltpu.VMEM((n, past), bf16),
                        pltpu.VMEM((n, hd), f32), pltpu.VMEM((n, hd), f32)],
        compiler_params=_cparams(("arbitrary", "arbitrary")),
        name="na_latent",
    )(p, p, p, cache_k, cache_v, bias_tbl, *side_args)


GLA_PAD = 32
GLA_UNROLL = 32


def _rope_tables(seq):
    t = np.arange(seq)
    half = GLA_DK // 2
    nf = half // 2
    inv = ROPE_BASE ** (-np.arange(nf, dtype=np.float64) / nf)
    cos, sin = [], []
    for pos in (t // GRID_W, t % GRID_W):
        ang = pos[:, None].astype(np.float64) * inv
        cos += [np.cos(ang), np.cos(ang)]
        sin += [-np.sin(ang), np.sin(ang)]
    return (jnp.asarray(np.concatenate(cos, axis=-1), f32), jnp.asarray(np.concatenate(sin, axis=-1), f32))


def _rope(x, cos, sin_signed):
    nf = GLA_DK // 4
    lane = lax.broadcasted_iota(jnp.int32, x.shape, 1)
    partner = jnp.where(lane % (2 * nf) < nf, pltpu.roll(x, GLA_DK - nf, 1), pltpu.roll(x, nf, 1))
    return x * cos + partner * sin_signed


def _log_sigmoid(x):
    return jnp.minimum(x, 0.0) - jnp.log1p(jnp.exp(-jnp.abs(x)))


def _gla_body(*refs, seq, rope, with_s0, with_sfin, sfin_layer, n_carried, side):
    refs = list(refs)
    q_ref, k_ref, v_ref, r_ref, z_ref, wg_ref, bg_ref, ng_ref = refs[:8]
    refs = refs[8:]
    if rope:
        cos_ref, sin_ref = refs[:2]
        refs = refs[2:]
    if with_s0:
        s0_ref = refs[0]
        refs = refs[1:]
    refs = refs[n_carried:]
    n_src = sum(n_in for _, n_in in side)
    side_src, refs = refs[:n_src], refs[n_src:]
    o_ref = refs[0]
    refs = refs[1:]
    if with_sfin:
        sfin_ref = refs[0]
        refs = refs[1:]
    side_dst, refs = refs[:len(side)], refs[len(side):]
    step = pl.program_id(0) * pl.num_programs(1) + pl.program_id(1)
    for job, src_refs, dst_ref in _side_refs(side, side_src, side_dst):
        job(step, *src_refs, dst_ref)
    qi_scr, kn_scr, kd_scr, dec_scr, scan_scr, vb_scr, u_scr, sb_scr, o_scr, st_scr = refs

    ch = GLA_CHUNK
    nch = seq // ch
    dk, dv = GLA_DK, GLA_DV

    q = q_ref[...].astype(f32)
    k = k_ref[...].astype(f32)
    if rope:
        q = _rope(q, cos_ref[...], sin_ref[...])
        k = _rope(k, cos_ref[...], sin_ref[...])
    q = q * (dk ** -0.5)

    zb = z_ref[...].astype(bf16)
    pos = lax.broadcasted_iota(jnp.int32, (seq, 1), 0) % ch
    zpad = jnp.zeros((GLA_PAD, dk), f32)
    scan_scr[pl.ds(0, GLA_PAD), :] = zpad
    scan_scr[pl.ds(GLA_PAD + seq, GLA_PAD), :] = zpad
    for d in range(2):
        g = _log_sigmoid(_dot(zb, wg_ref[d]) + bg_ref[d]) / GLA_TAU
        b = g
        sh = 1
        while sh < ch:
            scan_scr[pl.ds(GLA_PAD, seq), :] = b
            if d == 0:
                b = b + jnp.where(pos >= sh, scan_scr[pl.ds(GLA_PAD - sh, seq), :], 0.0)
            else:
                b = b + jnp.where(pos < ch - sh, scan_scr[pl.ds(GLA_PAD + sh, seq), :], 0.0)
            sh *= 2
        b3 = b.reshape(nch, ch, dk)
        b_end = b3[:, ch - 1:ch, :] if d == 0 else b3[:, 0:1, :]
        lanes = pl.ds(d * dk, dk)
        qi_scr[:, lanes] = (q * jnp.exp(b)).astype(bf16)
        kn_scr[d] = (k * jnp.exp(-b)).astype(bf16)
        kd_scr[:, lanes] = (k.reshape(nch, ch, dk) * jnp.exp(b_end - b3)).reshape(seq, dk).astype(bf16)
        dec_scr[d] = jnp.exp(b_end)

    for d in range(2):
        if with_s0:
            st_scr[d] = s0_ref[d].T
        else:
            st_scr[d] = jnp.zeros((dv, dk), f32)

    ri = lax.broadcasted_iota(jnp.int32, (ch, ch), 0)
    ci = lax.broadcasted_iota(jnp.int32, (ch, ch), 1)

    vb_scr[...] = v_ref[...].astype(bf16)
    chunk_rows = lambda c: pl.ds(pl.multiple_of(c * ch, ch), ch)
    fwd, bwd = pl.ds(0, dk), pl.ds(dk, dk)

    def increments(c, carry):
        rows = chunk_rows(c)
        u_scr[c] = lax.dot_general(vb_scr[rows, :], kd_scr[rows, :], (((0,), (0,)), ((), ())),
                                   preferred_element_type=f32)
        return carry

    lax.fori_loop(0, nch, increments, 0, unroll=min(GLA_UNROLL, nch))

    def states(i, carry):
        for d, c, lanes in ((0, i, fwd), (1, nch - 1 - i, bwd)):
            st = st_scr[d]
            sb_scr[c, :, lanes] = st.astype(bf16)
            st_scr[d] = st * dec_scr[d, c] + u_scr[c, :, lanes]
        return carry

    lax.fori_loop(0, nch, states, 0, unroll=4)

    if with_sfin:
        if not n_carried:
            for other in range(sfin_ref.shape[0]):
                if other != sfin_layer:
                    sfin_ref[other] = jnp.zeros(sfin_ref.shape[1:], sfin_ref.dtype)
            sfin_ref = sfin_ref.at[sfin_layer]
        for d in range(2):
            sfin_ref[d] = st_scr[d].T

    def outputs(c, carry):
        rows = chunk_rows(c)
        qi = qi_scr[rows, :]
        pf = _dot_nt(qi[:, 0:dk], kn_scr[0, rows, :])
        pb = _dot_nt(qi[:, dk:2 * dk], kn_scr[1, rows, :])
        a = jnp.where(ci < ri, pf, jnp.where(ci > ri, pb, pf + pb))
        o_scr[rows, :] = _dot(a.astype(bf16), vb_scr[rows, :]) + _dot_nt(qi, sb_scr[c])
        return carry

    lax.fori_loop(0, nch, outputs, 0, unroll=min(GLA_UNROLL, nch))

    o = o_scr[...]
    r = r_ref[...].astype(f32)
    o = o * lax.rsqrt(jnp.mean(o * o, axis=-1, keepdims=True) + EPS) * ng_ref[...]
    o_ref[...] = (o * (r * jax.nn.sigmoid(r))).astype(o_ref.dtype)


def _gla(p, wgate_p, b_gate, gla_norm, seq, row_block0, nreq, rope_tabs=None, state=None, layer=0,
         with_sfin=False, depth=1, carried=None, make_side=()):
    dk, dv = GLA_DK, GLA_DV
    nch = seq // GLA_CHUNK
    rope = rope_tabs is not None
    with_s0 = state is not None
    blk = lambda w, off: pl.BlockSpec((seq, w), lambda b, h: (row_block0 + b, off // w + h))
    in_specs = [blk(dk, OFF_GQ), blk(dk, OFF_GK), blk(dv, OFF_GV), blk(dv, OFF_GR),
                pl.BlockSpec((seq, LANE), lambda b, h: (row_block0 + b, OFF_GZ // LANE)),
                pl.BlockSpec((2, LANE, dk), lambda b, h: (0, 0, h)),
                pl.BlockSpec((2, 1, dk), lambda b, h: (0, 0, h)),
                pl.BlockSpec((1, dv), lambda b, h: (0, h))]
    args = [p, p, p, p, p, wgate_p, b_gate.reshape(2, 1, GLA_KW), gla_norm.reshape(1, GLA_VW)]
    if rope:
        in_specs += [pl.BlockSpec((seq, dk), lambda b, h: (0, 0))] * 2
        args += list(rope_tabs)
    if with_s0:
        in_specs.append(pl.BlockSpec((None, None, 2, None, dk, dv), lambda b, h: (b, layer, 0, h, 0, 0)))
        args.append(state)
    aliases = {}
    if carried is not None:
        aliases = {len(args): 1}
        in_specs.append(pl.BlockSpec(memory_space=pl.ANY))
        args.append(carried)
    side_in, side_args, side_out, side_shape, side = _side_io(
        [make(nreq * GLA_HEADS, lambda b, h: b * GLA_HEADS + h) for make in make_side])
    in_specs += side_in
    args += side_args
    out_specs = [pl.BlockSpec((seq, dv), lambda b, h: (b, h))]
    out_shape = [jax.ShapeDtypeStruct((nreq * seq, GLA_VW), bf16)]
    if with_sfin:
        if carried is None:
            out_specs.append(pl.BlockSpec((None, depth, 2, None, dk, dv), lambda b, h: (b, 0, 0, h, 0, 0)))
        else:
            out_specs.append(pl.BlockSpec((None, None, 2, None, dk, dv), lambda b, h: (b, layer, 0, h, 0, 0)))
        out_shape.append(jax.ShapeDtypeStruct((nreq, depth, 2, GLA_HEADS, dk, dv), f32))
    out_specs += side_out
    out_shape += side_shape
    scratch = [pltpu.VMEM((seq, 2 * dk), bf16), pltpu.VMEM((2, seq, dk), bf16), pltpu.VMEM((seq, 2 * dk), bf16),
               pltpu.VMEM((2, nch, 1, dk), f32), pltpu.VMEM((seq + 2 * GLA_PAD, dk), f32),
               pltpu.VMEM((seq, dv), bf16), pltpu.VMEM((nch, dv, 2 * dk), f32), pltpu.VMEM((nch, dv, 2 * dk), bf16),
               pltpu.VMEM((seq, dv), f32), pltpu.VMEM((2, dv, dk), f32)]
    return pl.pallas_call(
        functools.partial(_gla_body, seq=seq, rope=rope, with_s0=with_s0, with_sfin=with_sfin,
                          sfin_layer=layer, n_carried=len(aliases), side=side),
        grid=(nreq, GLA_HEADS),
        in_specs=in_specs,
        out_specs=out_specs,
        out_shape=out_shape,
        input_output_aliases=aliases,
        scratch_shapes=scratch,
        compiler_params=_cparams(("arbitrary", "arbitrary")),
        name="gla",
    )(*args)


TM_MERGE = 512
MERGE_SUB = 2


def _merge_body(x_ref, mod_ref, post_ref, bp_ref, bn_ref, bg_ref, gl_ref, w_ref, o_ref, m_scr, mb_scr, r_scr):
    n = pl.program_id(1)
    m_ref = m_scr.at[pl.program_id(2)]
    mb_ref = mb_scr.at[pl.program_id(2)]

    for bi, br_ref in enumerate((bp_ref, bn_ref, bg_ref)):
        @pl.when(n == bi)
        def _():
            y = jax.nn.sigmoid(gl_ref[...].astype(f32)) * _dot(br_ref[...], w_ref[...])
            m_ref[...] = y if bi == 0 else m_ref[...] + y

    @pl.when(n == N_BRANCH)
    def _():
        mb_ref[...] = m_ref[...].astype(bf16)
        m_ref[...] = _dot(mb_ref[:, 0:BRANCH_W], w_ref[...])

    @pl.when(n == N_BRANCH + 1)
    def _():
        m_ref[...] += _dot(mb_ref[:, BRANCH_W:2 * BRANCH_W], w_ref[...])
        _norm_gate_residual(m_ref, x_ref, mod_ref, post_ref, o_ref, r_scr, 1.0)


def _merge(x, mod_l, post, y_pool, y_na, y_gla, p, w_stack, cond):
    t, d = x.shape
    tm = TM_MERGE
    sub = MERGE_SUB
    nsteps = N_BRANCH + d // BRANCH_W
    tile = lambda i, s: i * sub + s

    def rows_at(first, last):
        def index(i, n, s):
            return jnp.where(n < first, jnp.maximum(tile(i, 0) - 1, 0),
                             jnp.where(n > last, tile(i, sub - 1), tile(i, s)))
        return index

    last = nsteps - 1
    br = lambda step: pl.BlockSpec((tm, BRANCH_W), lambda i, n, s: (rows_at(step, step)(i, n, s), 0))
    xo = pl.BlockSpec((tm, d), lambda i, n, s: (rows_at(last, last)(i, n, s), 0))
    return pl.pallas_call(
        _merge_body,
        grid=(t // (tm * sub), nsteps, sub),
        in_specs=[
            xo,
            pl.BlockSpec((None, 1, 3 * d), lambda i, n, s: (_cond_index(tile(i, s), tm, cond), 0, 1)),
            pl.BlockSpec((1, d), lambda i, n, s: (0, 0)),
            br(0), br(1), br(2),
            pl.BlockSpec((tm, d),
                         lambda i, n, s: (rows_at(0, N_BRANCH - 1)(i, n, s), jnp.minimum(n, N_BRANCH - 1))),
            pl.BlockSpec((BRANCH_W, d), lambda i, n, s: (n, 0)),
        ],
        out_specs=xo,
        out_shape=jax.ShapeDtypeStruct((t, d), f32),
        scratch_shapes=[pltpu.VMEM((sub, tm, d), f32), pltpu.VMEM((sub, tm, d), bf16), pltpu.VMEM((tm, LANE), f32)],
        compiler_params=_cparams(("arbitrary", "arbitrary", "arbitrary")),
        name="merge",
    )(x, mod_l, post, y_pool, y_na, y_gla, p, w_stack)


_IN_SPLITS = (POOL_WIDTH, NA_WIDTH, NA_WIDTH, NA_WIDTH, GLA_KW, GLA_KW, GLA_VW, 2 * GLA_RANK, GLA_VW, GATE_W)
_IN_OFFS = tuple(int(v) for v in np.cumsum((0,) + _IN_SPLITS))
_IN_RUNS = ((_IN_OFFS[9], _IN_OFFS[10]), (_IN_OFFS[0], _IN_OFFS[7]), (_IN_OFFS[8], _IN_OFFS[9]),
            (_IN_OFFS[7], _IN_OFFS[8]))


CT_IN = 512
_IN_TILE_STARTS = []
for _a, _b in _IN_RUNS:
    _IN_TILE_STARTS += [_a + CT_IN * _t for _t in range(-(-(_b - _a) // CT_IN))]
assert len(_IN_TILE_STARTS) * CT_IN == IN_COLS_P and all(v % 8 == 0 for v in _IN_TILE_STARTS)
_IN_LAST_VALID = (_IN_RUNS[-1][1] - _IN_RUNS[-1][0]) % CT_IN or CT_IN
assert all((b - a) % CT_IN == 0 for a, b in _IN_RUNS[:-1])


def _w_in_tile_start(j):
    out = jnp.int32(_IN_TILE_STARTS[0]) + CT_IN * j
    for t in range(1, len(_IN_TILE_STARTS)):
        if _IN_TILE_STARTS[t] != _IN_TILE_STARTS[t - 1] + CT_IN:
            out = jnp.where(j >= t, _IN_TILE_STARTS[t] + CT_IN * (j - t), out)
    return out


N_IN_TILES = IN_COLS_P // CT_IN


def _cast_w_in_body(step, w_ref, o_ref):
    @pl.when(step < N_IN_TILES)
    def _():
        col = lax.broadcasted_iota(jnp.int32, (1, CT_IN), 1)
        valid = jnp.where(step == N_IN_TILES - 1, _IN_LAST_VALID, CT_IN)
        o_ref[...] = jnp.where(col < valid, w_ref[...].T, 0.0).astype(bf16)


def _w_in_cast(wt, n_cols, layer, nsteps, step_of):
    assert nsteps >= N_IN_TILES
    d = wt.shape[1]
    tile = lambda *g: jnp.minimum(step_of(*g), N_IN_TILES - 1)
    return _SideCast(
        (wt,),
        (pl.BlockSpec((pl.Element(CT_IN), pl.Element(d)),
                      lambda *g: (pl.multiple_of(layer * n_cols + _w_in_tile_start(tile(*g)), 8), 0)),),
        pl.BlockSpec((d, CT_IN), lambda *g: (0, tile(*g))),
        jax.ShapeDtypeStruct((d, IN_COLS_P), bf16),
        _cast_w_in_body)


class _SideCast(NamedTuple):
    arrays: tuple
    in_specs: tuple
    out_spec: pl.BlockSpec
    out_shape: jax.ShapeDtypeStruct
    body: Callable


def _side_refs(side, src_refs, dst_refs):
    jobs, k = [], 0
    for (body, n_in), dst in zip(side, dst_refs):
        jobs.append((body, src_refs[k:k + n_in], dst))
        k += n_in
    return jobs


def _side_io(side):
    return ([sp for j in side for sp in j.in_specs], [a for j in side for a in j.arrays],
            [j.out_spec for j in side], [j.out_shape for j in side], tuple((j.body, len(j.arrays)) for j in side))


def _cast_ffn_in_body(step, w_ref, o_ref):
    del step
    for f in range(D_FF_P // TF):
        n = min(TF, D_FF - f * TF)
        for half in range(2):
            dst = (2 * f + half) * TF
            o_ref[:, dst:dst + n] = w_ref[:, half * D_FF + f * TF:half * D_FF + f * TF + n].astype(bf16)
            if n < TF:
                o_ref[:, dst + n:dst + TF] = jnp.zeros((w_ref.shape[0], TF - n), bf16)


def _ffn_in_cast(w, layer, slot, nsteps, step_of):
    d = w.shape[2]
    rows = d // nsteps
    assert rows * nsteps == d and rows % 16 == 0
    return _SideCast(
        (w,),
        (pl.BlockSpec((None, None, rows, 2 * D_FF), lambda *g: (layer, slot, step_of(*g), 0)),),
        pl.BlockSpec((rows, 2 * D_FF_P), lambda *g: (step_of(*g), 0)),
        jax.ShapeDtypeStruct((d, 2 * D_FF_P), bf16),
        _cast_ffn_in_body)


def _cast_ffn_out_body(step, w_ref, o_ref):
    rows = w_ref.shape[0]
    row = step * rows + lax.broadcasted_iota(jnp.int32, (rows, 1), 0)
    o_ref[...] = jnp.where(row < D_FF, w_ref[...], 0.0).astype(bf16)


def _ffn_out_cast(w, layer, slot, nsteps, step_of):
    d = w.shape[3]
    rows = D_FF_P // nsteps
    assert rows * nsteps == D_FF_P and rows % 16 == 0 and rows * (nsteps - 1) < D_FF
    return _SideCast(
        (w,),
        (pl.BlockSpec((None, None, rows, d), lambda *g: (layer, slot, step_of(*g), 0)),),
        pl.BlockSpec((rows, d), lambda *g: (step_of(*g), 0)),
        jax.ShapeDtypeStruct((D_FF_P, d), bf16),
        _cast_ffn_out_body)


def _run_cast_body(*refs, cast):
    cast(pl.program_id(0), *refs)


def _run_cast(make_job, nsteps, name):
    job = make_job(nsteps, lambda r: r)
    return pl.pallas_call(
        functools.partial(_run_cast_body, cast=job.body),
        grid=(nsteps,),
        in_specs=list(job.in_specs),
        out_specs=job.out_spec,
        out_shape=job.out_shape,
        compiler_params=_cparams(("parallel",)),
        name=name,
    )(*job.arrays)


W_STACK_ROWS = 256


def _cast_w_stack_body(step, wb_ref, wo_ref, o_ref, *, n_branch, n_out):
    @pl.when(step < n_branch)
    def _():
        o_ref[...] = wb_ref[...].astype(bf16)

    @pl.when(jnp.logical_and(step >= n_branch, step < n_branch + n_out))
    def _():
        o_ref[...] = wo_ref[...].astype(bf16)


def _w_stack_cast(w_branch, w_out, layer, nsteps, step_of):
    depth, nb, bw, d = w_branch.shape
    r = W_STACK_ROWS
    n_branch, n_out = nb * bw // r, d // r
    assert nsteps >= n_branch + n_out and bw % r == 0
    t = lambda *g: step_of(*g)
    return _SideCast(
        (w_branch.reshape(depth * nb * bw, d), w_out.reshape(depth * d, d)),
        (pl.BlockSpec((r, d), lambda *g: (layer * n_branch + jnp.minimum(t(*g), n_branch - 1), 0)),
         pl.BlockSpec((r, d), lambda *g: (layer * n_out + jnp.clip(t(*g) - n_branch, 0, n_out - 1), 0))),
        pl.BlockSpec((r, d), lambda *g: (jnp.minimum(t(*g), n_branch + n_out - 1), 0)),
        jax.ShapeDtypeStruct((nb * bw + d, d), bf16),
        functools.partial(_cast_w_stack_body, n_branch=n_branch, n_out=n_out))


def _prep_gate(w_gate):
    out = jnp.zeros((2, LANE, GLA_KW), f32)
    for d in range(2):
        out = out.at[d, d * GLA_RANK:(d + 1) * GLA_RANK].set(w_gate[d])
    return out.astype(bf16)


def kernel(x_prompt, x_sample, c, cache_na_k, cache_na_v, state_gla, c_ctx, w_mod, b_mod, norm_pre, norm_post,
           w_ffn_in, w_ffn_out, w_in, pool_w, pool_scale, na_rpb, gla_w_gate, gla_b_gate, gla_norm, w_branch,
           w_out):
    nb, seq, d = x_prompt.shape
    ndec, dseq, _ = x_sample.shape
    depth = w_mod.shape[0]
    n_ctx = nb * seq
    n_lat = ndec * dseq
    assert (seq, dseq, d) == (SEQ, DEC_SEQ, D_MODEL)
    assert n_ctx % (TM_MERGE * MERGE_SUB) == 0 and n_ctx % TM_FFN == 0 and dseq % TM_FFN == 0

    xs = [x_prompt.reshape(n_ctx, d), x_sample.reshape(n_lat, d)]
    conds = [(0, n_ctx), (1, dseq)]
    ncond = -(-(1 + ndec) // 8) * 8
    c_all = jnp.concatenate([c_ctx[None], c, jnp.zeros((ncond - 1 - ndec, d), f32)], axis=0)
    rope_tabs = _rope_tables(dseq)
    bias_tbl = _na_bias_table(na_rpb)

    mod_job = lambda l: functools.partial(_mod_job, c_all, w_mod, b_mod, l)
    in_cast = lambda l, s: functools.partial(_ffn_in_cast, w_ffn_in, l, s)
    out_cast = lambda l, s: functools.partial(_ffn_out_cast, w_ffn_out, l, s)
    wt_in = jnp.swapaxes(w_in, 1, 2).reshape(depth * w_in.shape[2], d)
    w_in_cast = lambda l: functools.partial(_w_in_cast, wt_in, w_in.shape[2], l)
    w_stack_cast = lambda l: functools.partial(_w_stack_cast, w_branch, w_out, l)
    mods = {0: _run_cast(mod_job(0), N_MOD * d // 1024, "modulation")}
    ffn_w_in = {(0, 0): _run_cast(in_cast(0, 0), d // 256, "cast_ffn_in")}
    ffn_w_out = {(0, 0): _run_cast(out_cast(0, 0), D_FF_P // TF, "cast_ffn_out")}
    w_in_p = {0: _run_cast(w_in_cast(0), N_IN_TILES, "cast_w_in")}

    caches = None
    new_s = None
    for l in range(depth):
        mod_l = mods[l].reshape(ncond, 1, N_MOD * d)
        pre = norm_pre[l].reshape(3, 1, d)
        post = norm_post[l].reshape(3, 1, d)
        wgate_p = _prep_gate(gla_w_gate[l])
        pw = pool_w[l].astype(bf16)
        psc = pool_scale[l].reshape(1, POOL_WIDTH)
        later = [(l, 1)] + ([(l + 1, 0)] if l + 1 < depth else [])
        nxt = [l + 1] if l + 1 < depth else []

        xs = [_ffn(x, mod_l, pre[0], post[0], ffn_w_in[l, 0], ffn_w_out[l, 0], 0, cond) for x, cond in zip(xs, conds)]
        p_ctx, p_lat = [_inproj(x, mod_l, pre[1], w_in_p[l], cond, dt) for x, cond, dt in zip(xs, conds, (f32, bf16))]

        y_pool = [_pool(p_ctx, pw, psc, seq, 0, nb), _pool(p_lat, pw, psc, dseq, 0, ndec)]
        na_ctx, new_k, new_v, *side = _ctx_attn(p_ctx, nb, l, depth, caches, make_side=[mod_job(j) for j in nxt])
        mods.update(zip(nxt, side))
        caches = (new_k, new_v)
        on_na = nxt if ndec * NA_HEADS >= N_IN_TILES else []
        na_lat, *side = _na_latent(p_lat, cache_na_k, cache_na_v, bias_tbl, l, 0, ndec,
                                   make_side=[in_cast(*ls) for ls in later] + [w_in_cast(j) for j in on_na])
        ffn_w_in.update(zip(later, side))
        w_in_p.update(zip(on_na, side[len(later):]))
        y_na = [na_ctx, na_lat]
        on_gla = [j for j in nxt if j not in on_na]
        g_ctx, new_s, w_stack, *side = _gla(p_ctx, wgate_p, gla_b_gate[l], gla_norm[l], seq, 0, nb, layer=l,
                                            with_sfin=True, depth=depth, carried=new_s,
                                            make_side=[w_stack_cast(l)] + [w_in_cast(j) for j in on_gla])
        w_in_p.update(zip(on_gla, side))
        g_lat, *side = _gla(p_lat, wgate_p, gla_b_gate[l], gla_norm[l], dseq, 0, ndec, rope_tabs=rope_tabs,
                            state=state_gla, layer=l, make_side=[out_cast(*ls) for ls in later])
        ffn_w_out.update(zip(later, side))
        y_gla = [g_ctx, g_lat]

        xs = [_merge(x, mod_l, post[1], yp, yn, yg, p, w_stack, cond)
              for x, yp, yn, yg, p, cond in zip(xs, y_pool, y_na, y_gla, (p_ctx, p_lat), conds)]
        xs = [_ffn(x, mod_l, pre[2], post[2], ffn_w_in[l, 1], ffn_w_out[l, 1], 2, cond) for x, cond in zip(xs, conds)]

    return (xs[0].reshape(nb, seq, d), xs[1].reshape(ndec, dseq, d), caches[0], caches[1], new_s)
```

```python
import functools
from typing import Any, Callable, NamedTuple

import numpy as np
import jax
import jax.numpy as jnp
from jax import lax
from jax.experimental import pallas as pl
from jax.experimental.pallas import tpu as pltpu

f32 = jnp.float32
bf16 = jnp.bfloat16

D_MODEL = 2048
SEQ = 256
DEC_SEQ = 2048
GRID_W = 64
N_MOD = 9
D_FF = 5504
FFN_RES = 0.5
EPS = 1e-6
NEG_INF = -1e30

POOL_GROUPS = 4
POOL_WINDOWS = (2, 4, 8, 16)
POOL_WIDTH = 1024
POOL_GC = POOL_WIDTH // POOL_GROUPS

NA_HEADS = 8
NA_HEAD_DIM = 128
NA_WIDTH = NA_HEADS * NA_HEAD_DIM
NA_WIN_H = 8
NA_WIN_W = 16

GLA_HEADS = 4
GLA_DK = 128
GLA_DV = 256
GLA_KW = GLA_HEADS * GLA_DK
GLA_VW = GLA_HEADS * GLA_DV
GLA_RANK = 16
GLA_TAU = 16.0
GLA_CHUNK = 64
ROPE_BASE = 10000.0

BRANCH_W = 1024
N_BRANCH = 3
GATE_W = N_BRANCH * D_MODEL

LANE = 128
VMEM_LIMIT = 56 * 1024 * 1024

OFF_GL = 0
OFF_POOL = OFF_GL + GATE_W
OFF_NQ = OFF_POOL + POOL_WIDTH
OFF_NK = OFF_NQ + NA_WIDTH
OFF_NV = OFF_NK + NA_WIDTH
OFF_GQ = OFF_NV + NA_WIDTH
OFF_GK = OFF_GQ + GLA_KW
OFF_GV = OFF_GK + GLA_KW
OFF_GR = OFF_GV + GLA_VW
OFF_GZ = OFF_GR + GLA_VW
TN_IN = 2304
IN_COLS_P = -(-(OFF_GZ + LANE) // TN_IN) * TN_IN

TM = 512
TF = 512
D_FF_P = -(-D_FF // TF) * TF


def _cparams(sem):
    return pltpu.CompilerParams(dimension_semantics=sem, vmem_limit_bytes=VMEM_LIMIT)


def _cond_index(i, tm, cond):
    return cond[0] + (i * tm) // cond[1]


ROW_CHUNK = 16


def _row_sweep(nrows, fn, unroll=4):
    def trip(i, carry):
        fn(pl.ds(pl.multiple_of(i * ROW_CHUNK, ROW_CHUNK), ROW_CHUNK))
        return carry

    lax.fori_loop(0, nrows // ROW_CHUNK, trip, 0, unroll=unroll)


def _row_rsqrt(x_ref, r_scr):
    n = x_ref.shape[1]

    def fn(rows):
        x = x_ref[rows, :]
        ss = jnp.sum(_lane_fold(x * x, jnp.add), axis=-1, keepdims=True)
        r_scr[rows, :] = jnp.broadcast_to(lax.rsqrt(ss * (1.0 / n) + EPS), (ROW_CHUNK, LANE))

    _row_sweep(x_ref.shape[0], fn, unroll=32)


def _lanes(r, n):
    return jnp.concatenate([r] * (n // LANE), axis=1)


def _norm_modulate(x_ref, mod_ref, g_ref, h_ref, r_scr):
    d = D_MODEL
    shift = mod_ref[:, 0:d]
    w = g_ref[...] * (1.0 + mod_ref[:, d:2 * d])
    _row_rsqrt(x_ref, r_scr)

    def fn(rows):
        h_ref[rows, :] = (x_ref[rows, :] * _lanes(r_scr[rows, :], d) * w + shift).astype(h_ref.dtype)

    _row_sweep(x_ref.shape[0], fn)


def _norm_gate_residual(y_ref, x_ref, mod_ref, g_ref, o_ref, r_scr, res_weight):
    d = D_MODEL
    w = (res_weight * mod_ref[:, 2 * d:3 * d]) * g_ref[...]
    _row_rsqrt(y_ref, r_scr)

    def fn(rows):
        o_ref[rows, :] = x_ref[rows, :] + y_ref[rows, :] * _lanes(r_scr[rows, :], d) * w

    _row_sweep(x_ref.shape[0], fn)


def _dot(a, b):
    return jnp.dot(a, b, preferred_element_type=f32)


def _dot_nt(a, b):
    return lax.dot_general(a, b, (((1,), (1,)), ((), ())), preferred_element_type=f32)


def _mod_body(step, c_ref, w_ref, b_ref, o_ref):
    del step
    c = c_ref[...]
    s = c * jax.nn.sigmoid(c)
    o_ref[...] = _dot(s.astype(bf16), w_ref[...].astype(bf16)) + b_ref[...]


def _mod_job(c_all, w_mod, b_mod, layer, nsteps, step_of):
    depth, d, n = w_mod.shape
    nc = c_all.shape[0]
    tn = n // nsteps
    assert tn * nsteps == n and tn % LANE == 0
    return _SideCast(
        (c_all, w_mod, b_mod.reshape(depth, 1, n)),
        (pl.BlockSpec((nc, d), lambda *g: (0, 0)),
         pl.BlockSpec((None, d, tn), lambda *g: (layer, 0, step_of(*g))),
         pl.BlockSpec((None, 1, tn), lambda *g: (layer, 0, step_of(*g)))),
        pl.BlockSpec((nc, tn), lambda *g: (0, step_of(*g))),
        jax.ShapeDtypeStruct((nc, n), f32),
        _mod_body)


TM_FFN = 1024


def _ffn_body(x_ref, mod_ref, pre_ref, post_ref, wgu_ref, wo_ref, o_ref, h_scr, r_scr):
    f = pl.program_id(1)

    @pl.when(f == 0)
    def _():
        _norm_modulate(x_ref, mod_ref, pre_ref, h_scr, r_scr)
        o_ref[...] = jnp.zeros_like(o_ref)

    h = h_scr[...]
    gu = _dot(h, wgu_ref[...])
    gt, up = gu[:, 0:TF], gu[:, TF:2 * TF]
    a = gt * jax.nn.sigmoid(gt) * up
    o_ref[...] += _dot(a.astype(bf16), wo_ref[...])

    @pl.when(f == pl.num_programs(1) - 1)
    def _():
        _norm_gate_residual(o_ref, x_ref, mod_ref, post_ref, o_ref, r_scr, FFN_RES)


def _ffn(x, mod_l, pre, post, w_gu, w_out2, sub, cond):
    t, d = x.shape
    fp = w_out2.shape[0]
    return pl.pallas_call(
        _ffn_body,
        grid=(t // TM_FFN, fp // TF),
        in_specs=[
            pl.BlockSpec((TM_FFN, d), lambda i, f: (i, 0)),
            pl.BlockSpec((None, 1, 3 * d), lambda i, f: (_cond_index(i, TM_FFN, cond), 0, sub)),
            pl.BlockSpec((1, d), lambda i, f: (0, 0)),
            pl.BlockSpec((1, d), lambda i, f: (0, 0)),
            pl.BlockSpec((d, 2 * TF), lambda i, f: (0, f)),
            pl.BlockSpec((TF, d), lambda i, f: (f, 0)),
        ],
        out_specs=pl.BlockSpec((TM_FFN, d), lambda i, f: (i, 0)),
        out_shape=jax.ShapeDtypeStruct((t, d), f32),
        scratch_shapes=[pltpu.VMEM((TM_FFN, d), bf16), pltpu.VMEM((TM_FFN, LANE), f32)],
        compiler_params=_cparams(("parallel", "arbitrary")),
        name="ffn",
    )(x, mod_l, pre, post, w_gu, w_out2)


def _inproj_body(x_ref, mod_ref, pre_ref, w_ref, o_ref, h_scr, r_scr):
    @pl.when(pl.program_id(1) == 0)
    def _():
        _norm_modulate(x_ref, mod_ref, pre_ref, h_scr, r_scr)

    o_ref[...] = _dot(h_scr[...], w_ref[...]).astype(o_ref.dtype)


def _inproj(x, mod_l, pre, w_in_p, cond, out_dtype):
    t, d = x.shape
    tm = TM * (4 // jnp.dtype(out_dtype).itemsize)
    n = w_in_p.shape[1]
    return pl.pallas_call(
        _inproj_body,
        grid=(t // tm, n // TN_IN),
        in_specs=[
            pl.BlockSpec((tm, d), lambda i, j: (i, 0)),
            pl.BlockSpec((None, 1, 3 * d), lambda i, j: (_cond_index(i, tm, cond), 0, 1)),
            pl.BlockSpec((1, d), lambda i, j: (0, 0)),
            pl.BlockSpec((d, TN_IN), lambda i, j: (0, j)),
        ],
        out_specs=pl.BlockSpec((tm, TN_IN), lambda i, j: (i, j)),
        out_shape=jax.ShapeDtypeStruct((t, n), out_dtype),
        scratch_shapes=[pltpu.VMEM((tm, d), bf16), pltpu.VMEM((tm, LANE), f32)],
        compiler_params=_cparams(("parallel", "arbitrary")),
        name="inproj",
    )(x, mod_l, pre, w_in_p)


POOL_PAD = 8


def _pool_body(u_ref, w_ref, sc_ref, o_ref, pad_scr, lvl_scr, *, seq):
    gc = POOL_GC
    pad = POOL_PAD
    n_lvl = seq + pad
    zeros = jnp.zeros((pad, POOL_WIDTH), f32)
    pad_scr[pl.ds(0, pad), :] = zeros
    pad_scr[pl.ds(pad + seq, pad), :] = zeros
    pad_scr[pl.ds(pad, seq), :] = u_ref[...].astype(f32)
    lvl_scr[:, pl.ds(n_lvl, pad), :] = jnp.zeros((2, pad, gc), f32)
    t = lax.broadcasted_iota(jnp.int32, (seq, 1), 0)
    for gi, win in enumerate(POOL_WINDOWS):
        cols = pl.ds(gi * gc, gc)
        read = lambda off, n: pad_scr[pl.ds(off, n), cols]
        k, slot = 1, 0
        while 2 * k < win:
            lvl_scr[slot, pl.ds(0, n_lvl), :] = read(0, n_lvl) + read(k, n_lvl)
            read = functools.partial(lambda s_, off, n: lvl_scr[s_, pl.ds(off, n), :], slot)
            k, slot = 2 * k, 1 - slot
        lo = jnp.maximum(t - win // 2, 0)
        hi = jnp.minimum(t + win - 1 - win // 2, seq - 1)
        inv_cnt = 1.0 / (hi - lo + 1).astype(f32)
        acc = read(pad - win // 2, seq) + read(pad, seq)
        pooled = acc * inv_cnt - pad_scr[pl.ds(pad, seq), cols]
        y = _dot(pooled.astype(bf16), w_ref[gi])
        o_ref[:, cols] = (y * sc_ref[:, cols]).astype(o_ref.dtype)


def _pool(p, pool_w, pool_scale, seq, row_block0, nseq):
    cb = OFF_POOL // POOL_WIDTH
    return pl.pallas_call(
        functools.partial(_pool_body, seq=seq),
        grid=(nseq,),
        in_specs=[
            pl.BlockSpec((seq, POOL_WIDTH), lambda s: (row_block0 + s, cb)),
            pl.BlockSpec((POOL_GROUPS, POOL_GC, POOL_GC), lambda s: (0, 0, 0)),
            pl.BlockSpec((1, POOL_WIDTH), lambda s: (0, 0)),
        ],
        out_specs=pl.BlockSpec((seq, POOL_WIDTH), lambda s: (s, 0)),
        out_shape=jax.ShapeDtypeStruct((nseq * seq, POOL_WIDTH), bf16),
        scratch_shapes=[pltpu.VMEM((seq + 2 * POOL_PAD, POOL_WIDTH), f32),
                        pltpu.VMEM((2, seq + 2 * POOL_PAD, POOL_GC), f32)],
        compiler_params=_cparams(("parallel",)),
        name="pool",
    )(p, pool_w, pool_scale)


def _lane_fold(x, op):
    parts = [x[:, i * LANE:(i + 1) * LANE] for i in range(x.shape[1] // LANE)]
    while len(parts) > 1:
        parts = [op(parts[i], parts[i + 1]) for i in range(0, len(parts) - 1, 2)] + parts[len(parts) & ~1:]
    return parts[0]


def _softmax_rows(s):
    m = jnp.max(_lane_fold(s, jnp.maximum), axis=-1, keepdims=True)
    e = jnp.exp(s - m)
    return e / jnp.sum(_lane_fold(e, jnp.add), axis=-1, keepdims=True)


def _ctx_attn_body(q_ref, k_ref, v_ref, *refs, n_alias, side, layer):
    refs = refs[n_alias:]
    n_src = sum(n_in for _, n_in in side)
    o_ref, nk_ref, nv_ref = refs[n_src:n_src + 3]
    for job, src_refs, dst_ref in _side_refs(side, refs[:n_src], refs[n_src + 3:]):
        job(pl.program_id(0), *src_refs, dst_ref)
    if not n_alias:
        for ref in (nk_ref, nv_ref):
            for other in range(ref.shape[0]):
                if other != layer:
                    ref[other] = jnp.zeros(ref.shape[1:], ref.dtype)
        nk_ref, nv_ref = nk_ref.at[layer], nv_ref.at[layer]
    hd = NA_HEAD_DIM
    for h in range(NA_HEADS):
        cols = pl.ds(h * hd, hd)
        kf = k_ref[:, cols]
        vf = v_ref[:, cols]
        nk_ref[h] = kf
        nv_ref[h] = vf
        p = _softmax_rows(_dot_nt(q_ref[:, cols].astype(bf16), kf.astype(bf16)) * (hd ** -0.5))
        o_ref[:, cols] = _dot(p.astype(bf16), vf.astype(bf16)).astype(o_ref.dtype)


def _ctx_attn(p, nseq, layer, depth, caches=None, make_side=()):
    spec = lambda off: pl.BlockSpec((SEQ, NA_WIDTH), lambda b: (b, off // NA_WIDTH))
    cache_shape = jax.ShapeDtypeStruct((nseq, depth, NA_HEADS, SEQ, NA_HEAD_DIM), f32)
    in_specs = [spec(OFF_NQ), spec(OFF_NK), spec(OFF_NV)]
    args = [p, p, p]
    aliases = {}
    if caches is None:
        cache_spec = pl.BlockSpec((None, depth, NA_HEADS, SEQ, NA_HEAD_DIM), lambda b: (b, 0, 0, 0, 0))
    else:
        cache_spec = pl.BlockSpec((None, None, NA_HEADS, SEQ, NA_HEAD_DIM), lambda b: (b, layer, 0, 0, 0))
        in_specs += [pl.BlockSpec(memory_space=pl.ANY)] * 2
        args += list(caches)
        aliases = {3: 1, 4: 2}
    side_in, side_args, side_out, side_shape, side = _side_io([make(nseq, lambda b: b) for make in make_side])
    return pl.pallas_call(
        functools.partial(_ctx_attn_body, n_alias=len(aliases), side=side, layer=layer),
        grid=(nseq,),
        in_specs=in_specs + side_in,
        out_specs=[pl.BlockSpec((SEQ, NA_WIDTH), lambda b: (b, 0)), cache_spec, cache_spec] + side_out,
        out_shape=[jax.ShapeDtypeStruct((nseq * SEQ, NA_WIDTH), bf16), cache_shape, cache_shape] + side_shape,
        input_output_aliases=aliases,
        compiler_params=_cparams(("arbitrary",)),
        name="ctx_attn",
    )(*args, *side_args)


LOG2E = float(np.log2(np.e))


def _na_bias_table(rpb):
    qc = np.arange(GRID_W)[:, None]
    kc = np.arange(GRID_W)[None, :]
    cs = np.clip(qc - NA_WIN_W // 2, 0, GRID_W - NA_WIN_W)
    ok = (kc >= cs) & (kc < cs + NA_WIN_W)
    cidx = np.clip(kc - qc + NA_WIN_W - 1, 0, 2 * NA_WIN_W - 2)
    onehot = jnp.asarray((cidx[None] == np.arange(2 * NA_WIN_W - 1)[:, None, None]) & ok[None], f32)
    toep = jnp.einsum('...rc,cqk->...rqk', rpb.astype(f32), onehot, precision=lax.Precision.HIGHEST)
    toep = jnp.where(ok, toep * LOG2E, NEG_INF)
    return jnp.concatenate([toep[..., :-1, :, :], toep[..., 1:, :, :]], axis=-1)


NA_UNROLL = 32
NA_SOFTMAX_UNROLL = 8
NA_CTX_ROWS = 256


def _na_body(*refs, rows, side):
    n_src = sum(n_in for _, n_in in side)
    qb_scr, kb_scr, vb_scr, ck_ref, cv_ref, bias_ref = refs[:6]
    o_ref = refs[6 + n_src]
    sl_scr, sc_scr, el_scr, ec_scr, den_scr, oc_scr = refs[7 + n_src + len(side):]
    step = pl.program_id(0) * pl.num_programs(1) + pl.program_id(1)
    for job, src_refs, dst_ref in _side_refs(side, refs[6:6 + n_src], refs[7 + n_src:7 + n_src + len(side)]):
        job(step, *src_refs, dst_ref)

    hd = NA_HEAD_DIM
    scale = hd ** -0.5 * LOG2E
    kh = min(NA_WIN_H, rows)
    nloc = kh * GRID_W
    n = rows * GRID_W
    assert qb_scr.dtype == bf16
    ck = ck_ref[...].astype(bf16)
    cv = cv_ref[...].astype(bf16)

    def row_slices(r):
        rs = jnp.clip(r - kh // 2, 0, rows - kh)
        q_rows = pl.ds(pl.multiple_of(r * GRID_W, GRID_W), GRID_W)
        k_rows = pl.ds(pl.multiple_of(rs * GRID_W, GRID_W), nloc)
        return rs, q_rows, k_rows

    def ctx_scores(i, carry):
        blk = pl.ds(pl.multiple_of(i * NA_CTX_ROWS, NA_CTX_ROWS), NA_CTX_ROWS)
        sc_scr[blk, :] = _dot_nt(qb_scr[blk, :], ck) * scale
        return carry

    lax.fori_loop(0, n // NA_CTX_ROWS, ctx_scores, 0, unroll=8)

    def loc_scores(r, carry):
        rs, q_rows, k_rows = row_slices(r)
        first = rs - r + NA_WIN_H - 1
        bias = jnp.concatenate([bias_ref[first + 2 * e] for e in range(kh // 2)], axis=1)
        sl_scr[q_rows, :] = _dot_nt(qb_scr[q_rows, :], kb_scr[k_rows, :]) * scale + bias
        return carry

    lax.fori_loop(0, rows, loc_scores, 0, unroll=NA_UNROLL)

    def numerators(r, carry):
        q_rows = pl.ds(pl.multiple_of(r * GRID_W, GRID_W), GRID_W)
        s_loc = sl_scr[q_rows, :]
        s_ctx = sc_scr[q_rows, :]
        m = jnp.max(jnp.maximum(_lane_fold(s_loc, jnp.maximum), _lane_fold(s_ctx, jnp.maximum)),
                    axis=-1, keepdims=True)
        e_loc = jnp.exp2(s_loc - m)
        e_ctx = jnp.exp2(s_ctx - m)
        den = jnp.sum(_lane_fold(e_loc, jnp.add) + _lane_fold(e_ctx, jnp.add), axis=-1, keepdims=True)
        el_scr[q_rows, :] = e_loc.astype(bf16)
        ec_scr[q_rows, :] = e_ctx.astype(bf16)
        den_scr[q_rows, :] = jnp.broadcast_to(den, (GRID_W, hd))
        return carry

    lax.fori_loop(0, rows, numerators, 0, unroll=NA_SOFTMAX_UNROLL)

    def ctx_values(i, carry):
        blk = pl.ds(pl.multiple_of(i * NA_CTX_ROWS, NA_CTX_ROWS), NA_CTX_ROWS)
        oc_scr[blk, :] = _dot(ec_scr[blk, :], cv)
        return carry

    lax.fori_loop(0, n // NA_CTX_ROWS, ctx_values, 0, unroll=8)

    def loc_values(r, carry):
        _, q_rows, k_rows = row_slices(r)
        o = _dot(el_scr[q_rows, :], vb_scr[k_rows, :]) + oc_scr[q_rows, :]
        o_ref[q_rows, :] = (o / den_scr[q_rows, :]).astype(o_ref.dtype)
        return carry

    lax.fori_loop(0, rows, loc_values, 0, unroll=NA_UNROLL)


def _na_latent(p, cache_k, cache_v, bias_tbl, layer, row_block0, nreq, make_side=()):
    n = DEC_SEQ
    hd = NA_HEAD_DIM
    past = cache_k.shape[3]
    rows = n // GRID_W
    side_in, side_args, side_out, side_shape, side = _side_io(
        [make(nreq * NA_HEADS, lambda b, h: b * NA_HEADS + h) for make in make_side])
    qkv = lambda off: pl.BlockSpec((n, hd), lambda b, h: (row_block0 + b, off // hd + h))
    cache = pl.BlockSpec((None, None, None, past, hd), lambda b, h: (b, layer, h, 0, 0))
    return pl.pallas_call(
        functools.partial(_na_body, rows=rows, side=side),
        grid=(nreq, NA_HEADS),
        in_specs=[qkv(OFF_NQ), qkv(OFF_NK), qkv(OFF_NV), cache, cache,
                  pl.BlockSpec((None, None, 2 * NA_WIN_H - 2, GRID_W, 2 * GRID_W),
                               lambda b, h: (layer, h, 0, 0, 0))] + side_in,
        out_specs=[pl.BlockSpec((n, hd), lambda b, h: (b, h))] + side_out,
        out_shape=[jax.ShapeDtypeStruct((nreq * n, NA_WIDTH), bf16)] + side_shape,
        scratch_shapes=[pltpu.VMEM((n, NA_WIN_H * GRID_W), f32), pltpu.VMEM((n, past), f32),
                        pltpu.VMEM((n, NA_WIN_H * GRID_W), bf16), pltpu.VMEM((n, past), bf16),
                        pltpu.VMEM((n, hd), f32), pltpu.VMEM((n, hd), f32)],
        compiler_params=_cparams(("arbitrary", "arbitrary")),
        name="na_latent",
    )(p, p, p, cache_k, cache_v, bias_tbl, *side_args)


GLA_PAD = 32
GLA_UNROLL = 32


def _rope_tables(seq):
    t = np.arange(seq)
    half = GLA_DK // 2
    nf = half // 2
    inv = ROPE_BASE ** (-np.arange(nf, dtype=np.float64) / nf)
    cos, sin = [], []
    for pos in (t // GRID_W, t % GRID_W):
        ang = pos[:, None].astype(np.float64) * inv
        cos += [np.cos(ang), np.cos(ang)]
        sin += [-np.sin(ang), np.sin(ang)]
    return (jnp.asarray(np.concatenate(cos, axis=-1), f32), jnp.asarray(np.concatenate(sin, axis=-1), f32))


def _rope(x, cos, sin_signed):
    nf = GLA_DK // 4
    lane = lax.broadcasted_iota(jnp.int32, x.shape, 1)
    partner = jnp.where(lane % (2 * nf) < nf, pltpu.roll(x, GLA_DK - nf, 1), pltpu.roll(x, nf, 1))
    return x * cos + partner * sin_signed


def _log_sigmoid(x):
    return jnp.minimum(x, 0.0) - jnp.log1p(jnp.exp(-jnp.abs(x)))


def _gla_body(*refs, seq, rope, with_s0, with_sfin, sfin_layer, n_carried, side):
    refs = list(refs)
    q_ref, k_ref, v_ref, r_ref, z_ref, wg_ref, bg_ref, ng_ref = refs[:8]
    refs = refs[8:]
    if rope:
        cos_ref, sin_ref = refs[:2]
        refs = refs[2:]
    if with_s0:
        s0_ref = refs[0]
        refs = refs[1:]
    refs = refs[n_carried:]
    n_src = sum(n_in for _, n_in in side)
    side_src, refs = refs[:n_src], refs[n_src:]
    o_ref = refs[0]
    refs = refs[1:]
    if with_sfin:
        sfin_ref = refs[0]
        refs = refs[1:]
    side_dst, refs = refs[:len(side)], refs[len(side):]
    step = pl.program_id(0) * pl.num_programs(1) + pl.program_id(1)
    for job, src_refs, dst_ref in _side_refs(side, side_src, side_dst):
        job(step, *src_refs, dst_ref)
    qi_scr, kn_scr, kd_scr, dec_scr, scan_scr, vb_scr, u_scr, sb_scr, o_scr, st_scr = refs

    ch = GLA_CHUNK
    nch = seq // ch
    dk, dv = GLA_DK, GLA_DV

    q = q_ref[...].astype(f32)
    k = k_ref[...].astype(f32)
    if rope:
        q = _rope(q, cos_ref[...], sin_ref[...])
        k = _rope(k, cos_ref[...], sin_ref[...])
    q = q * (dk ** -0.5)

    zb = z_ref[...].astype(bf16)
    pos = lax.broadcasted_iota(jnp.int32, (seq, 1), 0) % ch
    zpad = jnp.zeros((GLA_PAD, dk), f32)
    scan_scr[pl.ds(0, GLA_PAD), :] = zpad
    scan_scr[pl.ds(GLA_PAD + seq, GLA_PAD), :] = zpad
    for d in range(2):
        g = _log_sigmoid(_dot(zb, wg_ref[d]) + bg_ref[d]) / GLA_TAU
        b = g
        sh = 1
        while sh < ch:
            scan_scr[pl.ds(GLA_PAD, seq), :] = b
            if d == 0:
                b = b + jnp.where(pos >= sh, scan_scr[pl.ds(GLA_PAD - sh, seq), :], 0.0)
            else:
                b = b + jnp.where(pos < ch - sh, scan_scr[pl.ds(GLA_PAD + sh, seq), :], 0.0)
            sh *= 2
        b3 = b.reshape(nch, ch, dk)
        b_end = b3[:, ch - 1:ch, :] if d == 0 else b3[:, 0:1, :]
        lanes = pl.ds(d * dk, dk)
        qi_scr[:, lanes] = (q * jnp.exp(b)).astype(bf16)
        kn_scr[d] = (k * jnp.exp(-b)).astype(bf16)
        kd_scr[:, lanes] = (k.reshape(nch, ch, dk) * jnp.exp(b_end - b3)).reshape(seq, dk).astype(bf16)
        dec_scr[d] = jnp.exp(b_end)

    for d in range(2):
        if with_s0:
            st_scr[d] = s0_ref[d].T
        else:
            st_scr[d] = jnp.zeros((dv, dk), f32)

    ri = lax.broadcasted_iota(jnp.int32, (ch, ch), 0)
    ci = lax.broadcasted_iota(jnp.int32, (ch, ch), 1)

    vb_scr[...] = v_ref[...].astype(bf16)
    chunk_rows = lambda c: pl.ds(pl.multiple_of(c * ch, ch), ch)
    fwd, bwd = pl.ds(0, dk), pl.ds(dk, dk)

    def increments(c, carry):
        rows = chunk_rows(c)
        u_scr[c] = lax.dot_general(vb_scr[rows, :], kd_scr[rows, :], (((0,), (0,)), ((), ())),
                                   preferred_element_type=f32)
        return carry

    lax.fori_loop(0, nch, increments, 0, unroll=min(GLA_UNROLL, nch))

    def states(i, carry):
        for d, c, lanes in ((0, i, fwd), (1, nch - 1 - i, bwd)):
            st = st_scr[d]
            sb_scr[c, :, lanes] = st.astype(bf16)
            st_scr[d] = st * dec_scr[d, c] + u_scr[c, :, lanes]
        return carry

    lax.fori_loop(0, nch, states, 0, unroll=min(8, nch))

    if with_sfin:
        if not n_carried:
            for other in range(sfin_ref.shape[0]):
                if other != sfin_layer:
                    sfin_ref[other] = jnp.zeros(sfin_ref.shape[1:], sfin_ref.dtype)
            sfin_ref = sfin_ref.at[sfin_layer]
        for d in range(2):
            sfin_ref[d] = st_scr[d].T

    def outputs(c, carry):
        rows = chunk_rows(c)
        qi = qi_scr[rows, :]
        pf = _dot_nt(qi[:, 0:dk], kn_scr[0, rows, :])
        pb = _dot_nt(qi[:, dk:2 * dk], kn_scr[1, rows, :])
        a = jnp.where(ci < ri, pf, jnp.where(ci > ri, pb, pf + pb))
        o_scr[rows, :] = _dot(a.astype(bf16), vb_scr[rows, :]) + _dot_nt(qi, sb_scr[c])
        return carry

    lax.fori_loop(0, nch, outputs, 0, unroll=min(GLA_UNROLL, nch))

    o = o_scr[...]
    r = r_ref[...].astype(f32)
    o = o * lax.rsqrt(jnp.mean(o * o, axis=-1, keepdims=True) + EPS) * ng_ref[...]
    o_ref[...] = (o * (r * jax.nn.sigmoid(r))).astype(o_ref.dtype)


def _gla(p, wgate_p, b_gate, gla_norm, seq, row_block0, nreq, rope_tabs=None, state=None, layer=0,
         with_sfin=False, depth=1, carried=None, make_side=()):
    dk, dv = GLA_DK, GLA_DV
    nch = seq // GLA_CHUNK
    rope = rope_tabs is not None
    with_s0 = state is not None
    blk = lambda w, off: pl.BlockSpec((seq, w), lambda b, h: (row_block0 + b, off // w + h))
    in_specs = [blk(dk, OFF_GQ), blk(dk, OFF_GK), blk(dv, OFF_GV), blk(dv, OFF_GR),
                pl.BlockSpec((seq, LANE), lambda b, h: (row_block0 + b, OFF_GZ // LANE)),
                pl.BlockSpec((2, LANE, dk), lambda b, h: (0, 0, h)),
                pl.BlockSpec((2, 1, dk), lambda b, h: (0, 0, h)),
                pl.BlockSpec((1, dv), lambda b, h: (0, h))]
    args = [p, p, p, p, p, wgate_p, b_gate.reshape(2, 1, GLA_KW), gla_norm.reshape(1, GLA_VW)]
    if rope:
        in_specs += [pl.BlockSpec((seq, dk), lambda b, h: (0, 0))] * 2
        args += list(rope_tabs)
    if with_s0:
        in_specs.append(pl.BlockSpec((None, None, 2, None, dk, dv), lambda b, h: (b, layer, 0, h, 0, 0)))
        args.append(state)
    aliases = {}
    if carried is not None:
        aliases = {len(args): 1}
        in_specs.append(pl.BlockSpec(memory_space=pl.ANY))
        args.append(carried)
    side_in, side_args, side_out, side_shape, side = _side_io(
        [make(nreq * GLA_HEADS, lambda b, h: b * GLA_HEADS + h) for make in make_side])
    in_specs += side_in
    args += side_args
    out_specs = [pl.BlockSpec((seq, dv), lambda b, h: (b, h))]
    out_shape = [jax.ShapeDtypeStruct((nreq * seq, GLA_VW), bf16)]
    if with_sfin:
        if carried is None:
            out_specs.append(pl.BlockSpec((None, depth, 2, None, dk, dv), lambda b, h: (b, 0, 0, h, 0, 0)))
        else:
            out_specs.append(pl.BlockSpec((None, None, 2, None, dk, dv), lambda b, h: (b, layer, 0, h, 0, 0)))
        out_shape.append(jax.ShapeDtypeStruct((nreq, depth, 2, GLA_HEADS, dk, dv), f32))
    out_specs += side_out
    out_shape += side_shape
    scratch = [pltpu.VMEM((seq, 2 * dk), bf16), pltpu.VMEM((2, seq, dk), bf16), pltpu.VMEM((seq, 2 * dk), bf16),
               pltpu.VMEM((2, nch, 1, dk), f32), pltpu.VMEM((seq + 2 * GLA_PAD, dk), f32),
               pltpu.VMEM((seq, dv), bf16), pltpu.VMEM((nch, dv, 2 * dk), f32), pltpu.VMEM((nch, dv, 2 * dk), bf16),
               pltpu.VMEM((seq, dv), f32), pltpu.VMEM((2, dv, dk), f32)]
    return pl.pallas_call(
        functools.partial(_gla_body, seq=seq, rope=rope, with_s0=with_s0, with_sfin=with_sfin,
                          sfin_layer=layer, n_carried=len(aliases), side=side),
        grid=(nreq, GLA_HEADS),
        in_specs=in_specs,
        out_specs=out_specs,
        out_shape=out_shape,
        input_output_aliases=aliases,
        scratch_shapes=scratch,
        compiler_params=_cparams(("arbitrary", "arbitrary")),
        name="gla",
    )(*args)


TM_MERGE = 512
MERGE_SUB = 2


def _merge_body(x_ref, mod_ref, post_ref, bp_ref, bn_ref, bg_ref, gl_ref, w_ref, o_ref, m_scr, mb_scr, r_scr):
    n = pl.program_id(1)
    m_ref = m_scr.at[pl.program_id(2)]
    mb_ref = mb_scr.at[pl.program_id(2)]

    for bi, br_ref in enumerate((bp_ref, bn_ref, bg_ref)):
        @pl.when(n == bi)
        def _():
            y = jax.nn.sigmoid(gl_ref[...].astype(f32)) * _dot(br_ref[...], w_ref[...])
            m_ref[...] = y if bi == 0 else m_ref[...] + y

    @pl.when(n == N_BRANCH)
    def _():
        mb_ref[...] = m_ref[...].astype(bf16)
        m_ref[...] = _dot(mb_ref[:, 0:BRANCH_W], w_ref[...])

    @pl.when(n == N_BRANCH + 1)
    def _():
        m_ref[...] += _dot(mb_ref[:, BRANCH_W:2 * BRANCH_W], w_ref[...])
        _norm_gate_residual(m_ref, x_ref, mod_ref, post_ref, o_ref, r_scr, 1.0)


def _merge(x, mod_l, post, y_pool, y_na, y_gla, p, w_stack, cond):
    t, d = x.shape
    tm = TM_MERGE
    sub = MERGE_SUB
    nsteps = N_BRANCH + d // BRANCH_W
    tile = lambda i, s: i * sub + s

    def rows_at(first, last):
        def index(i, n, s):
            return jnp.where(n < first, jnp.maximum(tile(i, 0) - 1, 0),
                             jnp.where(n > last, tile(i, sub - 1), tile(i, s)))
        return index

    last = nsteps - 1
    br = lambda step: pl.BlockSpec((tm, BRANCH_W), lambda i, n, s: (rows_at(step, step)(i, n, s), 0))
    xo = pl.BlockSpec((tm, d), lambda i, n, s: (rows_at(last, last)(i, n, s), 0))
    return pl.pallas_call(
        _merge_body,
        grid=(t // (tm * sub), nsteps, sub),
        in_specs=[
            xo,
            pl.BlockSpec((None, 1, 3 * d), lambda i, n, s: (_cond_index(tile(i, s), tm, cond), 0, 1)),
            pl.BlockSpec((1, d), lambda i, n, s: (0, 0)),
            br(0), br(1), br(2),
            pl.BlockSpec((tm, d),
                         lambda i, n, s: (rows_at(0, N_BRANCH - 1)(i, n, s), jnp.minimum(n, N_BRANCH - 1))),
            pl.BlockSpec((BRANCH_W, d), lambda i, n, s: (n, 0)),
        ],
        out_specs=xo,
        out_shape=jax.ShapeDtypeStruct((t, d), f32),
        scratch_shapes=[pltpu.VMEM((sub, tm, d), f32), pltpu.VMEM((sub, tm, d), bf16), pltpu.VMEM((tm, LANE), f32)],
        compiler_params=_cparams(("arbitrary", "arbitrary", "arbitrary")),
        name="merge",
    )(x, mod_l, post, y_pool, y_na, y_gla, p, w_stack)


_IN_SPLITS = (POOL_WIDTH, NA_WIDTH, NA_WIDTH, NA_WIDTH, GLA_KW, GLA_KW, GLA_VW, 2 * GLA_RANK, GLA_VW, GATE_W)
_IN_OFFS = tuple(int(v) for v in np.cumsum((0,) + _IN_SPLITS))
_IN_RUNS = ((_IN_OFFS[9], _IN_OFFS[10]), (_IN_OFFS[0], _IN_OFFS[7]), (_IN_OFFS[8], _IN_OFFS[9]),
            (_IN_OFFS[7], _IN_OFFS[8]))


CT_IN = 512
_IN_TILE_STARTS = []
for _a, _b in _IN_RUNS:
    _IN_TILE_STARTS += [_a + CT_IN * _t for _t in range(-(-(_b - _a) // CT_IN))]
assert len(_IN_TILE_STARTS) * CT_IN == IN_COLS_P and all(v % 8 == 0 for v in _IN_TILE_STARTS)
_IN_LAST_VALID = (_IN_RUNS[-1][1] - _IN_RUNS[-1][0]) % CT_IN or CT_IN
assert all((b - a) % CT_IN == 0 for a, b in _IN_RUNS[:-1])


def _w_in_tile_start(j):
    out = jnp.int32(_IN_TILE_STARTS[0]) + CT_IN * j
    for t in range(1, len(_IN_TILE_STARTS)):
        if _IN_TILE_STARTS[t] != _IN_TILE_STARTS[t - 1] + CT_IN:
            out = jnp.where(j >= t, _IN_TILE_STARTS[t] + CT_IN * (j - t), out)
    return out


N_IN_TILES = IN_COLS_P // CT_IN


def _cast_w_in_body(step, w_ref, o_ref):
    @pl.when(step < N_IN_TILES)
    def _():
        col = lax.broadcasted_iota(jnp.int32, (1, CT_IN), 1)
        valid = jnp.where(step == N_IN_TILES - 1, _IN_LAST_VALID, CT_IN)
        o_ref[...] = jnp.where(col < valid, w_ref[...].T, 0.0).astype(bf16)


def _w_in_cast(wt, n_cols, layer, nsteps, step_of):
    assert nsteps >= N_IN_TILES
    d = wt.shape[1]
    tile = lambda *g: jnp.minimum(step_of(*g), N_IN_TILES - 1)
    return _SideCast(
        (wt,),
        (pl.BlockSpec((pl.Element(CT_IN), pl.Element(d)),
                      lambda *g: (pl.multiple_of(layer * n_cols + _w_in_tile_start(tile(*g)), 8), 0)),),
        pl.BlockSpec((d, CT_IN), lambda *g: (0, tile(*g))),
        jax.ShapeDtypeStruct((d, IN_COLS_P), bf16),
        _cast_w_in_body)


class _SideCast(NamedTuple):
    arrays: tuple
    in_specs: tuple
    out_spec: pl.BlockSpec
    out_shape: jax.ShapeDtypeStruct
    body: Callable


def _side_refs(side, src_refs, dst_refs):
    jobs, k = [], 0
    for (body, n_in), dst in zip(side, dst_refs):
        jobs.append((body, src_refs[k:k + n_in], dst))
        k += n_in
    return jobs


def _side_io(side):
    return ([sp for j in side for sp in j.in_specs], [a for j in side for a in j.arrays],
            [j.out_spec for j in side], [j.out_shape for j in side], tuple((j.body, len(j.arrays)) for j in side))


def _cast_ffn_in_body(step, w_ref, o_ref):
    del step
    for f in range(D_FF_P // TF):
        n = min(TF, D_FF - f * TF)
        for half in range(2):
            dst = (2 * f + half) * TF
            o_ref[:, dst:dst + n] = w_ref[:, half * D_FF + f * TF:half * D_FF + f * TF + n].astype(bf16)
            if n < TF:
                o_ref[:, dst + n:dst + TF] = jnp.zeros((w_ref.shape[0], TF - n), bf16)


def _ffn_in_cast(w, layer, slot, nsteps, step_of):
    d = w.shape[2]
    rows = d // nsteps
    assert rows * nsteps == d and rows % 16 == 0
    return _SideCast(
        (w,),
        (pl.BlockSpec((None, None, rows, 2 * D_FF), lambda *g: (layer, slot, step_of(*g), 0)),),
        pl.BlockSpec((rows, 2 * D_FF_P), lambda *g: (step_of(*g), 0)),
        jax.ShapeDtypeStruct((d, 2 * D_FF_P), bf16),
        _cast_ffn_in_body)


def _cast_ffn_out_body(step, w_ref, o_ref):
    rows = w_ref.shape[0]
    row = step * rows + lax.broadcasted_iota(jnp.int32, (rows, 1), 0)
    o_ref[...] = jnp.where(row < D_FF, w_ref[...], 0.0).astype(bf16)


def _ffn_out_cast(w, layer, slot, nsteps, step_of):
    d = w.shape[3]
    rows = D_FF_P // nsteps
    assert rows * nsteps == D_FF_P and rows % 16 == 0 and rows * (nsteps - 1) < D_FF
    return _SideCast(
        (w,),
        (pl.BlockSpec((None, None, rows, d), lambda *g: (layer, slot, step_of(*g), 0)),),
        pl.BlockSpec((rows, d), lambda *g: (step_of(*g), 0)),
        jax.ShapeDtypeStruct((D_FF_P, d), bf16),
        _cast_ffn_out_body)


def _run_cast_body(*refs, cast):
    cast(pl.program_id(0), *refs)


def _run_cast(make_job, nsteps, name):
    job = make_job(nsteps, lambda r: r)
    return pl.pallas_call(
        functools.partial(_run_cast_body, cast=job.body),
        grid=(nsteps,),
        in_specs=list(job.in_specs),
        out_specs=job.out_spec,
        out_shape=job.out_shape,
        compiler_params=_cparams(("parallel",)),
        name=name,
    )(*job.arrays)


W_STACK_ROWS = 256


def _cast_w_stack_body(step, wb_ref, wo_ref, o_ref, *, n_branch, n_out):
    @pl.when(step < n_branch)
    def _():
        o_ref[...] = wb_ref[...].astype(bf16)

    @pl.when(jnp.logical_and(step >= n_branch, step < n_branch + n_out))
    def _():
        o_ref[...] = wo_ref[...].astype(bf16)


def _w_stack_cast(w_branch, w_out, layer, nsteps, step_of):
    depth, nb, bw, d = w_branch.shape
    r = W_STACK_ROWS
    n_branch, n_out = nb * bw // r, d // r
    assert nsteps >= n_branch + n_out and bw % r == 0
    t = lambda *g: step_of(*g)
    return _SideCast(
        (w_branch.reshape(depth * nb * bw, d), w_out.reshape(depth * d, d)),
        (pl.BlockSpec((r, d), lambda *g: (layer * n_branch + jnp.minimum(t(*g), n_branch - 1), 0)),
         pl.BlockSpec((r, d), lambda *g: (layer * n_out + jnp.clip(t(*g) - n_branch, 0, n_out - 1), 0))),
        pl.BlockSpec((r, d), lambda *g: (jnp.minimum(t(*g), n_branch + n_out - 1), 0)),
        jax.ShapeDtypeStruct((nb * bw + d, d), bf16),
        functools.partial(_cast_w_stack_body, n_branch=n_branch, n_out=n_out))


def _prep_gate(w_gate):
    out = jnp.zeros((2, LANE, GLA_KW), f32)
    for d in range(2):
        out = out.at[d, d * GLA_RANK:(d + 1) * GLA_RANK].set(w_gate[d])
    return out.astype(bf16)


def kernel(x_prompt, x_sample, c, cache_na_k, cache_na_v, state_gla, c_ctx, w_mod, b_mod, norm_pre, norm_post,
           w_ffn_in, w_ffn_out, w_in, pool_w, pool_scale, na_rpb, gla_w_gate, gla_b_gate, gla_norm, w_branch,
           w_out):
    nb, seq, d = x_prompt.shape
    ndec, dseq, _ = x_sample.shape
    depth = w_mod.shape[0]
    n_ctx = nb * seq
    n_lat = ndec * dseq
    assert (seq, dseq, d) == (SEQ, DEC_SEQ, D_MODEL)
    assert n_ctx % (TM_MERGE * MERGE_SUB) == 0 and n_ctx % TM_FFN == 0 and dseq % TM_FFN == 0

    xs = [x_prompt.reshape(n_ctx, d), x_sample.reshape(n_lat, d)]
    conds = [(0, n_ctx), (1, dseq)]
    ncond = -(-(1 + ndec) // 8) * 8
    c_all = jnp.concatenate([c_ctx[None], c, jnp.zeros((ncond - 1 - ndec, d), f32)], axis=0)
    rope_tabs = _rope_tables(dseq)
    bias_tbl = _na_bias_table(na_rpb)

    mod_job = lambda l: functools.partial(_mod_job, c_all, w_mod, b_mod, l)
    in_cast = lambda l, s: functools.partial(_ffn_in_cast, w_ffn_in, l, s)
    out_cast = lambda l, s: functools.partial(_ffn_out_cast, w_ffn_out, l, s)
    wt_in = jnp.swapaxes(w_in, 1, 2).reshape(depth * w_in.shape[2], d)
    w_in_cast = lambda l: functools.partial(_w_in_cast, wt_in, w_in.shape[2], l)
    w_stack_cast = lambda l: functools.partial(_w_stack_cast, w_branch, w_out, l)
    mods = {0: _run_cast(mod_job(0), N_MOD * d // 1024, "modulation")}
    ffn_w_in = {(0, 0): _run_cast(in_cast(0, 0), d // 256, "cast_ffn_in")}
    ffn_w_out = {(0, 0): _run_cast(out_cast(0, 0), D_FF_P // TF, "cast_ffn_out")}
    w_in_p = {0: _run_cast(w_in_cast(0), N_IN_TILES, "cast_w_in")}

    caches = None
    new_s = None
    for l in range(depth):
        mod_l = mods[l].reshape(ncond, 1, N_MOD * d)
        pre = norm_pre[l].reshape(3, 1, d)
        post = norm_post[l].reshape(3, 1, d)
        wgate_p = _prep_gate(gla_w_gate[l])
        pw = pool_w[l].astype(bf16)
        psc = pool_scale[l].reshape(1, POOL_WIDTH)
        later = [(l, 1)] + ([(l + 1, 0)] if l + 1 < depth else [])
        nxt = [l + 1] if l + 1 < depth else []

        xs = [_ffn(x, mod_l, pre[0], post[0], ffn_w_in[l, 0], ffn_w_out[l, 0], 0, cond) for x, cond in zip(xs, conds)]
        p_ctx, p_lat = [_inproj(x, mod_l, pre[1], w_in_p[l], cond, dt) for x, cond, dt in zip(xs, conds, (f32, bf16))]

        y_pool = [_pool(p_ctx, pw, psc, seq, 0, nb), _pool(p_lat, pw, psc, dseq, 0, ndec)]
        na_ctx, new_k, new_v, *side = _ctx_attn(p_ctx, nb, l, depth, caches, make_side=[mod_job(j) for j in nxt])
        mods.update(zip(nxt, side))
        caches = (new_k, new_v)
        on_na = nxt if ndec * NA_HEADS >= N_IN_TILES else []
        na_lat, *side = _na_latent(p_lat, cache_na_k, cache_na_v, bias_tbl, l, 0, ndec,
                                   make_side=[in_cast(*ls) for ls in later] + [w_in_cast(j) for j in on_na])
        ffn_w_in.update(zip(later, side))
        w_in_p.update(zip(on_na, side[len(later):]))
        y_na = [na_ctx, na_lat]
        on_gla = [j for j in nxt if j not in on_na]
        g_ctx, new_s, w_stack, *side = _gla(p_ctx, wgate_p, gla_b_gate[l], gla_norm[l], seq, 0, nb, layer=l,
                                            with_sfin=True, depth=depth, carried=new_s,
                                            make_side=[w_stack_cast(l)] + [w_in_cast(j) for j in on_gla])
        w_in_p.update(zip(on_gla, side))
        g_lat, *side = _gla(p_lat, wgate_p, gla_b_gate[l], gla_norm[l], dseq, 0, ndec, rope_tabs=rope_tabs,
                            state=state_gla, layer=l, make_side=[out_cast(*ls) for ls in later])
        ffn_w_out.update(zip(later, side))
        y_gla = [g_ctx, g_lat]

        xs = [_merge(x, mod_l, post[1], yp, yn, yg, p, w_stack, cond)
              for x, yp, yn, yg, p, cond in zip(xs, y_pool, y_na, y_gla, (p_ctx, p_lat), conds)]
        xs = [_ffn(x, mod_l, pre[2], post[2], ffn_w_in[l, 1], ffn_w_out[l, 1], 2, cond) for x, cond in zip(xs, conds)]

    return (xs[0].reshape(nb, seq, d), xs[1].reshape(ndec, dseq, d), caches[0], caches[1], new_s)
```

```python
import functools
from typing import Any, Callable, NamedTuple

import numpy as np
import jax
import jax.numpy as jnp
from jax import lax
from jax.experimental import pallas as pl
from jax.experimental.pallas import tpu as pltpu

f32 = jnp.float32
bf16 = jnp.bfloat16

D_MODEL = 2048
SEQ = 256
DEC_SEQ = 2048
GRID_W = 64
N_MOD = 9
D_FF = 5504
FFN_RES = 0.5
EPS = 1e-6
NEG_INF = -1e30

POOL_GROUPS = 4
POOL_WINDOWS = (2, 4, 8, 16)
POOL_WIDTH = 1024
POOL_GC = POOL_WIDTH // POOL_GROUPS

NA_HEADS = 8
NA_HEAD_DIM = 128
NA_WIDTH = NA_HEADS * NA_HEAD_DIM
NA_WIN_H = 8
NA_WIN_W = 16

GLA_HEADS = 4
GLA_DK = 128
GLA_DV = 256
GLA_KW = GLA_HEADS * GLA_DK
GLA_VW = GLA_HEADS * GLA_DV
GLA_RANK = 16
GLA_TAU = 16.0
GLA_CHUNK = 64
ROPE_BASE = 10000.0

BRANCH_W = 1024
N_BRANCH = 3
GATE_W = N_BRANCH * D_MODEL

LANE = 128
VMEM_LIMIT = 56 * 1024 * 1024

OFF_GL = 0
OFF_POOL = OFF_GL + GATE_W
OFF_NQ = OFF_POOL + POOL_WIDTH
OFF_NK = OFF_NQ + NA_WIDTH
OFF_NV = OFF_NK + NA_WIDTH
OFF_GQ = OFF_NV + NA_WIDTH
OFF_GK = OFF_GQ + GLA_KW
OFF_GV = OFF_GK + GLA_KW
OFF_GR = OFF_GV + GLA_VW
OFF_GZ = OFF_GR + GLA_VW
TN_IN = 2304
IN_COLS_P = -(-(OFF_GZ + LANE) // TN_IN) * TN_IN

TM = 512
TF = 512
D_FF_P = -(-D_FF // TF) * TF


def _cparams(sem):
    return pltpu.CompilerParams(dimension_semantics=sem, vmem_limit_bytes=VMEM_LIMIT)


def _cond_index(i, tm, cond):
    return cond[0] + (i * tm) // cond[1]


ROW_CHUNK = 16


def _row_sweep(nrows, fn, unroll=4):
    def trip(i, carry):
        fn(pl.ds(pl.multiple_of(i * ROW_CHUNK, ROW_CHUNK), ROW_CHUNK))
        return carry

    lax.fori_loop(0, nrows // ROW_CHUNK, trip, 0, unroll=unroll)


def _row_rsqrt(x_ref, r_scr):
    n = x_ref.shape[1]

    def fn(rows):
        x = x_ref[rows, :]
        ss = jnp.sum(_lane_fold(x * x, jnp.add), axis=-1, keepdims=True)
        r_scr[rows, :] = jnp.broadcast_to(lax.rsqrt(ss * (1.0 / n) + EPS), (ROW_CHUNK, LANE))

    _row_sweep(x_ref.shape[0], fn, unroll=32)


def _lanes(r, n):
    return jnp.concatenate([r] * (n // LANE), axis=1)


def _norm_modulate(x_ref, mod_ref, g_ref, h_ref, r_scr):
    d = D_MODEL
    shift = mod_ref[:, 0:d]
    w = g_ref[...] * (1.0 + mod_ref[:, d:2 * d])
    _row_rsqrt(x_ref, r_scr)

    def fn(rows):
        h_ref[rows, :] = (x_ref[rows, :] * _lanes(r_scr[rows, :], d) * w + shift).astype(h_ref.dtype)

    _row_sweep(x_ref.shape[0], fn)


def _norm_gate_residual(y_ref, x_ref, mod_ref, g_ref, o_ref, r_scr, res_weight):
    d = D_MODEL
    w = (res_weight * mod_ref[:, 2 * d:3 * d]) * g_ref[...]
    _row_rsqrt(y_ref, r_scr)

    def fn(rows):
        o_ref[rows, :] = x_ref[rows, :] + y_ref[rows, :] * _lanes(r_scr[rows, :], d) * w

    _row_sweep(x_ref.shape[0], fn)


def _dot(a, b):
    return jnp.dot(a, b, preferred_element_type=f32)


def _dot_nt(a, b):
    return lax.dot_general(a, b, (((1,), (1,)), ((), ())), preferred_element_type=f32)


def _mod_body(step, c_ref, w_ref, b_ref, o_ref):
    del step
    c = c_ref[...]
    s = c * jax.nn.sigmoid(c)
    o_ref[...] = _dot(s.astype(bf16), w_ref[...].astype(bf16)) + b_ref[...]


def _mod_job(c_all, w_mod, b_mod, layer, nsteps, step_of):
    depth, d, n = w_mod.shape
    nc = c_all.shape[0]
    tn = n // nsteps
    assert tn * nsteps == n and tn % LANE == 0
    return _SideCast(
        (c_all, w_mod, b_mod.reshape(depth, 1, n)),
        (pl.BlockSpec((nc, d), lambda *g: (0, 0)),
         pl.BlockSpec((None, d, tn), lambda *g: (layer, 0, step_of(*g))),
         pl.BlockSpec((None, 1, tn), lambda *g: (layer, 0, step_of(*g)))),
        pl.BlockSpec((nc, tn), lambda *g: (0, step_of(*g))),
        jax.ShapeDtypeStruct((nc, n), f32),
        _mod_body)


TM_FFN = 1024


def _ffn_body(x_ref, mod_ref, pre_ref, post_ref, wgu_ref, wo_ref, o_ref, h_scr, r_scr):
    f = pl.program_id(1)

    @pl.when(f == 0)
    def _():
        _norm_modulate(x_ref, mod_ref, pre_ref, h_scr, r_scr)
        o_ref[...] = jnp.zeros_like(o_ref)

    h = h_scr[...]
    gu = _dot(h, wgu_ref[...])
    gt, up = gu[:, 0:TF], gu[:, TF:2 * TF]
    a = gt * jax.nn.sigmoid(gt) * up
    o_ref[...] += _dot(a.astype(bf16), wo_ref[...])

    @pl.when(f == pl.num_programs(1) - 1)
    def _():
        _norm_gate_residual(o_ref, x_ref, mod_ref, post_ref, o_ref, r_scr, FFN_RES)


def _ffn(x, mod_l, pre, post, w_gu, w_out2, sub, cond):
    t, d = x.shape
    fp = w_out2.shape[0]
    return pl.pallas_call(
        _ffn_body,
        grid=(t // TM_FFN, fp // TF),
        in_specs=[
            pl.BlockSpec((TM_FFN, d), lambda i, f: (i, 0)),
            pl.BlockSpec((None, 1, 3 * d), lambda i, f: (_cond_index(i, TM_FFN, cond), 0, sub)),
            pl.BlockSpec((1, d), lambda i, f: (0, 0)),
            pl.BlockSpec((1, d), lambda i, f: (0, 0)),
            pl.BlockSpec((d, 2 * TF), lambda i, f: (0, f)),
            pl.BlockSpec((TF, d), lambda i, f: (f, 0)),
        ],
        out_specs=pl.BlockSpec((TM_FFN, d), lambda i, f: (i, 0)),
        out_shape=jax.ShapeDtypeStruct((t, d), f32),
        scratch_shapes=[pltpu.VMEM((TM_FFN, d), bf16), pltpu.VMEM((TM_FFN, LANE), f32)],
        compiler_params=_cparams(("parallel", "arbitrary")),
        name="ffn",
    )(x, mod_l, pre, post, w_gu, w_out2)


def _inproj_body(x_ref, mod_ref, pre_ref, w_ref, o_ref, h_scr, r_scr):
    @pl.when(pl.program_id(1) == 0)
    def _():
        _norm_modulate(x_ref, mod_ref, pre_ref, h_scr, r_scr)

    o_ref[...] = _dot(h_scr[...], w_ref[...]).astype(o_ref.dtype)


def _inproj(x, mod_l, pre, w_in_p, cond, out_dtype):
    t, d = x.shape
    tm = TM * (4 // jnp.dtype(out_dtype).itemsize)
    n = w_in_p.shape[1]
    return pl.pallas_call(
        _inproj_body,
        grid=(t // tm, n // TN_IN),
        in_specs=[
            pl.BlockSpec((tm, d), lambda i, j: (i, 0)),
            pl.BlockSpec((None, 1, 3 * d), lambda i, j: (_cond_index(i, tm, cond), 0, 1)),
            pl.BlockSpec((1, d), lambda i, j: (0, 0)),
            pl.BlockSpec((d, TN_IN), lambda i, j: (0, j)),
        ],
        out_specs=pl.BlockSpec((tm, TN_IN), lambda i, j: (i, j)),
        out_shape=jax.ShapeDtypeStruct((t, n), out_dtype),
        scratch_shapes=[pltpu.VMEM((tm, d), bf16), pltpu.VMEM((tm, LANE), f32)],
        compiler_params=_cparams(("parallel", "arbitrary")),
        name="inproj",
    )(x, mod_l, pre, w_in_p)


POOL_PAD = 8


def _pool_body(u_ref, w_ref, sc_ref, o_ref, pad_scr, lvl_scr, *, seq):
    gc = POOL_GC
    pad = POOL_PAD
    n_lvl = seq + pad
    zeros = jnp.zeros((pad, POOL_WIDTH), f32)
    pad_scr[pl.ds(0, pad), :] = zeros
    pad_scr[pl.ds(pad + seq, pad), :] = zeros
    pad_scr[pl.ds(pad, seq), :] = u_ref[...].astype(f32)
    lvl_scr[:, pl.ds(n_lvl, pad), :] = jnp.zeros((2, pad, gc), f32)
    t = lax.broadcasted_iota(jnp.int32, (seq, 1), 0)
    for gi, win in enumerate(POOL_WINDOWS):
        cols = pl.ds(gi * gc, gc)
        read = lambda off, n: pad_scr[pl.ds(off, n), cols]
        k, slot = 1, 0
        while 2 * k < win:
            lvl_scr[slot, pl.ds(0, n_lvl), :] = read(0, n_lvl) + read(k, n_lvl)
            read = functools.partial(lambda s_, off, n: lvl_scr[s_, pl.ds(off, n), :], slot)
            k, slot = 2 * k, 1 - slot
        lo = jnp.maximum(t - win // 2, 0)
        hi = jnp.minimum(t + win - 1 - win // 2, seq - 1)
        inv_cnt = 1.0 / (hi - lo + 1).astype(f32)
        acc = read(pad - win // 2, seq) + read(pad, seq)
        pooled = acc * inv_cnt - pad_scr[pl.ds(pad, seq), cols]
        y = _dot(pooled.astype(bf16), w_ref[gi])
        o_ref[:, cols] = (y * sc_ref[:, cols]).astype(o_ref.dtype)


def _pool(p, pool_w, pool_scale, seq, row_block0, nseq):
    cb = OFF_POOL // POOL_WIDTH
    return pl.pallas_call(
        functools.partial(_pool_body, seq=seq),
        grid=(nseq,),
        in_specs=[
            pl.BlockSpec((seq, POOL_WIDTH), lambda s: (row_block0 + s, cb)),
            pl.BlockSpec((POOL_GROUPS, POOL_GC, POOL_GC), lambda s: (0, 0, 0)),
            pl.BlockSpec((1, POOL_WIDTH), lambda s: (0, 0)),
        ],
        out_specs=pl.BlockSpec((seq, POOL_WIDTH), lambda s: (s, 0)),
        out_shape=jax.ShapeDtypeStruct((nseq * seq, POOL_WIDTH), bf16),
        scratch_shapes=[pltpu.VMEM((seq + 2 * POOL_PAD, POOL_WIDTH), f32),
                        pltpu.VMEM((2, seq + 2 * POOL_PAD, POOL_GC), f32)],
        compiler_params=_cparams(("parallel",)),
        name="pool",
    )(p, pool_w, pool_scale)


def _lane_fold(x, op):
    parts = [x[:, i * LANE:(i + 1) * LANE] for i in range(x.shape[1] // LANE)]
    while len(parts) > 1:
        parts = [op(parts[i], parts[i + 1]) for i in range(0, len(parts) - 1, 2)] + parts[len(parts) & ~1:]
    return parts[0]


def _softmax_rows(s):
    m = jnp.max(_lane_fold(s, jnp.maximum), axis=-1, keepdims=True)
    e = jnp.exp(s - m)
    return e / jnp.sum(_lane_fold(e, jnp.add), axis=-1, keepdims=True)


def _ctx_attn_body(q_ref, k_ref, v_ref, *refs, n_alias, side, layer):
    refs = refs[n_alias:]
    n_src = sum(n_in for _, n_in in side)
    o_ref, nk_ref, nv_ref = refs[n_src:n_src + 3]
    for job, src_refs, dst_ref in _side_refs(side, refs[:n_src], refs[n_src + 3:]):
        job(pl.program_id(0), *src_refs, dst_ref)
    if not n_alias:
        for ref in (nk_ref, nv_ref):
            for other in range(ref.shape[0]):
                if other != layer:
                    ref[other] = jnp.zeros(ref.shape[1:], ref.dtype)
        nk_ref, nv_ref = nk_ref.at[layer], nv_ref.at[layer]
    hd = NA_HEAD_DIM
    for h in range(NA_HEADS):
        cols = pl.ds(h * hd, hd)
        kf = k_ref[:, cols]
        vf = v_ref[:, cols]
        nk_ref[h] = kf
        nv_ref[h] = vf
        p = _softmax_rows(_dot_nt(q_ref[:, cols].astype(bf16), kf.astype(bf16)) * (hd ** -0.5))
        o_ref[:, cols] = _dot(p.astype(bf16), vf.astype(bf16)).astype(o_ref.dtype)


def _ctx_attn(p, nseq, layer, depth, caches=None, make_side=()):
    spec = lambda off: pl.BlockSpec((SEQ, NA_WIDTH), lambda b: (b, off // NA_WIDTH))
    cache_shape = jax.ShapeDtypeStruct((nseq, depth, NA_HEADS, SEQ, NA_HEAD_DIM), f32)
    in_specs = [spec(OFF_NQ), spec(OFF_NK), spec(OFF_NV)]
    args = [p, p, p]
    aliases = {}
    if caches is None:
        cache_spec = pl.BlockSpec((None, depth, NA_HEADS, SEQ, NA_HEAD_DIM), lambda b: (b, 0, 0, 0, 0))
    else:
        cache_spec = pl.BlockSpec((None, None, NA_HEADS, SEQ, NA_HEAD_DIM), lambda b: (b, layer, 0, 0, 0))
        in_specs += [pl.BlockSpec(memory_space=pl.ANY)] * 2
        args += list(caches)
        aliases = {3: 1, 4: 2}
    side_in, side_args, side_out, side_shape, side = _side_io([make(nseq, lambda b: b) for make in make_side])
    return pl.pallas_call(
        functools.partial(_ctx_attn_body, n_alias=len(aliases), side=side, layer=layer),
        grid=(nseq,),
        in_specs=in_specs + side_in,
        out_specs=[pl.BlockSpec((SEQ, NA_WIDTH), lambda b: (b, 0)), cache_spec, cache_spec] + side_out,
        out_shape=[jax.ShapeDtypeStruct((nseq * SEQ, NA_WIDTH), bf16), cache_shape, cache_shape] + side_shape,
        input_output_aliases=aliases,
        compiler_params=_cparams(("arbitrary",)),
        name="ctx_attn",
    )(*args, *side_args)


LOG2E = float(np.log2(np.e))


def _na_bias_table(rpb):
    qc = np.arange(GRID_W)[:, None]
    kc = np.arange(GRID_W)[None, :]
    cs = np.clip(qc - NA_WIN_W // 2, 0, GRID_W - NA_WIN_W)
    ok = (kc >= cs) & (kc < cs + NA_WIN_W)
    cidx = np.clip(kc - qc + NA_WIN_W - 1, 0, 2 * NA_WIN_W - 2)
    onehot = jnp.asarray((cidx[None] == np.arange(2 * NA_WIN_W - 1)[:, None, None]) & ok[None], f32)
    toep = jnp.einsum('...rc,cqk->...rqk', rpb.astype(f32), onehot, precision=lax.Precision.HIGHEST)
    toep = jnp.where(ok, toep * LOG2E, NEG_INF)
    return jnp.concatenate([toep[..., :-1, :, :], toep[..., 1:, :, :]], axis=-1)


NA_UNROLL = 32
NA_SOFTMAX_UNROLL = 16
NA_CTX_ROWS = 256


def _na_body(*refs, rows, side):
    n_src = sum(n_in for _, n_in in side)
    qb_scr, kb_scr, vb_scr, ck_ref, cv_ref, bias_ref = refs[:6]
    o_ref = refs[6 + n_src]
    sl_scr, sc_scr, el_scr, ec_scr, den_scr, oc_scr = refs[7 + n_src + len(side):]
    step = pl.program_id(0) * pl.num_programs(1) + pl.program_id(1)
    for job, src_refs, dst_ref in _side_refs(side, refs[6:6 + n_src], refs[7 + n_src:7 + n_src + len(side)]):
        job(step, *src_refs, dst_ref)

    hd = NA_HEAD_DIM
    scale = hd ** -0.5 * LOG2E
    kh = min(NA_WIN_H, rows)
    nloc = kh * GRID_W
    n = rows * GRID_W
    assert qb_scr.dtype == bf16
    ck = ck_ref[...].astype(bf16)
    cv = cv_ref[...].astype(bf16)

    def row_slices(r):
        rs = jnp.clip(r - kh // 2, 0, rows - kh)
        q_rows = pl.ds(pl.multiple_of(r * GRID_W, GRID_W), GRID_W)
        k_rows = pl.ds(pl.multiple_of(rs * GRID_W, GRID_W), nloc)
        return rs, q_rows, k_rows

    def ctx_scores(i, carry):
        blk = pl.ds(pl.multiple_of(i * NA_CTX_ROWS, NA_CTX_ROWS), NA_CTX_ROWS)
        sc_scr[blk, :] = _dot_nt(qb_scr[blk, :], ck) * scale
        return carry

    lax.fori_loop(0, n // NA_CTX_ROWS, ctx_scores, 0, unroll=8)

    def loc_scores(r, carry):
        rs, q_rows, k_rows = row_slices(r)
        first = rs - r + NA_WIN_H - 1
        bias = jnp.concatenate([bias_ref[first + 2 * e] for e in range(kh // 2)], axis=1)
        sl_scr[q_rows, :] = _dot_nt(qb_scr[q_rows, :], kb_scr[k_rows, :]) * scale + bias
        return carry

    lax.fori_loop(0, rows, loc_scores, 0, unroll=NA_UNROLL)

    def row_max(r, carry):
        q_rows = pl.ds(pl.multiple_of(r * GRID_W, GRID_W), GRID_W)
        m = jnp.max(jnp.maximum(_lane_fold(sl_scr[q_rows, :], jnp.maximum),
                                _lane_fold(sc_scr[q_rows, :], jnp.maximum)), axis=-1, keepdims=True)
        oc_scr[q_rows, :] = jnp.broadcast_to(m, (GRID_W, hd))
        return carry

    lax.fori_loop(0, rows, row_max, 0, unroll=NA_UNROLL)

    def numerators(r, carry):
        q_rows = pl.ds(pl.multiple_of(r * GRID_W, GRID_W), GRID_W)
        m = oc_scr[q_rows, :]
        e_loc = jnp.exp2(sl_scr[q_rows, :] - _lanes(m, nloc))
        e_ctx = jnp.exp2(sc_scr[q_rows, :] - _lanes(m, sc_scr.shape[1]))
        den = jnp.sum(_lane_fold(e_loc, jnp.add) + _lane_fold(e_ctx, jnp.add), axis=-1, keepdims=True)
        el_scr[q_rows, :] = e_loc.astype(bf16)
        ec_scr[q_rows, :] = e_ctx.astype(bf16)
        den_scr[q_rows, :] = jnp.broadcast_to(den, (GRID_W, hd))
        return carry

    lax.fori_loop(0, rows, numerators, 0, unroll=NA_SOFTMAX_UNROLL)

    def ctx_values(i, carry):
        blk = pl.ds(pl.multiple_of(i * NA_CTX_ROWS, NA_CTX_ROWS), NA_CTX_ROWS)
        oc_scr[blk, :] = _dot(ec_scr[blk, :], cv)
        return carry

    lax.fori_loop(0, n // NA_CTX_ROWS, ctx_values, 0, unroll=8)

    def loc_values(r, carry):
        _, q_rows, k_rows = row_slices(r)
        o = _dot(el_scr[q_rows, :], vb_scr[k_rows, :]) + oc_scr[q_rows, :]
        o_ref[q_rows, :] = (o / den_scr[q_rows, :]).astype(o_ref.dtype)
        return carry

    lax.fori_loop(0, rows, loc_values, 0, unroll=NA_UNROLL)


def _na_latent(p, cache_k, cache_v, bias_tbl, layer, row_block0, nreq, make_side=()):
    n = DEC_SEQ
    hd = NA_HEAD_DIM
    past = cache_k.shape[3]
    rows = n // GRID_W
    side_in, side_args, side_out, side_shape, side = _side_io(
        [make(nreq * NA_HEADS, lambda b, h: b * NA_HEADS + h) for make in make_side])
    qkv = lambda off: pl.BlockSpec((n, hd), lambda b, h: (row_block0 + b, off // hd + h))
    cache = pl.BlockSpec((None, None, None, past, hd), lambda b, h: (b, layer, h, 0, 0))
    return pl.pallas_call(
        functools.partial(_na_body, rows=rows, side=side),
        grid=(nreq, NA_HEADS),
        in_specs=[qkv(OFF_NQ), qkv(OFF_NK), qkv(OFF_NV), cache, cache,
                  pl.BlockSpec((None, None, 2 * NA_WIN_H - 2, GRID_W, 2 * GRID_W),
                               lambda b, h: (layer, h, 0, 0, 0))] + side_in,
        out_specs=[pl.BlockSpec((n, hd), lambda b, h: (b, h))] + side_out,
        out_shape=[jax.ShapeDtypeStruct((nreq * n, NA_WIDTH), bf16)] + side_shape,
        scratch_shapes=[pltpu.VMEM((n, NA_WIN_H * GRID_W), f32), pltpu.VMEM((n, past), f32),
                        pltpu.VMEM((n, NA_WIN_H * GRID_W), bf16), pltpu.VMEM((n, past), bf16),
                        pltpu.VMEM((n, hd), f32), pltpu.VMEM((n, hd), f32)],
        compiler_params=_cparams(("arbitrary", "arbitrary")),
        name="na_latent",
    )(p, p, p, cache_k, cache_v, bias_tbl, *side_args)


GLA_PAD = 32
GLA_UNROLL = 32


def _rope_tables(seq):
    t = np.arange(seq)
    half = GLA_DK // 2
    nf = half // 2
    inv = ROPE_BASE ** (-np.arange(nf, dtype=np.float64) / nf)
    cos, sin = [], []
    for pos in (t // GRID_W, t % GRID_W):
        ang = pos[:, None].astype(np.float64) * inv
        cos += [np.cos(ang), np.cos(ang)]
        sin += [-np.sin(ang), np.sin(ang)]
    return (jnp.asarray(np.concatenate(cos, axis=-1), f32), jnp.asarray(np.concatenate(sin, axis=-1), f32))


def _rope(x, cos, sin_signed):
    nf = GLA_DK // 4
    lane = lax.broadcasted_iota(jnp.int32, x.shape, 1)
    partner = jnp.where(lane % (2 * nf) < nf, pltpu.roll(x, GLA_DK - nf, 1), pltpu.roll(x, nf, 1))
    return x * cos + partner * sin_signed


def _log_sigmoid(x):
    return jnp.minimum(x, 0.0) - jnp.log1p(jnp.exp(-jnp.abs(x)))


def _gla_body(*refs, seq, rope, with_s0, with_sfin, sfin_layer, n_carried, side):
    refs = list(refs)
    q_ref, k_ref, v_ref, r_ref, z_ref, wg_ref, bg_ref, ng_ref = refs[:8]
    refs = refs[8:]
    if rope:
        cos_ref, sin_ref = refs[:2]
        refs = refs[2:]
    if with_s0:
        s0_ref = refs[0]
        refs = refs[1:]
    refs = refs[n_carried:]
    n_src = sum(n_in for _, n_in in side)
    side_src, refs = refs[:n_src], refs[n_src:]
    o_ref = refs[0]
    refs = refs[1:]
    if with_sfin:
        sfin_ref = refs[0]
        refs = refs[1:]
    side_dst, refs = refs[:len(side)], refs[len(side):]
    step = pl.program_id(0) * pl.num_programs(1) + pl.program_id(1)
    for job, src_refs, dst_ref in _side_refs(side, side_src, side_dst):
        job(step, *src_refs, dst_ref)
    qi_scr, kn_scr, kd_scr, dec_scr, scan_scr, vb_scr, u_scr, sb_scr, o_scr, st_scr = refs

    ch = GLA_CHUNK
    nch = seq // ch
    dk, dv = GLA_DK, GLA_DV

    q = q_ref[...].astype(f32)
    k = k_ref[...].astype(f32)
    if rope:
        q = _rope(q, cos_ref[...], sin_ref[...])
        k = _rope(k, cos_ref[...], sin_ref[...])
    q = q * (dk ** -0.5)

    zb = z_ref[...].astype(bf16)
    pos = lax.broadcasted_iota(jnp.int32, (seq, 1), 0) % ch
    zpad = jnp.zeros((GLA_PAD, dk), f32)
    scan_scr[pl.ds(0, GLA_PAD), :] = zpad
    scan_scr[pl.ds(GLA_PAD + seq, GLA_PAD), :] = zpad
    for d in range(2):
        g = _log_sigmoid(_dot(zb, wg_ref[d]) + bg_ref[d]) / GLA_TAU
        b = g
        sh = 1
        while sh < ch:
            scan_scr[pl.ds(GLA_PAD, seq), :] = b
            if d == 0:
                b = b + jnp.where(pos >= sh, scan_scr[pl.ds(GLA_PAD - sh, seq), :], 0.0)
            else:
                b = b + jnp.where(pos < ch - sh, scan_scr[pl.ds(GLA_PAD + sh, seq), :], 0.0)
            sh *= 2
        b3 = b.reshape(nch, ch, dk)
        b_end = b3[:, ch - 1:ch, :] if d == 0 else b3[:, 0:1, :]
        lanes = pl.ds(d * dk, dk)
        qi_scr[:, lanes] = (q * jnp.exp(b)).astype(bf16)
        kn_scr[d] = (k * jnp.exp(-b)).astype(bf16)
        kd_scr[:, lanes] = (k.reshape(nch, ch, dk) * jnp.exp(b_end - b3)).reshape(seq, dk).astype(bf16)
        dec_scr[d] = jnp.exp(b_end)

    for d in range(2):
        if with_s0:
            st_scr[d] = s0_ref[d].T
        else:
            st_scr[d] = jnp.zeros((dv, dk), f32)

    ri = lax.broadcasted_iota(jnp.int32, (ch, ch), 0)
    ci = lax.broadcasted_iota(jnp.int32, (ch, ch), 1)

    vb_scr[...] = v_ref[...].astype(bf16)
    chunk_rows = lambda c: pl.ds(pl.multiple_of(c * ch, ch), ch)
    fwd, bwd = pl.ds(0, dk), pl.ds(dk, dk)

    def increments(c, carry):
        rows = chunk_rows(c)
        u_scr[c] = lax.dot_general(vb_scr[rows, :], kd_scr[rows, :], (((0,), (0,)), ((), ())),
                                   preferred_element_type=f32)
        return carry

    lax.fori_loop(0, nch, increments, 0, unroll=min(GLA_UNROLL, nch))

    def states(i, carry):
        for d, c, lanes in ((0, i, fwd), (1, nch - 1 - i, bwd)):
            st = st_scr[d]
            sb_scr[c, :, lanes] = st.astype(bf16)
            st_scr[d] = st * dec_scr[d, c] + u_scr[c, :, lanes]
        return carry

    lax.fori_loop(0, nch, states, 0, unroll=min(8, nch))

    if with_sfin:
        if not n_carried:
            for other in range(sfin_ref.shape[0]):
                if other != sfin_layer:
                    sfin_ref[other] = jnp.zeros(sfin_ref.shape[1:], sfin_ref.dtype)
            sfin_ref = sfin_ref.at[sfin_layer]
        for d in range(2):
            sfin_ref[d] = st_scr[d].T

    def outputs(c, carry):
        rows = chunk_rows(c)
        qi = qi_scr[rows, :]
        pf = _dot_nt(qi[:, 0:dk], kn_scr[0, rows, :])
        pb = _dot_nt(qi[:, dk:2 * dk], kn_scr[1, rows, :])
        a = jnp.where(ci < ri, pf, jnp.where(ci > ri, pb, pf + pb))
        o_scr[rows, :] = _dot(a.astype(bf16), vb_scr[rows, :]) + _dot_nt(qi, sb_scr[c])
        return carry

    lax.fori_loop(0, nch, outputs, 0, unroll=min(GLA_UNROLL, nch))

    o = o_scr[...]
    r = r_ref[...].astype(f32)
    o = o * lax.rsqrt(jnp.mean(o * o, axis=-1, keepdims=True) + EPS) * ng_ref[...]
    o_ref[...] = (o * (r * jax.nn.sigmoid(r))).astype(o_ref.dtype)


def _gla(p, wgate_p, b_gate, gla_norm, seq, row_block0, nreq, rope_tabs=None, state=None, layer=0,
         with_sfin=False, depth=1, carried=None, make_side=()):
    dk, dv = GLA_DK, GLA_DV
    nch = seq // GLA_CHUNK
    rope = rope_tabs is not None
    with_s0 = state is not None
    blk = lambda w, off: pl.BlockSpec((seq, w), lambda b, h: (row_block0 + b, off // w + h))
    in_specs = [blk(dk, OFF_GQ), blk(dk, OFF_GK), blk(dv, OFF_GV), blk(dv, OFF_GR),
                pl.BlockSpec((seq, LANE), lambda b, h: (row_block0 + b, OFF_GZ // LANE)),
                pl.BlockSpec((2, LANE, dk), lambda b, h: (0, 0, h)),
                pl.BlockSpec((2, 1, dk), lambda b, h: (0, 0, h)),
                pl.BlockSpec((1, dv), lambda b, h: (0, h))]
    args = [p, p, p, p, p, wgate_p, b_gate.reshape(2, 1, GLA_KW), gla_norm.reshape(1, GLA_VW)]
    if rope:
        in_specs += [pl.BlockSpec((seq, dk), lambda b, h: (0, 0))] * 2
        args += list(rope_tabs)
    if with_s0:
        in_specs.append(pl.BlockSpec((None, None, 2, None, dk, dv), lambda b, h: (b, layer, 0, h, 0, 0)))
        args.append(state)
    aliases = {}
    if carried is not None:
        aliases = {len(args): 1}
        in_specs.append(pl.BlockSpec(memory_space=pl.ANY))
        args.append(carried)
    side_in, side_args, side_out, side_shape, side = _side_io(
        [make(nreq * GLA_HEADS, lambda b, h: b * GLA_HEADS + h) for make in make_side])
    in_specs += side_in
    args += side_args
    out_specs = [pl.BlockSpec((seq, dv), lambda b, h: (b, h))]
    out_shape = [jax.ShapeDtypeStruct((nreq * seq, GLA_VW), bf16)]
    if with_sfin:
        if carried is None:
            out_specs.append(pl.BlockSpec((None, depth, 2, None, dk, dv), lambda b, h: (b, 0, 0, h, 0, 0)))
        else:
            out_specs.append(pl.BlockSpec((None, None, 2, None, dk, dv), lambda b, h: (b, layer, 0, h, 0, 0)))
        out_shape.append(jax.ShapeDtypeStruct((nreq, depth, 2, GLA_HEADS, dk, dv), f32))
    out_specs += side_out
    out_shape += side_shape
    scratch = [pltpu.VMEM((seq, 2 * dk), bf16), pltpu.VMEM((2, seq, dk), bf16), pltpu.VMEM((seq, 2 * dk), bf16),
               pltpu.VMEM((2, nch, 1, dk), f32), pltpu.VMEM((seq + 2 * GLA_PAD, dk), f32),
               pltpu.VMEM((seq, dv), bf16), pltpu.VMEM((nch, dv, 2 * dk), f32), pltpu.VMEM((nch, dv, 2 * dk), bf16),
               pltpu.VMEM((seq, dv), f32), pltpu.VMEM((2, dv, dk), f32)]
    return pl.pallas_call(
        functools.partial(_gla_body, seq=seq, rope=rope, with_s0=with_s0, with_sfin=with_sfin,
                          sfin_layer=layer, n_carried=len(aliases), side=side),
        grid=(nreq, GLA_HEADS),
        in_specs=in_specs,
        out_specs=out_specs,
        out_shape=out_shape,
        input_output_aliases=aliases,
        scratch_shapes=scratch,
        compiler_params=_cparams(("arbitrary", "arbitrary")),
        name="gla",
    )(*args)


TM_MERGE = 512
MERGE_SUB = 2


def _merge_body(x_ref, mod_ref, post_ref, bp_ref, bn_ref, bg_ref, gl_ref, w_ref, o_ref, m_scr, mb_scr, r_scr):
    n = pl.program_id(1)
    m_ref = m_scr.at[pl.program_id(2)]
    mb_ref = mb_scr.at[pl.program_id(2)]

    for bi, br_ref in enumerate((bp_ref, bn_ref, bg_ref)):
        @pl.when(n == bi)
        def _():
            y = jax.nn.sigmoid(gl_ref[...].astype(f32)) * _dot(br_ref[...], w_ref[...])
            m_ref[...] = y if bi == 0 else m_ref[...] + y

    @pl.when(n == N_BRANCH)
    def _():
        mb_ref[...] = m_ref[...].astype(bf16)
        m_ref[...] = _dot(mb_ref[:, 0:BRANCH_W], w_ref[...])

    @pl.when(n == N_BRANCH + 1)
    def _():
        m_ref[...] += _dot(mb_ref[:, BRANCH_W:2 * BRANCH_W], w_ref[...])
        _norm_gate_residual(m_ref, x_ref, mod_ref, post_ref, o_ref, r_scr, 1.0)


def _merge(x, mod_l, post, y_pool, y_na, y_gla, p, w_stack, cond):
    t, d = x.shape
    tm = TM_MERGE
    sub = MERGE_SUB
    nsteps = N_BRANCH + d // BRANCH_W
    tile = lambda i, s: i * sub + s

    def rows_at(first, last):
        def index(i, n, s):
            return jnp.where(n < first, jnp.maximum(tile(i, 0) - 1, 0),
                             jnp.where(n > last, tile(i, sub - 1), tile(i, s)))
        return index

    last = nsteps - 1
    br = lambda step: pl.BlockSpec((tm, BRANCH_W), lambda i, n, s: (rows_at(step, step)(i, n, s), 0))
    xo = pl.BlockSpec((tm, d), lambda i, n, s: (rows_at(last, last)(i, n, s), 0))
    return pl.pallas_call(
        _merge_body,
        grid=(t // (tm * sub), nsteps, sub),
        in_specs=[
            xo,
            pl.BlockSpec((None, 1, 3 * d), lambda i, n, s: (_cond_index(tile(i, s), tm, cond), 0, 1)),
            pl.BlockSpec((1, d), lambda i, n, s: (0, 0)),
            br(0), br(1), br(2),
            pl.BlockSpec((tm, d),
                         lambda i, n, s: (rows_at(0, N_BRANCH - 1)(i, n, s), jnp.minimum(n, N_BRANCH - 1))),
            pl.BlockSpec((BRANCH_W, d), lambda i, n, s: (n, 0)),
        ],
        out_specs=xo,
        out_shape=jax.ShapeDtypeStruct((t, d), f32),
        scratch_shapes=[pltpu.VMEM((sub, tm, d), f32), pltpu.VMEM((sub, tm, d), bf16), pltpu.VMEM((tm, LANE), f32)],
        compiler_params=_cparams(("arbitrary", "arbitrary", "arbitrary")),
        name="merge",
    )(x, mod_l, post, y_pool, y_na, y_gla, p, w_stack)


_IN_SPLITS = (POOL_WIDTH, NA_WIDTH, NA_WIDTH, NA_WIDTH, GLA_KW, GLA_KW, GLA_VW, 2 * GLA_RANK, GLA_VW, GATE_W)
_IN_OFFS = tuple(int(v) for v in np.cumsum((0,) + _IN_SPLITS))
_IN_RUNS = ((_IN_OFFS[9], _IN_OFFS[10]), (_IN_OFFS[0], _IN_OFFS[7]), (_IN_OFFS[8], _IN_OFFS[9]),
            (_IN_OFFS[7], _IN_OFFS[8]))


CT_IN = 512
_IN_TILE_STARTS = []
for _a, _b in _IN_RUNS:
    _IN_TILE_STARTS += [_a + CT_IN * _t for _t in range(-(-(_b - _a) // CT_IN))]
assert len(_IN_TILE_STARTS) * CT_IN == IN_COLS_P and all(v % 8 == 0 for v in _IN_TILE_STARTS)
_IN_LAST_VALID = (_IN_RUNS[-1][1] - _IN_RUNS[-1][0]) % CT_IN or CT_IN
assert all((b - a) % CT_IN == 0 for a, b in _IN_RUNS[:-1])


def _w_in_tile_start(j):
    out = jnp.int32(_IN_TILE_STARTS[0]) + CT_IN * j
    for t in range(1, len(_IN_TILE_STARTS)):
        if _IN_TILE_STARTS[t] != _IN_TILE_STARTS[t - 1] + CT_IN:
            out = jnp.where(j >= t, _IN_TILE_STARTS[t] + CT_IN * (j - t), out)
    return out


N_IN_TILES = IN_COLS_P // CT_IN


def _cast_w_in_body(step, w_ref, o_ref):
    @pl.when(step < N_IN_TILES)
    def _():
        col = lax.broadcasted_iota(jnp.int32, (1, CT_IN), 1)
        valid = jnp.where(step == N_IN_TILES - 1, _IN_LAST_VALID, CT_IN)
        o_ref[...] = jnp.where(col < valid, w_ref[...].T, 0.0).astype(bf16)


def _w_in_cast(wt, n_cols, layer, nsteps, step_of):
    assert nsteps >= N_IN_TILES
    d = wt.shape[1]
    tile = lambda *g: jnp.minimum(step_of(*g), N_IN_TILES - 1)
    return _SideCast(
        (wt,),
        (pl.BlockSpec((pl.Element(CT_IN), pl.Element(d)),
                      lambda *g: (pl.multiple_of(layer * n_cols + _w_in_tile_start(tile(*g)), 8), 0)),),
        pl.BlockSpec((d, CT_IN), lambda *g: (0, tile(*g))),
        jax.ShapeDtypeStruct((d, IN_COLS_P), bf16),
        _cast_w_in_body)


class _SideCast(NamedTuple):
    arrays: tuple
    in_specs: tuple
    out_spec: pl.BlockSpec
    out_shape: jax.ShapeDtypeStruct
    body: Callable


def _side_refs(side, src_refs, dst_refs):
    jobs, k = [], 0
    for (body, n_in), dst in zip(side, dst_refs):
        jobs.append((body, src_refs[k:k + n_in], dst))
        k += n_in
    return jobs


def _side_io(side):
    return ([sp for j in side for sp in j.in_specs], [a for j in side for a in j.arrays],
            [j.out_spec for j in side], [j.out_shape for j in side], tuple((j.body, len(j.arrays)) for j in side))


def _cast_ffn_in_body(step, w_ref, o_ref):
    del step
    for f in range(D_FF_P // TF):
        n = min(TF, D_FF - f * TF)
        for half in range(2):
            dst = (2 * f + half) * TF
            o_ref[:, dst:dst + n] = w_ref[:, half * D_FF + f * TF:half * D_FF + f * TF + n].astype(bf16)
            if n < TF:
                o_ref[:, dst + n:dst + TF] = jnp.zeros((w_ref.shape[0], TF - n), bf16)


def _ffn_in_cast(w, layer, slot, nsteps, step_of):
    d = w.shape[2]
    rows = d // nsteps
    assert rows * nsteps == d and rows % 16 == 0
    return _SideCast(
        (w,),
        (pl.BlockSpec((None, None, rows, 2 * D_FF), lambda *g: (layer, slot, step_of(*g), 0)),),
        pl.BlockSpec((rows, 2 * D_FF_P), lambda *g: (step_of(*g), 0)),
        jax.ShapeDtypeStruct((d, 2 * D_FF_P), bf16),
        _cast_ffn_in_body)


def _cast_ffn_out_body(step, w_ref, o_ref):
    rows = w_ref.shape[0]
    row = step * rows + lax.broadcasted_iota(jnp.int32, (rows, 1), 0)
    o_ref[...] = jnp.where(row < D_FF, w_ref[...], 0.0).astype(bf16)


def _ffn_out_cast(w, layer, slot, nsteps, step_of):
    d = w.shape[3]
    rows = D_FF_P // nsteps
    assert rows * nsteps == D_FF_P and rows % 16 == 0 and rows * (nsteps - 1) < D_FF
    return _SideCast(
        (w,),
        (pl.BlockSpec((None, None, rows, d), lambda *g: (layer, slot, step_of(*g), 0)),),
        pl.BlockSpec((rows, d), lambda *g: (step_of(*g), 0)),
        jax.ShapeDtypeStruct((D_FF_P, d), bf16),
        _cast_ffn_out_body)


def _run_cast_body(*refs, cast):
    cast(pl.program_id(0), *refs)


def _run_cast(make_job, nsteps, name):
    job = make_job(nsteps, lambda r: r)
    return pl.pallas_call(
        functools.partial(_run_cast_body, cast=job.body),
        grid=(nsteps,),
        in_specs=list(job.in_specs),
        out_specs=job.out_spec,
        out_shape=job.out_shape,
        compiler_params=_cparams(("parallel",)),
        name=name,
    )(*job.arrays)


W_STACK_ROWS = 256


def _cast_w_stack_body(step, wb_ref, wo_ref, o_ref, *, n_branch, n_out):
    @pl.when(step < n_branch)
    def _():
        o_ref[...] = wb_ref[...].astype(bf16)

    @pl.when(jnp.logical_and(step >= n_branch, step < n_branch + n_out))
    def _():
        o_ref[...] = wo_ref[...].astype(bf16)


def _w_stack_cast(w_branch, w_out, layer, nsteps, step_of):
    depth, nb, bw, d = w_branch.shape
    r = W_STACK_ROWS
    n_branch, n_out = nb * bw // r, d // r
    assert nsteps >= n_branch + n_out and bw % r == 0
    t = lambda *g: step_of(*g)
    return _SideCast(
        (w_branch.reshape(depth * nb * bw, d), w_out.reshape(depth * d, d)),
        (pl.BlockSpec((r, d), lambda *g: (layer * n_branch + jnp.minimum(t(*g), n_branch - 1), 0)),
         pl.BlockSpec((r, d), lambda *g: (layer * n_out + jnp.clip(t(*g) - n_branch, 0, n_out - 1), 0))),
        pl.BlockSpec((r, d), lambda *g: (jnp.minimum(t(*g), n_branch + n_out - 1), 0)),
        jax.ShapeDtypeStruct((nb * bw + d, d), bf16),
        functools.partial(_cast_w_stack_body, n_branch=n_branch, n_out=n_out))


def _prep_gate(w_gate):
    out = jnp.zeros((2, LANE, GLA_KW), f32)
    for d in range(2):
        out = out.at[d, d * GLA_RANK:(d + 1) * GLA_RANK].set(w_gate[d])
    return out.astype(bf16)


def kernel(x_prompt, x_sample, c, cache_na_k, cache_na_v, state_gla, c_ctx, w_mod, b_mod, norm_pre, norm_post,
           w_ffn_in, w_ffn_out, w_in, pool_w, pool_scale, na_rpb, gla_w_gate, gla_b_gate, gla_norm, w_branch,
           w_out):
    nb, seq, d = x_prompt.shape
    ndec, dseq, _ = x_sample.shape
    depth = w_mod.shape[0]
    n_ctx = nb * seq
    n_lat = ndec * dseq
    assert (seq, dseq, d) == (SEQ, DEC_SEQ, D_MODEL)
    assert n_ctx % (TM_MERGE * MERGE_SUB) == 0 and n_ctx % TM_FFN == 0 and dseq % TM_FFN == 0

    xs = [x_prompt.reshape(n_ctx, d), x_sample.reshape(n_lat, d)]
    conds = [(0, n_ctx), (1, dseq)]
    ncond = -(-(1 + ndec) // 8) * 8
    c_all = jnp.concatenate([c_ctx[None], c, jnp.zeros((ncond - 1 - ndec, d), f32)], axis=0)
    rope_tabs = _rope_tables(dseq)
    bias_tbl = _na_bias_table(na_rpb)

    mod_job = lambda l: functools.partial(_mod_job, c_all, w_mod, b_mod, l)
    in_cast = lambda l, s: functools.partial(_ffn_in_cast, w_ffn_in, l, s)
    out_cast = lambda l, s: functools.partial(_ffn_out_cast, w_ffn_out, l, s)
    wt_in = jnp.swapaxes(w_in, 1, 2).reshape(depth * w_in.shape[2], d)
    w_in_cast = lambda l: functools.partial(_w_in_cast, wt_in, w_in.shape[2], l)
    w_stack_cast = lambda l: functools.partial(_w_stack_cast, w_branch, w_out, l)
    mods = {0: _run_cast(mod_job(0), N_MOD * d // 1024, "modulation")}
    ffn_w_in = {(0, 0): _run_cast(in_cast(0, 0), d // 256, "cast_ffn_in")}
    ffn_w_out = {(0, 0): _run_cast(out_cast(0, 0), D_FF_P // TF, "cast_ffn_out")}
    w_in_p = {0: _run_cast(w_in_cast(0), N_IN_TILES, "cast_w_in")}

    caches = None
    new_s = None
    for l in range(depth):
        mod_l = mods[l].reshape(ncond, 1, N_MOD * d)
        pre = norm_pre[l].reshape(3, 1, d)
        post = norm_post[l].reshape(3, 1, d)
        wgate_p = _prep_gate(gla_w_gate[l])
        pw = pool_w[l].astype(bf16)
        psc = pool_scale[l].reshape(1, POOL_WIDTH)
        later = [(l, 1)] + ([(l + 1, 0)] if l + 1 < depth else [])
        nxt = [l + 1] if l + 1 < depth else []

        xs = [_ffn(x, mod_l, pre[0], post[0], ffn_w_in[l, 0], ffn_w_out[l, 0], 0, cond) for x, cond in zip(xs, conds)]
        p_ctx, p_lat = [_inproj(x, mod_l, pre[1], w_in_p[l], cond, dt) for x, cond, dt in zip(xs, conds, (f32, bf16))]

        y_pool = [_pool(p_ctx, pw, psc, seq, 0, nb), _pool(p_lat, pw, psc, dseq, 0, ndec)]
        na_ctx, new_k, new_v, *side = _ctx_attn(p_ctx, nb, l, depth, caches, make_side=[mod_job(j) for j in nxt])
        mods.update(zip(nxt, side))
        caches = (new_k, new_v)
        on_na = nxt if ndec * NA_HEADS >= N_IN_TILES else []
        na_lat, *side = _na_latent(p_lat, cache_na_k, cache_na_v, bias_tbl, l, 0, ndec,
                                   make_side=[in_cast(*ls) for ls in later] + [w_in_cast(j) for j in on_na])
        ffn_w_in.update(zip(later, side))
        w_in_p.update(zip(on_na, side[len(later):]))
        y_na = [na_ctx, na_lat]
        on_gla = [j for j in nxt if j not in on_na]
        g_ctx, new_s, w_stack, *side = _gla(p_ctx, wgate_p, gla_b_gate[l], gla_norm[l], seq, 0, nb, layer=l,
                                            with_sfin=True, depth=depth, carried=new_s,
                                            make_side=[w_stack_cast(l)] + [w_in_cast(j) for j in on_gla])
        w_in_p.update(zip(on_gla, side))
        g_lat, *side = _gla(p_lat, wgate_p, gla_b_gate[l], gla_norm[l], dseq, 0, ndec, rope_tabs=rope_tabs,
                            state=state_gla, layer=l, make_side=[out_cast(*ls) for ls in later])
        ffn_w_out.update(zip(later, side))
        y_gla = [g_ctx, g_lat]

        xs = [_merge(x, mod_l, post[1], yp, yn, yg, p, w_stack, cond)
              for x, yp, yn, yg, p, cond in zip(xs, y_pool, y_na, y_gla, (p_ctx, p_lat), conds)]
        xs = [_ffn(x, mod_l, pre[2], post[2], ffn_w_in[l, 1], ffn_w_out[l, 1], 2, cond) for x, cond in zip(xs, conds)]

    return (xs[0].reshape(nb, seq, d), xs[1].reshape(ndec, dseq, d), caches[0], caches[1], new_s)
```
